```python
import jax, jax.numpy as jnp
from jax import lax
import numpy as np

D_MODEL = 1024
BATCH = 4
SEQ = 4096
DEPTH = 2
DEC_BATCH = 128
DEC_SEQ = 4
PAST_LEN = 8192
PAGE_SIZE = 128

N_A_LAYERS = DEPTH // 2
N_B_LAYERS = DEPTH - N_A_LAYERS
ML_HEADS = 4
ML_DV = D_MODEL // ML_HEADS
ML_DK = ML_DV // 2
ML_CHUNK = 64
GATE_SOFTCAP = 15.0
ML_Q_END = ML_HEADS * ML_DK
ML_K_END = 2 * ML_HEADS * ML_DK
ML_V_END = ML_K_END + ML_HEADS * ML_DV
ML_O_END = ML_V_END + ML_HEADS * ML_DV
ML_I_END = ML_O_END + ML_HEADS
ML_IN = ML_I_END + ML_HEADS
ATT_HD = 64
ATT_QH = D_MODEL // ATT_HD
ATT_KVH = 4
ATT_GROUP = ATT_QH // ATT_KVH
WINDOW = 128
ATT_BLOCK = 128
D_FF = 4 * D_MODEL
EPS = 1e-6

kernel_name = "yoco_mlstm_swa_sink_decoder_step"


def rmsnorm(x, g):
    xf = x.astype(jnp.float32)
    y = xf * lax.rsqrt(jnp.mean(xf * xf, axis=-1, keepdims=True) + EPS)
    return (y * g.astype(jnp.float32)).astype(x.dtype)


def softcap(x):
    return GATE_SOFTCAP * jnp.tanh(x / GATE_SOFTCAP)


def mlstm_chunkwise(q, k, v, log_i, log_f, C0, n0, m0, chunk):
    B, L, H, DK = q.shape
    DV = v.shape[-1]
    nc = L // chunk

    def split(a):
        return jnp.moveaxis(a.reshape((B, nc, chunk) + a.shape[2:]), 1, 0)

    xs = (split(q), split(k), split(v), split(log_i), split(log_f))
    causal = jnp.tril(jnp.ones((chunk, chunk), dtype=bool))[None, :, :, None]

    def step(carry, inp):
        C, n, m = carry
        qc, kc, vc, lic, lfc = inp
        b = jnp.cumsum(lfc, axis=1)
        dmat = b[:, :, None, :] - b[:, None, :, :] + lic[:, None, :, :]
        dmat = jnp.where(causal, dmat, -jnp.inf)
        inter = b + m[:, None, :]
        m_t = jnp.maximum(inter, jnp.max(dmat, axis=2))
        w = jnp.exp(dmat - m_t[:, :, None, :])
        a_inter = jnp.exp(inter - m_t)
        s = jnp.einsum('bthd,bshd->btsh', qc, kc) * w
        num = jnp.einsum('btsh,bshv->bthv', s, vc) + a_inter[..., None] * jnp.einsum('bthd,bhdv->bthv', qc, C)
        den = jnp.sum(s, axis=2) + a_inter * jnp.einsum('bthd,bhd->bth', qc, n)
        h = num / jnp.maximum(jnp.abs(den), jnp.exp(-m_t))[..., None]
        m_new = m_t[:, -1]
        b_last = b[:, -1]
        decay = jnp.exp(b_last + m - m_new)
        wk = jnp.exp(b_last[:, None, :] - b + lic - m_new[:, None, :])
        C_new = decay[..., None, None] * C + jnp.einsum('bsh,bshd,bshv->bhdv', wk, kc, vc)
        n_new = decay[..., None] * n + jnp.einsum('bsh,bshd->bhd', wk, kc)
        return (C_new, n_new, m_new), h

    (C, n, m), hs = lax.scan(step, (C0, n0, m0), xs)
    h = jnp.moveaxis(hs, 0, 1).reshape(B, L, H, DV)
    return h, C, n, m


def mlstm_mixer(x, norm_g, w_in, b_i, b_f, head_g, w_out, C0, n0, m0, chunk):
    B, L, _ = x.shape
    p = rmsnorm(x, norm_g) @ w_in
    q = p[..., :ML_Q_END].reshape(B, L, ML_HEADS, ML_DK).astype(jnp.float32)
    k = p[..., ML_Q_END:ML_K_END].reshape(B, L, ML_HEADS, ML_DK).astype(jnp.float32) * (ML_DK ** -0.5)
    v = p[..., ML_K_END:ML_V_END].reshape(B, L, ML_HEADS, ML_DV).astype(jnp.float32)
    o = jax.nn.sigmoid(p[..., ML_V_END:ML_O_END].astype(jnp.float32))
    log_i = softcap(p[..., ML_O_END:ML_I_END].astype(jnp.float32) + b_i.astype(jnp.float32))
    log_f = jax.nn.log_sigmoid(softcap(p[..., ML_I_END:].astype(jnp.float32) + b_f.astype(jnp.float32)))
    h, C, n, m = mlstm_chunkwise(q, k, v, log_i, log_f, C0.astype(jnp.float32),
                                 n0.astype(jnp.float32), m0.astype(jnp.float32), chunk)
    h = rmsnorm(h, head_g.reshape(ML_HEADS, ML_DV)).reshape(B, L, ML_HEADS * ML_DV)
    return x + (o * h).astype(x.dtype) @ w_out, C, n, m


def sqrelu_mlp(x, g, w1, w2):
    return x + jnp.square(jax.nn.relu(rmsnorm(x, g) @ w1)) @ w2


def shared_kv(x, g, w_kv, k_norm_g):
    B, L, _ = x.shape
    kv = rmsnorm(x, g) @ w_kv
    k = kv[..., :ATT_KVH * ATT_HD].reshape(B, L, ATT_KVH, ATT_HD)
    v = kv[..., ATT_KVH * ATT_HD:].reshape(B, L, ATT_KVH, ATT_HD)
    return rmsnorm(k, k_norm_g), v


def sink_attention(q, k, v, mask, sinks):
    s = jnp.einsum('...qhgd,...khd->...hgqk', q.astype(jnp.float32), k.astype(jnp.float32)) * (ATT_HD ** -0.5)
    s = jnp.where(mask, s, -jnp.inf)
    sink_col = jnp.broadcast_to(sinks.astype(jnp.float32)[:, :, None, None], s.shape[:-1] + (1,))
    p = jax.nn.softmax(jnp.concatenate([s, sink_col], axis=-1), axis=-1)[..., :-1]
    return jnp.einsum('...hgqk,...khd->...qhgd', p, v.astype(jnp.float32))


def banded_window_attention(q, k, v, sinks):
    B, L = q.shape[:2]
    nb = L // ATT_BLOCK
    qb = q.reshape(B, nb, ATT_BLOCK, ATT_KVH, ATT_GROUP, ATT_HD)

    def band(a):
        ap = jnp.concatenate([jnp.zeros_like(a[:, :ATT_BLOCK]), a], axis=1)
        ap = ap.reshape(B, nb + 1, ATT_BLOCK, ATT_KVH, ATT_HD)
        return jnp.concatenate([ap[:, :-1], ap[:, 1:]], axis=2)

    kb, vb = band(k), band(v)
    qi = jnp.arange(ATT_BLOCK)[:, None] + ATT_BLOCK
    ki = jnp.arange(2 * ATT_BLOCK)[None, :]
    rel = qi - ki
    blk = jnp.arange(nb)[:, None, None]
    mask = (rel >= 0) & (rel <= WINDOW) & (blk * ATT_BLOCK - ATT_BLOCK + ki >= 0)
    o = sink_attention(qb, kb, vb, mask[None, :, None, None], sinks)
    return o.reshape(B, L, ATT_KVH, ATT_GROUP, ATT_HD)


def cached_window_attention(q, k, v, win_k, win_v, sinks):
    L = q.shape[1]
    W = win_k.shape[1]
    kk = jnp.concatenate([win_k.astype(k.dtype), k], axis=1)
    vv = jnp.concatenate([win_v.astype(v.dtype), v], axis=1)
    rel = (jnp.arange(L)[:, None] + W) - jnp.arange(W + L)[None, :]
    mask = (rel >= 0) & (rel <= WINDOW)
    o = sink_attention(q, kk, vv, mask, sinks)
    return o, kk[:, -W:], vv[:, -W:]


def swa_mixer(x, k, v, win_k, win_v, norm_g, w_q, q_norm_g, sinks, w_o):
    B, L, _ = x.shape
    q = (rmsnorm(x, norm_g) @ w_q).reshape(B, L, ATT_KVH, ATT_GROUP, ATT_HD)
    q = rmsnorm(q, q_norm_g)
    sk = sinks.reshape(ATT_KVH, ATT_GROUP)
    if win_k is None:
        o = banded_window_attention(q, k, v, sk)
        new_k, new_v = k[:, -WINDOW:], v[:, -WINDOW:]
    else:
        o, new_k, new_v = cached_window_attention(q, k, v, win_k, win_v, sk)
    return x + o.reshape(B, L, ATT_QH * ATT_HD).astype(x.dtype) @ w_o, new_k, new_v


def run_trunk(x, C0, n0, m0, win_k, win_v, chunk,
              ml_norm_g, ml_w_in, ml_b_i, ml_b_f, ml_head_g, ml_w_out,
              kv_norm_g, w_kv, k_norm_g,
              att_norm_g, att_w_q, q_norm_g, att_sinks, att_w_o,
              mlp_norm_g, mlp_w1, mlp_w2):
    Cs, ns, ms = [], [], []
    kv_k = kv_v = new_k = new_v = None
    for layer in range(DEPTH):
        if layer < N_A_LAYERS:
            x, C, n, m = mlstm_mixer(x, ml_norm_g[layer], ml_w_in[layer], ml_b_i[layer], ml_b_f[layer],
                                     ml_head_g[layer], ml_w_out[layer], C0[layer], n0[layer], m0[layer], chunk)
            Cs.append(C); ns.append(n); ms.append(m)
        else:
            j = layer - N_A_LAYERS
            x, new_k, new_v = swa_mixer(x, kv_k, kv_v, win_k, win_v, att_norm_g[j], att_w_q[j],
                                        q_norm_g[j], att_sinks[j], att_w_o[j])
        x = sqrelu_mlp(x, mlp_norm_g[layer], mlp_w1[layer], mlp_w2[layer])
        if layer == N_A_LAYERS - 1:
            kv_k, kv_v = shared_kv(x, kv_norm_g, w_kv, k_norm_g)
    return x, jnp.stack(Cs), jnp.stack(ns), jnp.stack(ms), new_k, new_v


def setup_inputs(seed: int = 0) -> dict:
    key = jax.random.key(seed)
    ks = jax.random.split(key, 32)
    f32 = jnp.float32
    nrm = lambda k, shape, s=1.0: (jax.random.normal(k, shape, f32) * s)
    return {
        "x_prompt": nrm(ks[0], (BATCH, SEQ, D_MODEL)),
        "x_sample": nrm(ks[1], (DEC_BATCH, DEC_SEQ, D_MODEL)),
        "state_mlstm_C": nrm(ks[2], (N_A_LAYERS, DEC_BATCH, ML_HEADS, ML_DK, ML_DV), 0.5),
        "state_mlstm_n": nrm(ks[3], (N_A_LAYERS, DEC_BATCH, ML_HEADS, ML_DK), 0.5),
        "state_mlstm_m": nrm(ks[4], (N_A_LAYERS, DEC_BATCH, ML_HEADS)),
        "cache_win_k": nrm(ks[5], (DEC_BATCH, WINDOW, ATT_KVH, ATT_HD)),
        "cache_win_v": nrm(ks[6], (DEC_BATCH, WINDOW, ATT_KVH, ATT_HD)),
        "ml_norm_g": 1.0 + nrm(ks[7], (N_A_LAYERS, D_MODEL), 0.02),
        "ml_w_in": nrm(ks[8], (N_A_LAYERS, D_MODEL, ML_IN), D_MODEL ** -0.5),
        "ml_b_i": nrm(ks[9], (N_A_LAYERS, ML_HEADS), 0.1),
        "ml_b_f": 3.0 + nrm(ks[10], (N_A_LAYERS, ML_HEADS), 0.1),
        "ml_head_g": 1.0 + nrm(ks[11], (N_A_LAYERS, ML_HEADS * ML_DV), 0.02),
        "ml_w_out": nrm(ks[12], (N_A_LAYERS, ML_HEADS * ML_DV, D_MODEL), (ML_HEADS * ML_DV) ** -0.5),
        "kv_norm_g": 1.0 + nrm(ks[13], (D_MODEL,), 0.02),
        "w_kv": nrm(ks[14], (D_MODEL, 2 * ATT_KVH * ATT_HD), D_MODEL ** -0.5),
        "k_norm_g": 1.0 + nrm(ks[15], (ATT_HD,), 0.02),
        "att_norm_g": 1.0 + nrm(ks[16], (N_B_LAYERS, D_MODEL), 0.02),
        "att_w_q": nrm(ks[17], (N_B_LAYERS, D_MODEL, ATT_QH * ATT_HD), D_MODEL ** -0.5),
        "q_norm_g": 1.0 + nrm(ks[18], (N_B_LAYERS, ATT_HD), 0.02),
        "att_sinks": nrm(ks[19], (N_B_LAYERS, ATT_QH), 0.5),
        "att_w_o": nrm(ks[20], (N_B_LAYERS, ATT_QH * ATT_HD, D_MODEL), (ATT_QH * ATT_HD) ** -0.5),
        "mlp_norm_g": 1.0 + nrm(ks[21], (DEPTH, D_MODEL), 0.02),
        "mlp_w1": nrm(ks[22], (DEPTH, D_MODEL, D_FF), D_MODEL ** -0.5),
        "mlp_w2": nrm(ks[23], (DEPTH, D_FF, D_MODEL), D_FF ** -0.5),
    }


def reference(x_prompt, x_sample, state_mlstm_C, state_mlstm_n, state_mlstm_m, cache_win_k, cache_win_v,
              ml_norm_g, ml_w_in, ml_b_i, ml_b_f, ml_head_g, ml_w_out,
              kv_norm_g, w_kv, k_norm_g,
              att_norm_g, att_w_q, q_norm_g, att_sinks, att_w_o,
              mlp_norm_g, mlp_w1, mlp_w2):
    B, L, _ = x_prompt.shape
    C0 = jnp.zeros((N_A_LAYERS, B, ML_HEADS, ML_DK, ML_DV), jnp.float32)
    n0 = jnp.zeros((N_A_LAYERS, B, ML_HEADS, ML_DK), jnp.float32)
    m0 = jnp.zeros((N_A_LAYERS, B, ML_HEADS), jnp.float32)
    prompt_chunk = ML_CHUNK if L % ML_CHUNK == 0 else L
    y_prompt, p_C, p_n, p_m, p_wk, p_wv = run_trunk(
        x_prompt, C0, n0, m0, None, None, prompt_chunk,
        ml_norm_g, ml_w_in, ml_b_i, ml_b_f, ml_head_g, ml_w_out,
        kv_norm_g, w_kv, k_norm_g, att_norm_g, att_w_q, q_norm_g, att_sinks, att_w_o,
        mlp_norm_g, mlp_w1, mlp_w2)
    y_sample, s_C, s_n, s_m, s_wk, s_wv = run_trunk(
        x_sample, state_mlstm_C, state_mlstm_n, state_mlstm_m, cache_win_k, cache_win_v, x_sample.shape[1],
        ml_norm_g, ml_w_in, ml_b_i, ml_b_f, ml_head_g, ml_w_out,
        kv_norm_g, w_kv, k_norm_g, att_norm_g, att_w_q, q_norm_g, att_sinks, att_w_o,
        mlp_norm_g, mlp_w1, mlp_w2)
    return (y_prompt, y_sample, p_C, p_n, p_m, p_wk, p_wv, s_C, s_n, s_m, s_wk, s_wv)
```

```python
import functools

import jax
import jax.numpy as jnp
from jax import lax
from jax.experimental import pallas as pl
from jax.experimental.pallas import tpu as pltpu

F32 = jnp.float32
BF16 = jnp.bfloat16

D_MODEL = 1024
ML_HEADS = 4
ML_DK = 128
ML_DV = 256
ML_QK = ML_HEADS * ML_DK
ML_VO = ML_HEADS * ML_DV
ML_QKVO = 2 * ML_QK + 2 * ML_VO
GATE_SOFTCAP = 15.0
ATT_HD = 64
ATT_QH = 16
ATT_KVH = 4
ATT_GROUP = 4
ATT_KV = ATT_KVH * ATT_HD
WINDOW = 128
D_FF = 4 * D_MODEL
EPS = 1e-6

LANES = 128
SUBLANES = 8
VMEM_LIMIT_CAP = 56 * 1024 * 1024

PROMPT_CHUNK = 256
SAMPLE_PAD = SUBLANES
SAMPLE_NB = 16
ROW_TILE = 512
FF_TILE = 1024


def _vmem_limit(*block_bytes):
    need = 4 * sum(block_bytes) + (8 << 20)
    return int(min(max(need, 32 << 20), VMEM_LIMIT_CAP))


def _nbytes(shape, dtype):
    n = 1
    for s in shape:
        n *= s
    return n * jnp.dtype(dtype).itemsize


def _rms(x, g):
    return x * lax.rsqrt(jnp.mean(x * x, axis=-1, keepdims=True) + EPS) * g


def _dot(a, b):
    return jnp.dot(a, b, preferred_element_type=F32)


def _split3(x):
    hi = x.astype(BF16)
    r1 = x - hi.astype(F32)
    mid = r1.astype(BF16)
    lo = (r1 - mid.astype(F32)).astype(BF16)
    return hi, mid, lo


def _dot_exactish(m01, x):
    hi, mid, lo = _split3(x)
    return _dot(m01, hi) + _dot(m01, mid) + _dot(m01, lo)


def _gate_act(z):
    cap = GATE_SOFTCAP * jnp.tanh(z * (1.0 / GATE_SOFTCAP))
    lsig = jnp.minimum(cap, 0.0) - jnp.log1p(jnp.exp(-jnp.abs(cap)))
    lane = lax.broadcasted_iota(jnp.int32, z.shape, 1)
    return jnp.where(lane < ML_HEADS, cap, lsig)


def _mlstm_head(q, k, v, dmat, li_col, b_col, C, n, m_prev, cdt):
    T = q.shape[0]
    qc, kc, vc = q.astype(cdt), k.astype(cdt), v.astype(cdt)
    qk = lax.dot_general(qc, kc, (((1,), (1,)), ((), ())), preferred_element_type=F32)
    inter = b_col + m_prev
    m_t = jnp.maximum(inter, jnp.max(dmat, axis=1, keepdims=True))
    s = qk * jnp.exp(dmat - m_t)
    a_inter = jnp.exp(inter - m_t)
    num = _dot(s.astype(cdt), vc) + a_inter * _dot(qc, C.astype(cdt))
    den = jnp.sum(s, axis=1, keepdims=True) + a_inter * jnp.sum(q * n, axis=1, keepdims=True)
    hh = num * (1.0 / jnp.maximum(jnp.abs(den), jnp.exp(-m_t)))
    m_new = m_t[T - 1:T, :]
    b_last = b_col[T - 1:T, :]
    decay = jnp.exp(b_last + m_prev - m_new)
    kw = k * jnp.exp(b_last - b_col + li_col - m_new)
    C_new = decay * C + lax.dot_general(kw.astype(cdt), vc, (((0,), (0,)), ((), ())),
                                        preferred_element_type=F32)
    n_new = decay * n + jnp.sum(kw, axis=0, keepdims=True)
    return hh, C_new, n_new, m_new


def _head_out(hh, po, hg):
    hn = hh * lax.rsqrt(jnp.mean(hh * hh, axis=-1, keepdims=True) + EPS) * hg
    return jax.nn.sigmoid(po) * hn


def _head_slices(p, h):
    q = p[:, h * ML_DK:(h + 1) * ML_DK]
    k = p[:, ML_QK + h * ML_DK:ML_QK + (h + 1) * ML_DK] * (ML_DK ** -0.5)
    v = p[:, 2 * ML_QK + h * ML_DV:2 * ML_QK + (h + 1) * ML_DV]
    po = p[:, 2 * ML_QK + ML_VO + h * ML_DV:2 * ML_QK + ML_VO + (h + 1) * ML_DV]
    return q, k, v, po


def _prompt_mixer_kernel(x_ref, g_ref, wqkvo_ref, wg_ref, bg_ref, hg_ref, wout_ref,
                         y_ref, C_ref, n_ref, m_ref, hs_ref):
    T = x_ref.shape[0]

    @pl.when(pl.program_id(1) == 0)
    def _():
        C_ref[...] = jnp.zeros_like(C_ref)
        n_ref[...] = jnp.zeros_like(n_ref)
        m_ref[...] = jnp.zeros_like(m_ref)

    x = x_ref[...]
    xn = _rms(x, g_ref[...]).astype(BF16)
    p = _dot(xn, wqkvo_ref[...])
    G = _gate_act(_dot(xn, wg_ref[...]) + bg_ref[...])

    row = lax.broadcasted_iota(jnp.int32, (T, T), 0)
    col = lax.broadcasted_iota(jnp.int32, (T, T), 1)
    causal = col <= row
    tril = jnp.where(causal, 1.0, 0.0).astype(BF16)
    Bc = _dot_exactish(tril, G)
    Gt = G.T
    Bt = Bc.T

    for h in range(ML_HEADS):
        q, k, v, po = _head_slices(p, h)
        li_col = G[:, h:h + 1]
        b_col = Bc[:, ML_HEADS + h:ML_HEADS + h + 1]
        li_row = Gt[h:h + 1, :]
        b_row = Bt[ML_HEADS + h:ML_HEADS + h + 1, :]
        dmat = jnp.where(causal, b_col - b_row + li_row, -jnp.inf)
        hh, C_new, n_new, m_new = _mlstm_head(
            q, k, v, dmat, li_col, b_col, C_ref[h], n_ref[h:h + 1, :], m_ref[h:h + 1, 0:1], BF16)
        C_ref[h] = C_new
        n_ref[h:h + 1, :] = n_new
        m_ref[h:h + 1, :] = jnp.broadcast_to(m_new, (1, LANES))
        hs_ref[:, h * ML_DV:(h + 1) * ML_DV] = _head_out(
            hh, po, hg_ref[:, h * ML_DV:(h + 1) * ML_DV]).astype(BF16)

    y_ref[...] = x + _dot(hs_ref[...], wout_ref[...])


def _prompt_mixer(x, g, wqkvo, wg, bg, hg, wout):
    B, L, D = x.shape
    T = PROMPT_CHUNK
    assert L % T == 0
    const = lambda b, c: (0, 0)
    vmem = _vmem_limit(2 * _nbytes((T, D), F32), _nbytes(wqkvo.shape, BF16), _nbytes(wout.shape, BF16),
                       _nbytes((ML_HEADS, ML_DK, ML_DV), F32), 2 * _nbytes((T, ML_QKVO), F32))
    return pl.pallas_call(
        _prompt_mixer_kernel,
        grid=(B, L // T),
        in_specs=[
            pl.BlockSpec((None, T, D), lambda b, c: (b, c, 0)),
            pl.BlockSpec((1, D), const),
            pl.BlockSpec((D, ML_QKVO), const),
            pl.BlockSpec((D, LANES), const),
            pl.BlockSpec((1, LANES), const),
            pl.BlockSpec((1, ML_VO), const),
            pl.BlockSpec((ML_VO, D), const),
        ],
        out_specs=[
            pl.BlockSpec((None, T, D), lambda b, c: (b, c, 0)),
            pl.BlockSpec((None, ML_HEADS, ML_DK, ML_DV), lambda b, c: (b, 0, 0, 0)),
            pl.BlockSpec((None, ML_HEADS, ML_DK), lambda b, c: (b, 0, 0)),
            pl.BlockSpec((None, ML_HEADS, LANES), lambda b, c: (b, 0, 0)),
        ],
        out_shape=[
            jax.ShapeDtypeStruct((B, L, D), F32),
            jax.ShapeDtypeStruct((B, ML_HEADS, ML_DK, ML_DV), F32),
            jax.ShapeDtypeStruct((B, ML_HEADS, ML_DK), F32),
            jax.ShapeDtypeStruct((B, ML_HEADS, LANES), F32),
        ],
        scratch_shapes=[pltpu.VMEM((T, ML_VO), BF16)],
        compiler_params=pltpu.CompilerParams(
            dimension_semantics=("arbitrary", "arbitrary"), vmem_limit_bytes=vmem),
        name="prompt_mlstm_mixer",
    )(x, g, wqkvo, wg, bg, hg, wout)


def _sample_mixer_kernel(x_ref, g_ref, wqkvo_ref, wg_ref, bg_ref, hg_ref, wout_ref,
                         C0_ref, n0_ref, m0_ref,
                         y_ref, C_ref, n_ref, m_ref, p_s, G_s, hs_s, *, n_valid):
    T = SAMPLE_PAD
    nb = C0_ref.shape[0]
    x = x_ref[...]
    xn = _rms(x, g_ref[...]).astype(BF16)
    p_s[...] = _dot(xn, wqkvo_ref[...])
    G_s[...] = _gate_act(_dot(xn, wg_ref[...]) + bg_ref[...])

    trow = lax.broadcasted_iota(jnp.int32, (T, LANES), 0)
    lane = lax.broadcasted_iota(jnp.int32, (T, LANES), 1)
    pad_gate = jnp.where(lane < ML_HEADS, -jnp.inf, 0.0)
    row8 = lax.broadcasted_iota(jnp.int32, (T, T), 0)
    col8 = lax.broadcasted_iota(jnp.int32, (T, T), 1)
    hg = hg_ref[...]

    def body(j, carry):
        r0 = pl.multiple_of(j * T, T)
        p = p_s[pl.ds(r0, T), :]
        G = jnp.where(trow < n_valid, G_s[pl.ds(r0, T), :], pad_gate)
        Bc = jnp.zeros((T, LANES), F32)
        for s in range(n_valid):
            Bc = Bc + jnp.where(trow >= s, G[s:s + 1, :], 0.0)
        for h in range(ML_HEADS):
            q, k, v, po = _head_slices(p, h)
            li_col = G[:, h:h + 1]
            b_col = Bc[:, ML_HEADS + h:ML_HEADS + h + 1]
            dmat = jnp.full((T, T), -jnp.inf, F32)
            for s in range(n_valid):
                d_s = b_col - (b_col[s:s + 1, :] - li_col[s:s + 1, :])
                dmat = jnp.where((col8 == s) & (row8 >= s), d_s, dmat)
            hh, C_new, n_new, m_new = _mlstm_head(
                q, k, v, dmat, li_col, b_col, C0_ref[j, h], n0_ref[j, h:h + 1, :],
                m0_ref[j, h:h + 1, 0:1], F32)
            C_ref[j, h] = C_new
            n_ref[j, h:h + 1, :] = n_new
            m_ref[j, h:h + 1, :] = jnp.broadcast_to(m_new, (1, LANES))
            hs_s[pl.ds(r0, T), h * ML_DV:(h + 1) * ML_DV] = _head_out(
                hh, po, hg[:, h * ML_DV:(h + 1) * ML_DV])
        return carry

    lax.fori_loop(0, nb, body, 0)
    y_ref[...] = x + _dot(hs_s[...].astype(BF16), wout_ref[...])


def _sample_mixer(x, g, wqkvo, wg, bg, hg, wout, C0, n0, m0, n_valid):
    NBT, D = x.shape
    nseq = C0.shape[0]
    nb = SAMPLE_NB
    R = nb * SAMPLE_PAD
    assert nseq % nb == 0 and NBT == nseq * SAMPLE_PAD
    const = lambda i: (0, 0)
    state_specs = [
        pl.BlockSpec((nb, ML_HEADS, ML_DK, ML_DV), lambda i: (i, 0, 0, 0)),
        pl.BlockSpec((nb, ML_HEADS, ML_DK), lambda i: (i, 0, 0)),
        pl.BlockSpec((nb, ML_HEADS, LANES), lambda i: (i, 0, 0)),
    ]
    vmem = _vmem_limit(2 * _nbytes((R, D), F32), _nbytes(wqkvo.shape, BF16), _nbytes(wout.shape, BF16),
                       2 * _nbytes((nb, ML_HEADS, ML_DK, ML_DV), F32), _nbytes((R, ML_QKVO), F32))
    return pl.pallas_call(
        functools.partial(_sample_mixer_kernel, n_valid=n_valid),
        grid=(nseq // nb,),
        in_specs=[
            pl.BlockSpec((R, D), lambda i: (i, 0)),
            pl.BlockSpec((1, D), const),
            pl.BlockSpec((D, ML_QKVO), const),
            pl.BlockSpec((D, LANES), const),
            pl.BlockSpec((1, LANES), const),
            pl.BlockSpec((1, ML_VO), const),
            pl.BlockSpec((ML_VO, D), const),
        ] + state_specs,
        out_specs=[pl.BlockSpec((R, D), lambda i: (i, 0))] + state_specs,
        out_shape=[
            jax.ShapeDtypeStruct((NBT, D), F32),
            jax.ShapeDtypeStruct(C0.shape, F32),
            jax.ShapeDtypeStruct(n0.shape, F32),
            jax.ShapeDtypeStruct(m0.shape, F32),
        ],
        scratch_shapes=[pltpu.VMEM((R, ML_QKVO), F32), pltpu.VMEM((R, LANES), F32),
                        pltpu.VMEM((R, ML_VO), F32)],
        compiler_params=pltpu.CompilerParams(
            dimension_semantics=("arbitrary",), vmem_limit_bytes=vmem),
        name="sample_mlstm_mixer",
    )(x, g, wqkvo, wg, bg, hg, wout, C0, n0, m0)


def _mlp_kernel(x_ref, g_ref, w1_ref, w2_ref, y_ref):
    x = x_ref[...]
    xn = _rms(x, g_ref[...]).astype(BF16)
    acc = x
    for c in range(D_FF // FF_TILE):
        hcol = _dot(xn, w1_ref[:, c * FF_TILE:(c + 1) * FF_TILE])
        hcol = jnp.square(jnp.maximum(hcol, 0.0)).astype(BF16)
        acc = acc + _dot(hcol, w2_ref[c * FF_TILE:(c + 1) * FF_TILE, :])
    y_ref[...] = acc


def _mlp(x, g, w1, w2):
    N, D = x.shape
    tm = min(ROW_TILE, N)
    assert N % tm == 0
    const = lambda i: (0, 0)
    vmem = _vmem_limit(2 * _nbytes((tm, D), F32), _nbytes(w1.shape, BF16), _nbytes(w2.shape, BF16),
                       _nbytes((tm, FF_TILE), F32))
    return pl.pallas_call(
        _mlp_kernel,
        grid=(N // tm,),
        in_specs=[
            pl.BlockSpec((tm, D), lambda i: (i, 0)),
            pl.BlockSpec((1, D), const),
            pl.BlockSpec((D, D_FF), const, pipeline_mode=pl.Buffered(1)),
            pl.BlockSpec((D_FF, D), const, pipeline_mode=pl.Buffered(1)),
        ],
        out_specs=pl.BlockSpec((tm, D), lambda i: (i, 0)),
        out_shape=jax.ShapeDtypeStruct((N, D), F32),
        compiler_params=pltpu.CompilerParams(
            dimension_semantics=("arbitrary",), vmem_limit_bytes=vmem),
        name="sqrelu_mlp",
    )(x, g, w1, w2)


def _kv_kernel(x_ref, g_ref, wkv_ref, kg_ref, k_ref, v_ref):
    xn = _rms(x_ref[...], g_ref[...]).astype(BF16)
    kv = _dot(xn, wkv_ref[...])
    kraw = kv[:, :ATT_KV]
    v_ref[...] = kv[:, ATT_KV:]
    r = lax.broadcasted_iota(jnp.int32, (ATT_KV, ATT_KV), 0) // ATT_HD
    c = lax.broadcasted_iota(jnp.int32, (ATT_KV, ATT_KV), 1) // ATT_HD
    seg = jnp.where(r == c, 1.0, 0.0).astype(BF16)
    ss = _dot_exactish_right(kraw * kraw, seg)
    k_ref[...] = kraw * lax.rsqrt(ss * (1.0 / ATT_HD) + EPS) * kg_ref[...]


def _dot_exactish_right(x, m01):
    hi, mid, lo = _split3(x)
    return _dot(hi, m01) + _dot(mid, m01) + _dot(lo, m01)


def _shared_kv(x, g, wkv, kg):
    N, D = x.shape
    tm = min(ROW_TILE, N)
    assert N % tm == 0
    const = lambda i: (0, 0)
    vmem = _vmem_limit(_nbytes((tm, D), F32), _nbytes(wkv.shape, BF16), 2 * _nbytes((tm, ATT_KV), F32))
    return pl.pallas_call(
        _kv_kernel,
        grid=(N // tm,),
        in_specs=[
            pl.BlockSpec((tm, D), lambda i: (i, 0)),
            pl.BlockSpec((1, D), const),
            pl.BlockSpec((D, 2 * ATT_KV), const),
            pl.BlockSpec((1, ATT_KV), const),
        ],
        out_specs=[pl.BlockSpec((tm, ATT_KV), lambda i: (i, 0)),
                   pl.BlockSpec((tm, ATT_KV), lambda i: (i, 0))],
        out_shape=[jax.ShapeDtypeStruct((N, ATT_KV), F32), jax.ShapeDtypeStruct((N, ATT_KV), F32)],
        compiler_params=pltpu.CompilerParams(
            dimension_semantics=("arbitrary",), vmem_limit_bytes=vmem),
        name="shared_kv",
    )(x, g, wkv, kg)


def _attend(qraw, kk, vv, mask, qg, sinks_ref, store, cdt):
    TQ = qraw.shape[0]
    NK = kk.shape[0]
    lo_q = lax.broadcasted_iota(jnp.int32, (TQ, LANES), 1) < ATT_HD
    lo_k = lax.broadcasted_iota(jnp.int32, (NK, LANES), 1) < ATT_HD
    for pr in range(ATT_KVH // 2):
        ks = kk[:, pr * LANES:(pr + 1) * LANES]
        vs = vv[:, pr * LANES:(pr + 1) * LANES]
        ks_sw = pltpu.roll(ks, ATT_HD, 1)
        vs_sw = pltpu.roll(vs, ATT_HD, 1)
        for e in range(2):
            kvh = 2 * pr + e
            if e == 0:
                kdup = jnp.where(lo_k, ks, ks_sw)
                vdup = jnp.where(lo_k, vs, vs_sw)
            else:
                kdup = jnp.where(lo_k, ks_sw, ks)
                vdup = jnp.where(lo_k, vs_sw, vs)
            kdup = kdup.astype(cdt)
            vdup = vdup.astype(cdt)
            qs = []
            for cc in range(2):
                q2 = qraw[:, kvh * 2 * LANES + cc * LANES:kvh * 2 * LANES + (cc + 1) * LANES]
                for half in range(2):
                    qm = jnp.where(lo_q, q2, 0.0) if half == 0 else jnp.where(lo_q, 0.0, q2)
                    ss = jnp.sum(qm * qm, axis=1, keepdims=True)
                    qs.append(qm * lax.rsqrt(ss * (1.0 / ATT_HD) + EPS) * qg * (ATT_HD ** -0.5))
            Q = jnp.concatenate(qs, axis=0).astype(cdt)
            S = lax.dot_general(Q, kdup, (((1,), (1,)), ((), ())), preferred_element_type=F32)
            ps, rden = [], []
            for r in range(ATT_GROUP):
                Sr = jnp.where(mask, S[r * TQ:(r + 1) * TQ, :], -jnp.inf)
                sink = sinks_ref[kvh * ATT_GROUP + r]
                M = jnp.maximum(jnp.max(Sr, axis=1, keepdims=True), sink)
                P = jnp.exp(Sr - M)
                rden.append(1.0 / (jnp.sum(P, axis=1, keepdims=True) + jnp.exp(sink - M)))
                ps.append(P)
            O = _dot(jnp.concatenate(ps, axis=0).astype(cdt), vdup)
            for cc in range(2):
                o_lo = O[(2 * cc) * TQ:(2 * cc + 1) * TQ, :] * rden[2 * cc]
                o_hi = O[(2 * cc + 1) * TQ:(2 * cc + 2) * TQ, :] * rden[2 * cc + 1]
                store(kvh * 2 * LANES + cc * LANES, jnp.where(lo_q, o_lo, o_hi))


def _window_mask(TQ, NK):
    qi = lax.broadcasted_iota(jnp.int32, (TQ, NK), 0)
    ki = lax.broadcasted_iota(jnp.int32, (TQ, NK), 1)
    rel = qi + WINDOW - ki
    return (rel >= 0) & (rel <= WINDOW), ki


def _prompt_attn_kernel(sinks_ref, x_ref, g_ref, wq_ref, qg_ref, kp_ref, kc_ref, vp_ref, vc_ref, wo_ref,
                        y_ref, o_s):
    TQ = x_ref.shape[0]
    x = x_ref[...]
    q = _dot(_rms(x, g_ref[...]).astype(BF16), wq_ref[...])
    kk = jnp.concatenate([kp_ref[...], kc_ref[...]], axis=0)
    vv = jnp.concatenate([vp_ref[...], vc_ref[...]], axis=0)
    mask, ki = _window_mask(TQ, WINDOW + TQ)
    mask = mask & ((ki >= WINDOW) | (pl.program_id(1) > 0))

    def store(off, val):
        o_s[:, off:off + LANES] = val.astype(BF16)

    _attend(q, kk, vv, mask, qg_ref[...], sinks_ref, store, BF16)
    y_ref[...] = x + _dot(o_s[...], wo_ref[...])


def _prompt_attn(x, k, v, sinks, g, wq, qg, wo):
    B, L, D = x.shape
    TQ = WINDOW
    assert L % TQ == 0
    const = lambda b, i: (0, 0)
    cur = lambda b, i: (b, i, 0)
    prev = lambda b, i: (b, jnp.maximum(i - 1, 0), 0)
    vmem = _vmem_limit(2 * _nbytes((TQ, D), F32), _nbytes(wq.shape, BF16), _nbytes(wo.shape, BF16),
                       4 * _nbytes((TQ, ATT_KV), F32), 4 * _nbytes((4 * TQ, 2 * TQ), F32))
    return pl.pallas_call(
        _prompt_attn_kernel,
        grid=(B, L // TQ),
        in_specs=[
            pl.BlockSpec(memory_space=pltpu.SMEM),
            pl.BlockSpec((None, TQ, D), cur),
            pl.BlockSpec((1, D), const),
            pl.BlockSpec((D, D), const),
            pl.BlockSpec((1, LANES), const),
            pl.BlockSpec((None, TQ, ATT_KV), prev),
            pl.BlockSpec((None, TQ, ATT_KV), cur),
            pl.BlockSpec((None, TQ, ATT_KV), prev),
            pl.BlockSpec((None, TQ, ATT_KV), cur),
            pl.BlockSpec((D, D), const),
        ],
        out_specs=pl.BlockSpec((None, TQ, D), cur),
        out_shape=jax.ShapeDtypeStruct((B, L, D), F32),
        scratch_shapes=[pltpu.VMEM((TQ, D), BF16)],
        compiler_params=pltpu.CompilerParams(
            dimension_semantics=("arbitrary", "arbitrary"), vmem_limit_bytes=vmem),
        name="prompt_window_attention",
    )(sinks, x, g, wq, qg, k, k, v, v, wo)


def _sample_attn_kernel(sinks_ref, x_ref, g_ref, wq_ref, qg_ref, kc_ref, vc_ref, wk_ref, wv_ref, wo_ref,
                        y_ref, nk_ref, nv_ref, q_s, o_s, *, n_valid):
    T = SAMPLE_PAD
    nb = wk_ref.shape[0]
    x = x_ref[...]
    q_s[...] = _dot(_rms(x, g_ref[...]).astype(BF16), wq_ref[...])
    mask, _ = _window_mask(T, WINDOW + T)
    qg = qg_ref[...]

    def body(j, carry):
        r0 = pl.multiple_of(j * T, T)
        knew = kc_ref[pl.ds(r0, T), :]
        vnew = vc_ref[pl.ds(r0, T), :]
        kk = jnp.concatenate([wk_ref[j], knew], axis=0)
        vv = jnp.concatenate([wv_ref[j], vnew], axis=0)

        def store(off, val):
            o_s[pl.ds(r0, T), off:off + LANES] = val

        _attend(q_s[pl.ds(r0, T), :], kk, vv, mask, qg, sinks_ref, store, F32)
        nk_ref[j, 0:WINDOW - n_valid, :] = wk_ref[j, n_valid:WINDOW, :]
        nk_ref[j, WINDOW - n_valid:WINDOW, :] = knew[0:n_valid, :]
        nv_ref[j, 0:WINDOW - n_valid, :] = wv_ref[j, n_valid:WINDOW, :]
        nv_ref[j, WINDOW - n_valid:WINDOW, :] = vnew[0:n_valid, :]
        return carry

    lax.fori_loop(0, nb, body, 0)
    y_ref[...] = x + _dot(o_s[...].astype(BF16), wo_ref[...])


def _sample_attn(x, k, v, win_k, win_v, sinks, g, wq, qg, wo, n_valid):
    NBT, D = x.shape
    nseq = win_k.shape[0]
    nb = SAMPLE_NB
    R = nb * SAMPLE_PAD
    assert nseq % nb == 0 and NBT == nseq * SAMPLE_PAD
    const = lambda i: (0, 0)
    rows = lambda i: (i, 0)
    cache = pl.BlockSpec((nb, WINDOW, ATT_KV), lambda i: (i, 0, 0))
    vmem = _vmem_limit(2 * _nbytes((R, D), F32), _nbytes(wq.shape, BF16), _nbytes(wo.shape, BF16),
                       4 * _nbytes((nb, WINDOW, ATT_KV), F32), 2 * _nbytes((R, D), F32))
    return pl.pallas_call(
        functools.partial(_sample_attn_kernel, n_valid=n_valid),
        grid=(nseq // nb,),
        in_specs=[
            pl.BlockSpec(memory_space=pltpu.SMEM),
            pl.BlockSpec((R, D), rows),
            pl.BlockSpec((1, D), const),
            pl.BlockSpec((D, D), const),
            pl.BlockSpec((1, LANES), const),
            pl.BlockSpec((R, ATT_KV), rows),
            pl.BlockSpec((R, ATT_KV), rows),
            cache, cache,
            pl.BlockSpec((D, D), const),
        ],
        out_specs=[pl.BlockSpec((R, D), rows), cache, cache],
        out_shape=[jax.ShapeDtypeStruct((NBT, D), F32),
                   jax.ShapeDtypeStruct(win_k.shape, F32), jax.ShapeDtypeStruct(win_v.shape, F32)],
        scratch_shapes=[pltpu.VMEM((R, D), F32), pltpu.VMEM((R, D), F32)],
        compiler_params=pltpu.CompilerParams(
            dimension_semantics=("arbitrary",), vmem_limit_bytes=vmem),
        name="sample_window_attention",
    )(sinks, x, g, wq, qg, k, v, win_k, win_v, wo)


def kernel(x_prompt, x_sample, state_mlstm_C, state_mlstm_n, state_mlstm_m, cache_win_k, cache_win_v,
           ml_norm_g, ml_w_in, ml_b_i, ml_b_f, ml_head_g, ml_w_out, kv_norm_g, w_kv, k_norm_g,
           att_norm_g, att_w_q, q_norm_g, att_sinks, att_w_o, mlp_norm_g, mlp_w1, mlp_w2):
    B, L, D = x_prompt.shape
    NS, LS, _ = x_sample.shape
    assert ml_w_in.shape[0] == 1 and att_w_q.shape[0] == 1 and mlp_w1.shape[0] == 2

    w_in = ml_w_in[0]
    wqkvo = w_in[:, :ML_QKVO].astype(BF16)
    wg = jnp.pad(w_in[:, ML_QKVO:], ((0, 0), (0, LANES - 2 * ML_HEADS))).astype(BF16)
    bg = jnp.pad(jnp.concatenate([ml_b_i[0], ml_b_f[0]]), (0, LANES - 2 * ML_HEADS)).reshape(1, LANES)
    ml_g = ml_norm_g[0].reshape(1, D)
    hg = ml_head_g[0].reshape(1, ML_VO)
    wout = ml_w_out[0].astype(BF16)
    w1 = mlp_w1.astype(BF16)
    w2 = mlp_w2.astype(BF16)
    mlp_g = mlp_norm_g.reshape(2, 1, D)
    wkv = w_kv.astype(BF16)
    kv_g = kv_norm_g.reshape(1, D)
    kg = jnp.tile(k_norm_g, ATT_KVH).reshape(1, ATT_KV)
    att_g = att_norm_g[0].reshape(1, D)
    wq = att_w_q[0].astype(BF16)
    qg = jnp.tile(q_norm_g[0], 2).reshape(1, LANES)
    sinks = att_sinks[0]
    wo = att_w_o[0].astype(BF16)

    xp, p_C, p_n, p_m = _prompt_mixer(x_prompt, ml_g, wqkvo, wg, bg, hg, wout)
    xp = _mlp(xp.reshape(B * L, D), mlp_g[0], w1[0], w2[0])
    kp, vp = _shared_kv(xp, kv_g, wkv, kg)
    xp = _prompt_attn(xp.reshape(B, L, D), kp.reshape(B, L, ATT_KV), vp.reshape(B, L, ATT_KV),
                      sinks, att_g, wq, qg, wo)
    y_prompt = _mlp(xp.reshape(B * L, D), mlp_g[1], w1[1], w2[1]).reshape(B, L, D)
    p_wk = kp.reshape(B, L, ATT_KVH, ATT_HD)[:, L - WINDOW:]
    p_wv = vp.reshape(B, L, ATT_KVH, ATT_HD)[:, L - WINDOW:]

    xs = jnp.pad(x_sample, ((0, 0), (0, SAMPLE_PAD - LS), (0, 0))).reshape(NS * SAMPLE_PAD, D)
    m0 = jnp.broadcast_to(state_mlstm_m[0][:, :, None], (NS, ML_HEADS, LANES))
    xs, s_C, s_n, s_m = _sample_mixer(xs, ml_g, wqkvo, wg, bg, hg, wout,
                                      state_mlstm_C[0], state_mlstm_n[0], m0, LS)
    xs = _mlp(xs, mlp_g[0], w1[0], w2[0])
    ks, vs = _shared_kv(xs, kv_g, wkv, kg)
    xs, s_wk, s_wv = _sample_attn(xs, ks, vs, cache_win_k.reshape(NS, WINDOW, ATT_KV),
                                  cache_win_v.reshape(NS, WINDOW, ATT_KV), sinks, att_g, wq, qg, wo, LS)
    y_sample = _mlp(xs, mlp_g[1], w1[1], w2[1]).reshape(NS, SAMPLE_PAD, D)[:, :LS]

    return (y_prompt, y_sample,
            p_C[None], p_n[None], p_m[None, :, :, 0], p_wk, p_wv,
            s_C[None], s_n[None], s_m[None, :, :, 0],
            s_wk.reshape(NS, WINDOW, ATT_KVH, ATT_HD), s_wv.reshape(NS, WINDOW, ATT_KVH, ATT_HD))
```

```python
import functools

import jax
import jax.numpy as jnp
from jax import lax
from jax.experimental import pallas as pl
from jax.experimental.pallas import tpu as pltpu

F32 = jnp.float32
BF16 = jnp.bfloat16

D_MODEL = 1024
ML_HEADS = 4
ML_DK = 128
ML_DV = 256
ML_QK = ML_HEADS * ML_DK
ML_VO = ML_HEADS * ML_DV
ML_QKVO = 2 * ML_QK + 2 * ML_VO
GATE_SOFTCAP = 15.0
ATT_HD = 64
ATT_QH = 16
ATT_KVH = 4
ATT_GROUP = 4
ATT_KV = ATT_KVH * ATT_HD
WINDOW = 128
D_FF = 4 * D_MODEL
EPS = 1e-6

LANES = 128
SUBLANES = 8
VMEM_LIMIT_CAP = 56 * 1024 * 1024

PROMPT_CHUNK = 256
SAMPLE_PAD = SUBLANES
SAMPLE_NB = 16
ROW_TILE = 512
FF_TILE = 1024


def _vmem_limit(*block_bytes):
    need = 4 * sum(block_bytes) + (8 << 20)
    return int(min(max(need, 32 << 20), VMEM_LIMIT_CAP))


def _nbytes(shape, dtype):
    n = 1
    for s in shape:
        n *= s
    return n * jnp.dtype(dtype).itemsize


def _rms(x, g):
    return x * lax.rsqrt(jnp.mean(x * x, axis=-1, keepdims=True) + EPS) * g


def _dot(a, b):
    return jnp.dot(a, b, preferred_element_type=F32)


def _split3(x):
    hi = x.astype(BF16)
    r1 = x - hi.astype(F32)
    mid = r1.astype(BF16)
    lo = (r1 - mid.astype(F32)).astype(BF16)
    return hi, mid, lo


def _dot_exactish(m01, x):
    hi, mid, lo = _split3(x)
    return _dot(m01, hi) + _dot(m01, mid) + _dot(m01, lo)


def _gate_act(z):
    cap = GATE_SOFTCAP * jnp.tanh(z * (1.0 / GATE_SOFTCAP))
    lsig = jnp.minimum(cap, 0.0) - jnp.log1p(jnp.exp(-jnp.abs(cap)))
    lane = lax.broadcasted_iota(jnp.int32, z.shape, 1)
    return jnp.where(lane < ML_HEADS, cap, lsig)


def _mlstm_head(q, k, v, dmat, li_col, b_col, C, n, m_prev, cdt):
    T = q.shape[0]
    qc, kc, vc = q.astype(cdt), k.astype(cdt), v.astype(cdt)
    qk = lax.dot_general(qc, kc, (((1,), (1,)), ((), ())), preferred_element_type=F32)
    inter = b_col + m_prev
    m_t = jnp.maximum(inter, jnp.max(dmat, axis=1, keepdims=True))
    s = qk * jnp.exp(dmat - m_t)
    a_inter = jnp.exp(inter - m_t)
    num = _dot(s.astype(cdt), vc) + a_inter * _dot(qc, C.astype(cdt))
    den = jnp.sum(s, axis=1, keepdims=True) + a_inter * jnp.sum(q * n, axis=1, keepdims=True)
    hh = num * (1.0 / jnp.maximum(jnp.abs(den), jnp.exp(-m_t)))
    m_new = m_t[T - 1:T, :]
    b_last = b_col[T - 1:T, :]
    decay = jnp.exp(b_last + m_prev - m_new)
    kw = k * jnp.exp(b_last - b_col + li_col - m_new)
    C_new = decay * C + lax.dot_general(kw.astype(cdt), vc, (((0,), (0,)), ((), ())),
                                        preferred_element_type=F32)
    n_new = decay * n + jnp.sum(kw, axis=0, keepdims=True)
    return hh, C_new, n_new, m_new


def _head_out(hh, po, hg):
    hn = hh * lax.rsqrt(jnp.mean(hh * hh, axis=-1, keepdims=True) + EPS) * hg
    return jax.nn.sigmoid(po) * hn


def _head_slices(p, h):
    q = p[:, h * ML_DK:(h + 1) * ML_DK]
    k = p[:, ML_QK + h * ML_DK:ML_QK + (h + 1) * ML_DK] * (ML_DK ** -0.5)
    v = p[:, 2 * ML_QK + h * ML_DV:2 * ML_QK + (h + 1) * ML_DV]
    po = p[:, 2 * ML_QK + ML_VO + h * ML_DV:2 * ML_QK + ML_VO + (h + 1) * ML_DV]
    return q, k, v, po


def _prompt_mixer_kernel(x_ref, g_ref, wqkvo_ref, wg_ref, bg_ref, hg_ref, wout_ref,
                         y_ref, C_ref, n_ref, m_ref, hs_ref):
    T = x_ref.shape[0]

    @pl.when(pl.program_id(1) == 0)
    def _():
        C_ref[...] = jnp.zeros_like(C_ref)
        n_ref[...] = jnp.zeros_like(n_ref)
        m_ref[...] = jnp.zeros_like(m_ref)

    x = x_ref[...]
    xn = _rms(x, g_ref[...]).astype(BF16)
    p = _dot(xn, wqkvo_ref[...])
    G = _gate_act(_dot(xn, wg_ref[...]) + bg_ref[...])

    row = lax.broadcasted_iota(jnp.int32, (T, T), 0)
    col = lax.broadcasted_iota(jnp.int32, (T, T), 1)
    causal = col <= row
    tril = jnp.where(causal, 1.0, 0.0).astype(BF16)
    Bc = _dot_exactish(tril, G)
    Gt = G.T
    Bt = Bc.T

    for h in range(ML_HEADS):
        q, k, v, po = _head_slices(p, h)
        li_col = G[:, h:h + 1]
        b_col = Bc[:, ML_HEADS + h:ML_HEADS + h + 1]
        li_row = Gt[h:h + 1, :]
        b_row = Bt[ML_HEADS + h:ML_HEADS + h + 1, :]
        dmat = jnp.where(causal, b_col - b_row + li_row, -jnp.inf)
        hh, C_new, n_new, m_new = _mlstm_head(
            q, k, v, dmat, li_col, b_col, C_ref[h], n_ref[h:h + 1, :], m_ref[h:h + 1, 0:1], BF16)
        C_ref[h] = C_new
        n_ref[h:h + 1, :] = n_new
        m_ref[h:h + 1, :] = jnp.broadcast_to(m_new, (1, LANES))
        hs_ref[:, h * ML_DV:(h + 1) * ML_DV] = _head_out(
            hh, po, hg_ref[:, h * ML_DV:(h + 1) * ML_DV]).astype(BF16)

    y_ref[...] = x + _dot(hs_ref[...], wout_ref[...])


def _prompt_mixer(x, g, wqkvo, wg, bg, hg, wout):
    B, L, D = x.shape
    T = PROMPT_CHUNK
    assert L % T == 0
    const = lambda b, c: (0, 0)
    vmem = _vmem_limit(2 * _nbytes((T, D), F32), _nbytes(wqkvo.shape, BF16), _nbytes(wout.shape, BF16),
                       _nbytes((ML_HEADS, ML_DK, ML_DV), F32), 2 * _nbytes((T, ML_QKVO), F32))
    return pl.pallas_call(
        _prompt_mixer_kernel,
        grid=(B, L // T),
        in_specs=[
            pl.BlockSpec((None, T, D), lambda b, c: (b, c, 0)),
            pl.BlockSpec((1, D), const),
            pl.BlockSpec((D, ML_QKVO), const),
            pl.BlockSpec((D, LANES), const),
            pl.BlockSpec((1, LANES), const),
            pl.BlockSpec((1, ML_VO), const),
            pl.BlockSpec((ML_VO, D), const),
        ],
        out_specs=[
            pl.BlockSpec((None, T, D), lambda b, c: (b, c, 0)),
            pl.BlockSpec((None, ML_HEADS, ML_DK, ML_DV), lambda b, c: (b, 0, 0, 0)),
            pl.BlockSpec((None, ML_HEADS, ML_DK), lambda b, c: (b, 0, 0)),
            pl.BlockSpec((None, ML_HEADS, LANES), lambda b, c: (b, 0, 0)),
        ],
        out_shape=[
            jax.ShapeDtypeStruct((B, L, D), F32),
            jax.ShapeDtypeStruct((B, ML_HEADS, ML_DK, ML_DV), F32),
            jax.ShapeDtypeStruct((B, ML_HEADS, ML_DK), F32),
            jax.ShapeDtypeStruct((B, ML_HEADS, LANES), F32),
        ],
        scratch_shapes=[pltpu.VMEM((T, ML_VO), BF16)],
        compiler_params=pltpu.CompilerParams(
            dimension_semantics=("arbitrary", "arbitrary"), vmem_limit_bytes=vmem),
        name="prompt_mlstm_mixer",
    )(x, g, wqkvo, wg, bg, hg, wout)


def _sample_mixer_kernel(x_ref, g_ref, wqkvo_ref, wg_ref, bg_ref, hg_ref, wout_ref,
                         C0_ref, n0_ref, m0_ref,
                         y_ref, C_ref, n_ref, m_ref, hs_s, *, n_valid):
    T = SAMPLE_PAD
    nb = C0_ref.shape[0]
    R = nb * T
    x = x_ref[...]
    xn = _rms(x, g_ref[...]).astype(BF16)
    p = _dot(xn, wqkvo_ref[...])

    rowt = lax.broadcasted_iota(jnp.int32, (R, LANES), 0) % T
    lane = lax.broadcasted_iota(jnp.int32, (R, LANES), 1)
    G = jnp.where(rowt < n_valid, _gate_act(_dot(xn, wg_ref[...]) + bg_ref[...]),
                  jnp.where(lane < ML_HEADS, -jnp.inf, 0.0))
    row = lax.broadcasted_iota(jnp.int32, (R, R), 0)
    col = lax.broadcasted_iota(jnp.int32, (R, R), 1)
    causal = (row // T == col // T) & (col <= row)
    Bc = _dot_exactish(jnp.where(causal, 1.0, 0.0).astype(BF16), jnp.where(lane < ML_HEADS, 0.0, G))
    Gt = G.T
    Bt = Bc.T

    def per_seq(fn):
        return jnp.concatenate([fn(a) for a in range(nb)], axis=0)

    def seq_last(colvec):
        return per_seq(lambda a: jnp.broadcast_to(colvec[T * a + T - 1:T * a + T, :], (T, 1)))

    for h in range(ML_HEADS):
        q, k, v, po = _head_slices(p, h)
        li_col = G[:, h:h + 1]
        b_col = Bc[:, ML_HEADS + h:ML_HEADS + h + 1]
        li_row = Gt[h:h + 1, :]
        b_row = Bt[ML_HEADS + h:ML_HEADS + h + 1, :]
        dmat = jnp.where(causal, b_col - (b_row - li_row), -jnp.inf)
        m_prev = per_seq(lambda a: jnp.broadcast_to(m0_ref[a, h:h + 1, 0:1], (T, 1)))
        n_rows = per_seq(lambda a: jnp.broadcast_to(n0_ref[a, h:h + 1, :], (T, ML_DK)))
        qc, kc, vc = q.astype(BF16), k.astype(BF16), v.astype(BF16)
        qk = lax.dot_general(qc, kc, (((1,), (1,)), ((), ())), preferred_element_type=F32)
        inter = b_col + m_prev
        m_t = jnp.maximum(inter, jnp.max(dmat, axis=1, keepdims=True))
        s = qk * jnp.exp(dmat - m_t)
        a_inter = jnp.exp(inter - m_t)
        qC = per_seq(lambda a: _dot(q[T * a:T * a + T, :], C0_ref[a, h]))
        num = _dot(s.astype(BF16), vc) + a_inter * qC
        den = jnp.sum(s, axis=1, keepdims=True) + a_inter * jnp.sum(q * n_rows, axis=1, keepdims=True)
        hh = num * (1.0 / jnp.maximum(jnp.abs(den), jnp.exp(-m_t)))
        hs_s[:, h * ML_DV:(h + 1) * ML_DV] = _head_out(
            hh, po, hg_ref[:, h * ML_DV:(h + 1) * ML_DV]).astype(BF16)

        m_new = seq_last(m_t)
        b_last = seq_last(b_col)
        decay = jnp.exp(b_last + m_prev - m_new)
        kw = k * jnp.exp(b_last - b_col + li_col - m_new)
        for a in range(nb):
            dec = decay[T * a:T * a + 1, :]
            kw_a = kw[T * a:T * a + T, :]
            upd = lax.dot_general(kw_a, v[T * a:T * a + T, :], (((0,), (0,)), ((), ())),
                                  preferred_element_type=F32)
            C_ref[a, h] = dec * C0_ref[a, h] + upd
            n_ref[a, h:h + 1, :] = dec * n0_ref[a, h:h + 1, :] + jnp.sum(kw_a, axis=0, keepdims=True)
            m_ref[a, h:h + 1, :] = jnp.broadcast_to(m_new[T * a:T * a + 1, :], (1, LANES))

    y_ref[...] = x + _dot(hs_s[...], wout_ref[...])


def _sample_mixer(x, g, wqkvo, wg, bg, hg, wout, C0, n0, m0, n_valid):
    NBT, D = x.shape
    nseq = C0.shape[0]
    nb = SAMPLE_NB
    R = nb * SAMPLE_PAD
    assert nseq % nb == 0 and NBT == nseq * SAMPLE_PAD
    const = lambda i: (0, 0)
    state_specs = [
        pl.BlockSpec((nb, ML_HEADS, ML_DK, ML_DV), lambda i: (i, 0, 0, 0)),
        pl.BlockSpec((nb, ML_HEADS, ML_DK), lambda i: (i, 0, 0)),
        pl.BlockSpec((nb, ML_HEADS, LANES), lambda i: (i, 0, 0)),
    ]
    vmem = _vmem_limit(2 * _nbytes((R, D), F32), _nbytes(wqkvo.shape, BF16), _nbytes(wout.shape, BF16),
                       2 * _nbytes((nb, ML_HEADS, ML_DK, ML_DV), F32), _nbytes((R, ML_QKVO), F32))
    return pl.pallas_call(
        functools.partial(_sample_mixer_kernel, n_valid=n_valid),
        grid=(nseq // nb,),
        in_specs=[
            pl.BlockSpec((R, D), lambda i: (i, 0)),
            pl.BlockSpec((1, D), const),
            pl.BlockSpec((D, ML_QKVO), const),
            pl.BlockSpec((D, LANES), const),
            pl.BlockSpec((1, LANES), const),
            pl.BlockSpec((1, ML_VO), const),
            pl.BlockSpec((ML_VO, D), const),
        ] + state_specs,
        out_specs=[pl.BlockSpec((R, D), lambda i: (i, 0))] + state_specs,
        out_shape=[
            jax.ShapeDtypeStruct((NBT, D), F32),
            jax.ShapeDtypeStruct(C0.shape, F32),
            jax.ShapeDtypeStruct(n0.shape, F32),
            jax.ShapeDtypeStruct(m0.shape, F32),
        ],
        scratch_shapes=[pltpu.VMEM((R, ML_VO), BF16)],
        compiler_params=pltpu.CompilerParams(
            dimension_semantics=("arbitrary",), vmem_limit_bytes=vmem),
        name="sample_mlstm_mixer",
    )(x, g, wqkvo, wg, bg, hg, wout, C0, n0, m0)


def _mlp_kernel(x_ref, g_ref, w1_ref, w2_ref, y_ref):
    x = x_ref[...]
    xn = _rms(x, g_ref[...]).astype(BF16)
    acc = x
    for c in range(D_FF // FF_TILE):
        hcol = _dot(xn, w1_ref[:, c * FF_TILE:(c + 1) * FF_TILE])
        hcol = jnp.square(jnp.maximum(hcol, 0.0)).astype(BF16)
        acc = acc + _dot(hcol, w2_ref[c * FF_TILE:(c + 1) * FF_TILE, :])
    y_ref[...] = acc


def _mlp(x, g, w1, w2):
    N, D = x.shape
    tm = min(ROW_TILE, N)
    assert N % tm == 0
    const = lambda i: (0, 0)
    vmem = _vmem_limit(2 * _nbytes((tm, D), F32), _nbytes(w1.shape, BF16), _nbytes(w2.shape, BF16),
                       _nbytes((tm, FF_TILE), F32))
    return pl.pallas_call(
        _mlp_kernel,
        grid=(N // tm,),
        in_specs=[
            pl.BlockSpec((tm, D), lambda i: (i, 0)),
            pl.BlockSpec((1, D), const),
            pl.BlockSpec((D, D_FF), const, pipeline_mode=pl.Buffered(1)),
            pl.BlockSpec((D_FF, D), const, pipeline_mode=pl.Buffered(1)),
        ],
        out_specs=pl.BlockSpec((tm, D), lambda i: (i, 0)),
        out_shape=jax.ShapeDtypeStruct((N, D), F32),
        compiler_params=pltpu.CompilerParams(
            dimension_semantics=("arbitrary",), vmem_limit_bytes=vmem),
        name="sqrelu_mlp",
    )(x, g, w1, w2)


def _kv_kernel(x_ref, g_ref, wkv_ref, kg_ref, k_ref, v_ref):
    xn = _rms(x_ref[...], g_ref[...]).astype(BF16)
    kv = _dot(xn, wkv_ref[...])
    kraw = kv[:, :ATT_KV]
    v_ref[...] = kv[:, ATT_KV:]
    r = lax.broadcasted_iota(jnp.int32, (ATT_KV, ATT_KV), 0) // ATT_HD
    c = lax.broadcasted_iota(jnp.int32, (ATT_KV, ATT_KV), 1) // ATT_HD
    seg = jnp.where(r == c, 1.0, 0.0).astype(BF16)
    ss = _dot_exactish_right(kraw * kraw, seg)
    k_ref[...] = kraw * lax.rsqrt(ss * (1.0 / ATT_HD) + EPS) * kg_ref[...]


def _dot_exactish_right(x, m01):
    hi, mid, lo = _split3(x)
    return _dot(hi, m01) + _dot(mid, m01) + _dot(lo, m01)


def _shared_kv(x, g, wkv, kg):
    N, D = x.shape
    tm = min(ROW_TILE, N)
    assert N % tm == 0
    const = lambda i: (0, 0)
    vmem = _vmem_limit(_nbytes((tm, D), F32), _nbytes(wkv.shape, BF16), 2 * _nbytes((tm, ATT_KV), F32))
    return pl.pallas_call(
        _kv_kernel,
        grid=(N // tm,),
        in_specs=[
            pl.BlockSpec((tm, D), lambda i: (i, 0)),
            pl.BlockSpec((1, D), const),
            pl.BlockSpec((D, 2 * ATT_KV), const),
            pl.BlockSpec((1, ATT_KV), const),
        ],
        out_specs=[pl.BlockSpec((tm, ATT_KV), lambda i: (i, 0)),
                   pl.BlockSpec((tm, ATT_KV), lambda i: (i, 0))],
        out_shape=[jax.ShapeDtypeStruct((N, ATT_KV), F32), jax.ShapeDtypeStruct((N, ATT_KV), F32)],
        compiler_params=pltpu.CompilerParams(
            dimension_semantics=("arbitrary",), vmem_limit_bytes=vmem),
        name="shared_kv",
    )(x, g, wkv, kg)


def _pair_queries(qraw, pr, qg):
    TQ = qraw.shape[0]
    lo = lax.broadcasted_iota(jnp.int32, (TQ, LANES), 1) < ATT_HD
    out = []
    for e in range(2):
        kvh = 2 * pr + e
        for g in range(ATT_GROUP):
            cc, half = divmod(g, 2)
            q2 = qraw[:, (2 * kvh + cc) * LANES:(2 * kvh + cc + 1) * LANES]
            qm = jnp.where(lo, q2, 0.0) if half == 0 else jnp.where(lo, 0.0, q2)
            ss = jnp.sum(qm * qm, axis=1, keepdims=True)
            qn = qm * (lax.rsqrt(ss * (1.0 / ATT_HD) + EPS) * (ATT_HD ** -0.5)) * qg
            out.append(qn if half == e else pltpu.roll(qn, ATT_HD, 1))
    return out


def _pair_outputs(o_heads, rden, pr, store):
    TQ = o_heads[0].shape[0]
    lo = lax.broadcasted_iota(jnp.int32, (TQ, LANES), 1) < ATT_HD
    for e in range(2):
        kvh = 2 * pr + e
        for cc in range(2):
            tiles = []
            for half in range(2):
                o = o_heads[e * ATT_GROUP + 2 * cc + half] * rden[e * ATT_GROUP + 2 * cc + half]
                tiles.append(o if half == e else pltpu.roll(o, ATT_HD, 1))
            store((2 * kvh + cc) * LANES, jnp.where(lo, tiles[0], tiles[1]))


def _softmax_with_sink(parts, sink):
    M = sink
    for s in parts:
        M = jnp.maximum(M, jnp.max(s, axis=1, keepdims=True))
    ps = [jnp.exp(s - M) for s in parts]
    den = jnp.exp(sink - M)
    for p in ps:
        den = den + jnp.sum(p, axis=1, keepdims=True)
    return ps, 1.0 / den


def _window_mask(TQ, NK):
    qi = lax.broadcasted_iota(jnp.int32, (TQ, NK), 0)
    ki = lax.broadcasted_iota(jnp.int32, (TQ, NK), 1)
    rel = qi + WINDOW - ki
    return (rel >= 0) & (rel <= WINDOW), ki


def _prompt_attn_kernel(sinks_ref, x_ref, g_ref, wq_ref, qg_ref, kp_ref, kc_ref, vp_ref, vc_ref, wo_ref,
                        y_ref, o_s):
    TQ = x_ref.shape[0]
    x = x_ref[...]
    q = _dot(_rms(x, g_ref[...]).astype(BF16), wq_ref[...])
    kk = jnp.concatenate([kp_ref[...], kc_ref[...]], axis=0)
    vv = jnp.concatenate([vp_ref[...], vc_ref[...]], axis=0)
    mask, ki = _window_mask(TQ, WINDOW + TQ)
    mask = mask & ((ki >= WINDOW) | (pl.program_id(1) > 0))

    def store(off, val):
        o_s[:, off:off + LANES] = val.astype(BF16)

    for pr in range(ATT_KVH // 2):
        sl = slice(pr * LANES, (pr + 1) * LANES)
        qs = _pair_queries(q, pr, qg_ref[...])
        S = lax.dot_general(jnp.concatenate(qs, axis=0).astype(BF16), kk[:, sl].astype(BF16),
                            (((1,), (1,)), ((), ())), preferred_element_type=F32)
        ps, rden = [], []
        for i in range(2 * ATT_GROUP):
            (p,), r = _softmax_with_sink([jnp.where(mask, S[i * TQ:(i + 1) * TQ, :], -jnp.inf)],
                                         sinks_ref[pr * 2 * ATT_GROUP + i])
            ps.append(p)
            rden.append(r)
        O = _dot(jnp.concatenate(ps, axis=0).astype(BF16), vv[:, sl].astype(BF16))
        _pair_outputs([O[i * TQ:(i + 1) * TQ, :] for i in range(2 * ATT_GROUP)], rden, pr, store)
    y_ref[...] = x + _dot(o_s[...], wo_ref[...])


def _prompt_attn(x, k, v, sinks, g, wq, qg, wo):
    B, L, D = x.shape
    TQ = WINDOW
    assert L % TQ == 0
    const = lambda b, i: (0, 0)
    cur = lambda b, i: (b, i, 0)
    prev = lambda b, i: (b, jnp.maximum(i - 1, 0), 0)
    vmem = _vmem_limit(2 * _nbytes((TQ, D), F32), _nbytes(wq.shape, BF16), _nbytes(wo.shape, BF16),
                       4 * _nbytes((TQ, ATT_KV), F32), 4 * _nbytes((4 * TQ, 2 * TQ), F32))
    return pl.pallas_call(
        _prompt_attn_kernel,
        grid=(B, L // TQ),
        in_specs=[
            pl.BlockSpec(memory_space=pltpu.SMEM),
            pl.BlockSpec((None, TQ, D), cur),
            pl.BlockSpec((1, D), const),
            pl.BlockSpec((D, D), const),
            pl.BlockSpec((1, LANES), const),
            pl.BlockSpec((None, TQ, ATT_KV), prev),
            pl.BlockSpec((None, TQ, ATT_KV), cur),
            pl.BlockSpec((None, TQ, ATT_KV), prev),
            pl.BlockSpec((None, TQ, ATT_KV), cur),
            pl.BlockSpec((D, D), const),
        ],
        out_specs=pl.BlockSpec((None, TQ, D), cur),
        out_shape=jax.ShapeDtypeStruct((B, L, D), F32),
        scratch_shapes=[pltpu.VMEM((TQ, D), BF16)],
        compiler_params=pltpu.CompilerParams(
            dimension_semantics=("arbitrary", "arbitrary"), vmem_limit_bytes=vmem),
        name="prompt_window_attention",
    )(sinks, x, g, wq, qg, k, k, v, v, wo)


def _sample_attn_kernel(sinks_ref, x_ref, g_ref, wq_ref, qg_ref, kc_ref, vc_ref, wk_ref, wv_ref, wo_ref,
                        y_ref, nk_ref, nv_ref, o_s, *, n_valid):
    T = SAMPLE_PAD
    nb = wk_ref.shape[0]
    R = nb * T
    x = x_ref[...]
    q = _dot(_rms(x, g_ref[...]).astype(BF16), wq_ref[...])
    qg = qg_ref[...]
    knew = kc_ref[...]
    vnew = vc_ref[...]

    cmask = (lax.broadcasted_iota(jnp.int32, (R, WINDOW), 1)
             >= lax.broadcasted_iota(jnp.int32, (R, WINDOW), 0) % T)
    row = lax.broadcasted_iota(jnp.int32, (R, R), 0)
    col = lax.broadcasted_iota(jnp.int32, (R, R), 1)
    nmask = (row // T == col // T) & (col % T <= row % T) & (col % T < n_valid)

    def store(off, val):
        o_s[:, off:off + LANES] = val.astype(BF16)

    def regroup(per_seq, i):
        return jnp.concatenate([per_seq[a][i * T:(i + 1) * T, :] for a in range(nb)], axis=0)

    for pr in range(ATT_KVH // 2):
        sl = slice(pr * LANES, (pr + 1) * LANES)
        qs = _pair_queries(q, pr, qg)
        S_new = lax.dot_general(jnp.concatenate(qs, axis=0).astype(BF16), knew[:, sl].astype(BF16),
                                (((1,), (1,)), ((), ())), preferred_element_type=F32)
        sc = []
        for a in range(nb):
            q_a = jnp.concatenate([qi[T * a:T * a + T, :] for qi in qs], axis=0).astype(BF16)
            sc.append(lax.dot_general(q_a, wk_ref[a, :, sl].astype(BF16), (((1,), (1,)), ((), ())),
                                      preferred_element_type=F32))
        pcs, pns, rden = [], [], []
        for i in range(2 * ATT_GROUP):
            s_c = jnp.where(cmask, regroup(sc, i), -jnp.inf)
            s_n = jnp.where(nmask, S_new[i * R:(i + 1) * R, :], -jnp.inf)
            (p_c, p_n), r = _softmax_with_sink([s_c, s_n], sinks_ref[pr * 2 * ATT_GROUP + i])
            pcs.append(p_c)
            pns.append(p_n)
            rden.append(r)
        O_new = _dot(jnp.concatenate(pns, axis=0).astype(BF16), vnew[:, sl].astype(BF16))
        oc = []
        for a in range(nb):
            p_a = jnp.concatenate([pc[T * a:T * a + T, :] for pc in pcs], axis=0).astype(BF16)
            oc.append(_dot(p_a, wv_ref[a, :, sl].astype(BF16)))
        _pair_outputs([O_new[i * R:(i + 1) * R, :] + regroup(oc, i) for i in range(2 * ATT_GROUP)],
                      rden, pr, store)

    for a in range(nb):
        nk_ref[a, 0:WINDOW - n_valid, :] = wk_ref[a, n_valid:WINDOW, :]
        nk_ref[a, WINDOW - n_valid:WINDOW, :] = knew[T * a:T * a + n_valid, :]
        nv_ref[a, 0:WINDOW - n_valid, :] = wv_ref[a, n_valid:WINDOW, :]
        nv_ref[a, WINDOW - n_valid:WINDOW, :] = vnew[T * a:T * a + n_valid, :]

    y_ref[...] = x + _dot(o_s[...], wo_ref[...])


def _sample_attn(x, k, v, win_k, win_v, sinks, g, wq, qg, wo, n_valid):
    NBT, D = x.shape
    nseq = win_k.shape[0]
    nb = SAMPLE_NB
    R = nb * SAMPLE_PAD
    assert nseq % nb == 0 and NBT == nseq * SAMPLE_PAD
    const = lambda i: (0, 0)
    rows = lambda i: (i, 0)
    cache = pl.BlockSpec((nb, WINDOW, ATT_KV), lambda i: (i, 0, 0))
    vmem = _vmem_limit(2 * _nbytes((R, D), F32), _nbytes(wq.shape, BF16), _nbytes(wo.shape, BF16),
                       4 * _nbytes((nb, WINDOW, ATT_KV), F32), 2 * _nbytes((R, D), F32))
    return pl.pallas_call(
        functools.partial(_sample_attn_kernel, n_valid=n_valid),
        grid=(nseq // nb,),
        in_specs=[
            pl.BlockSpec(memory_space=pltpu.SMEM),
            pl.BlockSpec((R, D), rows),
            pl.BlockSpec((1, D), const),
            pl.BlockSpec((D, D), const),
            pl.BlockSpec((1, LANES), const),
            pl.BlockSpec((R, ATT_KV), rows),
            pl.BlockSpec((R, ATT_KV), rows),
            cache, cache,
            pl.BlockSpec((D, D), const),
        ],
        out_specs=[pl.BlockSpec((R, D), rows), cache, cache],
        out_shape=[jax.ShapeDtypeStruct((NBT, D), F32),
                   jax.ShapeDtypeStruct(win_k.shape, F32), jax.ShapeDtypeStruct(win_v.shape, F32)],
        scratch_shapes=[pltpu.VMEM((R, D), BF16)],
        compiler_params=pltpu.CompilerParams(
            dimension_semantics=("arbitrary",), vmem_limit_bytes=vmem),
        name="sample_window_attention",
    )(sinks, x, g, wq, qg, k, v, win_k, win_v, wo)


def kernel(x_prompt, x_sample, state_mlstm_C, state_mlstm_n, state_mlstm_m, cache_win_k, cache_win_v,
           ml_norm_g, ml_w_in, ml_b_i, ml_b_f, ml_head_g, ml_w_out, kv_norm_g, w_kv, k_norm_g,
           att_norm_g, att_w_q, q_norm_g, att_sinks, att_w_o, mlp_norm_g, mlp_w1, mlp_w2):
    B, L, D = x_prompt.shape
    NS, LS, _ = x_sample.shape
    assert ml_w_in.shape[0] == 1 and att_w_q.shape[0] == 1 and mlp_w1.shape[0] == 2

    w_in = ml_w_in[0]
    wqkvo = w_in[:, :ML_QKVO].astype(BF16)
    wg = jnp.pad(w_in[:, ML_QKVO:], ((0, 0), (0, LANES - 2 * ML_HEADS))).astype(BF16)
    bg = jnp.pad(jnp.concatenate([ml_b_i[0], ml_b_f[0]]), (0, LANES - 2 * ML_HEADS)).reshape(1, LANES)
    ml_g = ml_norm_g[0].reshape(1, D)
    hg = ml_head_g[0].reshape(1, ML_VO)
    wout = ml_w_out[0].astype(BF16)
    w1 = mlp_w1.astype(BF16)
    w2 = mlp_w2.astype(BF16)
    mlp_g = mlp_norm_g.reshape(2, 1, D)
    wkv = w_kv.astype(BF16)
    kv_g = kv_norm_g.reshape(1, D)
    kg = jnp.tile(k_norm_g, ATT_KVH).reshape(1, ATT_KV)
    att_g = att_norm_g[0].reshape(1, D)
    wq = att_w_q[0].astype(BF16)
    qg = jnp.tile(q_norm_g[0], 2).reshape(1, LANES)
    sinks = att_sinks[0]
    wo = att_w_o[0].astype(BF16)

    xp, p_C, p_n, p_m = _prompt_mixer(x_prompt, ml_g, wqkvo, wg, bg, hg, wout)
    xp = _mlp(xp.reshape(B * L, D), mlp_g[0], w1[0], w2[0])
    kp, vp = _shared_kv(xp, kv_g, wkv, kg)
    xp = _prompt_attn(xp.reshape(B, L, D), kp.reshape(B, L, ATT_KV), vp.reshape(B, L, ATT_KV),
                      sinks, att_g, wq, qg, wo)
    y_prompt = _mlp(xp.reshape(B * L, D), mlp_g[1], w1[1], w2[1]).reshape(B, L, D)
    p_wk = kp.reshape(B, L, ATT_KVH, ATT_HD)[:, L - WINDOW:]
    p_wv = vp.reshape(B, L, ATT_KVH, ATT_HD)[:, L - WINDOW:]

    xs = jnp.pad(x_sample, ((0, 0), (0, SAMPLE_PAD - LS), (0, 0))).reshape(NS * SAMPLE_PAD, D)
    m0 = jnp.broadcast_to(state_mlstm_m[0][:, :, None], (NS, ML_HEADS, LANES))
    xs, s_C, s_n, s_m = _sample_mixer(xs, ml_g, wqkvo, wg, bg, hg, wout,
                                      state_mlstm_C[0], state_mlstm_n[0], m0, LS)
    xs = _mlp(xs, mlp_g[0], w1[0], w2[0])
    ks, vs = _shared_kv(xs, kv_g, wkv, kg)
    xs, s_wk, s_wv = _sample_attn(xs, ks, vs, cache_win_k.reshape(NS, WINDOW, ATT_KV),
                                  cache_win_v.reshape(NS, WINDOW, ATT_KV), sinks, att_g, wq, qg, wo, LS)
    y_sample = _mlp(xs, mlp_g[1], w1[1], w2[1]).reshape(NS, SAMPLE_PAD, D)[:, :LS]

    return (y_prompt, y_sample,
            p_C[None], p_n[None], p_m[None, :, :, 0], p_wk, p_wv,
            s_C[None], s_n[None], s_m[None, :, :, 0],
            s_wk.reshape(NS, WINDOW, ATT_KVH, ATT_HD), s_wv.reshape(NS, WINDOW, ATT_KVH, ATT_HD))
```

```python
import functools

import jax
import jax.numpy as jnp
from jax import lax
from jax.experimental import pallas as pl
from jax.experimental.pallas import tpu as pltpu

F32 = jnp.float32
BF16 = jnp.bfloat16

D_MODEL = 1024
ML_HEADS = 4
ML_DK = 128
ML_DV = 256
ML_QK = ML_HEADS * ML_DK
ML_VO = ML_HEADS * ML_DV
ML_QKVO = 2 * ML_QK + 2 * ML_VO
GATE_SOFTCAP = 15.0
ATT_HD = 64
ATT_QH = 16
ATT_KVH = 4
ATT_GROUP = 4
ATT_KV = ATT_KVH * ATT_HD
WINDOW = 128
D_FF = 4 * D_MODEL
EPS = 1e-6
LOG2E = 1.4426950408889634

LANES = 128
SUBLANES = 8
VMEM_LIMIT_CAP = 56 * 1024 * 1024

PROMPT_CHUNK = 256
SAMPLE_PAD = SUBLANES
SAMPLE_NB = 16
ATTN_ROWS = 1024
ROW_TILE = 512
FF_TILE = 1024


def _vmem_limit(*block_bytes):
    need = 4 * sum(block_bytes) + (8 << 20)
    return int(min(max(need, 32 << 20), VMEM_LIMIT_CAP))


def _nbytes(shape, dtype):
    n = 1
    for s in shape:
        n *= s
    return n * jnp.dtype(dtype).itemsize


def _rms(x, g):
    return x * lax.rsqrt(jnp.mean(x * x, axis=-1, keepdims=True) + EPS) * g


def _dot(a, b):
    return jnp.dot(a, b, preferred_element_type=F32)


def _split3(x):
    hi = x.astype(BF16)
    r1 = x - hi.astype(F32)
    mid = r1.astype(BF16)
    lo = (r1 - mid.astype(F32)).astype(BF16)
    return hi, mid, lo


def _dot_exactish(m01, x):
    hi, mid, lo = _split3(x)
    return _dot(m01, hi) + _dot(m01, mid) + _dot(m01, lo)


def _gate_act(z):
    cap = GATE_SOFTCAP * jnp.tanh(z * (1.0 / GATE_SOFTCAP))
    lsig = jnp.minimum(cap, 0.0) - jnp.log1p(jnp.exp(-jnp.abs(cap)))
    lane = lax.broadcasted_iota(jnp.int32, z.shape, 1)
    return jnp.where(lane < ML_HEADS, cap, lsig)


def _mlstm_head(q, k, v, dmat, li_col, b_col, C, n, m_prev, cdt):
    T = q.shape[0]
    qc, kc, vc = q.astype(cdt), k.astype(cdt), v.astype(cdt)
    qk = lax.dot_general(qc, kc, (((1,), (1,)), ((), ())), preferred_element_type=F32)
    inter = b_col + m_prev
    m_t = jnp.maximum(inter, jnp.max(dmat, axis=1, keepdims=True))
    s = qk * jnp.exp(dmat - m_t)
    a_inter = jnp.exp(inter - m_t)
    num = _dot(s.astype(cdt), vc) + a_inter * _dot(qc, C.astype(cdt))
    den = jnp.sum(s, axis=1, keepdims=True) + a_inter * jnp.sum(q * n, axis=1, keepdims=True)
    hh = num * (1.0 / jnp.maximum(jnp.abs(den), jnp.exp(-m_t)))
    m_new = m_t[T - 1:T, :]
    b_last = b_col[T - 1:T, :]
    decay = jnp.exp(b_last + m_prev - m_new)
    kw = k * jnp.exp(b_last - b_col + li_col - m_new)
    C_new = decay * C + lax.dot_general(kw.astype(cdt), vc, (((0,), (0,)), ((), ())),
                                        preferred_element_type=F32)
    n_new = decay * n + jnp.sum(kw, axis=0, keepdims=True)
    return hh, C_new, n_new, m_new


def _head_out(hh, po, hg):
    hn = hh * lax.rsqrt(jnp.mean(hh * hh, axis=-1, keepdims=True) + EPS) * hg
    return jax.nn.sigmoid(po) * hn


def _head_slices(p, h):
    q = p[:, h * ML_DK:(h + 1) * ML_DK]
    k = p[:, ML_QK + h * ML_DK:ML_QK + (h + 1) * ML_DK] * (ML_DK ** -0.5)
    v = p[:, 2 * ML_QK + h * ML_DV:2 * ML_QK + (h + 1) * ML_DV]
    po = p[:, 2 * ML_QK + ML_VO + h * ML_DV:2 * ML_QK + ML_VO + (h + 1) * ML_DV]
    return q, k, v, po


def _prompt_mixer_kernel(x_ref, g_ref, wqkvo_ref, wg_ref, bg_ref, hg_ref, wout_ref,
                         y_ref, C_ref, n_ref, m_ref, hs_ref):
    T = x_ref.shape[0]

    @pl.when(pl.program_id(1) == 0)
    def _():
        C_ref[...] = jnp.zeros_like(C_ref)
        n_ref[...] = jnp.zeros_like(n_ref)
        m_ref[...] = jnp.zeros_like(m_ref)

    x = x_ref[...]
    xn = _rms(x, g_ref[...]).astype(BF16)
    p = _dot(xn, wqkvo_ref[...])
    G = _gate_act(_dot(xn, wg_ref[...]) + bg_ref[...])

    row = lax.broadcasted_iota(jnp.int32, (T, T), 0)
    col = lax.broadcasted_iota(jnp.int32, (T, T), 1)
    causal = col <= row
    tril = jnp.where(causal, 1.0, 0.0).astype(BF16)
    Bc = _dot_exactish(tril, G)
    Gt = G.T
    Bt = Bc.T

    for h in range(ML_HEADS):
        q, k, v, po = _head_slices(p, h)
        li_col = G[:, h:h + 1]
        b_col = Bc[:, ML_HEADS + h:ML_HEADS + h + 1]
        li_row = Gt[h:h + 1, :]
        b_row = Bt[ML_HEADS + h:ML_HEADS + h + 1, :]
        dmat = jnp.where(causal, b_col - b_row + li_row, -jnp.inf)
        hh, C_new, n_new, m_new = _mlstm_head(
            q, k, v, dmat, li_col, b_col, C_ref[h], n_ref[h:h + 1, :], m_ref[h:h + 1, 0:1], BF16)
        C_ref[h] = C_new
        n_ref[h:h + 1, :] = n_new
        m_ref[h:h + 1, :] = jnp.broadcast_to(m_new, (1, LANES))
        hs_ref[:, h * ML_DV:(h + 1) * ML_DV] = _head_out(
            hh, po, hg_ref[:, h * ML_DV:(h + 1) * ML_DV]).astype(BF16)

    y_ref[...] = x + _dot(hs_ref[...], wout_ref[...])


def _prompt_mixer(x, g, wqkvo, wg, bg, hg, wout):
    B, L, D = x.shape
    T = PROMPT_CHUNK
    assert L % T == 0
    const = lambda b, c: (0, 0)
    vmem = _vmem_limit(2 * _nbytes((T, D), F32), _nbytes(wqkvo.shape, BF16), _nbytes(wout.shape, BF16),
                       _nbytes((ML_HEADS, ML_DK, ML_DV), F32), 2 * _nbytes((T, ML_QKVO), F32))
    return pl.pallas_call(
        _prompt_mixer_kernel,
        grid=(B, L // T),
        in_specs=[
            pl.BlockSpec((None, T, D), lambda b, c: (b, c, 0)),
            pl.BlockSpec((1, D), const),
            pl.BlockSpec((D, ML_QKVO), const),
            pl.BlockSpec((D, LANES), const),
            pl.BlockSpec((1, LANES), const),
            pl.BlockSpec((1, ML_VO), const),
            pl.BlockSpec((ML_VO, D), const),
        ],
        out_specs=[
            pl.BlockSpec((None, T, D), lambda b, c: (b, c, 0)),
            pl.BlockSpec((None, ML_HEADS, ML_DK, ML_DV), lambda b, c: (b, 0, 0, 0)),
            pl.BlockSpec((None, ML_HEADS, ML_DK), lambda b, c: (b, 0, 0)),
            pl.BlockSpec((None, ML_HEADS, LANES), lambda b, c: (b, 0, 0)),
        ],
        out_shape=[
            jax.ShapeDtypeStruct((B, L, D), F32),
            jax.ShapeDtypeStruct((B, ML_HEADS, ML_DK, ML_DV), F32),
            jax.ShapeDtypeStruct((B, ML_HEADS, ML_DK), F32),
            jax.ShapeDtypeStruct((B, ML_HEADS, LANES), F32),
        ],
        scratch_shapes=[pltpu.VMEM((T, ML_VO), BF16)],
        compiler_params=pltpu.CompilerParams(
            dimension_semantics=("arbitrary", "arbitrary"), vmem_limit_bytes=vmem),
        name="prompt_mlstm_mixer",
    )(x, g, wqkvo, wg, bg, hg, wout)


def _sample_mixer_kernel(x_ref, g_ref, wqkvo_ref, wg_ref, bg_ref, hg_ref, wout_ref,
                         C0_ref, n0_ref, m0_ref,
                         y_ref, C_ref, n_ref, m_ref, hs_s, *, n_valid):
    T = SAMPLE_PAD
    nb = C0_ref.shape[0]
    R = nb * T
    x = x_ref[...]
    xn = _rms(x, g_ref[...]).astype(BF16)
    p = _dot(xn, wqkvo_ref[...])

    rowt = lax.broadcasted_iota(jnp.int32, (R, LANES), 0) % T
    lane = lax.broadcasted_iota(jnp.int32, (R, LANES), 1)
    G = jnp.where(rowt < n_valid, _gate_act(_dot(xn, wg_ref[...]) + bg_ref[...]),
                  jnp.where(lane < ML_HEADS, -jnp.inf, 0.0))
    row = lax.broadcasted_iota(jnp.int32, (R, R), 0)
    col = lax.broadcasted_iota(jnp.int32, (R, R), 1)
    causal = (row // T == col // T) & (col <= row)
    Bc = _dot_exactish(jnp.where(causal, 1.0, 0.0).astype(BF16), jnp.where(lane < ML_HEADS, 0.0, G))
    Gt = G.T
    Bt = Bc.T

    def per_seq(fn):
        return jnp.concatenate([fn(a) for a in range(nb)], axis=0)

    def seq_last(colvec):
        return per_seq(lambda a: jnp.broadcast_to(colvec[T * a + T - 1:T * a + T, :], (T, 1)))

    for h in range(ML_HEADS):
        q, k, v, po = _head_slices(p, h)
        li_col = G[:, h:h + 1]
        b_col = Bc[:, ML_HEADS + h:ML_HEADS + h + 1]
        li_row = Gt[h:h + 1, :]
        b_row = Bt[ML_HEADS + h:ML_HEADS + h + 1, :]
        dmat = jnp.where(causal, b_col - (b_row - li_row), -jnp.inf)
        m_prev = per_seq(lambda a: jnp.broadcast_to(m0_ref[a, h:h + 1, 0:1], (T, 1)))
        n_rows = per_seq(lambda a: jnp.broadcast_to(n0_ref[a, h:h + 1, :], (T, ML_DK)))
        qc, kc, vc = q.astype(BF16), k.astype(BF16), v.astype(BF16)
        qk = lax.dot_general(qc, kc, (((1,), (1,)), ((), ())), preferred_element_type=F32)
        inter = b_col + m_prev
        m_t = jnp.maximum(inter, jnp.max(dmat, axis=1, keepdims=True))
        s = qk * jnp.exp(dmat - m_t)
        a_inter = jnp.exp(inter - m_t)
        qC = per_seq(lambda a: _dot(q[T * a:T * a + T, :], C0_ref[a, h]))
        num = _dot(s.astype(BF16), vc) + a_inter * qC
        den = jnp.sum(s, axis=1, keepdims=True) + a_inter * jnp.sum(q * n_rows, axis=1, keepdims=True)
        hh = num * (1.0 / jnp.maximum(jnp.abs(den), jnp.exp(-m_t)))
        hs_s[:, h * ML_DV:(h + 1) * ML_DV] = _head_out(
            hh, po, hg_ref[:, h * ML_DV:(h + 1) * ML_DV]).astype(BF16)

        m_new = seq_last(m_t)
        b_last = seq_last(b_col)
        decay = jnp.exp(b_last + m_prev - m_new)
        kw = k * jnp.exp(b_last - b_col + li_col - m_new)
        for a in range(nb):
            dec = decay[T * a:T * a + 1, :]
            kw_a = kw[T * a:T * a + T, :]
            upd = lax.dot_general(kw_a, v[T * a:T * a + T, :], (((0,), (0,)), ((), ())),
                                  preferred_element_type=F32)
            C_ref[a, h] = dec * C0_ref[a, h] + upd
            n_ref[a, h:h + 1, :] = dec * n0_ref[a, h:h + 1, :] + jnp.sum(kw_a, axis=0, keepdims=True)
            m_ref[a, h:h + 1, :] = jnp.broadcast_to(m_new[T * a:T * a + 1, :], (1, LANES))

    y_ref[...] = x + _dot(hs_s[...], wout_ref[...])


def _sample_mixer(x, g, wqkvo, wg, bg, hg, wout, C0, n0, m0, n_valid):
    NBT, D = x.shape
    nseq = C0.shape[0]
    nb = SAMPLE_NB
    R = nb * SAMPLE_PAD
    assert nseq % nb == 0 and NBT == nseq * SAMPLE_PAD
    const = lambda i: (0, 0)
    state_specs = [
        pl.BlockSpec((nb, ML_HEADS, ML_DK, ML_DV), lambda i: (i, 0, 0, 0)),
        pl.BlockSpec((nb, ML_HEADS, ML_DK), lambda i: (i, 0, 0)),
        pl.BlockSpec((nb, ML_HEADS, LANES), lambda i: (i, 0, 0)),
    ]
    vmem = _vmem_limit(2 * _nbytes((R, D), F32), _nbytes(wqkvo.shape, BF16), _nbytes(wout.shape, BF16),
                       2 * _nbytes((nb, ML_HEADS, ML_DK, ML_DV), F32), _nbytes((R, ML_QKVO), F32))
    return pl.pallas_call(
        functools.partial(_sample_mixer_kernel, n_valid=n_valid),
        grid=(nseq // nb,),
        in_specs=[
            pl.BlockSpec((R, D), lambda i: (i, 0)),
            pl.BlockSpec((1, D), const),
            pl.BlockSpec((D, ML_QKVO), const),
            pl.BlockSpec((D, LANES), const),
            pl.BlockSpec((1, LANES), const),
            pl.BlockSpec((1, ML_VO), const),
            pl.BlockSpec((ML_VO, D), const),
        ] + state_specs,
        out_specs=[pl.BlockSpec((R, D), lambda i: (i, 0))] + state_specs,
        out_shape=[
            jax.ShapeDtypeStruct((NBT, D), F32),
            jax.ShapeDtypeStruct(C0.shape, F32),
            jax.ShapeDtypeStruct(n0.shape, F32),
            jax.ShapeDtypeStruct(m0.shape, F32),
        ],
        scratch_shapes=[pltpu.VMEM((R, ML_VO), BF16)],
        compiler_params=pltpu.CompilerParams(
            dimension_semantics=("arbitrary",), vmem_limit_bytes=vmem),
        name="sample_mlstm_mixer",
    )(x, g, wqkvo, wg, bg, hg, wout, C0, n0, m0)


def _mlp_kernel(x_ref, g_ref, w1_ref, w2_ref, y_ref):
    x = x_ref[...]
    xn = _rms(x, g_ref[...]).astype(BF16)
    acc = x
    for c in range(D_FF // FF_TILE):
        hcol = _dot(xn, w1_ref[:, c * FF_TILE:(c + 1) * FF_TILE])
        hcol = jnp.square(jnp.maximum(hcol, 0.0)).astype(BF16)
        acc = acc + _dot(hcol, w2_ref[c * FF_TILE:(c + 1) * FF_TILE, :])
    y_ref[...] = acc


def _mlp(x, g, w1, w2):
    N, D = x.shape
    tm = min(ROW_TILE, N)
    assert N % tm == 0
    const = lambda i: (0, 0)
    vmem = _vmem_limit(2 * _nbytes((tm, D), F32), _nbytes(w1.shape, BF16), _nbytes(w2.shape, BF16),
                       _nbytes((tm, FF_TILE), F32))
    return pl.pallas_call(
        _mlp_kernel,
        grid=(N // tm,),
        in_specs=[
            pl.BlockSpec((tm, D), lambda i: (i, 0)),
            pl.BlockSpec((1, D), const),
            pl.BlockSpec((D, D_FF), const, pipeline_mode=pl.Buffered(1)),
            pl.BlockSpec((D_FF, D), const, pipeline_mode=pl.Buffered(1)),
        ],
        out_specs=pl.BlockSpec((tm, D), lambda i: (i, 0)),
        out_shape=jax.ShapeDtypeStruct((N, D), F32),
        compiler_params=pltpu.CompilerParams(
            dimension_semantics=("arbitrary",), vmem_limit_bytes=vmem),
        name="sqrelu_mlp",
    )(x, g, w1, w2)


def _kv_kernel(x_ref, g_ref, wkv_ref, kg_ref, k_ref, v_ref):
    xn = _rms(x_ref[...], g_ref[...]).astype(BF16)
    kv = _dot(xn, wkv_ref[...])
    kraw = kv[:, :ATT_KV]
    v_ref[...] = kv[:, ATT_KV:]
    r = lax.broadcasted_iota(jnp.int32, (ATT_KV, ATT_KV), 0) // ATT_HD
    c = lax.broadcasted_iota(jnp.int32, (ATT_KV, ATT_KV), 1) // ATT_HD
    seg = jnp.where(r == c, 1.0, 0.0).astype(BF16)
    ss = _dot_exactish_right(kraw * kraw, seg)
    k_ref[...] = kraw * lax.rsqrt(ss * (1.0 / ATT_HD) + EPS) * kg_ref[...]


def _dot_exactish_right(x, m01):
    hi, mid, lo = _split3(x)
    return _dot(hi, m01) + _dot(mid, m01) + _dot(lo, m01)


def _shared_kv(x, g, wkv, kg):
    N, D = x.shape
    tm = min(ROW_TILE, N)
    assert N % tm == 0
    const = lambda i: (0, 0)
    vmem = _vmem_limit(_nbytes((tm, D), F32), _nbytes(wkv.shape, BF16), 2 * _nbytes((tm, ATT_KV), F32))
    return pl.pallas_call(
        _kv_kernel,
        grid=(N // tm,),
        in_specs=[
            pl.BlockSpec((tm, D), lambda i: (i, 0)),
            pl.BlockSpec((1, D), const),
            pl.BlockSpec((D, 2 * ATT_KV), const),
            pl.BlockSpec((1, ATT_KV), const),
        ],
        out_specs=[pl.BlockSpec((tm, ATT_KV), lambda i: (i, 0)),
                   pl.BlockSpec((tm, ATT_KV), lambda i: (i, 0))],
        out_shape=[jax.ShapeDtypeStruct((N, ATT_KV), F32), jax.ShapeDtypeStruct((N, ATT_KV), F32)],
        compiler_params=pltpu.CompilerParams(
            dimension_semantics=("arbitrary",), vmem_limit_bytes=vmem),
        name="shared_kv",
    )(x, g, wkv, kg)


def _pair_queries(qraw, pr, qscale, col0=0):
    TQ = qraw.shape[0]
    lo = lax.broadcasted_iota(jnp.int32, (TQ, LANES), 1) < ATT_HD
    out = []
    for e in range(2):
        kvh = 2 * pr + e
        for g in range(ATT_GROUP):
            cc, half = divmod(g, 2)
            c0 = (2 * kvh + cc) * LANES - col0
            q2 = qraw[:, c0:c0 + LANES]
            qm = jnp.where(lo, q2, 0.0) if half == 0 else jnp.where(lo, 0.0, q2)
            ss = jnp.sum(qm * qm, axis=1, keepdims=True)
            qn = qm * lax.rsqrt(ss * (1.0 / ATT_HD) + EPS)
            if qscale is not None:
                qn = qn * qscale
            out.append(qn if half == e else pltpu.roll(qn, ATT_HD, 1))
    return out


def _pair_outputs(o_heads, rden, pr, store):
    TQ = o_heads[0].shape[0]
    lo = lax.broadcasted_iota(jnp.int32, (TQ, LANES), 1) < ATT_HD
    for e in range(2):
        kvh = 2 * pr + e
        for cc in range(2):
            tiles = []
            for half in range(2):
                o = o_heads[e * ATT_GROUP + 2 * cc + half] * rden[e * ATT_GROUP + 2 * cc + half]
                tiles.append(o if half == e else pltpu.roll(o, ATT_HD, 1))
            store((2 * kvh + cc) * LANES, jnp.where(lo, tiles[0], tiles[1]))


def _softmax_with_sink(parts, sink):
    M = sink
    for s in parts:
        M = jnp.maximum(M, jnp.max(s, axis=1, keepdims=True))
    ps = [jnp.exp2(s - M) for s in parts]
    den = jnp.exp2(sink - M)
    for p in ps:
        den = den + jnp.sum(p, axis=1, keepdims=True)
    return ps, 1.0 / den


def _prompt_attn_kernel(sinks_ref, x_ref, g_ref, wq_ref, qg_ref, kp_ref, kc_ref, vp_ref, vc_ref, wo_ref,
                        y_ref):
    TQ = WINDOW
    TR = x_ref.shape[0]
    kj = lax.broadcasted_iota(jnp.int32, (WINDOW + TQ, TQ), 0)
    qi = lax.broadcasted_iota(jnp.int32, (WINDOW + TQ, TQ), 1)
    band = (qi + WINDOW - kj >= 0) & (qi - kj <= 0)
    band_first = band & ((kj >= WINDOW) | (pl.program_id(1) > 0))
    kscale = qg_ref[...] * (ATT_HD ** -0.5 * LOG2E)
    slabs = []
    for pr in range(ATT_KVH // 2):
        sl = slice(pr * LANES, (pr + 1) * LANES)
        slabs.append(((jnp.concatenate([kp_ref[:, sl], kc_ref[:, sl]], axis=0) * kscale).astype(BF16),
                      jnp.concatenate([vp_ref[:, sl], vc_ref[:, sl]], axis=0).T))

    PW = 2 * ATT_GROUP * ATT_HD
    zeros_hd = jnp.zeros((ATT_HD, TQ), F32)
    ones_rows = jnp.ones((2 * SUBLANES, WINDOW + TQ), F32)

    def chain(sb):
        rows = slice(sb * TQ, (sb + 1) * TQ)
        keys = slice(sb * TQ, sb * TQ + WINDOW + TQ)
        mask = band_first if sb == 0 else band
        x = x_ref[rows, :]
        xn = _rms(x, g_ref[...]).astype(BF16)
        acc = x
        for pr in range(ATT_KVH // 2):
            kslab, vslab_t = slabs[pr]
            qt = _dot(xn, wq_ref[:, pr * PW:(pr + 1) * PW]).T
            yield
            tiles, sink_rows = [], []
            for e in range(2):
                for g in range(ATT_GROUP):
                    blk = qt[(e * ATT_GROUP + g) * ATT_HD:(e * ATT_GROUP + g + 1) * ATT_HD, :]
                    qn = blk * lax.rsqrt(jnp.sum(blk * blk, axis=0, keepdims=True) * (1.0 / ATT_HD) + EPS)
                    tiles.append(jnp.concatenate([qn, zeros_hd] if e == 0 else [zeros_hd, qn], axis=0))
                    sink_rows.append(jnp.full((1, TQ), sinks_ref[(2 * pr + e) * ATT_GROUP + g] * LOG2E, F32))
            qmat = jnp.concatenate(tiles, axis=1).astype(BF16)
            st = _dot(kslab[keys, :], qmat)
            yield
            parts = []
            for e in range(2):
                ps, ms = [], []
                for g in range(ATT_GROUP):
                    c0 = (e * ATT_GROUP + g) * TQ
                    s_h = jnp.where(mask, st[:, c0:c0 + TQ], -jnp.inf)
                    m_h = jnp.maximum(jnp.max(s_h, axis=0, keepdims=True), sink_rows[e * ATT_GROUP + g])
                    ps.append(jnp.exp2(s_h - m_h).astype(BF16))
                    ms.append(m_h)
                p_e = jnp.concatenate(ps, axis=1)
                v_aug = jnp.concatenate([vslab_t[e * ATT_HD:(e + 1) * ATT_HD, keys], ones_rows],
                                        axis=0).astype(BF16)
                ot = _dot(v_aug, p_e)
                yield
                sink_term = jnp.exp2(jnp.concatenate(sink_rows[e * ATT_GROUP:(e + 1) * ATT_GROUP], axis=1)
                                     - jnp.concatenate(ms, axis=1))
                on = ot[0:ATT_HD, :] * (1.0 / (ot[ATT_HD:ATT_HD + 1, :] + sink_term))
                parts += [on[:, g * TQ:(g + 1) * TQ] for g in range(ATT_GROUP)]
            o_pair = jnp.concatenate(parts, axis=0).T.astype(BF16)
            acc = acc + _dot(o_pair, wo_ref[pr * PW:(pr + 1) * PW, :])
            yield
        y_ref[rows, :] = acc

    chains = [chain(sb) for sb in range(TR // TQ)]
    live = []
    while chains or live:
        if chains:
            live.append(chains.pop(0))
        for c in list(live):
            if next(c, StopIteration) is StopIteration:
                live.remove(c)


def _prompt_attn(x, k, v, sinks, g, wq, qg, wo):
    B, L, D = x.shape
    TR = ATTN_ROWS
    per = TR // WINDOW
    assert L % TR == 0 and TR % WINDOW == 0
    const = lambda b, i: (0, 0)
    cur = lambda b, i: (b, i, 0)
    prev = lambda b, i: (b, jnp.maximum(i * per - 1, 0), 0)
    vmem = _vmem_limit(2 * _nbytes((TR, D), F32), _nbytes(wq.shape, BF16), _nbytes(wo.shape, BF16),
                       4 * _nbytes((TR, ATT_KV), F32), 8 * _nbytes((8 * WINDOW, 2 * WINDOW), F32))
    return pl.pallas_call(
        _prompt_attn_kernel,
        grid=(B, L // TR),
        in_specs=[
            pl.BlockSpec(memory_space=pltpu.SMEM),
            pl.BlockSpec((None, TR, D), cur),
            pl.BlockSpec((1, D), const),
            pl.BlockSpec((D, D), const),
            pl.BlockSpec((1, LANES), const),
            pl.BlockSpec((None, WINDOW, ATT_KV), prev),
            pl.BlockSpec((None, TR, ATT_KV), cur),
            pl.BlockSpec((None, WINDOW, ATT_KV), prev),
            pl.BlockSpec((None, TR, ATT_KV), cur),
            pl.BlockSpec((D, D), const),
        ],
        out_specs=pl.BlockSpec((None, TR, D), cur),
        out_shape=jax.ShapeDtypeStruct((B, L, D), F32),
        compiler_params=pltpu.CompilerParams(
            dimension_semantics=("arbitrary", "arbitrary"), vmem_limit_bytes=vmem),
        name="prompt_window_attention",
    )(sinks, x, g, wq, qg, k, k, v, v, wo)


def _sample_attn_kernel(sinks_ref, x_ref, g_ref, wq_ref, qg_ref, kc_ref, vc_ref, wk_ref, wv_ref, wo_ref,
                        y_ref, nk_ref, nv_ref, o_s, *, n_valid):
    T = SAMPLE_PAD
    nb = wk_ref.shape[0]
    R = nb * T
    x = x_ref[...]
    q = _dot(_rms(x, g_ref[...]).astype(BF16), wq_ref[...])
    qscale = qg_ref[...] * (ATT_HD ** -0.5 * LOG2E)
    knew = kc_ref[...]
    vnew = vc_ref[...]

    cmask = (lax.broadcasted_iota(jnp.int32, (R, WINDOW), 1)
             >= lax.broadcasted_iota(jnp.int32, (R, WINDOW), 0) % T)
    row = lax.broadcasted_iota(jnp.int32, (R, R), 0)
    col = lax.broadcasted_iota(jnp.int32, (R, R), 1)
    nmask = (row // T == col // T) & (col % T <= row % T) & (col % T < n_valid)

    def store(off, val):
        o_s[:, off:off + LANES] = val.astype(BF16)

    def regroup(per_seq, i):
        return jnp.concatenate([per_seq[a][i * T:(i + 1) * T, :] for a in range(nb)], axis=0)

    for pr in range(ATT_KVH // 2):
        sl = slice(pr * LANES, (pr + 1) * LANES)
        qs = _pair_queries(q, pr, qscale)
        S_new = lax.dot_general(jnp.concatenate(qs, axis=0).astype(BF16), knew[:, sl].astype(BF16),
                                (((1,), (1,)), ((), ())), preferred_element_type=F32)
        sc = []
        for a in range(nb):
            q_a = jnp.concatenate([qi[T * a:T * a + T, :] for qi in qs], axis=0).astype(BF16)
            sc.append(lax.dot_general(q_a, wk_ref[a, :, sl].astype(BF16), (((1,), (1,)), ((), ())),
                                      preferred_element_type=F32))
        pcs, pns, rden = [], [], []
        for i in range(2 * ATT_GROUP):
            s_c = jnp.where(cmask, regroup(sc, i), -jnp.inf)
            s_n = jnp.where(nmask, S_new[i * R:(i + 1) * R, :], -jnp.inf)
            (p_c, p_n), r = _softmax_with_sink([s_c, s_n], sinks_ref[pr * 2 * ATT_GROUP + i] * LOG2E)
            pcs.append(p_c)
            pns.append(p_n)
            rden.append(r)
        O_new = _dot(jnp.concatenate(pns, axis=0).astype(BF16), vnew[:, sl].astype(BF16))
        oc = []
        for a in range(nb):
            p_a = jnp.concatenate([pc[T * a:T * a + T, :] for pc in pcs], axis=0).astype(BF16)
            oc.append(_dot(p_a, wv_ref[a, :, sl].astype(BF16)))
        _pair_outputs([O_new[i * R:(i + 1) * R, :] + regroup(oc, i) for i in range(2 * ATT_GROUP)],
                      rden, pr, store)

    for a in range(nb):
        nk_ref[a, 0:WINDOW - n_valid, :] = wk_ref[a, n_valid:WINDOW, :]
        nk_ref[a, WINDOW - n_valid:WINDOW, :] = knew[T * a:T * a + n_valid, :]
        nv_ref[a, 0:WINDOW - n_valid, :] = wv_ref[a, n_valid:WINDOW, :]
        nv_ref[a, WINDOW - n_valid:WINDOW, :] = vnew[T * a:T * a + n_valid, :]

    y_ref[...] = x + _dot(o_s[...], wo_ref[...])


def _sample_attn(x, k, v, win_k, win_v, sinks, g, wq, qg, wo, n_valid):
    NBT, D = x.shape
    nseq = win_k.shape[0]
    nb = SAMPLE_NB
    R = nb * SAMPLE_PAD
    assert nseq % nb == 0 and NBT == nseq * SAMPLE_PAD
    const = lambda i: (0, 0)
    rows = lambda i: (i, 0)
    cache = pl.BlockSpec((nb, WINDOW, ATT_KV), lambda i: (i, 0, 0))
    vmem = _vmem_limit(2 * _nbytes((R, D), F32), _nbytes(wq.shape, BF16), _nbytes(wo.shape, BF16),
                       4 * _nbytes((nb, WINDOW, ATT_KV), F32), 2 * _nbytes((R, D), F32))
    return pl.pallas_call(
        functools.partial(_sample_attn_kernel, n_valid=n_valid),
        grid=(nseq // nb,),
        in_specs=[
            pl.BlockSpec(memory_space=pltpu.SMEM),
            pl.BlockSpec((R, D), rows),
            pl.BlockSpec((1, D), const),
            pl.BlockSpec((D, D), const),
            pl.BlockSpec((1, LANES), const),
            pl.BlockSpec((R, ATT_KV), rows),
            pl.BlockSpec((R, ATT_KV), rows),
            cache, cache,
            pl.BlockSpec((D, D), const),
        ],
        out_specs=[pl.BlockSpec((R, D), rows), cache, cache],
        out_shape=[jax.ShapeDtypeStruct((NBT, D), F32),
                   jax.ShapeDtypeStruct(win_k.shape, F32), jax.ShapeDtypeStruct(win_v.shape, F32)],
        scratch_shapes=[pltpu.VMEM((R, D), BF16)],
        compiler_params=pltpu.CompilerParams(
            dimension_semantics=("arbitrary",), vmem_limit_bytes=vmem),
        name="sample_window_attention",
    )(sinks, x, g, wq, qg, k, v, win_k, win_v, wo)


def kernel(x_prompt, x_sample, state_mlstm_C, state_mlstm_n, state_mlstm_m, cache_win_k, cache_win_v,
           ml_norm_g, ml_w_in, ml_b_i, ml_b_f, ml_head_g, ml_w_out, kv_norm_g, w_kv, k_norm_g,
           att_norm_g, att_w_q, q_norm_g, att_sinks, att_w_o, mlp_norm_g, mlp_w1, mlp_w2):
    B, L, D = x_prompt.shape
    NS, LS, _ = x_sample.shape
    assert ml_w_in.shape[0] == 1 and att_w_q.shape[0] == 1 and mlp_w1.shape[0] == 2

    w_in = ml_w_in[0]
    wqkvo = w_in[:, :ML_QKVO].astype(BF16)
    wg = jnp.pad(w_in[:, ML_QKVO:], ((0, 0), (0, LANES - 2 * ML_HEADS))).astype(BF16)
    bg = jnp.pad(jnp.concatenate([ml_b_i[0], ml_b_f[0]]), (0, LANES - 2 * ML_HEADS)).reshape(1, LANES)
    ml_g = ml_norm_g[0].reshape(1, D)
    hg = ml_head_g[0].reshape(1, ML_VO)
    wout = ml_w_out[0].astype(BF16)
    w1 = mlp_w1.astype(BF16)
    w2 = mlp_w2.astype(BF16)
    mlp_g = mlp_norm_g.reshape(2, 1, D)
    wkv = w_kv.astype(BF16)
    kv_g = kv_norm_g.reshape(1, D)
    kg = jnp.tile(k_norm_g, ATT_KVH).reshape(1, ATT_KV)
    att_g = att_norm_g[0].reshape(1, D)
    wq = att_w_q[0].astype(BF16)
    qg = jnp.tile(q_norm_g[0], 2).reshape(1, LANES)
    sinks = att_sinks[0]
    wo = att_w_o[0].astype(BF16)

    xp, p_C, p_n, p_m = _prompt_mixer(x_prompt, ml_g, wqkvo, wg, bg, hg, wout)
    xp = _mlp(xp.reshape(B * L, D), mlp_g[0], w1[0], w2[0])
    kp, vp = _shared_kv(xp, kv_g, wkv, kg)
    xp = _prompt_attn(xp.reshape(B, L, D), kp.reshape(B, L, ATT_KV), vp.reshape(B, L, ATT_KV),
                      sinks, att_g, wq, qg, wo)
    y_prompt = _mlp(xp.reshape(B * L, D), mlp_g[1], w1[1], w2[1]).reshape(B, L, D)
    p_wk = kp.reshape(B, L, ATT_KVH, ATT_HD)[:, L - WINDOW:]
    p_wv = vp.reshape(B, L, ATT_KVH, ATT_HD)[:, L - WINDOW:]

    xs = jnp.pad(x_sample, ((0, 0), (0, SAMPLE_PAD - LS), (0, 0))).reshape(NS * SAMPLE_PAD, D)
    m0 = jnp.broadcast_to(state_mlstm_m[0][:, :, None], (NS, ML_HEADS, LANES))
    xs, s_C, s_n, s_m = _sample_mixer(xs, ml_g, wqkvo, wg, bg, hg, wout,
                                      state_mlstm_C[0], state_mlstm_n[0], m0, LS)
    xs = _mlp(xs, mlp_g[0], w1[0], w2[0])
    ks, vs = _shared_kv(xs, kv_g, wkv, kg)
    xs, s_wk, s_wv = _sample_attn(xs, ks, vs, cache_win_k.reshape(NS, WINDOW, ATT_KV),
                                  cache_win_v.reshape(NS, WINDOW, ATT_KV), sinks, att_g, wq, qg, wo, LS)
    y_sample = _mlp(xs, mlp_g[1], w1[1], w2[1]).reshape(NS, SAMPLE_PAD, D)[:, :LS]

    return (y_prompt, y_sample,
            p_C[None], p_n[None], p_m[None, :, :, 0], p_wk, p_wv,
            s_C[None], s_n[None], s_m[None, :, :, 0],
            s_wk.reshape(NS, WINDOW, ATT_KVH, ATT_HD), s_wv.reshape(NS, WINDOW, ATT_KVH, ATT_HD))
```

```python
import functools

import jax
import jax.numpy as jnp
from jax import lax
from jax.experimental import pallas as pl
from jax.experimental.pallas import tpu as pltpu

F32 = jnp.float32
BF16 = jnp.bfloat16

D_MODEL = 1024
ML_HEADS = 4
ML_DK = 128
ML_DV = 256
ML_QK = ML_HEADS * ML_DK
ML_VO = ML_HEADS * ML_DV
ML_QKVO = 2 * ML_QK + 2 * ML_VO
GATE_SOFTCAP = 15.0
ATT_HD = 64
ATT_QH = 16
ATT_KVH = 4
ATT_GROUP = 4
ATT_KV = ATT_KVH * ATT_HD
WINDOW = 128
D_FF = 4 * D_MODEL
EPS = 1e-6
LOG2E = 1.4426950408889634

LANES = 128
SUBLANES = 8
VMEM_LIMIT_CAP = 56 * 1024 * 1024

PROMPT_CHUNK = 256
PROMPT_ROWS = 2
PROJ_COLS = 256
SAMPLE_PAD = SUBLANES
SAMPLE_NB = 16
ATTN_ROWS = 1024
ROW_TILE = 512
FF_TILE = 1024


def _vmem_limit(*block_bytes):
    need = 4 * sum(block_bytes) + (8 << 20)
    return int(min(max(need, 32 << 20), VMEM_LIMIT_CAP))


def _nbytes(shape, dtype):
    n = 1
    for s in shape:
        n *= s
    return n * jnp.dtype(dtype).itemsize


def _rms(x, g):
    return x * lax.rsqrt(jnp.mean(x * x, axis=-1, keepdims=True) + EPS) * g


def _dot(a, b):
    return jnp.dot(a, b, preferred_element_type=F32)


def _split3(x):
    hi = x.astype(BF16)
    r1 = x - hi.astype(F32)
    mid = r1.astype(BF16)
    lo = (r1 - mid.astype(F32)).astype(BF16)
    return hi, mid, lo


def _dot_exactish(m01, x):
    hi, mid, lo = _split3(x)
    return _dot(m01, hi) + _dot(m01, mid) + _dot(m01, lo)


def _gate_act(z):
    cap = GATE_SOFTCAP * jnp.tanh(z * (1.0 / GATE_SOFTCAP))
    lsig = jnp.minimum(cap, 0.0) - jnp.log1p(jnp.exp(-jnp.abs(cap)))
    lane = lax.broadcasted_iota(jnp.int32, z.shape, 1)
    return jnp.where(lane < ML_HEADS, cap, lsig)


_DONE = object()


def _rounds(chains):
    live = list(chains)
    rnd = 0
    while live:
        for item in list(live):
            gen, start = item
            if rnd >= start and next(gen, _DONE) is _DONE:
                live.remove(item)
        rnd += 1
        yield


def _run(chains):
    for _ in _rounds(chains):
        pass


def _head_out(hh, po, hg):
    hn = hh * lax.rsqrt(jnp.mean(hh * hh, axis=-1, keepdims=True) + EPS) * hg
    return jax.nn.sigmoid(po) * hn


def _head_slices(p, h):
    q = p[:, h * ML_DK:(h + 1) * ML_DK]
    k = p[:, ML_QK + h * ML_DK:ML_QK + (h + 1) * ML_DK] * (ML_DK ** -0.5)
    v = p[:, 2 * ML_QK + h * ML_DV:2 * ML_QK + (h + 1) * ML_DV]
    po = p[:, 2 * ML_QK + ML_VO + h * ML_DV:2 * ML_QK + ML_VO + (h + 1) * ML_DV]
    return q, k, v, po


def _prompt_mixer_kernel(xc_ref, xnext_ref, g_ref, wqkvo_ref, wg_ref, bg_ref, hg_ref, wout_ref,
                         y_ref, C_ref, n_ref, m_ref, p_s, gz_s, *, chunks_per_seq):
    R, T = xc_ref.shape[0], xc_ref.shape[1]
    f = pl.program_id(0)
    slot = f % 2

    def project(x_ref, s, r):
        xn = _rms(x_ref[r], g_ref[...]).astype(BF16)
        yield
        for j in range(ML_QKVO // PROJ_COLS):
            cols = slice(j * PROJ_COLS, (j + 1) * PROJ_COLS)
            p_s[s, r, :, cols] = _dot(xn, wqkvo_ref[:, cols])
            yield
        gz_s[s, r] = _dot(xn, wg_ref[...])
        yield

    @pl.when(f == 0)
    def _():
        _run([(project(xc_ref, 0, r), 0) for r in range(R)])

    @pl.when(f % chunks_per_seq == 0)
    def _():
        C_ref[...] = jnp.zeros_like(C_ref)
        n_ref[...] = jnp.zeros_like(n_ref)
        m_ref[...] = jnp.zeros_like(m_ref)

    def recurrence(r):
        G = _gate_act(gz_s[slot, r] + bg_ref[...])
        row = lax.broadcasted_iota(jnp.int32, (T, T), 0)
        col = lax.broadcasted_iota(jnp.int32, (T, T), 1)
        causal = col <= row
        Bc = _dot_exactish(jnp.where(causal, 1.0, 0.0).astype(BF16), G)
        yield
        Gt = G.T
        Bt = Bc.T
        yield
        parts = [None] * ML_HEADS

        def head(h):
            def cols(base, width):
                return p_s[slot, r, :, base + h * width:base + (h + 1) * width]

            q = cols(0, ML_DK)
            k = cols(ML_QK, ML_DK) * (ML_DK ** -0.5)
            v = cols(2 * ML_QK, ML_DV)
            qc, kc, vc = q.astype(BF16), k.astype(BF16), v.astype(BF16)
            qk = lax.dot_general(qc, kc, (((1,), (1,)), ((), ())), preferred_element_type=F32)
            yield
            li_col = G[:, h:h + 1]
            b_col = Bc[:, ML_HEADS + h:ML_HEADS + h + 1]
            dmat = jnp.where(causal, b_col - Bt[ML_HEADS + h:ML_HEADS + h + 1, :] + Gt[h:h + 1, :], -jnp.inf)
            m_prev = m_ref[r, h:h + 1, 0:1]
            inter = b_col + m_prev
            m_t = jnp.maximum(inter, jnp.max(dmat, axis=1, keepdims=True))
            yield
            s = qk * jnp.exp(dmat - m_t)
            a_inter = jnp.exp(inter - m_t)
            den = (jnp.sum(s, axis=1, keepdims=True)
                   + a_inter * jnp.sum(q * n_ref[r, h:h + 1, :], axis=1, keepdims=True))
            yield
            C = C_ref[r, h]
            num = _dot(s.astype(BF16), vc) + a_inter * _dot(qc, C.astype(BF16))
            yield
            hh = num * (1.0 / jnp.maximum(jnp.abs(den), jnp.exp(-m_t)))
            hs = _head_out(hh, cols(2 * ML_QK + ML_VO, ML_DV), hg_ref[:, h * ML_DV:(h + 1) * ML_DV])
            yield
            parts[h] = _dot(hs.astype(BF16), wout_ref[h * ML_DV:(h + 1) * ML_DV, :])
            yield
            m_new = m_t[T - 1:T, :]
            b_last = b_col[T - 1:T, :]
            decay = jnp.exp(b_last + m_prev - m_new)
            kw = k * jnp.exp(b_last - b_col + li_col - m_new)
            C_ref[r, h] = decay * C + lax.dot_general(kw.astype(BF16), vc, (((0,), (0,)), ((), ())),
                                                      preferred_element_type=F32)
            n_ref[r, h:h + 1, :] = decay * n_ref[r, h:h + 1, :] + jnp.sum(kw, axis=0, keepdims=True)
            m_ref[r, h:h + 1, :] = jnp.broadcast_to(m_new, (1, LANES))
            yield

        yield from _rounds([(head(h), h) for h in range(ML_HEADS)])
        y_ref[r] = xc_ref[r] + ((parts[0] + parts[1]) + (parts[2] + parts[3]))

    _run([(project(xnext_ref, 1 - slot, r), 0) for r in range(R)] + [(recurrence(r), r) for r in range(R)])


def _prompt_mixer(x, g, wqkvo, wg, bg, hg, wout):
    B, L, D = x.shape
    T = PROMPT_CHUNK
    assert L % T == 0
    R = PROMPT_ROWS
    assert B % R == 0
    nc = L // T
    steps = (B // R) * nc
    const = lambda f: (0, 0)
    cur = lambda f: (f // nc, f % nc, 0)
    nxt = lambda f: (jnp.minimum(f + 1, steps - 1) // nc, jnp.minimum(f + 1, steps - 1) % nc, 0)
    once = dict(pipeline_mode=pl.Buffered(1))
    vmem = _vmem_limit(3 * _nbytes((R, T, D), F32), _nbytes(wqkvo.shape, BF16) // 2,
                       _nbytes(wout.shape, BF16) // 2, _nbytes((R, ML_HEADS, ML_DK, ML_DV), F32),
                       2 * _nbytes((R, T, ML_QKVO), F32))
    y, C, n, m = pl.pallas_call(
        functools.partial(_prompt_mixer_kernel, chunks_per_seq=nc),
        grid=(steps,),
        in_specs=[
            pl.BlockSpec((R, T, D), cur),
            pl.BlockSpec((R, T, D), nxt),
            pl.BlockSpec((1, D), const),
            pl.BlockSpec((D, ML_QKVO), const, **once),
            pl.BlockSpec((D, LANES), const, **once),
            pl.BlockSpec((1, LANES), const),
            pl.BlockSpec((1, ML_VO), const),
            pl.BlockSpec((ML_VO, D), const, **once),
        ],
        out_specs=[
            pl.BlockSpec((R, T, D), cur),
            pl.BlockSpec((R, ML_HEADS, ML_DK, ML_DV), lambda f: (f // nc, 0, 0, 0)),
            pl.BlockSpec((R, ML_HEADS, ML_DK), lambda f: (f // nc, 0, 0)),
            pl.BlockSpec((R, ML_HEADS, LANES), lambda f: (f // nc, 0, 0)),
        ],
        out_shape=[
            jax.ShapeDtypeStruct((B, L, D), F32),
            jax.ShapeDtypeStruct((B, ML_HEADS, ML_DK, ML_DV), F32),
            jax.ShapeDtypeStruct((B, ML_HEADS, ML_DK), F32),
            jax.ShapeDtypeStruct((B, ML_HEADS, LANES), F32),
        ],
        scratch_shapes=[pltpu.VMEM((2, R, T, ML_QKVO), F32), pltpu.VMEM((2, R, T, LANES), F32)],
        compiler_params=pltpu.CompilerParams(
            dimension_semantics=("arbitrary",), vmem_limit_bytes=vmem),
        name="prompt_mlstm_mixer",
    )(x, x, g, wqkvo, wg, bg, hg, wout)
    return y, C, n, m


def _sample_mixer_kernel(x_ref, g_ref, wqkvo_ref, wg_ref, bg_ref, hg_ref, wout_ref,
                         C0_ref, n0_ref, m0_ref,
                         y_ref, C_ref, n_ref, m_ref, hs_s, *, n_valid):
    T = SAMPLE_PAD
    nb = C0_ref.shape[0]
    R = nb * T
    x = x_ref[...]
    xn = _rms(x, g_ref[...]).astype(BF16)
    p = _dot(xn, wqkvo_ref[...])

    rowt = lax.broadcasted_iota(jnp.int32, (R, LANES), 0) % T
    lane = lax.broadcasted_iota(jnp.int32, (R, LANES), 1)
    G = jnp.where(rowt < n_valid, _gate_act(_dot(xn, wg_ref[...]) + bg_ref[...]),
                  jnp.where(lane < ML_HEADS, -jnp.inf, 0.0))
    row = lax.broadcasted_iota(jnp.int32, (R, R), 0)
    col = lax.broadcasted_iota(jnp.int32, (R, R), 1)
    causal = (row // T == col // T) & (col <= row)
    Bc = _dot_exactish(jnp.where(causal, 1.0, 0.0).astype(BF16), jnp.where(lane < ML_HEADS, 0.0, G))
    Gt = G.T
    Bt = Bc.T

    def per_seq(fn):
        return jnp.concatenate([fn(a) for a in range(nb)], axis=0)

    def seq_last(colvec):
        return per_seq(lambda a: jnp.broadcast_to(colvec[T * a + T - 1:T * a + T, :], (T, 1)))

    for h in range(ML_HEADS):
        q, k, v, po = _head_slices(p, h)
        li_col = G[:, h:h + 1]
        b_col = Bc[:, ML_HEADS + h:ML_HEADS + h + 1]
        li_row = Gt[h:h + 1, :]
        b_row = Bt[ML_HEADS + h:ML_HEADS + h + 1, :]
        dmat = jnp.where(causal, b_col - (b_row - li_row), -jnp.inf)
        m_prev = per_seq(lambda a: jnp.broadcast_to(m0_ref[a, h:h + 1, 0:1], (T, 1)))
        n_rows = per_seq(lambda a: jnp.broadcast_to(n0_ref[a, h:h + 1, :], (T, ML_DK)))
        qc, kc, vc = q.astype(BF16), k.astype(BF16), v.astype(BF16)
        qk = lax.dot_general(qc, kc, (((1,), (1,)), ((), ())), preferred_element_type=F32)
        inter = b_col + m_prev
        m_t = jnp.maximum(inter, jnp.max(dmat, axis=1, keepdims=True))
        s = qk * jnp.exp(dmat - m_t)
        a_inter = jnp.exp(inter - m_t)
        qC = per_seq(lambda a: _dot(q[T * a:T * a + T, :], C0_ref[a, h]))
        num = _dot(s.astype(BF16), vc) + a_inter * qC
        den = jnp.sum(s, axis=1, keepdims=True) + a_inter * jnp.sum(q * n_rows, axis=1, keepdims=True)
        hh = num * (1.0 / jnp.maximum(jnp.abs(den), jnp.exp(-m_t)))
        hs_s[:, h * ML_DV:(h + 1) * ML_DV] = _head_out(
            hh, po, hg_ref[:, h * ML_DV:(h + 1) * ML_DV]).astype(BF16)

        m_new = seq_last(m_t)
        b_last = seq_last(b_col)
        decay = jnp.exp(b_last + m_prev - m_new)
        kw = k * jnp.exp(b_last - b_col + li_col - m_new)
        for a in range(nb):
            dec = decay[T * a:T * a + 1, :]
            kw_a = kw[T * a:T * a + T, :]
            upd = lax.dot_general(kw_a, v[T * a:T * a + T, :], (((0,), (0,)), ((), ())),
                                  preferred_element_type=F32)
            C_ref[a, h] = dec * C0_ref[a, h] + upd
            n_ref[a, h:h + 1, :] = dec * n0_ref[a, h:h + 1, :] + jnp.sum(kw_a, axis=0, keepdims=True)
            m_ref[a, h:h + 1, :] = jnp.broadcast_to(m_new[T * a:T * a + 1, :], (1, LANES))

    y_ref[...] = x + _dot(hs_s[...], wout_ref[...])


def _sample_mixer(x, g, wqkvo, wg, bg, hg, wout, C0, n0, m0, n_valid):
    NBT, D = x.shape
    nseq = C0.shape[0]
    nb = SAMPLE_NB
    R = nb * SAMPLE_PAD
    assert nseq % nb == 0 and NBT == nseq * SAMPLE_PAD
    const = lambda i: (0, 0)
    state_specs = [
        pl.BlockSpec((nb, ML_HEADS, ML_DK, ML_DV), lambda i: (i, 0, 0, 0)),
        pl.BlockSpec((nb, ML_HEADS, ML_DK), lambda i: (i, 0, 0)),
        pl.BlockSpec((nb, ML_HEADS, LANES), lambda i: (i, 0, 0)),
    ]
    vmem = _vmem_limit(2 * _nbytes((R, D), F32), _nbytes(wqkvo.shape, BF16), _nbytes(wout.shape, BF16),
                       2 * _nbytes((nb, ML_HEADS, ML_DK, ML_DV), F32), _nbytes((R, ML_QKVO), F32))
    return pl.pallas_call(
        functools.partial(_sample_mixer_kernel, n_valid=n_valid),
        grid=(nseq // nb,),
        in_specs=[
            pl.BlockSpec((R, D), lambda i: (i, 0)),
            pl.BlockSpec((1, D), const),
            pl.BlockSpec((D, ML_QKVO), const),
            pl.BlockSpec((D, LANES), const),
            pl.BlockSpec((1, LANES), const),
            pl.BlockSpec((1, ML_VO), const),
            pl.BlockSpec((ML_VO, D), const),
        ] + state_specs,
        out_specs=[pl.BlockSpec((R, D), lambda i: (i, 0))] + state_specs,
        out_shape=[
            jax.ShapeDtypeStruct((NBT, D), F32),
            jax.ShapeDtypeStruct(C0.shape, F32),
            jax.ShapeDtypeStruct(n0.shape, F32),
            jax.ShapeDtypeStruct(m0.shape, F32),
        ],
        scratch_shapes=[pltpu.VMEM((R, ML_VO), BF16)],
        compiler_params=pltpu.CompilerParams(
            dimension_semantics=("arbitrary",), vmem_limit_bytes=vmem),
        name="sample_mlstm_mixer",
    )(x, g, wqkvo, wg, bg, hg, wout, C0, n0, m0)


def _dot_exactish_right(x, m01):
    hi, mid, lo = _split3(x)
    return _dot(hi, m01) + _dot(mid, m01) + _dot(lo, m01)


def _mlp_kernel(x_ref, g_ref, w1_ref, w2_ref, *rest):
    x = x_ref[...]
    xn = _rms(x, g_ref[...]).astype(BF16)
    acc = x
    for c in range(D_FF // FF_TILE):
        hcol = _dot(xn, w1_ref[:, c * FF_TILE:(c + 1) * FF_TILE])
        hcol = jnp.square(jnp.maximum(hcol, 0.0)).astype(BF16)
        acc = acc + _dot(hcol, w2_ref[c * FF_TILE:(c + 1) * FF_TILE, :])
    if len(rest) == 1:
        (y_ref,) = rest
        y_ref[...] = acc
        return
    gkv_ref, wkv_ref, kg_ref, y_ref, k_ref, v_ref = rest
    y_ref[...] = acc
    kv = _dot(_rms(acc, gkv_ref[...]).astype(BF16), wkv_ref[...])
    kraw = kv[:, :ATT_KV]
    v_ref[...] = kv[:, ATT_KV:]
    r = lax.broadcasted_iota(jnp.int32, (ATT_KV, ATT_KV), 0) // ATT_HD
    c = lax.broadcasted_iota(jnp.int32, (ATT_KV, ATT_KV), 1) // ATT_HD
    seg = jnp.where(r == c, 1.0, 0.0).astype(BF16)
    ss = _dot_exactish_right(kraw * kraw, seg)
    k_ref[...] = kraw * lax.rsqrt(ss * (1.0 / ATT_HD) + EPS) * kg_ref[...]


def _mlp(x, g, w1, w2, kv=None):
    N, D = x.shape
    tm = min(ROW_TILE, N)
    assert N % tm == 0
    const = lambda i: (0, 0)
    rows = lambda i: (i, 0)
    once = dict(pipeline_mode=pl.Buffered(1))
    in_specs = [
        pl.BlockSpec((tm, D), rows),
        pl.BlockSpec((1, D), const),
        pl.BlockSpec((D, D_FF), const, **once),
        pl.BlockSpec((D_FF, D), const, **once),
    ]
    out_specs = [pl.BlockSpec((tm, D), rows)]
    out_shape = [jax.ShapeDtypeStruct((N, D), F32)]
    args = [x, g, w1, w2]
    if kv is not None:
        in_specs += [pl.BlockSpec((1, D), const), pl.BlockSpec((D, 2 * ATT_KV), const, **once),
                     pl.BlockSpec((1, ATT_KV), const)]
        out_specs += [pl.BlockSpec((tm, ATT_KV), rows)] * 2
        out_shape += [jax.ShapeDtypeStruct((N, ATT_KV), F32)] * 2
        args += list(kv)
    vmem = _vmem_limit(2 * _nbytes((tm, D), F32), _nbytes(w1.shape, BF16) // 2, _nbytes(w2.shape, BF16) // 2,
                       2 * _nbytes((tm, FF_TILE), F32))
    out = pl.pallas_call(
        _mlp_kernel,
        grid=(N // tm,),
        in_specs=in_specs,
        out_specs=out_specs,
        out_shape=out_shape,
        compiler_params=pltpu.CompilerParams(
            dimension_semantics=("arbitrary",), vmem_limit_bytes=vmem),
        name="sqrelu_mlp",
    )(*args)
    return out[0] if kv is None else out


def _pair_queries(qraw, pr, qscale, col0=0):
    TQ = qraw.shape[0]
    lo = lax.broadcasted_iota(jnp.int32, (TQ, LANES), 1) < ATT_HD
    out = []
    for e in range(2):
        kvh = 2 * pr + e
        for g in range(ATT_GROUP):
            cc, half = divmod(g, 2)
            c0 = (2 * kvh + cc) * LANES - col0
            q2 = qraw[:, c0:c0 + LANES]
            qm = jnp.where(lo, q2, 0.0) if half == 0 else jnp.where(lo, 0.0, q2)
            ss = jnp.sum(qm * qm, axis=1, keepdims=True)
            qn = qm * lax.rsqrt(ss * (1.0 / ATT_HD) + EPS)
            if qscale is not None:
                qn = qn * qscale
            out.append(qn if half == e else pltpu.roll(qn, ATT_HD, 1))
    return out


def _pair_outputs(o_heads, rden, pr, store):
    TQ = o_heads[0].shape[0]
    lo = lax.broadcasted_iota(jnp.int32, (TQ, LANES), 1) < ATT_HD
    for e in range(2):
        kvh = 2 * pr + e
        for cc in range(2):
            tiles = []
            for half in range(2):
                o = o_heads[e * ATT_GROUP + 2 * cc + half] * rden[e * ATT_GROUP + 2 * cc + half]
                tiles.append(o if half == e else pltpu.roll(o, ATT_HD, 1))
            store((2 * kvh + cc) * LANES, jnp.where(lo, tiles[0], tiles[1]))


def _softmax_with_sink(parts, sink):
    M = sink
    for s in parts:
        M = jnp.maximum(M, jnp.max(s, axis=1, keepdims=True))
    ps = [jnp.exp2(s - M) for s in parts]
    den = jnp.exp2(sink - M)
    for p in ps:
        den = den + jnp.sum(p, axis=1, keepdims=True)
    return ps, 1.0 / den


def _prompt_attn_kernel(sinks_ref, x_ref, g_ref, wq_ref, qg_ref, kp_ref, kc_ref, vp_ref, vc_ref, wo_ref,
                        y_ref):
    TQ = WINDOW
    TR = x_ref.shape[0]
    kj = lax.broadcasted_iota(jnp.int32, (WINDOW + TQ, TQ), 0)
    qi = lax.broadcasted_iota(jnp.int32, (WINDOW + TQ, TQ), 1)
    band = (qi + WINDOW - kj >= 0) & (qi - kj <= 0)
    band_first = band & ((kj >= WINDOW) | (pl.program_id(1) > 0))
    kscale = qg_ref[...] * (ATT_HD ** -0.5 * LOG2E)
    slabs = []
    for pr in range(ATT_KVH // 2):
        sl = slice(pr * LANES, (pr + 1) * LANES)
        slabs.append(((jnp.concatenate([kp_ref[:, sl], kc_ref[:, sl]], axis=0) * kscale).astype(BF16),
                      jnp.concatenate([vp_ref[:, sl], vc_ref[:, sl]], axis=0).T))

    PW = 2 * ATT_GROUP * ATT_HD
    zeros_hd = jnp.zeros((ATT_HD, TQ), F32)
    ones_rows = jnp.ones((2 * SUBLANES, WINDOW + TQ), F32)

    def chain(sb):
        rows = slice(sb * TQ, (sb + 1) * TQ)
        keys = slice(sb * TQ, sb * TQ + WINDOW + TQ)
        mask = band_first if sb == 0 else band
        x = x_ref[rows, :]
        xn = _rms(x, g_ref[...]).astype(BF16)
        acc = x
        for pr in range(ATT_KVH // 2):
            kslab, vslab_t = slabs[pr]
            qt = _dot(xn, wq_ref[:, pr * PW:(pr + 1) * PW]).T
            yield
            tiles, sink_rows = [], []
            for e in range(2):
                for g in range(ATT_GROUP):
                    blk = qt[(e * ATT_GROUP + g) * ATT_HD:(e * ATT_GROUP + g + 1) * ATT_HD, :]
                    qn = blk * lax.rsqrt(jnp.sum(blk * blk, axis=0, keepdims=True) * (1.0 / ATT_HD) + EPS)
                    tiles.append(jnp.concatenate([qn, zeros_hd] if e == 0 else [zeros_hd, qn], axis=0))
                    sink_rows.append(jnp.full((1, TQ), sinks_ref[(2 * pr + e) * ATT_GROUP + g] * LOG2E, F32))
            qmat = jnp.concatenate(tiles, axis=1).astype(BF16)
            st = _dot(kslab[keys, :], qmat)
            yield
            parts = []
            for e in range(2):
                ps, ms = [], []
                for g in range(ATT_GROUP):
                    c0 = (e * ATT_GROUP + g) * TQ
                    s_h = jnp.where(mask, st[:, c0:c0 + TQ], -jnp.inf)
                    m_h = jnp.maximum(jnp.max(s_h, axis=0, keepdims=True), sink_rows[e * ATT_GROUP + g])
                    ps.append(jnp.exp2(s_h - m_h).astype(BF16))
                    ms.append(m_h)
                p_e = jnp.concatenate(ps, axis=1)
                v_aug = jnp.concatenate([vslab_t[e * ATT_HD:(e + 1) * ATT_HD, keys], ones_rows],
                                        axis=0).astype(BF16)
                ot = _dot(v_aug, p_e)
                yield
                sink_term = jnp.exp2(jnp.concatenate(sink_rows[e * ATT_GROUP:(e + 1) * ATT_GROUP], axis=1)
                                     - jnp.concatenate(ms, axis=1))
                on = ot[0:ATT_HD, :] * (1.0 / (ot[ATT_HD:ATT_HD + 1, :] + sink_term))
                parts += [on[:, g * TQ:(g + 1) * TQ] for g in range(ATT_GROUP)]
            o_pair = jnp.concatenate(parts, axis=0).T.astype(BF16)
            acc = acc + _dot(o_pair, wo_ref[pr * PW:(pr + 1) * PW, :])
            yield
        y_ref[rows, :] = acc

    _run([(chain(sb), sb) for sb in range(TR // TQ)])


def _prompt_attn(x, k, v, sinks, g, wq, qg, wo):
    B, L, D = x.shape
    TR = ATTN_ROWS
    per = TR // WINDOW
    assert L % TR == 0 and TR % WINDOW == 0
    const = lambda b, i: (0, 0)
    cur = lambda b, i: (b, i, 0)
    prev = lambda b, i: (b, jnp.maximum(i * per - 1, 0), 0)
    vmem = _vmem_limit(2 * _nbytes((TR, D), F32), _nbytes(wq.shape, BF16), _nbytes(wo.shape, BF16),
                       4 * _nbytes((TR, ATT_KV), F32), 8 * _nbytes((8 * WINDOW, 2 * WINDOW), F32))
    return pl.pallas_call(
        _prompt_attn_kernel,
        grid=(B, L // TR),
        in_specs=[
            pl.BlockSpec(memory_space=pltpu.SMEM),
            pl.BlockSpec((None, TR, D), cur),
            pl.BlockSpec((1, D), const),
            pl.BlockSpec((D, D), const),
            pl.BlockSpec((1, LANES), const),
            pl.BlockSpec((None, WINDOW, ATT_KV), prev),
            pl.BlockSpec((None, TR, ATT_KV), cur),
            pl.BlockSpec((None, WINDOW, ATT_KV), prev),
            pl.BlockSpec((None, TR, ATT_KV), cur),
            pl.BlockSpec((D, D), const),
        ],
        out_specs=pl.BlockSpec((None, TR, D), cur),
        out_shape=jax.ShapeDtypeStruct((B, L, D), F32),
        compiler_params=pltpu.CompilerParams(
            dimension_semantics=("arbitrary", "arbitrary"), vmem_limit_bytes=vmem),
        name="prompt_window_attention",
    )(sinks, x, g, wq, qg, k, k, v, v, wo)


def _sample_attn_kernel(sinks_ref, x_ref, g_ref, wq_ref, qg_ref, kc_ref, vc_ref, wk_ref, wv_ref, wo_ref,
                        y_ref, nk_ref, nv_ref, o_s, *, n_valid):
    T = SAMPLE_PAD
    nb = wk_ref.shape[0]
    R = nb * T
    x = x_ref[...]
    q = _dot(_rms(x, g_ref[...]).astype(BF16), wq_ref[...])
    qscale = qg_ref[...] * (ATT_HD ** -0.5 * LOG2E)
    knew = kc_ref[...]
    vnew = vc_ref[...]

    cmask = (lax.broadcasted_iota(jnp.int32, (R, WINDOW), 1)
             >= lax.broadcasted_iota(jnp.int32, (R, WINDOW), 0) % T)
    row = lax.broadcasted_iota(jnp.int32, (R, R), 0)
    col = lax.broadcasted_iota(jnp.int32, (R, R), 1)
    nmask = (row // T == col // T) & (col % T <= row % T) & (col % T < n_valid)

    def store(off, val):
        o_s[:, off:off + LANES] = val.astype(BF16)

    def regroup(per_seq, i):
        return jnp.concatenate([per_seq[a][i * T:(i + 1) * T, :] for a in range(nb)], axis=0)

    for pr in range(ATT_KVH // 2):
        sl = slice(pr * LANES, (pr + 1) * LANES)
        qs = _pair_queries(q, pr, qscale)
        S_new = lax.dot_general(jnp.concatenate(qs, axis=0).astype(BF16), knew[:, sl].astype(BF16),
                                (((1,), (1,)), ((), ())), preferred_element_type=F32)
        sc = []
        for a in range(nb):
            q_a = jnp.concatenate([qi[T * a:T * a + T, :] for qi in qs], axis=0).astype(BF16)
            sc.append(lax.dot_general(q_a, wk_ref[a, :, sl].astype(BF16), (((1,), (1,)), ((), ())),
                                      preferred_element_type=F32))
        pcs, pns, rden = [], [], []
        for i in range(2 * ATT_GROUP):
            s_c = jnp.where(cmask, regroup(sc, i), -jnp.inf)
            s_n = jnp.where(nmask, S_new[i * R:(i + 1) * R, :], -jnp.inf)
            (p_c, p_n), r = _softmax_with_sink([s_c, s_n], sinks_ref[pr * 2 * ATT_GROUP + i] * LOG2E)
            pcs.append(p_c)
            pns.append(p_n)
            rden.append(r)
        O_new = _dot(jnp.concatenate(pns, axis=0).astype(BF16), vnew[:, sl].astype(BF16))
        oc = []
        for a in range(nb):
            p_a = jnp.concatenate([pc[T * a:T * a + T, :] for pc in pcs], axis=0).astype(BF16)
            oc.append(_dot(p_a, wv_ref[a, :, sl].astype(BF16)))
        _pair_outputs([O_new[i * R:(i + 1) * R, :] + regroup(oc, i) for i in range(2 * ATT_GROUP)],
                      rden, pr, store)

    for a in range(nb):
        nk_ref[a, 0:WINDOW - n_valid, :] = wk_ref[a, n_valid:WINDOW, :]
        nk_ref[a, WINDOW - n_valid:WINDOW, :] = knew[T * a:T * a + n_valid, :]
        nv_ref[a, 0:WINDOW - n_valid, :] = wv_ref[a, n_valid:WINDOW, :]
        nv_ref[a, WINDOW - n_valid:WINDOW, :] = vnew[T * a:T * a + n_valid, :]

    y_ref[...] = x + _dot(o_s[...], wo_ref[...])


def _sample_attn(x, k, v, win_k, win_v, sinks, g, wq, qg, wo, n_valid):
    NBT, D = x.shape
    nseq = win_k.shape[0]
    nb = SAMPLE_NB
    R = nb * SAMPLE_PAD
    assert nseq % nb == 0 and NBT == nseq * SAMPLE_PAD
    const = lambda i: (0, 0)
    rows = lambda i: (i, 0)
    cache = pl.BlockSpec((nb, WINDOW, ATT_KV), lambda i: (i, 0, 0))
    vmem = _vmem_limit(2 * _nbytes((R, D), F32), _nbytes(wq.shape, BF16), _nbytes(wo.shape, BF16),
                       4 * _nbytes((nb, WINDOW, ATT_KV), F32), 2 * _nbytes((R, D), F32))
    return pl.pallas_call(
        functools.partial(_sample_attn_kernel, n_valid=n_valid),
        grid=(nseq // nb,),
        in_specs=[
            pl.BlockSpec(memory_space=pltpu.SMEM),
            pl.BlockSpec((R, D), rows),
            pl.BlockSpec((1, D), const),
            pl.BlockSpec((D, D), const),
            pl.BlockSpec((1, LANES), const),
            pl.BlockSpec((R, ATT_KV), rows),
            pl.BlockSpec((R, ATT_KV), rows),
            cache, cache,
            pl.BlockSpec((D, D), const),
        ],
        out_specs=[pl.BlockSpec((R, D), rows), cache, cache],
        out_shape=[jax.ShapeDtypeStruct((NBT, D), F32),
                   jax.ShapeDtypeStruct(win_k.shape, F32), jax.ShapeDtypeStruct(win_v.shape, F32)],
        scratch_shapes=[pltpu.VMEM((R, D), BF16)],
        compiler_params=pltpu.CompilerParams(
            dimension_semantics=("arbitrary",), vmem_limit_bytes=vmem),
        name="sample_window_attention",
    )(sinks, x, g, wq, qg, k, v, win_k, win_v, wo)


def kernel(x_prompt, x_sample, state_mlstm_C, state_mlstm_n, state_mlstm_m, cache_win_k, cache_win_v,
           ml_norm_g, ml_w_in, ml_b_i, ml_b_f, ml_head_g, ml_w_out, kv_norm_g, w_kv, k_norm_g,
           att_norm_g, att_w_q, q_norm_g, att_sinks, att_w_o, mlp_norm_g, mlp_w1, mlp_w2):
    B, L, D = x_prompt.shape
    NS, LS, _ = x_sample.shape
    assert ml_w_in.shape[0] == 1 and att_w_q.shape[0] == 1 and mlp_w1.shape[0] == 2

    w_in = ml_w_in[0]
    wqkvo = w_in[:, :ML_QKVO].astype(BF16)
    wg = jnp.pad(w_in[:, ML_QKVO:], ((0, 0), (0, LANES - 2 * ML_HEADS))).astype(BF16)
    bg = jnp.pad(jnp.concatenate([ml_b_i[0], ml_b_f[0]]), (0, LANES - 2 * ML_HEADS)).reshape(1, LANES)
    ml_g = ml_norm_g[0].reshape(1, D)
    hg = ml_head_g[0].reshape(1, ML_VO)
    wout = ml_w_out[0].astype(BF16)
    w1 = mlp_w1.astype(BF16)
    w2 = mlp_w2.astype(BF16)
    mlp_g = mlp_norm_g.reshape(2, 1, D)
    wkv = w_kv.astype(BF16)
    kv_g = kv_norm_g.reshape(1, D)
    kg = jnp.tile(k_norm_g, ATT_KVH).reshape(1, ATT_KV)
    att_g = att_norm_g[0].reshape(1, D)
    wq = att_w_q[0].astype(BF16)
    qg = jnp.tile(q_norm_g[0], 2).reshape(1, LANES)
    sinks = att_sinks[0]
    wo = att_w_o[0].astype(BF16)

    xp, p_C, p_n, p_m = _prompt_mixer(x_prompt, ml_g, wqkvo, wg, bg, hg, wout)
    xp, kp, vp = _mlp(xp.reshape(B * L, D), mlp_g[0], w1[0], w2[0], kv=(kv_g, wkv, kg))
    xp = _prompt_attn(xp.reshape(B, L, D), kp.reshape(B, L, ATT_KV), vp.reshape(B, L, ATT_KV),
                      sinks, att_g, wq, qg, wo)
    y_prompt = _mlp(xp.reshape(B * L, D), mlp_g[1], w1[1], w2[1]).reshape(B, L, D)
    p_wk = kp.reshape(B, L, ATT_KVH, ATT_HD)[:, L - WINDOW:]
    p_wv = vp.reshape(B, L, ATT_KVH, ATT_HD)[:, L - WINDOW:]

    xs = jnp.pad(x_sample, ((0, 0), (0, SAMPLE_PAD - LS), (0, 0))).reshape(NS * SAMPLE_PAD, D)
    m0 = jnp.broadcast_to(state_mlstm_m[0][:, :, None], (NS, ML_HEADS, LANES))
    xs, s_C, s_n, s_m = _sample_mixer(xs, ml_g, wqkvo, wg, bg, hg, wout,
                                      state_mlstm_C[0], state_mlstm_n[0], m0, LS)
    xs, ks, vs = _mlp(xs, mlp_g[0], w1[0], w2[0], kv=(kv_g, wkv, kg))
    xs, s_wk, s_wv = _sample_attn(xs, ks, vs, cache_win_k.reshape(NS, WINDOW, ATT_KV),
                                  cache_win_v.reshape(NS, WINDOW, ATT_KV), sinks, att_g, wq, qg, wo, LS)
    y_sample = _mlp(xs, mlp_g[1], w1[1], w2[1]).reshape(NS, SAMPLE_PAD, D)[:, :LS]

    return (y_prompt, y_sample,
            p_C[None], p_n[None], p_m[None, :, :, 0], p_wk, p_wv,
            s_C[None], s_n[None], s_m[None, :, :, 0],
            s_wk.reshape(NS, WINDOW, ATT_KVH, ATT_HD), s_wv.reshape(NS, WINDOW, ATT_KVH, ATT_HD))
```

```python
import functools

import jax
import jax.numpy as jnp
from jax import lax
from jax.experimental import pallas as pl
from jax.experimental.pallas import tpu as pltpu

F32 = jnp.float32
BF16 = jnp.bfloat16

D_MODEL = 1024
ML_HEADS = 4
ML_DK = 128
ML_DV = 256
ML_QK = ML_HEADS * ML_DK
ML_VO = ML_HEADS * ML_DV
ML_QKVO = 2 * ML_QK + 2 * ML_VO
GATE_SOFTCAP = 15.0
ATT_HD = 64
ATT_QH = 16
ATT_KVH = 4
ATT_GROUP = 4
ATT_KV = ATT_KVH * ATT_HD
WINDOW = 128
D_FF = 4 * D_MODEL
EPS = 1e-6
LOG2E = 1.4426950408889634

LANES = 128
SUBLANES = 8
VMEM_LIMIT_CAP = 56 * 1024 * 1024

PROMPT_CHUNK = 256
PROMPT_ROWS = 2
PROJ_COLS = 256
SAMPLE_PAD = SUBLANES
SAMPLE_NB = 16
ATTN_ROWS = 1024
ROW_TILE = 512
FF_TILE = 1024


def _vmem_limit(*block_bytes):
    need = 4 * sum(block_bytes) + (8 << 20)
    return int(min(max(need, 32 << 20), VMEM_LIMIT_CAP))


def _nbytes(shape, dtype):
    n = 1
    for s in shape:
        n *= s
    return n * jnp.dtype(dtype).itemsize


def _rms(x, g):
    return x * lax.rsqrt(jnp.mean(x * x, axis=-1, keepdims=True) + EPS) * g


def _dot(a, b):
    return jnp.dot(a, b, preferred_element_type=F32)


def _split3(x):
    hi = x.astype(BF16)
    r1 = x - hi.astype(F32)
    mid = r1.astype(BF16)
    lo = (r1 - mid.astype(F32)).astype(BF16)
    return hi, mid, lo


def _dot_exactish(m01, x):
    hi, mid, lo = _split3(x)
    return _dot(m01, hi) + _dot(m01, mid) + _dot(m01, lo)


def _gate_act(z):
    cap = GATE_SOFTCAP * jnp.tanh(z * (1.0 / GATE_SOFTCAP))
    lsig = jnp.minimum(cap, 0.0) - jnp.log1p(jnp.exp(-jnp.abs(cap)))
    lane = lax.broadcasted_iota(jnp.int32, z.shape, 1)
    return jnp.where(lane < ML_HEADS, cap, lsig)


_DONE = object()


def _rounds(chains):
    live = list(chains)
    rnd = 0
    while live:
        for item in list(live):
            gen, start = item
            if rnd >= start and next(gen, _DONE) is _DONE:
                live.remove(item)
        rnd += 1
        yield


def _run(chains):
    for _ in _rounds(chains):
        pass


def _head_out(hh, po, hg):
    hn = hh * lax.rsqrt(jnp.mean(hh * hh, axis=-1, keepdims=True) + EPS) * hg
    return jax.nn.sigmoid(po) * hn


def _head_slices(p, h):
    q = p[:, h * ML_DK:(h + 1) * ML_DK]
    k = p[:, ML_QK + h * ML_DK:ML_QK + (h + 1) * ML_DK] * (ML_DK ** -0.5)
    v = p[:, 2 * ML_QK + h * ML_DV:2 * ML_QK + (h + 1) * ML_DV]
    po = p[:, 2 * ML_QK + ML_VO + h * ML_DV:2 * ML_QK + ML_VO + (h + 1) * ML_DV]
    return q, k, v, po


def _prompt_mixer_kernel(xc_ref, xnext_ref, g_ref, wqkvo_ref, wg_ref, bg_ref, hg_ref, wout_ref,
                         y_ref, C_ref, n_ref, m_ref, p_s, gz_s, *, chunks_per_seq):
    R, T = xc_ref.shape[0], xc_ref.shape[1]
    f = pl.program_id(0)
    slot = f % 2

    def project(x_ref, s, r):
        xn = _rms(x_ref[r], g_ref[...]).astype(BF16)
        yield
        for j in range(ML_QKVO // PROJ_COLS):
            cols = slice(j * PROJ_COLS, (j + 1) * PROJ_COLS)
            p_s[s, r, :, cols] = _dot(xn, wqkvo_ref[:, cols])
            yield
        gz_s[s, r] = _dot(xn, wg_ref[...])
        yield

    @pl.when(f == 0)
    def _():
        _run([(project(xc_ref, 0, r), 0) for r in range(R)])

    @pl.when(f % chunks_per_seq == 0)
    def _():
        C_ref[...] = jnp.zeros_like(C_ref)
        n_ref[...] = jnp.zeros_like(n_ref)
        m_ref[...] = jnp.zeros_like(m_ref)

    def recurrence(r):
        G = _gate_act(gz_s[slot, r] + bg_ref[...])
        row = lax.broadcasted_iota(jnp.int32, (T, T), 0)
        col = lax.broadcasted_iota(jnp.int32, (T, T), 1)
        causal = col <= row
        Bc = _dot_exactish(jnp.where(causal, 1.0, 0.0).astype(BF16), G)
        yield
        Gt = G.T
        Bt = Bc.T
        yield
        parts = [None] * ML_HEADS

        def head(h):
            def cols(base, width):
                return p_s[slot, r, :, base + h * width:base + (h + 1) * width]

            q = cols(0, ML_DK)
            k = cols(ML_QK, ML_DK) * (ML_DK ** -0.5)
            v = cols(2 * ML_QK, ML_DV)
            qc, kc, vc = q.astype(BF16), k.astype(BF16), v.astype(BF16)
            qk = lax.dot_general(qc, kc, (((1,), (1,)), ((), ())), preferred_element_type=F32)
            yield
            li_col = G[:, h:h + 1]
            b_col = Bc[:, ML_HEADS + h:ML_HEADS + h + 1]
            dmat = jnp.where(causal, b_col - Bt[ML_HEADS + h:ML_HEADS + h + 1, :] + Gt[h:h + 1, :], -jnp.inf)
            m_prev = m_ref[r, h:h + 1, 0:1]
            inter = b_col + m_prev
            m_t = jnp.maximum(inter, jnp.max(dmat, axis=1, keepdims=True))
            yield
            s = qk * jnp.exp(dmat - m_t)
            a_inter = jnp.exp(inter - m_t)
            den = (jnp.sum(s, axis=1, keepdims=True)
                   + a_inter * jnp.sum(q * n_ref[r, h:h + 1, :], axis=1, keepdims=True))
            yield
            C = C_ref[r, h]
            num = _dot(s.astype(BF16), vc) + a_inter * _dot(qc, C.astype(BF16))
            yield
            hh = num * (1.0 / jnp.maximum(jnp.abs(den), jnp.exp(-m_t)))
            hs = _head_out(hh, cols(2 * ML_QK + ML_VO, ML_DV), hg_ref[:, h * ML_DV:(h + 1) * ML_DV])
            yield
            parts[h] = _dot(hs.astype(BF16), wout_ref[h * ML_DV:(h + 1) * ML_DV, :])
            yield
            m_new = m_t[T - 1:T, :]
            b_last = b_col[T - 1:T, :]
            decay = jnp.exp(b_last + m_prev - m_new)
            kw = k * jnp.exp(b_last - b_col + li_col - m_new)
            C_ref[r, h] = decay * C + lax.dot_general(kw.astype(BF16), vc, (((0,), (0,)), ((), ())),
                                                      preferred_element_type=F32)
            n_ref[r, h:h + 1, :] = decay * n_ref[r, h:h + 1, :] + jnp.sum(kw, axis=0, keepdims=True)
            m_ref[r, h:h + 1, :] = jnp.broadcast_to(m_new, (1, LANES))
            yield

        yield from _rounds([(head(h), h) for h in range(ML_HEADS)])
        y_ref[r] = xc_ref[r] + ((parts[0] + parts[1]) + (parts[2] + parts[3]))

    _run([(project(xnext_ref, 1 - slot, r), 0) for r in range(R)] + [(recurrence(r), r) for r in range(R)])


def _prompt_mixer(x, g, wqkvo, wg, bg, hg, wout):
    B, L, D = x.shape
    T = PROMPT_CHUNK
    assert L % T == 0
    R = PROMPT_ROWS
    assert B % R == 0
    nc = L // T
    steps = (B // R) * nc
    const = lambda f: (0, 0)
    cur = lambda f: (f // nc, f % nc, 0)
    nxt = lambda f: (jnp.minimum(f + 1, steps - 1) // nc, jnp.minimum(f + 1, steps - 1) % nc, 0)
    once = dict(pipeline_mode=pl.Buffered(1))
    vmem = _vmem_limit(3 * _nbytes((R, T, D), F32), _nbytes(wqkvo.shape, BF16) // 2,
                       _nbytes(wout.shape, BF16) // 2, _nbytes((R, ML_HEADS, ML_DK, ML_DV), F32),
                       2 * _nbytes((R, T, ML_QKVO), F32))
    y, C, n, m = pl.pallas_call(
        functools.partial(_prompt_mixer_kernel, chunks_per_seq=nc),
        grid=(steps,),
        in_specs=[
            pl.BlockSpec((R, T, D), cur),
            pl.BlockSpec((R, T, D), nxt),
            pl.BlockSpec((1, D), const),
            pl.BlockSpec((D, ML_QKVO), const, **once),
            pl.BlockSpec((D, LANES), const, **once),
            pl.BlockSpec((1, LANES), const),
            pl.BlockSpec((1, ML_VO), const),
            pl.BlockSpec((ML_VO, D), const, **once),
        ],
        out_specs=[
            pl.BlockSpec((R, T, D), cur),
            pl.BlockSpec((R, ML_HEADS, ML_DK, ML_DV), lambda f: (f // nc, 0, 0, 0)),
            pl.BlockSpec((R, ML_HEADS, ML_DK), lambda f: (f // nc, 0, 0)),
            pl.BlockSpec((R, ML_HEADS, LANES), lambda f: (f // nc, 0, 0)),
        ],
        out_shape=[
            jax.ShapeDtypeStruct((B, L, D), F32),
            jax.ShapeDtypeStruct((B, ML_HEADS, ML_DK, ML_DV), F32),
            jax.ShapeDtypeStruct((B, ML_HEADS, ML_DK), F32),
            jax.ShapeDtypeStruct((B, ML_HEADS, LANES), F32),
        ],
        scratch_shapes=[pltpu.VMEM((2, R, T, ML_QKVO), F32), pltpu.VMEM((2, R, T, LANES), F32)],
        compiler_params=pltpu.CompilerParams(
            dimension_semantics=("arbitrary",), vmem_limit_bytes=vmem),
        name="prompt_mlstm_mixer",
    )(x, x, g, wqkvo, wg, bg, hg, wout)
    return y, C, n, m


def _sample_mixer_kernel(x_ref, g_ref, wqkvo_ref, wg_ref, bg_ref, hg_ref, wout_ref,
                         C0_ref, n0_ref, m0_ref,
                         y_ref, C_ref, n_ref, m_ref, hs_s, *, n_valid):
    T = SAMPLE_PAD
    nb = C0_ref.shape[0]
    R = nb * T
    x = x_ref[...]
    xn = _rms(x, g_ref[...]).astype(BF16)
    p = _dot(xn, wqkvo_ref[...])

    rowt = lax.broadcasted_iota(jnp.int32, (R, LANES), 0) % T
    lane = lax.broadcasted_iota(jnp.int32, (R, LANES), 1)
    G = jnp.where(rowt < n_valid, _gate_act(_dot(xn, wg_ref[...]) + bg_ref[...]),
                  jnp.where(lane < ML_HEADS, -jnp.inf, 0.0))
    row = lax.broadcasted_iota(jnp.int32, (R, R), 0)
    col = lax.broadcasted_iota(jnp.int32, (R, R), 1)
    causal = (row // T == col // T) & (col <= row)
    Bc = _dot_exactish(jnp.where(causal, 1.0, 0.0).astype(BF16), jnp.where(lane < ML_HEADS, 0.0, G))
    Gt = G.T
    Bt = Bc.T

    def per_seq(fn):
        return jnp.concatenate([fn(a) for a in range(nb)], axis=0)

    def seq_last(colvec):
        return per_seq(lambda a: jnp.broadcast_to(colvec[T * a + T - 1:T * a + T, :], (T, 1)))

    def head(h):
        q, k, v, po = _head_slices(p, h)
        li_col = G[:, h:h + 1]
        b_col = Bc[:, ML_HEADS + h:ML_HEADS + h + 1]
        li_row = Gt[h:h + 1, :]
        b_row = Bt[ML_HEADS + h:ML_HEADS + h + 1, :]
        dmat = jnp.where(causal, b_col - (b_row - li_row), -jnp.inf)
        m_prev = per_seq(lambda a: jnp.broadcast_to(m0_ref[a, h:h + 1, 0:1], (T, 1)))
        n_rows = per_seq(lambda a: jnp.broadcast_to(n0_ref[a, h:h + 1, :], (T, ML_DK)))
        qc, kc, vc = q.astype(BF16), k.astype(BF16), v.astype(BF16)
        qk = lax.dot_general(qc, kc, (((1,), (1,)), ((), ())), preferred_element_type=F32)
        yield
        inter = b_col + m_prev
        m_t = jnp.maximum(inter, jnp.max(dmat, axis=1, keepdims=True))
        s = qk * jnp.exp(dmat - m_t)
        a_inter = jnp.exp(inter - m_t)
        yield
        qC = per_seq(lambda a: _dot(q[T * a:T * a + T, :], C0_ref[a, h]))
        yield
        num = _dot(s.astype(BF16), vc) + a_inter * qC
        den = jnp.sum(s, axis=1, keepdims=True) + a_inter * jnp.sum(q * n_rows, axis=1, keepdims=True)
        hh = num * (1.0 / jnp.maximum(jnp.abs(den), jnp.exp(-m_t)))
        hs_s[:, h * ML_DV:(h + 1) * ML_DV] = _head_out(
            hh, po, hg_ref[:, h * ML_DV:(h + 1) * ML_DV]).astype(BF16)
        yield
        m_new = seq_last(m_t)
        b_last = seq_last(b_col)
        decay = jnp.exp(b_last + m_prev - m_new)
        kw = k * jnp.exp(b_last - b_col + li_col - m_new)
        for a in range(nb):
            dec = decay[T * a:T * a + 1, :]
            kw_a = kw[T * a:T * a + T, :]
            upd = lax.dot_general(kw_a, v[T * a:T * a + T, :], (((0,), (0,)), ((), ())),
                                  preferred_element_type=F32)
            C_ref[a, h] = dec * C0_ref[a, h] + upd
            n_ref[a, h:h + 1, :] = dec * n0_ref[a, h:h + 1, :] + jnp.sum(kw_a, axis=0, keepdims=True)
            m_ref[a, h:h + 1, :] = jnp.broadcast_to(m_new[T * a:T * a + 1, :], (1, LANES))
            if a % 4 == 3:
                yield

    _run([(head(h), h) for h in range(ML_HEADS)])
    y_ref[...] = x + _dot(hs_s[...], wout_ref[...])


def _sample_mixer(x, g, wqkvo, wg, bg, hg, wout, C0, n0, m0, n_valid):
    NBT, D = x.shape
    nseq = C0.shape[0]
    nb = SAMPLE_NB
    R = nb * SAMPLE_PAD
    assert nseq % nb == 0 and NBT == nseq * SAMPLE_PAD
    const = lambda i: (0, 0)
    state_specs = [
        pl.BlockSpec((nb, ML_HEADS, ML_DK, ML_DV), lambda i: (i, 0, 0, 0)),
        pl.BlockSpec((nb, ML_HEADS, ML_DK), lambda i: (i, 0, 0)),
        pl.BlockSpec((nb, ML_HEADS, LANES), lambda i: (i, 0, 0)),
    ]
    vmem = _vmem_limit(2 * _nbytes((R, D), F32), _nbytes(wqkvo.shape, BF16), _nbytes(wout.shape, BF16),
                       2 * _nbytes((nb, ML_HEADS, ML_DK, ML_DV), F32), _nbytes((R, ML_QKVO), F32))
    return pl.pallas_call(
        functools.partial(_sample_mixer_kernel, n_valid=n_valid),
        grid=(nseq // nb,),
        in_specs=[
            pl.BlockSpec((R, D), lambda i: (i, 0)),
            pl.BlockSpec((1, D), const),
            pl.BlockSpec((D, ML_QKVO), const),
            pl.BlockSpec((D, LANES), const),
            pl.BlockSpec((1, LANES), const),
            pl.BlockSpec((1, ML_VO), const),
            pl.BlockSpec((ML_VO, D), const),
        ] + state_specs,
        out_specs=[pl.BlockSpec((R, D), lambda i: (i, 0))] + state_specs,
        out_shape=[
            jax.ShapeDtypeStruct((NBT, D), F32),
            jax.ShapeDtypeStruct(C0.shape, F32),
            jax.ShapeDtypeStruct(n0.shape, F32),
            jax.ShapeDtypeStruct(m0.shape, F32),
        ],
        scratch_shapes=[pltpu.VMEM((R, ML_VO), BF16)],
        compiler_params=pltpu.CompilerParams(
            dimension_semantics=("arbitrary",), vmem_limit_bytes=vmem),
        name="sample_mlstm_mixer",
    )(x, g, wqkvo, wg, bg, hg, wout, C0, n0, m0)


def _dot_exactish_right(x, m01):
    hi, mid, lo = _split3(x)
    return _dot(hi, m01) + _dot(mid, m01) + _dot(lo, m01)


def _mlp_kernel(x_ref, g_ref, w1_ref, w2_ref, *rest):
    x = x_ref[...]
    xn = _rms(x, g_ref[...]).astype(BF16)
    acc = x
    for c in range(D_FF // FF_TILE):
        hcol = _dot(xn, w1_ref[:, c * FF_TILE:(c + 1) * FF_TILE])
        hcol = jnp.square(jnp.maximum(hcol, 0.0)).astype(BF16)
        acc = acc + _dot(hcol, w2_ref[c * FF_TILE:(c + 1) * FF_TILE, :])
    if len(rest) == 1:
        (y_ref,) = rest
        y_ref[...] = acc
        return
    gkv_ref, wkv_ref, kg_ref, y_ref, k_ref, v_ref = rest
    y_ref[...] = acc
    kv = _dot(_rms(acc, gkv_ref[...]).astype(BF16), wkv_ref[...])
    kraw = kv[:, :ATT_KV]
    v_ref[...] = kv[:, ATT_KV:]
    r = lax.broadcasted_iota(jnp.int32, (ATT_KV, ATT_KV), 0) // ATT_HD
    c = lax.broadcasted_iota(jnp.int32, (ATT_KV, ATT_KV), 1) // ATT_HD
    seg = jnp.where(r == c, 1.0, 0.0).astype(BF16)
    ss = _dot_exactish_right(kraw * kraw, seg)
    k_ref[...] = kraw * lax.rsqrt(ss * (1.0 / ATT_HD) + EPS) * kg_ref[...]


def _mlp(x, g, w1, w2, kv=None):
    N, D = x.shape
    tm = min(ROW_TILE, N)
    assert N % tm == 0
    const = lambda i: (0, 0)
    rows = lambda i: (i, 0)
    once = dict(pipeline_mode=pl.Buffered(1))
    in_specs = [
        pl.BlockSpec((tm, D), rows),
        pl.BlockSpec((1, D), const),
        pl.BlockSpec((D, D_FF), const, **once),
        pl.BlockSpec((D_FF, D), const, **once),
    ]
    out_specs = [pl.BlockSpec((tm, D), rows)]
    out_shape = [jax.ShapeDtypeStruct((N, D), F32)]
    args = [x, g, w1, w2]
    if kv is not None:
        in_specs += [pl.BlockSpec((1, D), const), pl.BlockSpec((D, 2 * ATT_KV), const, **once),
                     pl.BlockSpec((1, ATT_KV), const)]
        out_specs += [pl.BlockSpec((tm, ATT_KV), rows)] * 2
        out_shape += [jax.ShapeDtypeStruct((N, ATT_KV), F32)] * 2
        args += list(kv)
    vmem = _vmem_limit(2 * _nbytes((tm, D), F32), _nbytes(w1.shape, BF16) // 2, _nbytes(w2.shape, BF16) // 2,
                       2 * _nbytes((tm, FF_TILE), F32))
    out = pl.pallas_call(
        _mlp_kernel,
        grid=(N // tm,),
        in_specs=in_specs,
        out_specs=out_specs,
        out_shape=out_shape,
        compiler_params=pltpu.CompilerParams(
            dimension_semantics=("arbitrary",), vmem_limit_bytes=vmem),
        name="sqrelu_mlp",
    )(*args)
    return out[0] if kv is None else out


def _pair_queries(qraw, pr, qscale, col0=0):
    TQ = qraw.shape[0]
    lo = lax.broadcasted_iota(jnp.int32, (TQ, LANES), 1) < ATT_HD
    out = []
    for e in range(2):
        kvh = 2 * pr + e
        for g in range(ATT_GROUP):
            cc, half = divmod(g, 2)
            c0 = (2 * kvh + cc) * LANES - col0
            q2 = qraw[:, c0:c0 + LANES]
            qm = jnp.where(lo, q2, 0.0) if half == 0 else jnp.where(lo, 0.0, q2)
            ss = jnp.sum(qm * qm, axis=1, keepdims=True)
            qn = qm * lax.rsqrt(ss * (1.0 / ATT_HD) + EPS)
            if qscale is not None:
                qn = qn * qscale
            out.append(qn if half == e else pltpu.roll(qn, ATT_HD, 1))
    return out


def _pair_outputs(o_heads, rden, pr, store):
    TQ = o_heads[0].shape[0]
    lo = lax.broadcasted_iota(jnp.int32, (TQ, LANES), 1) < ATT_HD
    for e in range(2):
        kvh = 2 * pr + e
        for cc in range(2):
            tiles = []
            for half in range(2):
                o = o_heads[e * ATT_GROUP + 2 * cc + half] * rden[e * ATT_GROUP + 2 * cc + half]
                tiles.append(o if half == e else pltpu.roll(o, ATT_HD, 1))
            store((2 * kvh + cc) * LANES, jnp.where(lo, tiles[0], tiles[1]))


def _softmax_with_sink(parts, sink):
    M = sink
    for s in parts:
        M = jnp.maximum(M, jnp.max(s, axis=1, keepdims=True))
    ps = [jnp.exp2(s - M) for s in parts]
    den = jnp.exp2(sink - M)
    for p in ps:
        den = den + jnp.sum(p, axis=1, keepdims=True)
    return ps, 1.0 / den


def _prompt_attn_kernel(sinks_ref, x_ref, g_ref, wq_ref, qg_ref, kp_ref, kc_ref, vp_ref, vc_ref, wo_ref,
                        y_ref):
    TQ = WINDOW
    TR = x_ref.shape[0]
    kj = lax.broadcasted_iota(jnp.int32, (WINDOW + TQ, TQ), 0)
    qi = lax.broadcasted_iota(jnp.int32, (WINDOW + TQ, TQ), 1)
    band = (qi + WINDOW - kj >= 0) & (qi - kj <= 0)
    band_first = band & ((kj >= WINDOW) | (pl.program_id(1) > 0))
    kscale = qg_ref[...] * (ATT_HD ** -0.5 * LOG2E)
    slabs = []
    for pr in range(ATT_KVH // 2):
        sl = slice(pr * LANES, (pr + 1) * LANES)
        slabs.append(((jnp.concatenate([kp_ref[:, sl], kc_ref[:, sl]], axis=0) * kscale).astype(BF16),
                      jnp.concatenate([vp_ref[:, sl], vc_ref[:, sl]], axis=0).T))

    PW = 2 * ATT_GROUP * ATT_HD
    zeros_hd = jnp.zeros((ATT_HD, TQ), F32)
    ones_rows = jnp.ones((2 * SUBLANES, WINDOW + TQ), F32)

    def chain(sb):
        rows = slice(sb * TQ, (sb + 1) * TQ)
        keys = slice(sb * TQ, sb * TQ + WINDOW + TQ)
        mask = band_first if sb == 0 else band
        x = x_ref[rows, :]
        xn = _rms(x, g_ref[...]).astype(BF16)
        acc = x
        for pr in range(ATT_KVH // 2):
            kslab, vslab_t = slabs[pr]
            qt = _dot(xn, wq_ref[:, pr * PW:(pr + 1) * PW]).T
            yield
            tiles, sink_rows = [], []
            for e in range(2):
                for g in range(ATT_GROUP):
                    blk = qt[(e * ATT_GROUP + g) * ATT_HD:(e * ATT_GROUP + g + 1) * ATT_HD, :]
                    qn = blk * lax.rsqrt(jnp.sum(blk * blk, axis=0, keepdims=True) * (1.0 / ATT_HD) + EPS)
                    tiles.append(jnp.concatenate([qn, zeros_hd] if e == 0 else [zeros_hd, qn], axis=0))
                    sink_rows.append(jnp.full((1, TQ), sinks_ref[(2 * pr + e) * ATT_GROUP + g] * LOG2E, F32))
            qmat = jnp.concatenate(tiles, axis=1).astype(BF16)
            st = _dot(kslab[keys, :], qmat)
            yield
            parts = []
            for e in range(2):
                ps, ms = [], []
                for g in range(ATT_GROUP):
                    c0 = (e * ATT_GROUP + g) * TQ
                    s_h = jnp.where(mask, st[:, c0:c0 + TQ], -jnp.inf)
                    m_h = jnp.maximum(jnp.max(s_h, axis=0, keepdims=True), sink_rows[e * ATT_GROUP + g])
                    ps.append(jnp.exp2(s_h - m_h).astype(BF16))
                    ms.append(m_h)
                p_e = jnp.concatenate(ps, axis=1)
                v_aug = jnp.concatenate([vslab_t[e * ATT_HD:(e + 1) * ATT_HD, keys], ones_rows],
                                        axis=0).astype(BF16)
                ot = _dot(v_aug, p_e)
                yield
                sink_term = jnp.exp2(jnp.concatenate(sink_rows[e * ATT_GROUP:(e + 1) * ATT_GROUP], axis=1)
                                     - jnp.concatenate(ms, axis=1))
                on = ot[0:ATT_HD, :] * (1.0 / (ot[ATT_HD:ATT_HD + 1, :] + sink_term))
                parts += [on[:, g * TQ:(g + 1) * TQ] for g in range(ATT_GROUP)]
            o_pair = jnp.concatenate(parts, axis=0).T.astype(BF16)
            acc = acc + _dot(o_pair, wo_ref[pr * PW:(pr + 1) * PW, :])
            yield
        y_ref[rows, :] = acc

    _run([(chain(sb), sb) for sb in range(TR // TQ)])


def _prompt_attn(x, k, v, sinks, g, wq, qg, wo):
    B, L, D = x.shape
    TR = ATTN_ROWS
    per = TR // WINDOW
    assert L % TR == 0 and TR % WINDOW == 0
    const = lambda b, i: (0, 0)
    cur = lambda b, i: (b, i, 0)
    prev = lambda b, i: (b, jnp.maximum(i * per - 1, 0), 0)
    vmem = _vmem_limit(2 * _nbytes((TR, D), F32), _nbytes(wq.shape, BF16), _nbytes(wo.shape, BF16),
                       4 * _nbytes((TR, ATT_KV), F32), 8 * _nbytes((8 * WINDOW, 2 * WINDOW), F32))
    return pl.pallas_call(
        _prompt_attn_kernel,
        grid=(B, L // TR),
        in_specs=[
            pl.BlockSpec(memory_space=pltpu.SMEM),
            pl.BlockSpec((None, TR, D), cur),
            pl.BlockSpec((1, D), const),
            pl.BlockSpec((D, D), const),
            pl.BlockSpec((1, LANES), const),
            pl.BlockSpec((None, WINDOW, ATT_KV), prev),
            pl.BlockSpec((None, TR, ATT_KV), cur),
            pl.BlockSpec((None, WINDOW, ATT_KV), prev),
            pl.BlockSpec((None, TR, ATT_KV), cur),
            pl.BlockSpec((D, D), const),
        ],
        out_specs=pl.BlockSpec((None, TR, D), cur),
        out_shape=jax.ShapeDtypeStruct((B, L, D), F32),
        compiler_params=pltpu.CompilerParams(
            dimension_semantics=("arbitrary", "arbitrary"), vmem_limit_bytes=vmem),
        name="prompt_window_attention",
    )(sinks, x, g, wq, qg, k, k, v, v, wo)


def _sample_attn_kernel(sinks_ref, x_ref, g_ref, wq_ref, qg_ref, kc_ref, vc_ref, wk_ref, wv_ref, wo_ref,
                        y_ref, nk_ref, nv_ref, o_s, *, n_valid):
    T = SAMPLE_PAD
    nb = wk_ref.shape[0]
    R = nb * T
    x = x_ref[...]
    q = _dot(_rms(x, g_ref[...]).astype(BF16), wq_ref[...])
    qscale = qg_ref[...] * (ATT_HD ** -0.5 * LOG2E)
    knew = kc_ref[...]
    vnew = vc_ref[...]

    cmask = (lax.broadcasted_iota(jnp.int32, (R, WINDOW), 1)
             >= lax.broadcasted_iota(jnp.int32, (R, WINDOW), 0) % T)
    row = lax.broadcasted_iota(jnp.int32, (R, R), 0)
    col = lax.broadcasted_iota(jnp.int32, (R, R), 1)
    nmask = (row // T == col // T) & (col % T <= row % T) & (col % T < n_valid)

    def store(off, val):
        o_s[:, off:off + LANES] = val.astype(BF16)

    def regroup(per_seq, i):
        return jnp.concatenate([per_seq[a][i * T:(i + 1) * T, :] for a in range(nb)], axis=0)

    for pr in range(ATT_KVH // 2):
        sl = slice(pr * LANES, (pr + 1) * LANES)
        qs = _pair_queries(q, pr, qscale)
        S_new = lax.dot_general(jnp.concatenate(qs, axis=0).astype(BF16), knew[:, sl].astype(BF16),
                                (((1,), (1,)), ((), ())), preferred_element_type=F32)
        sc = []
        for a in range(nb):
            q_a = jnp.concatenate([qi[T * a:T * a + T, :] for qi in qs], axis=0).astype(BF16)
            sc.append(lax.dot_general(q_a, wk_ref[a, :, sl].astype(BF16), (((1,), (1,)), ((), ())),
                                      preferred_element_type=F32))
        pcs, pns, rden = [], [], []
        for i in range(2 * ATT_GROUP):
            s_c = jnp.where(cmask, regroup(sc, i), -jnp.inf)
            s_n = jnp.where(nmask, S_new[i * R:(i + 1) * R, :], -jnp.inf)
            (p_c, p_n), r = _softmax_with_sink([s_c, s_n], sinks_ref[pr * 2 * ATT_GROUP + i] * LOG2E)
            pcs.append(p_c)
            pns.append(p_n)
            rden.append(r)
        O_new = _dot(jnp.concatenate(pns, axis=0).astype(BF16), vnew[:, sl].astype(BF16))
        oc = []
        for a in range(nb):
            p_a = jnp.concatenate([pc[T * a:T * a + T, :] for pc in pcs], axis=0).astype(BF16)
            oc.append(_dot(p_a, wv_ref[a, :, sl].astype(BF16)))
        _pair_outputs([O_new[i * R:(i + 1) * R, :] + regroup(oc, i) for i in range(2 * ATT_GROUP)],
                      rden, pr, store)

    for a in range(nb):
        nk_ref[a, 0:WINDOW - n_valid, :] = wk_ref[a, n_valid:WINDOW, :]
        nk_ref[a, WINDOW - n_valid:WINDOW, :] = knew[T * a:T * a + n_valid, :]
        nv_ref[a, 0:WINDOW - n_valid, :] = wv_ref[a, n_valid:WINDOW, :]
        nv_ref[a, WINDOW - n_valid:WINDOW, :] = vnew[T * a:T * a + n_valid, :]

    y_ref[...] = x + _dot(o_s[...], wo_ref[...])


def _sample_attn(x, k, v, win_k, win_v, sinks, g, wq, qg, wo, n_valid):
    NBT, D = x.shape
    nseq = win_k.shape[0]
    nb = SAMPLE_NB
    R = nb * SAMPLE_PAD
    assert nseq % nb == 0 and NBT == nseq * SAMPLE_PAD
    const = lambda i: (0, 0)
    rows = lambda i: (i, 0)
    cache = pl.BlockSpec((nb, WINDOW, ATT_KV), lambda i: (i, 0, 0))
    vmem = _vmem_limit(2 * _nbytes((R, D), F32), _nbytes(wq.shape, BF16), _nbytes(wo.shape, BF16),
                       4 * _nbytes((nb, WINDOW, ATT_KV), F32), 2 * _nbytes((R, D), F32))
    return pl.pallas_call(
        functools.partial(_sample_attn_kernel, n_valid=n_valid),
        grid=(nseq // nb,),
        in_specs=[
            pl.BlockSpec(memory_space=pltpu.SMEM),
            pl.BlockSpec((R, D), rows),
            pl.BlockSpec((1, D), const),
            pl.BlockSpec((D, D), const),
            pl.BlockSpec((1, LANES), const),
            pl.BlockSpec((R, ATT_KV), rows),
            pl.BlockSpec((R, ATT_KV), rows),
            cache, cache,
            pl.BlockSpec((D, D), const),
        ],
        out_specs=[pl.BlockSpec((R, D), rows), cache, cache],
        out_shape=[jax.ShapeDtypeStruct((NBT, D), F32),
                   jax.ShapeDtypeStruct(win_k.shape, F32), jax.ShapeDtypeStruct(win_v.shape, F32)],
        scratch_shapes=[pltpu.VMEM((R, D), BF16)],
        compiler_params=pltpu.CompilerParams(
            dimension_semantics=("arbitrary",), vmem_limit_bytes=vmem),
        name="sample_window_attention",
    )(sinks, x, g, wq, qg, k, v, win_k, win_v, wo)


def kernel(x_prompt, x_sample, state_mlstm_C, state_mlstm_n, state_mlstm_m, cache_win_k, cache_win_v,
           ml_norm_g, ml_w_in, ml_b_i, ml_b_f, ml_head_g, ml_w_out, kv_norm_g, w_kv, k_norm_g,
           att_norm_g, att_w_q, q_norm_g, att_sinks, att_w_o, mlp_norm_g, mlp_w1, mlp_w2):
    B, L, D = x_prompt.shape
    NS, LS, _ = x_sample.shape
    assert ml_w_in.shape[0] == 1 and att_w_q.shape[0] == 1 and mlp_w1.shape[0] == 2

    w_in = ml_w_in[0]
    wqkvo = w_in[:, :ML_QKVO].astype(BF16)
    wg = jnp.pad(w_in[:, ML_QKVO:], ((0, 0), (0, LANES - 2 * ML_HEADS))).astype(BF16)
    bg = jnp.pad(jnp.concatenate([ml_b_i[0], ml_b_f[0]]), (0, LANES - 2 * ML_HEADS)).reshape(1, LANES)
    ml_g = ml_norm_g[0].reshape(1, D)
    hg = ml_head_g[0].reshape(1, ML_VO)
    wout = ml_w_out[0].astype(BF16)
    w1 = [mlp_w1[layer].astype(BF16) for layer in range(2)]
    w2 = [mlp_w2[layer].astype(BF16) for layer in range(2)]
    mlp_g = mlp_norm_g.reshape(2, 1, D)
    wkv = w_kv.astype(BF16)
    kv_g = kv_norm_g.reshape(1, D)
    kg = jnp.tile(k_norm_g, ATT_KVH).reshape(1, ATT_KV)
    att_g = att_norm_g[0].reshape(1, D)
    wq = att_w_q[0].astype(BF16)
    qg = jnp.tile(q_norm_g[0], 2).reshape(1, LANES)
    sinks = att_sinks[0]
    wo = att_w_o[0].astype(BF16)

    xp, p_C, p_n, p_m = _prompt_mixer(x_prompt, ml_g, wqkvo, wg, bg, hg, wout)
    xp, kp, vp = _mlp(xp.reshape(B * L, D), mlp_g[0], w1[0], w2[0], kv=(kv_g, wkv, kg))
    xp = _prompt_attn(xp.reshape(B, L, D), kp.reshape(B, L, ATT_KV), vp.reshape(B, L, ATT_KV),
                      sinks, att_g, wq, qg, wo)
    y_prompt = _mlp(xp.reshape(B * L, D), mlp_g[1], w1[1], w2[1]).reshape(B, L, D)
    p_wk = kp.reshape(B, L, ATT_KV)[:, L - WINDOW:].reshape(B, WINDOW, ATT_KVH, ATT_HD)
    p_wv = vp.reshape(B, L, ATT_KV)[:, L - WINDOW:].reshape(B, WINDOW, ATT_KVH, ATT_HD)

    xs = jnp.pad(x_sample, ((0, 0), (0, SAMPLE_PAD - LS), (0, 0))).reshape(NS * SAMPLE_PAD, D)
    m0 = jnp.broadcast_to(state_mlstm_m[0][:, :, None], (NS, ML_HEADS, LANES))
    xs, s_C, s_n, s_m = _sample_mixer(xs, ml_g, wqkvo, wg, bg, hg, wout,
                                      state_mlstm_C[0], state_mlstm_n[0], m0, LS)
    xs, ks, vs = _mlp(xs, mlp_g[0], w1[0], w2[0], kv=(kv_g, wkv, kg))
    xs, s_wk, s_wv = _sample_attn(xs, ks, vs, cache_win_k.reshape(NS, WINDOW, ATT_KV),
                                  cache_win_v.reshape(NS, WINDOW, ATT_KV), sinks, att_g, wq, qg, wo, LS)
    y_sample = _mlp(xs, mlp_g[1], w1[1], w2[1]).reshape(NS, SAMPLE_PAD, D)[:, :LS]

    return (y_prompt, y_sample,
            p_C[None], p_n[None], p_m[None, :, :, 0], p_wk, p_wv,
            s_C[None], s_n[None], s_m[None, :, :, 0],
            s_wk.reshape(NS, WINDOW, ATT_KVH, ATT_HD), s_wv.reshape(NS, WINDOW, ATT_KVH, ATT_HD))
```

```python
import functools

import jax
import jax.numpy as jnp
from jax import lax
from jax.experimental import pallas as pl
from jax.experimental.pallas import tpu as pltpu

F32 = jnp.float32
BF16 = jnp.bfloat16

D_MODEL = 1024
ML_HEADS = 4
ML_DK = 128
ML_DV = 256
ML_QK = ML_HEADS * ML_DK
ML_VO = ML_HEADS * ML_DV
ML_QKVO = 2 * ML_QK + 2 * ML_VO
GATE_SOFTCAP = 15.0
ATT_HD = 64
ATT_QH = 16
ATT_KVH = 4
ATT_GROUP = 4
ATT_KV = ATT_KVH * ATT_HD
WINDOW = 128
D_FF = 4 * D_MODEL
EPS = 1e-6
LOG2E = 1.4426950408889634

LANES = 128
SUBLANES = 8
VMEM_LIMIT_CAP = 56 * 1024 * 1024

PROMPT_CHUNK = 256
PROMPT_ROWS = 2
PROJ_COLS = 256
SAMPLE_PAD = SUBLANES
SAMPLE_NB = 16
ATTN_ROWS = 1024
ROW_TILE = 512
FF_TILE = 1024


def _vmem_limit(*block_bytes):
    need = 4 * sum(block_bytes) + (8 << 20)
    return int(min(max(need, 32 << 20), VMEM_LIMIT_CAP))


def _nbytes(shape, dtype):
    n = 1
    for s in shape:
        n *= s
    return n * jnp.dtype(dtype).itemsize


def _rms(x, g):
    return x * lax.rsqrt(jnp.mean(x * x, axis=-1, keepdims=True) + EPS) * g


def _dot(a, b):
    return jnp.dot(a, b, preferred_element_type=F32)


def _split3(x):
    hi = x.astype(BF16)
    r1 = x - hi.astype(F32)
    mid = r1.astype(BF16)
    lo = (r1 - mid.astype(F32)).astype(BF16)
    return hi, mid, lo


def _dot_exactish(m01, x):
    hi, mid, lo = _split3(x)
    return _dot(m01, hi) + _dot(m01, mid) + _dot(m01, lo)


def _gate_act(z):
    cap = GATE_SOFTCAP * jnp.tanh(z * (1.0 / GATE_SOFTCAP))
    lsig = jnp.minimum(cap, 0.0) - jnp.log1p(jnp.exp(-jnp.abs(cap)))
    lane = lax.broadcasted_iota(jnp.int32, z.shape, 1)
    return jnp.where(lane < ML_HEADS, cap, lsig)


_DONE = object()


def _rounds(chains):
    live = list(chains)
    rnd = 0
    while live:
        for item in list(live):
            gen, start = item
            if rnd >= start and next(gen, _DONE) is _DONE:
                live.remove(item)
        rnd += 1
        yield


def _run(chains):
    for _ in _rounds(chains):
        pass


def _head_out(hh, po, hg):
    hn = hh * lax.rsqrt(jnp.mean(hh * hh, axis=-1, keepdims=True) + EPS) * hg
    return jax.nn.sigmoid(po) * hn


def _head_slices(p, h):
    q = p[:, h * ML_DK:(h + 1) * ML_DK]
    k = p[:, ML_QK + h * ML_DK:ML_QK + (h + 1) * ML_DK] * (ML_DK ** -0.5)
    v = p[:, 2 * ML_QK + h * ML_DV:2 * ML_QK + (h + 1) * ML_DV]
    po = p[:, 2 * ML_QK + ML_VO + h * ML_DV:2 * ML_QK + ML_VO + (h + 1) * ML_DV]
    return q, k, v, po


def _prompt_mixer_kernel(xc_ref, xnext_ref, g_ref, wqkvo_ref, wg_ref, bg_ref, hg_ref, wout_ref,
                         y_ref, C_ref, n_ref, m_ref, p_s, gz_s, *, chunks_per_seq):
    R, T = xc_ref.shape[0], xc_ref.shape[1]
    f = pl.program_id(0)
    slot = f % 2

    def project(x_ref, s, r):
        xn = _rms(x_ref[r], g_ref[...]).astype(BF16)
        yield
        for j in range(ML_QKVO // PROJ_COLS):
            cols = slice(j * PROJ_COLS, (j + 1) * PROJ_COLS)
            p_s[s, r, :, cols] = _dot(xn, wqkvo_ref[:, cols])
            yield
        gz_s[s, r] = _dot(xn, wg_ref[...])
        yield

    @pl.when(f == 0)
    def _():
        _run([(project(xc_ref, 0, r), 0) for r in range(R)])

    @pl.when(f % chunks_per_seq == 0)
    def _():
        C_ref[...] = jnp.zeros_like(C_ref)
        n_ref[...] = jnp.zeros_like(n_ref)
        m_ref[...] = jnp.zeros_like(m_ref)

    def recurrence(r):
        G = _gate_act(gz_s[slot, r] + bg_ref[...])
        row = lax.broadcasted_iota(jnp.int32, (T, T), 0)
        col = lax.broadcasted_iota(jnp.int32, (T, T), 1)
        causal = col <= row
        Bc = _dot_exactish(jnp.where(causal, 1.0, 0.0).astype(BF16), G)
        yield
        Gt = G.T
        Bt = Bc.T
        yield
        parts = [None] * ML_HEADS

        def head(h):
            def cols(base, width):
                return p_s[slot, r, :, base + h * width:base + (h + 1) * width]

            q = cols(0, ML_DK)
            k = cols(ML_QK, ML_DK) * (ML_DK ** -0.5)
            v = cols(2 * ML_QK, ML_DV)
            qc, kc, vc = q.astype(BF16), k.astype(BF16), v.astype(BF16)
            qk = lax.dot_general(qc, kc, (((1,), (1,)), ((), ())), preferred_element_type=F32)
            yield
            li_col = G[:, h:h + 1]
            b_col = Bc[:, ML_HEADS + h:ML_HEADS + h + 1]
            dmat = jnp.where(causal, b_col - Bt[ML_HEADS + h:ML_HEADS + h + 1, :] + Gt[h:h + 1, :], -jnp.inf)
            m_prev = m_ref[r, h:h + 1, 0:1]
            inter = b_col + m_prev
            m_t = jnp.maximum(inter, jnp.max(dmat, axis=1, keepdims=True))
            yield
            s = qk * jnp.exp(dmat - m_t)
            a_inter = jnp.exp(inter - m_t)
            den = (jnp.sum(s, axis=1, keepdims=True)
                   + a_inter * jnp.sum(q * n_ref[r, h:h + 1, :], axis=1, keepdims=True))
            yield
            C = C_ref[r, h]
            num = _dot(s.astype(BF16), vc) + a_inter * _dot(qc, C.astype(BF16))
            yield
            hh = num * (1.0 / jnp.maximum(jnp.abs(den), jnp.exp(-m_t)))
            hs = _head_out(hh, cols(2 * ML_QK + ML_VO, ML_DV), hg_ref[:, h * ML_DV:(h + 1) * ML_DV])
            yield
            parts[h] = _dot(hs.astype(BF16), wout_ref[h * ML_DV:(h + 1) * ML_DV, :])
            yield
            m_new = m_t[T - 1:T, :]
            b_last = b_col[T - 1:T, :]
            decay = jnp.exp(b_last + m_prev - m_new)
            kw = k * jnp.exp(b_last - b_col + li_col - m_new)
            C_ref[r, h] = decay * C + lax.dot_general(kw.astype(BF16), vc, (((0,), (0,)), ((), ())),
                                                      preferred_element_type=F32)
            n_ref[r, h:h + 1, :] = decay * n_ref[r, h:h + 1, :] + jnp.sum(kw, axis=0, keepdims=True)
            m_ref[r, h:h + 1, :] = jnp.broadcast_to(m_new, (1, LANES))
            yield

        yield from _rounds([(head(h), h) for h in range(ML_HEADS)])
        y_ref[r] = xc_ref[r] + ((parts[0] + parts[1]) + (parts[2] + parts[3]))

    _run([(project(xnext_ref, 1 - slot, r), 0) for r in range(R)] + [(recurrence(r), r) for r in range(R)])


def _prompt_mixer(x, g, wqkvo, wg, bg, hg, wout):
    B, L, D = x.shape
    T = PROMPT_CHUNK
    assert L % T == 0
    R = PROMPT_ROWS
    assert B % R == 0
    nc = L // T
    steps = (B // R) * nc
    const = lambda f: (0, 0)
    cur = lambda f: (f // nc, f % nc, 0)
    nxt = lambda f: (jnp.minimum(f + 1, steps - 1) // nc, jnp.minimum(f + 1, steps - 1) % nc, 0)
    once = dict(pipeline_mode=pl.Buffered(1))
    vmem = _vmem_limit(3 * _nbytes((R, T, D), F32), _nbytes(wqkvo.shape, BF16) // 2,
                       _nbytes(wout.shape, BF16) // 2, _nbytes((R, ML_HEADS, ML_DK, ML_DV), F32),
                       2 * _nbytes((R, T, ML_QKVO), F32))
    y, C, n, m = pl.pallas_call(
        functools.partial(_prompt_mixer_kernel, chunks_per_seq=nc),
        grid=(steps,),
        in_specs=[
            pl.BlockSpec((R, T, D), cur),
            pl.BlockSpec((R, T, D), nxt),
            pl.BlockSpec((1, D), const),
            pl.BlockSpec((D, ML_QKVO), const, **once),
            pl.BlockSpec((D, LANES), const, **once),
            pl.BlockSpec((1, LANES), const),
            pl.BlockSpec((1, ML_VO), const),
            pl.BlockSpec((ML_VO, D), const, **once),
        ],
        out_specs=[
            pl.BlockSpec((R, T, D), cur),
            pl.BlockSpec((R, ML_HEADS, ML_DK, ML_DV), lambda f: (f // nc, 0, 0, 0)),
            pl.BlockSpec((R, ML_HEADS, ML_DK), lambda f: (f // nc, 0, 0)),
            pl.BlockSpec((R, ML_HEADS, LANES), lambda f: (f // nc, 0, 0)),
        ],
        out_shape=[
            jax.ShapeDtypeStruct((B, L, D), F32),
            jax.ShapeDtypeStruct((B, ML_HEADS, ML_DK, ML_DV), F32),
            jax.ShapeDtypeStruct((B, ML_HEADS, ML_DK), F32),
            jax.ShapeDtypeStruct((B, ML_HEADS, LANES), F32),
        ],
        scratch_shapes=[pltpu.VMEM((2, R, T, ML_QKVO), F32), pltpu.VMEM((2, R, T, LANES), F32)],
        compiler_params=pltpu.CompilerParams(
            dimension_semantics=("arbitrary",), vmem_limit_bytes=vmem),
        name="prompt_mlstm_mixer",
    )(x, x, g, wqkvo, wg, bg, hg, wout)
    return y, C, n, m


def _sample_mixer_kernel(x_ref, g_ref, wqkvo_ref, wg_ref, bg_ref, hg_ref, wout_ref,
                         C0_ref, n0_ref, m0_ref,
                         y_ref, C_ref, n_ref, m_ref, hs_s, *, n_valid):
    T = SAMPLE_PAD
    nb = C0_ref.shape[0]
    R = nb * T
    x = x_ref[...]
    xn = _rms(x, g_ref[...]).astype(BF16)
    p = _dot(xn, wqkvo_ref[...])

    rowt = lax.broadcasted_iota(jnp.int32, (R, LANES), 0) % T
    lane = lax.broadcasted_iota(jnp.int32, (R, LANES), 1)
    G = jnp.where(rowt < n_valid, _gate_act(_dot(xn, wg_ref[...]) + bg_ref[...]),
                  jnp.where(lane < ML_HEADS, -jnp.inf, 0.0))
    row = lax.broadcasted_iota(jnp.int32, (R, R), 0)
    col = lax.broadcasted_iota(jnp.int32, (R, R), 1)
    causal = (row // T == col // T) & (col <= row)
    Bc = _dot_exactish(jnp.where(causal, 1.0, 0.0).astype(BF16), jnp.where(lane < ML_HEADS, 0.0, G))
    Gt = G.T
    Bt = Bc.T

    def per_seq(fn):
        return jnp.concatenate([fn(a) for a in range(nb)], axis=0)

    def seq_last(colvec):
        return per_seq(lambda a: jnp.broadcast_to(colvec[T * a + T - 1:T * a + T, :], (T, 1)))

    def head(h):
        q, k, v, po = _head_slices(p, h)
        li_col = G[:, h:h + 1]
        b_col = Bc[:, ML_HEADS + h:ML_HEADS + h + 1]
        li_row = Gt[h:h + 1, :]
        b_row = Bt[ML_HEADS + h:ML_HEADS + h + 1, :]
        dmat = jnp.where(causal, b_col - (b_row - li_row), -jnp.inf)
        m_prev = per_seq(lambda a: jnp.broadcast_to(m0_ref[a, h:h + 1, 0:1], (T, 1)))
        n_rows = per_seq(lambda a: jnp.broadcast_to(n0_ref[a, h:h + 1, :], (T, ML_DK)))
        qc, kc, vc = q.astype(BF16), k.astype(BF16), v.astype(BF16)
        qk = lax.dot_general(qc, kc, (((1,), (1,)), ((), ())), preferred_element_type=F32)
        yield
        inter = b_col + m_prev
        m_t = jnp.maximum(inter, jnp.max(dmat, axis=1, keepdims=True))
        s = qk * jnp.exp(dmat - m_t)
        a_inter = jnp.exp(inter - m_t)
        yield
        qC = per_seq(lambda a: _dot(q[T * a:T * a + T, :], C0_ref[a, h]))
        yield
        num = _dot(s.astype(BF16), vc) + a_inter * qC
        den = jnp.sum(s, axis=1, keepdims=True) + a_inter * jnp.sum(q * n_rows, axis=1, keepdims=True)
        hh = num * (1.0 / jnp.maximum(jnp.abs(den), jnp.exp(-m_t)))
        hs_s[:, h * ML_DV:(h + 1) * ML_DV] = _head_out(
            hh, po, hg_ref[:, h * ML_DV:(h + 1) * ML_DV]).astype(BF16)
        yield
        m_new = seq_last(m_t)
        b_last = seq_last(b_col)
        decay = jnp.exp(b_last + m_prev - m_new)
        kw = k * jnp.exp(b_last - b_col + li_col - m_new)
        for a in range(nb):
            dec = decay[T * a:T * a + 1, :]
            kw_a = kw[T * a:T * a + T, :]
            upd = lax.dot_general(kw_a, v[T * a:T * a + T, :], (((0,), (0,)), ((), ())),
                                  preferred_element_type=F32)
            C_ref[a, h] = dec * C0_ref[a, h] + upd
            n_ref[a, h:h + 1, :] = dec * n0_ref[a, h:h + 1, :] + jnp.sum(kw_a, axis=0, keepdims=True)
            m_ref[a, h:h + 1, :] = jnp.broadcast_to(m_new[T * a:T * a + 1, :], (1, LANES))
            if a % 4 == 3:
                yield

    _run([(head(h), h) for h in range(ML_HEADS)])
    y_ref[...] = x + _dot(hs_s[...], wout_ref[...])


def _sample_mixer(x, g, wqkvo, wg, bg, hg, wout, C0, n0, m0, n_valid):
    NBT, D = x.shape
    nseq = C0.shape[0]
    nb = SAMPLE_NB
    R = nb * SAMPLE_PAD
    assert nseq % nb == 0 and NBT == nseq * SAMPLE_PAD
    const = lambda i: (0, 0)
    state_specs = [
        pl.BlockSpec((nb, ML_HEADS, ML_DK, ML_DV), lambda i: (i, 0, 0, 0)),
        pl.BlockSpec((nb, ML_HEADS, ML_DK), lambda i: (i, 0, 0)),
        pl.BlockSpec((nb, ML_HEADS, LANES), lambda i: (i, 0, 0)),
    ]
    vmem = _vmem_limit(2 * _nbytes((R, D), F32), _nbytes(wqkvo.shape, BF16), _nbytes(wout.shape, BF16),
                       2 * _nbytes((nb, ML_HEADS, ML_DK, ML_DV), F32), _nbytes((R, ML_QKVO), F32))
    return pl.pallas_call(
        functools.partial(_sample_mixer_kernel, n_valid=n_valid),
        grid=(nseq // nb,),
        in_specs=[
            pl.BlockSpec((R, D), lambda i: (i, 0)),
            pl.BlockSpec((1, D), const),
            pl.BlockSpec((D, ML_QKVO), const),
            pl.BlockSpec((D, LANES), const),
            pl.BlockSpec((1, LANES), const),
            pl.BlockSpec((1, ML_VO), const),
            pl.BlockSpec((ML_VO, D), const),
        ] + state_specs,
        out_specs=[pl.BlockSpec((R, D), lambda i: (i, 0))] + state_specs,
        out_shape=[
            jax.ShapeDtypeStruct((NBT, D), F32),
            jax.ShapeDtypeStruct(C0.shape, F32),
            jax.ShapeDtypeStruct(n0.shape, F32),
            jax.ShapeDtypeStruct(m0.shape, F32),
        ],
        scratch_shapes=[pltpu.VMEM((R, ML_VO), BF16)],
        compiler_params=pltpu.CompilerParams(
            dimension_semantics=("arbitrary",), vmem_limit_bytes=vmem),
        name="sample_mlstm_mixer",
    )(x, g, wqkvo, wg, bg, hg, wout, C0, n0, m0)


def _dot_exactish_right(x, m01):
    hi, mid, lo = _split3(x)
    return _dot(hi, m01) + _dot(mid, m01) + _dot(lo, m01)


def _mlp_kernel(x_ref, g_ref, w1_ref, w2_ref, *rest):
    x = x_ref[...]
    xn = _rms(x, g_ref[...]).astype(BF16)
    acc = x
    for c in range(D_FF // FF_TILE):
        hcol = _dot(xn, w1_ref[:, c * FF_TILE:(c + 1) * FF_TILE])
        hcol = jnp.square(jnp.maximum(hcol, 0.0)).astype(BF16)
        acc = acc + _dot(hcol, w2_ref[c * FF_TILE:(c + 1) * FF_TILE, :])
    if len(rest) == 1:
        (y_ref,) = rest
        y_ref[...] = acc
        return
    gkv_ref, wkv_ref, kg_ref, y_ref, k_ref, v_ref = rest
    y_ref[...] = acc
    kv = _dot(_rms(acc, gkv_ref[...]).astype(BF16), wkv_ref[...])
    kraw = kv[:, :ATT_KV]
    v_ref[...] = kv[:, ATT_KV:]
    r = lax.broadcasted_iota(jnp.int32, (ATT_KV, ATT_KV), 0) // ATT_HD
    c = lax.broadcasted_iota(jnp.int32, (ATT_KV, ATT_KV), 1) // ATT_HD
    seg = jnp.where(r == c, 1.0, 0.0).astype(BF16)
    ss = _dot_exactish_right(kraw * kraw, seg)
    k_ref[...] = kraw * lax.rsqrt(ss * (1.0 / ATT_HD) + EPS) * kg_ref[...]


def _mlp(x, g, w1, w2, layer, kv=None):
    N, D = x.shape
    tm = min(ROW_TILE, N)
    assert N % tm == 0
    const = lambda i: (0, 0)
    rows = lambda i: (i, 0)
    once = dict(pipeline_mode=pl.Buffered(1))
    in_specs = [
        pl.BlockSpec((tm, D), rows),
        pl.BlockSpec((1, D), const),
        pl.BlockSpec((None, D, D_FF), lambda i: (layer, 0, 0), **once),
        pl.BlockSpec((None, D_FF, D), lambda i: (layer, 0, 0), **once),
    ]
    out_specs = [pl.BlockSpec((tm, D), rows)]
    out_shape = [jax.ShapeDtypeStruct((N, D), F32)]
    args = [x, g, w1, w2]
    if kv is not None:
        in_specs += [pl.BlockSpec((1, D), const), pl.BlockSpec((D, 2 * ATT_KV), const, **once),
                     pl.BlockSpec((1, ATT_KV), const)]
        out_specs += [pl.BlockSpec((tm, ATT_KV), rows)] * 2
        out_shape += [jax.ShapeDtypeStruct((N, ATT_KV), F32)] * 2
        args += list(kv)
    vmem = _vmem_limit(2 * _nbytes((tm, D), F32), _nbytes(w1.shape[1:], BF16) // 2,
                       _nbytes(w2.shape[1:], BF16) // 2, 2 * _nbytes((tm, FF_TILE), F32))
    out = pl.pallas_call(
        _mlp_kernel,
        grid=(N // tm,),
        in_specs=in_specs,
        out_specs=out_specs,
        out_shape=out_shape,
        compiler_params=pltpu.CompilerParams(
            dimension_semantics=("arbitrary",), vmem_limit_bytes=vmem),
        name="sqrelu_mlp",
    )(*args)
    return out[0] if kv is None else out


def _pair_queries(qraw, pr, qscale, col0=0):
    TQ = qraw.shape[0]
    lo = lax.broadcasted_iota(jnp.int32, (TQ, LANES), 1) < ATT_HD
    out = []
    for e in range(2):
        kvh = 2 * pr + e
        for g in range(ATT_GROUP):
            cc, half = divmod(g, 2)
            c0 = (2 * kvh + cc) * LANES - col0
            q2 = qraw[:, c0:c0 + LANES]
            qm = jnp.where(lo, q2, 0.0) if half == 0 else jnp.where(lo, 0.0, q2)
            ss = jnp.sum(qm * qm, axis=1, keepdims=True)
            qn = qm * lax.rsqrt(ss * (1.0 / ATT_HD) + EPS)
            if qscale is not None:
                qn = qn * qscale
            out.append(qn if half == e else pltpu.roll(qn, ATT_HD, 1))
    return out


def _pair_outputs(o_heads, rden, pr, store):
    TQ = o_heads[0].shape[0]
    lo = lax.broadcasted_iota(jnp.int32, (TQ, LANES), 1) < ATT_HD
    for e in range(2):
        kvh = 2 * pr + e
        for cc in range(2):
            tiles = []
            for half in range(2):
                o = o_heads[e * ATT_GROUP + 2 * cc + half] * rden[e * ATT_GROUP + 2 * cc + half]
                tiles.append(o if half == e else pltpu.roll(o, ATT_HD, 1))
            store((2 * kvh + cc) * LANES, jnp.where(lo, tiles[0], tiles[1]))


def _softmax_with_sink(parts, sink):
    M = sink
    for s in parts:
        M = jnp.maximum(M, jnp.max(s, axis=1, keepdims=True))
    ps = [jnp.exp2(s - M) for s in parts]
    den = jnp.exp2(sink - M)
    for p in ps:
        den = den + jnp.sum(p, axis=1, keepdims=True)
    return ps, 1.0 / den


def _prompt_attn_kernel(sinks_ref, x_ref, g_ref, wq_ref, qg_ref, kp_ref, kc_ref, vp_ref, vc_ref, wo_ref,
                        y_ref):
    TQ = WINDOW
    TR = x_ref.shape[0]
    kj = lax.broadcasted_iota(jnp.int32, (WINDOW + TQ, TQ), 0)
    qi = lax.broadcasted_iota(jnp.int32, (WINDOW + TQ, TQ), 1)
    band = (qi + WINDOW - kj >= 0) & (qi - kj <= 0)
    band_first = band & ((kj >= WINDOW) | (pl.program_id(1) > 0))
    kscale = qg_ref[...] * (ATT_HD ** -0.5 * LOG2E)
    slabs = []
    for pr in range(ATT_KVH // 2):
        sl = slice(pr * LANES, (pr + 1) * LANES)
        slabs.append(((jnp.concatenate([kp_ref[:, sl], kc_ref[:, sl]], axis=0) * kscale).astype(BF16),
                      jnp.concatenate([vp_ref[:, sl], vc_ref[:, sl]], axis=0).T))

    PW = 2 * ATT_GROUP * ATT_HD
    zeros_hd = jnp.zeros((ATT_HD, TQ), F32)
    ones_rows = jnp.ones((2 * SUBLANES, WINDOW + TQ), F32)

    def chain(sb):
        rows = slice(sb * TQ, (sb + 1) * TQ)
        keys = slice(sb * TQ, sb * TQ + WINDOW + TQ)
        mask = band_first if sb == 0 else band
        x = x_ref[rows, :]
        xn = _rms(x, g_ref[...]).astype(BF16)
        acc = x
        for pr in range(ATT_KVH // 2):
            kslab, vslab_t = slabs[pr]
            qt = _dot(xn, wq_ref[:, pr * PW:(pr + 1) * PW]).T
            yield
            tiles, sink_rows = [], []
            for e in range(2):
                for g in range(ATT_GROUP):
                    blk = qt[(e * ATT_GROUP + g) * ATT_HD:(e * ATT_GROUP + g + 1) * ATT_HD, :]
                    qn = blk * lax.rsqrt(jnp.sum(blk * blk, axis=0, keepdims=True) * (1.0 / ATT_HD) + EPS)
                    tiles.append(jnp.concatenate([qn, zeros_hd] if e == 0 else [zeros_hd, qn], axis=0))
                    sink_rows.append(jnp.full((1, TQ), sinks_ref[(2 * pr + e) * ATT_GROUP + g] * LOG2E, F32))
            qmat = jnp.concatenate(tiles, axis=1).astype(BF16)
            st = _dot(kslab[keys, :], qmat)
            yield
            parts = []
            for e in range(2):
                ps, ms = [], []
                for g in range(ATT_GROUP):
                    c0 = (e * ATT_GROUP + g) * TQ
                    s_h = jnp.where(mask, st[:, c0:c0 + TQ], -jnp.inf)
                    m_h = jnp.maximum(jnp.max(s_h, axis=0, keepdims=True), sink_rows[e * ATT_GROUP + g])
                    ps.append(jnp.exp2(s_h - m_h).astype(BF16))
                    ms.append(m_h)
                p_e = jnp.concatenate(ps, axis=1)
                v_aug = jnp.concatenate([vslab_t[e * ATT_HD:(e + 1) * ATT_HD, keys], ones_rows],
                                        axis=0).astype(BF16)
                ot = _dot(v_aug, p_e)
                yield
                sink_term = jnp.exp2(jnp.concatenate(sink_rows[e * ATT_GROUP:(e + 1) * ATT_GROUP], axis=1)
                                     - jnp.concatenate(ms, axis=1))
                on = ot[0:ATT_HD, :] * (1.0 / (ot[ATT_HD:ATT_HD + 1, :] + sink_term))
                parts += [on[:, g * TQ:(g + 1) * TQ] for g in range(ATT_GROUP)]
            o_pair = jnp.concatenate(parts, axis=0).T.astype(BF16)
            acc = acc + _dot(o_pair, wo_ref[pr * PW:(pr + 1) * PW, :])
            yield
        y_ref[rows, :] = acc

    _run([(chain(sb), sb) for sb in range(TR // TQ)])


def _prompt_attn(x, k, v, sinks, g, wq, qg, wo):
    B, L, D = x.shape
    TR = ATTN_ROWS
    per = TR // WINDOW
    assert L % TR == 0 and TR % WINDOW == 0
    const = lambda b, i: (0, 0)
    cur = lambda b, i: (b, i, 0)
    prev = lambda b, i: (b, jnp.maximum(i * per - 1, 0), 0)
    vmem = _vmem_limit(2 * _nbytes((TR, D), F32), _nbytes(wq.shape, BF16), _nbytes(wo.shape, BF16),
                       4 * _nbytes((TR, ATT_KV), F32), 8 * _nbytes((8 * WINDOW, 2 * WINDOW), F32))
    return pl.pallas_call(
        _prompt_attn_kernel,
        grid=(B, L // TR),
        in_specs=[
            pl.BlockSpec(memory_space=pltpu.SMEM),
            pl.BlockSpec((None, TR, D), cur),
            pl.BlockSpec((1, D), const),
            pl.BlockSpec((D, D), const),
            pl.BlockSpec((1, LANES), const),
            pl.BlockSpec((None, WINDOW, ATT_KV), prev),
            pl.BlockSpec((None, TR, ATT_KV), cur),
            pl.BlockSpec((None, WINDOW, ATT_KV), prev),
            pl.BlockSpec((None, TR, ATT_KV), cur),
            pl.BlockSpec((D, D), const),
        ],
        out_specs=pl.BlockSpec((None, TR, D), cur),
        out_shape=jax.ShapeDtypeStruct((B, L, D), F32),
        compiler_params=pltpu.CompilerParams(
            dimension_semantics=("arbitrary", "arbitrary"), vmem_limit_bytes=vmem),
        name="prompt_window_attention",
    )(sinks, x, g, wq, qg, k, k, v, v, wo)


def _sample_attn_kernel(sinks_ref, x_ref, g_ref, wq_ref, qg_ref, kc_ref, vc_ref, wk_ref, wv_ref, wo_ref,
                        y_ref, nk_ref, nv_ref, o_s, *, n_valid):
    T = SAMPLE_PAD
    nb = wk_ref.shape[0]
    R = nb * T
    assert R == WINDOW
    x = x_ref[...]
    q = _dot(_rms(x, g_ref[...]).astype(BF16), wq_ref[...])
    qscale = qg_ref[...] * (ATT_HD ** -0.5 * LOG2E)
    knew = kc_ref[...]
    vnew = vc_ref[...]

    cmask = (lax.broadcasted_iota(jnp.int32, (R, WINDOW), 1)
             >= lax.broadcasted_iota(jnp.int32, (R, WINDOW), 0) % T)
    row = lax.broadcasted_iota(jnp.int32, (R, R), 0)
    col = lax.broadcasted_iota(jnp.int32, (R, R), 1)
    nmask = (row // T == col // T) & (col % T <= row % T) & (col % T < n_valid)

    def store(off, val):
        o_s[:, off:off + LANES] = val.astype(BF16)

    def regroup(per_seq, i):
        return jnp.concatenate([per_seq[a][i * T:(i + 1) * T, :] for a in range(nb)], axis=0)

    def pair(pr):
        sl = slice(pr * LANES, (pr + 1) * LANES)
        qs = _pair_queries(q, pr, qscale)
        S_new = lax.dot_general(jnp.concatenate(qs, axis=0).astype(BF16), knew[:, sl].astype(BF16),
                                (((1,), (1,)), ((), ())), preferred_element_type=F32)
        yield
        sc = []
        for a in range(nb):
            q_a = jnp.concatenate([qi[T * a:T * a + T, :] for qi in qs], axis=0).astype(BF16)
            sc.append(_dot(q_a, wk_ref[a, sl, :].astype(BF16)))
            if a % 4 == 3:
                yield
        pcs, pns, rden = [], [], []
        for i in range(2 * ATT_GROUP):
            s_c = jnp.where(cmask, regroup(sc, i), -jnp.inf)
            s_n = jnp.where(nmask, S_new[i * R:(i + 1) * R, :], -jnp.inf)
            (p_c, p_n), r = _softmax_with_sink([s_c, s_n], sinks_ref[pr * 2 * ATT_GROUP + i] * LOG2E)
            pcs.append(p_c)
            pns.append(p_n)
            rden.append(r)
            if i % 2 == 1:
                yield
        O_new = _dot(jnp.concatenate(pns, axis=0).astype(BF16), vnew[:, sl].astype(BF16))
        yield
        oc = []
        for a in range(nb):
            p_a = jnp.concatenate([pc[T * a:T * a + T, :] for pc in pcs], axis=0).astype(BF16)
            oc.append(lax.dot_general(p_a, wv_ref[a, sl, :].astype(BF16), (((1,), (1,)), ((), ())),
                                      preferred_element_type=F32))
            if a % 4 == 3:
                yield
        _pair_outputs([O_new[i * R:(i + 1) * R, :] + regroup(oc, i) for i in range(2 * ATT_GROUP)],
                      rden, pr, store)
        yield

    def roll_cache():
        keep = lax.broadcasted_iota(jnp.int32, (ATT_KV, WINDOW), 1) < WINDOW - n_valid
        knew_t = knew.T
        vnew_t = vnew.T
        yield
        for a in range(nb):
            shift = (WINDOW - n_valid - T * a) % WINDOW
            nk_ref[a] = jnp.where(keep, pltpu.roll(wk_ref[a], WINDOW - n_valid, 1),
                                  pltpu.roll(knew_t, shift, 1) if shift else knew_t)
            nv_ref[a] = jnp.where(keep, pltpu.roll(wv_ref[a], WINDOW - n_valid, 1),
                                  pltpu.roll(vnew_t, shift, 1) if shift else vnew_t)
            yield

    _run([(pair(0), 0), (pair(1), 2), (roll_cache(), 0)])
    y_ref[...] = x + _dot(o_s[...], wo_ref[...])


def _sample_attn(x, k, v, win_k, win_v, sinks, g, wq, qg, wo, n_valid):
    NBT, D = x.shape
    nseq = win_k.shape[0]
    nb = SAMPLE_NB
    R = nb * SAMPLE_PAD
    assert nseq % nb == 0 and NBT == nseq * SAMPLE_PAD
    const = lambda i: (0, 0)
    rows = lambda i: (i, 0)
    cache = pl.BlockSpec((nb, ATT_KV, WINDOW), lambda i: (i, 0, 0))
    vmem = _vmem_limit(2 * _nbytes((R, D), F32), _nbytes(wq.shape, BF16), _nbytes(wo.shape, BF16),
                       4 * _nbytes((nb, ATT_KV, WINDOW), F32), 2 * _nbytes((R, D), F32))
    return pl.pallas_call(
        functools.partial(_sample_attn_kernel, n_valid=n_valid),
        grid=(nseq // nb,),
        in_specs=[
            pl.BlockSpec(memory_space=pltpu.SMEM),
            pl.BlockSpec((R, D), rows),
            pl.BlockSpec((1, D), const),
            pl.BlockSpec((D, D), const),
            pl.BlockSpec((1, LANES), const),
            pl.BlockSpec((R, ATT_KV), rows),
            pl.BlockSpec((R, ATT_KV), rows),
            cache, cache,
            pl.BlockSpec((D, D), const),
        ],
        out_specs=[pl.BlockSpec((R, D), rows), cache, cache],
        out_shape=[jax.ShapeDtypeStruct((NBT, D), F32),
                   jax.ShapeDtypeStruct(win_k.shape, F32), jax.ShapeDtypeStruct(win_v.shape, F32)],
        scratch_shapes=[pltpu.VMEM((R, D), BF16)],
        compiler_params=pltpu.CompilerParams(
            dimension_semantics=("arbitrary",), vmem_limit_bytes=vmem),
        name="sample_window_attention",
    )(sinks, x, g, wq, qg, k, v, win_k, win_v, wo)


def kernel(x_prompt, x_sample, state_mlstm_C, state_mlstm_n, state_mlstm_m, cache_win_k, cache_win_v,
           ml_norm_g, ml_w_in, ml_b_i, ml_b_f, ml_head_g, ml_w_out, kv_norm_g, w_kv, k_norm_g,
           att_norm_g, att_w_q, q_norm_g, att_sinks, att_w_o, mlp_norm_g, mlp_w1, mlp_w2):
    B, L, D = x_prompt.shape
    NS, LS, _ = x_sample.shape
    assert ml_w_in.shape[0] == 1 and att_w_q.shape[0] == 1 and mlp_w1.shape[0] == 2

    w_in = ml_w_in[0]
    wqkvo = w_in[:, :ML_QKVO].astype(BF16)
    wg = jnp.pad(w_in[:, ML_QKVO:], ((0, 0), (0, LANES - 2 * ML_HEADS))).astype(BF16)
    bg = jnp.pad(jnp.concatenate([ml_b_i[0], ml_b_f[0]]), (0, LANES - 2 * ML_HEADS)).reshape(1, LANES)
    ml_g = ml_norm_g[0].reshape(1, D)
    hg = ml_head_g[0].reshape(1, ML_VO)
    wout = ml_w_out[0].astype(BF16)
    w1 = mlp_w1.astype(BF16)
    w2 = mlp_w2.astype(BF16)
    mlp_g = mlp_norm_g.reshape(2, 1, D)
    wkv = w_kv.astype(BF16)
    kv_g = kv_norm_g.reshape(1, D)
    kg = jnp.tile(k_norm_g, ATT_KVH).reshape(1, ATT_KV)
    att_g = att_norm_g[0].reshape(1, D)
    wq = att_w_q[0].astype(BF16)
    qg = jnp.tile(q_norm_g[0], 2).reshape(1, LANES)
    sinks = att_sinks[0]
    wo = att_w_o[0].astype(BF16)

    xp, p_C, p_n, p_m = _prompt_mixer(x_prompt, ml_g, wqkvo, wg, bg, hg, wout)
    xp, kp, vp = _mlp(xp.reshape(B * L, D), mlp_g[0], w1, w2, 0, kv=(kv_g, wkv, kg))
    xp = _prompt_attn(xp.reshape(B, L, D), kp.reshape(B, L, ATT_KV), vp.reshape(B, L, ATT_KV),
                      sinks, att_g, wq, qg, wo)
    y_prompt = _mlp(xp.reshape(B * L, D), mlp_g[1], w1, w2, 1).reshape(B, L, D)
    p_wk = kp.reshape(B, L, ATT_KV)[:, L - WINDOW:].reshape(B, WINDOW, ATT_KVH, ATT_HD)
    p_wv = vp.reshape(B, L, ATT_KV)[:, L - WINDOW:].reshape(B, WINDOW, ATT_KVH, ATT_HD)

    xs = jnp.pad(x_sample, ((0, 0), (0, SAMPLE_PAD - LS), (0, 0))).reshape(NS * SAMPLE_PAD, D)
    m0 = jnp.broadcast_to(state_mlstm_m[0][:, :, None], (NS, ML_HEADS, LANES))
    xs, s_C, s_n, s_m = _sample_mixer(xs, ml_g, wqkvo, wg, bg, hg, wout,
                                      state_mlstm_C[0], state_mlstm_n[0], m0, LS)
    xs, ks, vs = _mlp(xs, mlp_g[0], w1, w2, 0, kv=(kv_g, wkv, kg))
    to_t = lambda c: c.transpose(0, 2, 3, 1).reshape(NS, ATT_KV, WINDOW)
    from_t = lambda c: c.reshape(NS, ATT_KVH, ATT_HD, WINDOW).transpose(0, 3, 1, 2)
    xs, s_wk, s_wv = _sample_attn(xs, ks, vs, to_t(cache_win_k), to_t(cache_win_v),
                                  sinks, att_g, wq, qg, wo, LS)
    y_sample = _mlp(xs, mlp_g[1], w1, w2, 1).reshape(NS, SAMPLE_PAD, D)[:, :LS]

    return (y_prompt, y_sample,
            p_C[None], p_n[None], p_m[None, :, :, 0], p_wk, p_wv,
            s_C[None], s_n[None], s_m[None, :, :, 0],
            from_t(s_wk), from_t(s_wv))
```

```python
import functools

import jax
import jax.numpy as jnp
from jax import lax
from jax.experimental import pallas as pl
from jax.experimental.pallas import tpu as pltpu

F32 = jnp.float32
BF16 = jnp.bfloat16

D_MODEL = 1024
ML_HEADS = 4
ML_DK = 128
ML_DV = 256
ML_QK = ML_HEADS * ML_DK
ML_VO = ML_HEADS * ML_DV
ML_QKVO = 2 * ML_QK + 2 * ML_VO
GATE_SOFTCAP = 15.0
ATT_HD = 64
ATT_QH = 16
ATT_KVH = 4
ATT_GROUP = 4
ATT_KV = ATT_KVH * ATT_HD
WINDOW = 128
D_FF = 4 * D_MODEL
EPS = 1e-6
LOG2E = 1.4426950408889634

LANES = 128
SUBLANES = 8
VMEM_LIMIT_CAP = 56 * 1024 * 1024

PROMPT_CHUNK = 256
PROMPT_ROWS = 2
PROJ_COLS = 512
SAMPLE_PAD = SUBLANES
SAMPLE_NB = 16
ATTN_ROWS = 1024
ROW_TILE = 512
FF_TILE = 1024


def _vmem_limit(*block_bytes):
    need = 4 * sum(block_bytes) + (8 << 20)
    return int(min(max(need, 32 << 20), VMEM_LIMIT_CAP))


def _nbytes(shape, dtype):
    n = 1
    for s in shape:
        n *= s
    return n * jnp.dtype(dtype).itemsize


def _rms(x, g):
    return x * lax.rsqrt(jnp.mean(x * x, axis=-1, keepdims=True) + EPS) * g


def _dot(a, b):
    return jnp.dot(a, b, preferred_element_type=F32)


def _split3(x):
    hi = x.astype(BF16)
    r1 = x - hi.astype(F32)
    mid = r1.astype(BF16)
    lo = (r1 - mid.astype(F32)).astype(BF16)
    return hi, mid, lo


def _dot_exactish(m01, x):
    hi, mid, lo = _split3(x)
    return _dot(m01, hi) + _dot(m01, mid) + _dot(m01, lo)


def _gate_act(z):
    cap = GATE_SOFTCAP * jnp.tanh(z * (1.0 / GATE_SOFTCAP))
    lsig = jnp.minimum(cap, 0.0) - jnp.log1p(jnp.exp(-jnp.abs(cap)))
    lane = lax.broadcasted_iota(jnp.int32, z.shape, 1)
    return jnp.where(lane < ML_HEADS, cap, lsig)


_DONE = object()


def _rounds(chains):
    live = [(c[0], c[1], c[2] if len(c) > 2 else 1) for c in chains]
    rnd = 0
    while live:
        for item in list(live):
            gen, start, stride = item
            if rnd >= start and (rnd - start) % stride == 0 and next(gen, _DONE) is _DONE:
                live.remove(item)
        rnd += 1
        yield


def _run(chains):
    for _ in _rounds(chains):
        pass


def _head_out(hh, po, hg):
    hn = hh * lax.rsqrt(jnp.mean(hh * hh, axis=-1, keepdims=True) + EPS) * hg
    return jax.nn.sigmoid(po) * hn


def _head_slices(p, h):
    q = p[:, h * ML_DK:(h + 1) * ML_DK]
    k = p[:, ML_QK + h * ML_DK:ML_QK + (h + 1) * ML_DK] * (ML_DK ** -0.5)
    v = p[:, 2 * ML_QK + h * ML_DV:2 * ML_QK + (h + 1) * ML_DV]
    po = p[:, 2 * ML_QK + ML_VO + h * ML_DV:2 * ML_QK + ML_VO + (h + 1) * ML_DV]
    return q, k, v, po


def _prompt_mixer_kernel(xc_ref, xnext_ref, g_ref, wqkvo_ref, wg_ref, bg_ref, hg_ref, wout_ref,
                         y_ref, C_ref, n_ref, m_ref, p_s, gz_s, *, chunks_per_seq):
    R, T = xc_ref.shape[0], xc_ref.shape[1]
    f = pl.program_id(0)
    slot = f % 2

    def project(x_ref, s, r):
        xn = _rms(x_ref[r], g_ref[...]).astype(BF16)
        yield
        for j in range(ML_QKVO // PROJ_COLS):
            cols = slice(j * PROJ_COLS, (j + 1) * PROJ_COLS)
            p_s[s, r, :, cols] = _dot(xn, wqkvo_ref[:, cols])
            yield
        gz_s[s, r] = _dot(xn, wg_ref[...])
        yield

    @pl.when(f == 0)
    def _():
        _run([(project(xc_ref, 0, r), 0) for r in range(R)])

    @pl.when(f % chunks_per_seq == 0)
    def _():
        C_ref[...] = jnp.zeros_like(C_ref)
        n_ref[...] = jnp.zeros_like(n_ref)
        m_ref[...] = jnp.zeros_like(m_ref)

    def recurrence(r):
        G = _gate_act(gz_s[slot, r] + bg_ref[...])
        row = lax.broadcasted_iota(jnp.int32, (T, T), 0)
        col = lax.broadcasted_iota(jnp.int32, (T, T), 1)
        causal = col <= row
        Bc = _dot_exactish(jnp.where(causal, 1.0, 0.0).astype(BF16), G)
        yield
        Gt = G.T
        Bt = Bc.T
        yield
        parts = [None] * ML_HEADS

        def head(h):
            def cols(base, width):
                return p_s[slot, r, :, base + h * width:base + (h + 1) * width]

            q = cols(0, ML_DK)
            k = cols(ML_QK, ML_DK) * (ML_DK ** -0.5)
            v = cols(2 * ML_QK, ML_DV)
            qc, kc, vc = q.astype(BF16), k.astype(BF16), v.astype(BF16)
            qk = lax.dot_general(qc, kc, (((1,), (1,)), ((), ())), preferred_element_type=F32)
            yield
            li_col = G[:, h:h + 1]
            b_col = Bc[:, ML_HEADS + h:ML_HEADS + h + 1]
            dmat = jnp.where(causal, b_col - Bt[ML_HEADS + h:ML_HEADS + h + 1, :] + Gt[h:h + 1, :], -jnp.inf)
            m_prev = m_ref[r, h:h + 1, 0:1]
            inter = b_col + m_prev
            m_t = jnp.maximum(inter, jnp.max(dmat, axis=1, keepdims=True))
            yield
            s = qk * jnp.exp(dmat - m_t)
            a_inter = jnp.exp(inter - m_t)
            den = (jnp.sum(s, axis=1, keepdims=True)
                   + a_inter * jnp.sum(q * n_ref[r, h:h + 1, :], axis=1, keepdims=True))
            yield
            C = C_ref[r, h]
            num = _dot(s.astype(BF16), vc) + a_inter * _dot(qc, C.astype(BF16))
            yield
            hh = num * (1.0 / jnp.maximum(jnp.abs(den), jnp.exp(-m_t)))
            hs = _head_out(hh, cols(2 * ML_QK + ML_VO, ML_DV), hg_ref[:, h * ML_DV:(h + 1) * ML_DV])
            yield
            parts[h] = _dot(hs.astype(BF16), wout_ref[h * ML_DV:(h + 1) * ML_DV, :])
            yield
            m_new = m_t[T - 1:T, :]
            b_last = b_col[T - 1:T, :]
            decay = jnp.exp(b_last + m_prev - m_new)
            kw = k * jnp.exp(b_last - b_col + li_col - m_new)
            C_ref[r, h] = decay * C + lax.dot_general(kw.astype(BF16), vc, (((0,), (0,)), ((), ())),
                                                      preferred_element_type=F32)
            n_ref[r, h:h + 1, :] = decay * n_ref[r, h:h + 1, :] + jnp.sum(kw, axis=0, keepdims=True)
            m_ref[r, h:h + 1, :] = jnp.broadcast_to(m_new, (1, LANES))
            yield

        yield from _rounds([(head(h), 2 * h) for h in range(ML_HEADS)])
        y_ref[r] = xc_ref[r] + ((parts[0] + parts[1]) + (parts[2] + parts[3]))

    _run([(project(xnext_ref, 1 - slot, r), 0) for r in range(R)] + [(recurrence(r), r) for r in range(R)])


def _prompt_mixer(x, g, wqkvo, wg, bg, hg, wout):
    B, L, D = x.shape
    T = PROMPT_CHUNK
    assert L % T == 0
    R = PROMPT_ROWS
    assert B % R == 0
    nc = L // T
    steps = (B // R) * nc
    const = lambda f: (0, 0)
    cur = lambda f: (f // nc, f % nc, 0)
    nxt = lambda f: (jnp.minimum(f + 1, steps - 1) // nc, jnp.minimum(f + 1, steps - 1) % nc, 0)
    once = dict(pipeline_mode=pl.Buffered(1))
    vmem = _vmem_limit(3 * _nbytes((R, T, D), F32), _nbytes(wqkvo.shape, BF16) // 2,
                       _nbytes(wout.shape, BF16) // 2, _nbytes((R, ML_HEADS, ML_DK, ML_DV), F32),
                       2 * _nbytes((R, T, ML_QKVO), F32))
    y, C, n, m = pl.pallas_call(
        functools.partial(_prompt_mixer_kernel, chunks_per_seq=nc),
        grid=(steps,),
        in_specs=[
            pl.BlockSpec((R, T, D), cur),
            pl.BlockSpec((R, T, D), nxt),
            pl.BlockSpec((1, D), const),
            pl.BlockSpec((D, ML_QKVO), const, **once),
            pl.BlockSpec((D, LANES), const, **once),
            pl.BlockSpec((1, LANES), const),
            pl.BlockSpec((1, ML_VO), const),
            pl.BlockSpec((ML_VO, D), const, **once),
        ],
        out_specs=[
            pl.BlockSpec((R, T, D), cur),
            pl.BlockSpec((R, ML_HEADS, ML_DK, ML_DV), lambda f: (f // nc, 0, 0, 0)),
            pl.BlockSpec((R, ML_HEADS, ML_DK), lambda f: (f // nc, 0, 0)),
            pl.BlockSpec((R, ML_HEADS, LANES), lambda f: (f // nc, 0, 0)),
        ],
        out_shape=[
            jax.ShapeDtypeStruct((B, L, D), F32),
            jax.ShapeDtypeStruct((B, ML_HEADS, ML_DK, ML_DV), F32),
            jax.ShapeDtypeStruct((B, ML_HEADS, ML_DK), F32),
            jax.ShapeDtypeStruct((B, ML_HEADS, LANES), F32),
        ],
        scratch_shapes=[pltpu.VMEM((2, R, T, ML_QKVO), F32), pltpu.VMEM((2, R, T, LANES), F32)],
        compiler_params=pltpu.CompilerParams(
            dimension_semantics=("arbitrary",), vmem_limit_bytes=vmem),
        name="prompt_mlstm_mixer",
    )(x, x, g, wqkvo, wg, bg, hg, wout)
    return y, C, n, m


def _sample_mixer_kernel(x_ref, g_ref, wqkvo_ref, wg_ref, bg_ref, hg_ref, wout_ref,
                         C0_ref, n0_ref, m0_ref,
                         y_ref, C_ref, n_ref, m_ref, hs_s, *, n_valid):
    T = SAMPLE_PAD
    nb = C0_ref.shape[0]
    R = nb * T
    x = x_ref[...]
    xn = _rms(x, g_ref[...]).astype(BF16)
    p = _dot(xn, wqkvo_ref[...])

    rowt = lax.broadcasted_iota(jnp.int32, (R, LANES), 0) % T
    lane = lax.broadcasted_iota(jnp.int32, (R, LANES), 1)
    G = jnp.where(rowt < n_valid, _gate_act(_dot(xn, wg_ref[...]) + bg_ref[...]),
                  jnp.where(lane < ML_HEADS, -jnp.inf, 0.0))
    row = lax.broadcasted_iota(jnp.int32, (R, R), 0)
    col = lax.broadcasted_iota(jnp.int32, (R, R), 1)
    causal = (row // T == col // T) & (col <= row)
    Bc = _dot_exactish(jnp.where(causal, 1.0, 0.0).astype(BF16), jnp.where(lane < ML_HEADS, 0.0, G))
    Gt = G.T
    Bt = Bc.T

    def per_seq(fn):
        return jnp.concatenate([fn(a) for a in range(nb)], axis=0)

    def seq_last(colvec):
        return per_seq(lambda a: jnp.broadcast_to(colvec[T * a + T - 1:T * a + T, :], (T, 1)))

    def head(h):
        q, k, v, po = _head_slices(p, h)
        li_col = G[:, h:h + 1]
        b_col = Bc[:, ML_HEADS + h:ML_HEADS + h + 1]
        li_row = Gt[h:h + 1, :]
        b_row = Bt[ML_HEADS + h:ML_HEADS + h + 1, :]
        dmat = jnp.where(causal, b_col - (b_row - li_row), -jnp.inf)
        m_prev = per_seq(lambda a: jnp.broadcast_to(m0_ref[a, h:h + 1, 0:1], (T, 1)))
        n_rows = per_seq(lambda a: jnp.broadcast_to(n0_ref[a, h:h + 1, :], (T, ML_DK)))
        qc, kc, vc = q.astype(BF16), k.astype(BF16), v.astype(BF16)
        qk = lax.dot_general(qc, kc, (((1,), (1,)), ((), ())), preferred_element_type=F32)
        yield
        inter = b_col + m_prev
        m_t = jnp.maximum(inter, jnp.max(dmat, axis=1, keepdims=True))
        s = qk * jnp.exp(dmat - m_t)
        a_inter = jnp.exp(inter - m_t)
        yield
        qC = per_seq(lambda a: _dot(q[T * a:T * a + T, :], C0_ref[a, h]))
        yield
        num = _dot(s.astype(BF16), vc) + a_inter * qC
        den = jnp.sum(s, axis=1, keepdims=True) + a_inter * jnp.sum(q * n_rows, axis=1, keepdims=True)
        hh = num * (1.0 / jnp.maximum(jnp.abs(den), jnp.exp(-m_t)))
        hs_s[:, h * ML_DV:(h + 1) * ML_DV] = _head_out(
            hh, po, hg_ref[:, h * ML_DV:(h + 1) * ML_DV]).astype(BF16)
        yield
        m_new = seq_last(m_t)
        b_last = seq_last(b_col)
        decay = jnp.exp(b_last + m_prev - m_new)
        kw = k * jnp.exp(b_last - b_col + li_col - m_new)
        for a in range(nb):
            dec = decay[T * a:T * a + 1, :]
            kw_a = kw[T * a:T * a + T, :]
            upd = lax.dot_general(kw_a, v[T * a:T * a + T, :], (((0,), (0,)), ((), ())),
                                  preferred_element_type=F32)
            C_ref[a, h] = dec * C0_ref[a, h] + upd
            n_ref[a, h:h + 1, :] = dec * n0_ref[a, h:h + 1, :] + jnp.sum(kw_a, axis=0, keepdims=True)
            m_ref[a, h:h + 1, :] = jnp.broadcast_to(m_new[T * a:T * a + 1, :], (1, LANES))
            if a % 4 == 3:
                yield

    _run([(head(h), h) for h in range(ML_HEADS)])
    y_ref[...] = x + _dot(hs_s[...], wout_ref[...])


def _sample_mixer(x, g, wqkvo, wg, bg, hg, wout, C0, n0, m0, n_valid):
    NBT, D = x.shape
    nseq = C0.shape[0]
    nb = SAMPLE_NB
    R = nb * SAMPLE_PAD
    assert nseq % nb == 0 and NBT == nseq * SAMPLE_PAD
    const = lambda i: (0, 0)
    state_specs = [
        pl.BlockSpec((nb, ML_HEADS, ML_DK, ML_DV), lambda i: (i, 0, 0, 0)),
        pl.BlockSpec((nb, ML_HEADS, ML_DK), lambda i: (i, 0, 0)),
        pl.BlockSpec((nb, ML_HEADS, LANES), lambda i: (i, 0, 0)),
    ]
    vmem = _vmem_limit(2 * _nbytes((R, D), F32), _nbytes(wqkvo.shape, BF16), _nbytes(wout.shape, BF16),
                       2 * _nbytes((nb, ML_HEADS, ML_DK, ML_DV), F32), _nbytes((R, ML_QKVO), F32))
    return pl.pallas_call(
        functools.partial(_sample_mixer_kernel, n_valid=n_valid),
        grid=(nseq // nb,),
        in_specs=[
            pl.BlockSpec((R, D), lambda i: (i, 0)),
            pl.BlockSpec((1, D), const),
            pl.BlockSpec((D, ML_QKVO), const),
            pl.BlockSpec((D, LANES), const),
            pl.BlockSpec((1, LANES), const),
            pl.BlockSpec((1, ML_VO), const),
            pl.BlockSpec((ML_VO, D), const),
        ] + state_specs,
        out_specs=[pl.BlockSpec((R, D), lambda i: (i, 0))] + state_specs,
        out_shape=[
            jax.ShapeDtypeStruct((NBT, D), F32),
            jax.ShapeDtypeStruct(C0.shape, F32),
            jax.ShapeDtypeStruct(n0.shape, F32),
            jax.ShapeDtypeStruct(m0.shape, F32),
        ],
        scratch_shapes=[pltpu.VMEM((R, ML_VO), BF16)],
        compiler_params=pltpu.CompilerParams(
            dimension_semantics=("arbitrary",), vmem_limit_bytes=vmem),
        name="sample_mlstm_mixer",
    )(x, g, wqkvo, wg, bg, hg, wout, C0, n0, m0)


def _dot_exactish_right(x, m01):
    hi, mid, lo = _split3(x)
    return _dot(hi, m01) + _dot(mid, m01) + _dot(lo, m01)


def _mlp_kernel(x_ref, g_ref, w1_ref, w2_ref, *rest):
    x = x_ref[...]
    xn = _rms(x, g_ref[...]).astype(BF16)
    acc = x
    for c in range(D_FF // FF_TILE):
        hcol = _dot(xn, w1_ref[:, c * FF_TILE:(c + 1) * FF_TILE])
        hcol = jnp.square(jnp.maximum(hcol, 0.0)).astype(BF16)
        acc = acc + _dot(hcol, w2_ref[c * FF_TILE:(c + 1) * FF_TILE, :])
    if len(rest) == 1:
        (y_ref,) = rest
        y_ref[...] = acc
        return
    gkv_ref, wkv_ref, kg_ref, y_ref, k_ref, v_ref = rest
    y_ref[...] = acc
    kv = _dot(_rms(acc, gkv_ref[...]).astype(BF16), wkv_ref[...])
    kraw = kv[:, :ATT_KV]
    v_ref[...] = kv[:, ATT_KV:]
    r = lax.broadcasted_iota(jnp.int32, (ATT_KV, ATT_KV), 0) // ATT_HD
    c = lax.broadcasted_iota(jnp.int32, (ATT_KV, ATT_KV), 1) // ATT_HD
    seg = jnp.where(r == c, 1.0, 0.0).astype(BF16)
    ss = _dot_exactish_right(kraw * kraw, seg)
    k_ref[...] = kraw * lax.rsqrt(ss * (1.0 / ATT_HD) + EPS) * kg_ref[...]


def _mlp(x, g, w1, w2, layer, kv=None):
    N, D = x.shape
    tm = min(ROW_TILE, N)
    assert N % tm == 0
    const = lambda i: (0, 0)
    rows = lambda i: (i, 0)
    once = dict(pipeline_mode=pl.Buffered(1))
    in_specs = [
        pl.BlockSpec((tm, D), rows),
        pl.BlockSpec((1, D), const),
        pl.BlockSpec((None, D, D_FF), lambda i: (layer, 0, 0), **once),
        pl.BlockSpec((None, D_FF, D), lambda i: (layer, 0, 0), **once),
    ]
    out_specs = [pl.BlockSpec((tm, D), rows)]
    out_shape = [jax.ShapeDtypeStruct((N, D), F32)]
    args = [x, g, w1, w2]
    if kv is not None:
        in_specs += [pl.BlockSpec((1, D), const), pl.BlockSpec((D, 2 * ATT_KV), const, **once),
                     pl.BlockSpec((1, ATT_KV), const)]
        out_specs += [pl.BlockSpec((tm, ATT_KV), rows)] * 2
        out_shape += [jax.ShapeDtypeStruct((N, ATT_KV), F32)] * 2
        args += list(kv)
    vmem = _vmem_limit(2 * _nbytes((tm, D), F32), _nbytes(w1.shape[1:], BF16) // 2,
                       _nbytes(w2.shape[1:], BF16) // 2, 2 * _nbytes((tm, FF_TILE), F32))
    out = pl.pallas_call(
        _mlp_kernel,
        grid=(N // tm,),
        in_specs=in_specs,
        out_specs=out_specs,
        out_shape=out_shape,
        compiler_params=pltpu.CompilerParams(
            dimension_semantics=("arbitrary",), vmem_limit_bytes=vmem),
        name="sqrelu_mlp",
    )(*args)
    return out[0] if kv is None else out


def _pair_queries(qraw, pr, qscale, col0=0):
    TQ = qraw.shape[0]
    lo = lax.broadcasted_iota(jnp.int32, (TQ, LANES), 1) < ATT_HD
    out = []
    for e in range(2):
        kvh = 2 * pr + e
        for g in range(ATT_GROUP):
            cc, half = divmod(g, 2)
            c0 = (2 * kvh + cc) * LANES - col0
            q2 = qraw[:, c0:c0 + LANES]
            qm = jnp.where(lo, q2, 0.0) if half == 0 else jnp.where(lo, 0.0, q2)
            ss = jnp.sum(qm * qm, axis=1, keepdims=True)
            qn = qm * lax.rsqrt(ss * (1.0 / ATT_HD) + EPS)
            if qscale is not None:
                qn = qn * qscale
            out.append(qn if half == e else pltpu.roll(qn, ATT_HD, 1))
    return out


def _pair_outputs(o_heads, rden, pr, store):
    TQ = o_heads[0].shape[0]
    lo = lax.broadcasted_iota(jnp.int32, (TQ, LANES), 1) < ATT_HD
    for e in range(2):
        kvh = 2 * pr + e
        for cc in range(2):
            tiles = []
            for half in range(2):
                o = o_heads[e * ATT_GROUP + 2 * cc + half] * rden[e * ATT_GROUP + 2 * cc + half]
                tiles.append(o if half == e else pltpu.roll(o, ATT_HD, 1))
            store((2 * kvh + cc) * LANES, jnp.where(lo, tiles[0], tiles[1]))


def _softmax_with_sink(parts, sink):
    M = sink
    for s in parts:
        M = jnp.maximum(M, jnp.max(s, axis=1, keepdims=True))
    ps = [jnp.exp2(s - M) for s in parts]
    den = jnp.exp2(sink - M)
    for p in ps:
        den = den + jnp.sum(p, axis=1, keepdims=True)
    return ps, 1.0 / den


def _prompt_attn_kernel(sinks_ref, x_ref, g_ref, wq_ref, qg_ref, kp_ref, kc_ref, vp_ref, vc_ref, wo_ref,
                        y_ref):
    TQ = WINDOW
    TR = x_ref.shape[0]
    kj = lax.broadcasted_iota(jnp.int32, (WINDOW + TQ, TQ), 0)
    qi = lax.broadcasted_iota(jnp.int32, (WINDOW + TQ, TQ), 1)
    band = (qi + WINDOW - kj >= 0) & (qi - kj <= 0)
    band_first = band & ((kj >= WINDOW) | (pl.program_id(1) > 0))
    kscale = qg_ref[...] * (ATT_HD ** -0.5 * LOG2E)
    slabs = []
    for pr in range(ATT_KVH // 2):
        sl = slice(pr * LANES, (pr + 1) * LANES)
        slabs.append(((jnp.concatenate([kp_ref[:, sl], kc_ref[:, sl]], axis=0) * kscale).astype(BF16),
                      jnp.concatenate([vp_ref[:, sl], vc_ref[:, sl]], axis=0).T))

    PW = 2 * ATT_GROUP * ATT_HD
    zeros_hd = jnp.zeros((ATT_HD, TQ), F32)
    ones_rows = jnp.ones((2 * SUBLANES, WINDOW + TQ), F32)

    def chain(sb):
        rows = slice(sb * TQ, (sb + 1) * TQ)
        keys = slice(sb * TQ, sb * TQ + WINDOW + TQ)
        mask = band_first if sb == 0 else band
        x = x_ref[rows, :]
        xn = _rms(x, g_ref[...]).astype(BF16)
        acc = x
        for pr in range(ATT_KVH // 2):
            kslab, vslab_t = slabs[pr]
            qt = _dot(xn, wq_ref[:, pr * PW:(pr + 1) * PW]).T
            yield
            tiles, sink_rows = [], []
            for e in range(2):
                for g in range(ATT_GROUP):
                    blk = qt[(e * ATT_GROUP + g) * ATT_HD:(e * ATT_GROUP + g + 1) * ATT_HD, :]
                    qn = blk * lax.rsqrt(jnp.sum(blk * blk, axis=0, keepdims=True) * (1.0 / ATT_HD) + EPS)
                    tiles.append(jnp.concatenate([qn, zeros_hd] if e == 0 else [zeros_hd, qn], axis=0))
                    sink_rows.append(jnp.full((1, TQ), sinks_ref[(2 * pr + e) * ATT_GROUP + g] * LOG2E, F32))
            qmat = jnp.concatenate(tiles, axis=1).astype(BF16)
            st = _dot(kslab[keys, :], qmat)
            yield
            parts = []
            for e in range(2):
                ps, ms = [], []
                for g in range(ATT_GROUP):
                    c0 = (e * ATT_GROUP + g) * TQ
                    s_h = jnp.where(mask, st[:, c0:c0 + TQ], -jnp.inf)
                    m_h = jnp.maximum(jnp.max(s_h, axis=0, keepdims=True), sink_rows[e * ATT_GROUP + g])
                    ps.append(jnp.exp2(s_h - m_h).astype(BF16))
                    ms.append(m_h)
                p_e = jnp.concatenate(ps, axis=1)
                v_aug = jnp.concatenate([vslab_t[e * ATT_HD:(e + 1) * ATT_HD, keys], ones_rows],
                                        axis=0).astype(BF16)
                ot = _dot(v_aug, p_e)
                yield
                sink_term = jnp.exp2(jnp.concatenate(sink_rows[e * ATT_GROUP:(e + 1) * ATT_GROUP], axis=1)
                                     - jnp.concatenate(ms, axis=1))
                on = ot[0:ATT_HD, :] * (1.0 / (ot[ATT_HD:ATT_HD + 1, :] + sink_term))
                parts += [on[:, g * TQ:(g + 1) * TQ] for g in range(ATT_GROUP)]
            o_pair = jnp.concatenate(parts, axis=0).T.astype(BF16)
            acc = acc + _dot(o_pair, wo_ref[pr * PW:(pr + 1) * PW, :])
            yield
        y_ref[rows, :] = acc

    _run([(chain(sb), 0) for sb in range(TR // TQ)])


def _prompt_attn(x, k, v, sinks, g, wq, qg, wo):
    B, L, D = x.shape
    TR = ATTN_ROWS
    per = TR // WINDOW
    assert L % TR == 0 and TR % WINDOW == 0
    const = lambda b, i: (0, 0)
    cur = lambda b, i: (b, i, 0)
    prev = lambda b, i: (b, jnp.maximum(i * per - 1, 0), 0)
    vmem = _vmem_limit(2 * _nbytes((TR, D), F32), _nbytes(wq.shape, BF16), _nbytes(wo.shape, BF16),
                       4 * _nbytes((TR, ATT_KV), F32), 8 * _nbytes((8 * WINDOW, 2 * WINDOW), F32))
    return pl.pallas_call(
        _prompt_attn_kernel,
        grid=(B, L // TR),
        in_specs=[
            pl.BlockSpec(memory_space=pltpu.SMEM),
            pl.BlockSpec((None, TR, D), cur),
            pl.BlockSpec((1, D), const),
            pl.BlockSpec((D, D), const),
            pl.BlockSpec((1, LANES), const),
            pl.BlockSpec((None, WINDOW, ATT_KV), prev),
            pl.BlockSpec((None, TR, ATT_KV), cur),
            pl.BlockSpec((None, WINDOW, ATT_KV), prev),
            pl.BlockSpec((None, TR, ATT_KV), cur),
            pl.BlockSpec((D, D), const),
        ],
        out_specs=pl.BlockSpec((None, TR, D), cur),
        out_shape=jax.ShapeDtypeStruct((B, L, D), F32),
        compiler_params=pltpu.CompilerParams(
            dimension_semantics=("arbitrary", "arbitrary"), vmem_limit_bytes=vmem),
        name="prompt_window_attention",
    )(sinks, x, g, wq, qg, k, k, v, v, wo)


def _sample_attn_kernel(sinks_ref, x_ref, g_ref, wq_ref, qg_ref, kc_ref, vc_ref, wk_ref, wv_ref, wo_ref,
                        y_ref, nk_ref, nv_ref, o_s, *, n_valid):
    T = SAMPLE_PAD
    nb = wk_ref.shape[0]
    R = nb * T
    assert R == WINDOW
    x = x_ref[...]
    q = _dot(_rms(x, g_ref[...]).astype(BF16), wq_ref[...])
    qscale = qg_ref[...] * (ATT_HD ** -0.5 * LOG2E)
    knew = kc_ref[...]
    vnew = vc_ref[...]

    cmask = (lax.broadcasted_iota(jnp.int32, (R, WINDOW), 1)
             >= lax.broadcasted_iota(jnp.int32, (R, WINDOW), 0) % T)
    row = lax.broadcasted_iota(jnp.int32, (R, R), 0)
    col = lax.broadcasted_iota(jnp.int32, (R, R), 1)
    nmask = (row // T == col // T) & (col % T <= row % T) & (col % T < n_valid)

    def store(off, val):
        o_s[:, off:off + LANES] = val.astype(BF16)

    def regroup(per_seq, i):
        return jnp.concatenate([per_seq[a][i * T:(i + 1) * T, :] for a in range(nb)], axis=0)

    def pair(pr):
        sl = slice(pr * LANES, (pr + 1) * LANES)
        qs = _pair_queries(q, pr, qscale)
        S_new = lax.dot_general(jnp.concatenate(qs, axis=0).astype(BF16), knew[:, sl].astype(BF16),
                                (((1,), (1,)), ((), ())), preferred_element_type=F32)
        yield
        sc = []
        for a in range(nb):
            q_a = jnp.concatenate([qi[T * a:T * a + T, :] for qi in qs], axis=0).astype(BF16)
            sc.append(_dot(q_a, wk_ref[a, sl, :].astype(BF16)))
            if a % 4 == 3:
                yield
        pcs, pns, rden = [], [], []
        for i in range(2 * ATT_GROUP):
            s_c = jnp.where(cmask, regroup(sc, i), -jnp.inf)
            s_n = jnp.where(nmask, S_new[i * R:(i + 1) * R, :], -jnp.inf)
            (p_c, p_n), r = _softmax_with_sink([s_c, s_n], sinks_ref[pr * 2 * ATT_GROUP + i] * LOG2E)
            pcs.append(p_c)
            pns.append(p_n)
            rden.append(r)
            if i % 2 == 1:
                yield
        O_new = _dot(jnp.concatenate(pns, axis=0).astype(BF16), vnew[:, sl].astype(BF16))
        yield
        oc = []
        for a in range(nb):
            p_a = jnp.concatenate([pc[T * a:T * a + T, :] for pc in pcs], axis=0).astype(BF16)
            oc.append(lax.dot_general(p_a, wv_ref[a, sl, :].astype(BF16), (((1,), (1,)), ((), ())),
                                      preferred_element_type=F32))
            if a % 4 == 3:
                yield
        _pair_outputs([O_new[i * R:(i + 1) * R, :] + regroup(oc, i) for i in range(2 * ATT_GROUP)],
                      rden, pr, store)
        yield

    def roll_cache():
        keep = lax.broadcasted_iota(jnp.int32, (ATT_KV, WINDOW), 1) < WINDOW - n_valid
        knew_t = knew.T
        vnew_t = vnew.T
        yield
        for a in range(nb):
            shift = (WINDOW - n_valid - T * a) % WINDOW
            nk_ref[a] = jnp.where(keep, pltpu.roll(wk_ref[a], WINDOW - n_valid, 1),
                                  pltpu.roll(knew_t, shift, 1) if shift else knew_t)
            nv_ref[a] = jnp.where(keep, pltpu.roll(wv_ref[a], WINDOW - n_valid, 1),
                                  pltpu.roll(vnew_t, shift, 1) if shift else vnew_t)
            yield

    _run([(pair(0), 0), (pair(1), 2), (roll_cache(), 0)])
    y_ref[...] = x + _dot(o_s[...], wo_ref[...])


def _sample_attn(x, k, v, win_k, win_v, sinks, g, wq, qg, wo, n_valid):
    NBT, D = x.shape
    nseq = win_k.shape[0]
    nb = SAMPLE_NB
    R = nb * SAMPLE_PAD
    assert nseq % nb == 0 and NBT == nseq * SAMPLE_PAD
    const = lambda i: (0, 0)
    rows = lambda i: (i, 0)
    cache = pl.BlockSpec((nb, ATT_KV, WINDOW), lambda i: (i, 0, 0))
    vmem = _vmem_limit(2 * _nbytes((R, D), F32), _nbytes(wq.shape, BF16), _nbytes(wo.shape, BF16),
                       4 * _nbytes((nb, ATT_KV, WINDOW), F32), 2 * _nbytes((R, D), F32))
    return pl.pallas_call(
        functools.partial(_sample_attn_kernel, n_valid=n_valid),
        grid=(nseq // nb,),
        in_specs=[
            pl.BlockSpec(memory_space=pltpu.SMEM),
            pl.BlockSpec((R, D), rows),
            pl.BlockSpec((1, D), const),
            pl.BlockSpec((D, D), const),
            pl.BlockSpec((1, LANES), const),
            pl.BlockSpec((R, ATT_KV), rows),
            pl.BlockSpec((R, ATT_KV), rows),
            cache, cache,
            pl.BlockSpec((D, D), const),
        ],
        out_specs=[pl.BlockSpec((R, D), rows), cache, cache],
        out_shape=[jax.ShapeDtypeStruct((NBT, D), F32),
                   jax.ShapeDtypeStruct(win_k.shape, F32), jax.ShapeDtypeStruct(win_v.shape, F32)],
        scratch_shapes=[pltpu.VMEM((R, D), BF16)],
        compiler_params=pltpu.CompilerParams(
            dimension_semantics=("arbitrary",), vmem_limit_bytes=vmem),
        name="sample_window_attention",
    )(sinks, x, g, wq, qg, k, v, win_k, win_v, wo)


def kernel(x_prompt, x_sample, state_mlstm_C, state_mlstm_n, state_mlstm_m, cache_win_k, cache_win_v,
           ml_norm_g, ml_w_in, ml_b_i, ml_b_f, ml_head_g, ml_w_out, kv_norm_g, w_kv, k_norm_g,
           att_norm_g, att_w_q, q_norm_g, att_sinks, att_w_o, mlp_norm_g, mlp_w1, mlp_w2):
    B, L, D = x_prompt.shape
    NS, LS, _ = x_sample.shape
    assert ml_w_in.shape[0] == 1 and att_w_q.shape[0] == 1 and mlp_w1.shape[0] == 2

    w_in = ml_w_in[0]
    wqkvo = w_in[:, :ML_QKVO].astype(BF16)
    wg = jnp.pad(w_in[:, ML_QKVO:], ((0, 0), (0, LANES - 2 * ML_HEADS))).astype(BF16)
    bg = jnp.pad(jnp.concatenate([ml_b_i[0], ml_b_f[0]]), (0, LANES - 2 * ML_HEADS)).reshape(1, LANES)
    ml_g = ml_norm_g[0].reshape(1, D)
    hg = ml_head_g[0].reshape(1, ML_VO)
    wout = ml_w_out[0].astype(BF16)
    w1 = mlp_w1.astype(BF16)
    w2 = mlp_w2.astype(BF16)
    mlp_g = mlp_norm_g.reshape(2, 1, D)
    wkv = w_kv.astype(BF16)
    kv_g = kv_norm_g.reshape(1, D)
    kg = jnp.tile(k_norm_g, ATT_KVH).reshape(1, ATT_KV)
    att_g = att_norm_g[0].reshape(1, D)
    wq = att_w_q[0].astype(BF16)
    qg = jnp.tile(q_norm_g[0], 2).reshape(1, LANES)
    sinks = att_sinks[0]
    wo = att_w_o[0].astype(BF16)

    xp, p_C, p_n, p_m = _prompt_mixer(x_prompt, ml_g, wqkvo, wg, bg, hg, wout)
    xp, kp, vp = _mlp(xp.reshape(B * L, D), mlp_g[0], w1, w2, 0, kv=(kv_g, wkv, kg))
    xp = _prompt_attn(xp.reshape(B, L, D), kp.reshape(B, L, ATT_KV), vp.reshape(B, L, ATT_KV),
                      sinks, att_g, wq, qg, wo)
    y_prompt = _mlp(xp.reshape(B * L, D), mlp_g[1], w1, w2, 1).reshape(B, L, D)
    p_wk = kp.reshape(B, L, ATT_KV)[:, L - WINDOW:].reshape(B, WINDOW, ATT_KVH, ATT_HD)
    p_wv = vp.reshape(B, L, ATT_KV)[:, L - WINDOW:].reshape(B, WINDOW, ATT_KVH, ATT_HD)

    xs = jnp.pad(x_sample, ((0, 0), (0, SAMPLE_PAD - LS), (0, 0))).reshape(NS * SAMPLE_PAD, D)
    m0 = jnp.broadcast_to(state_mlstm_m[0][:, :, None], (NS, ML_HEADS, LANES))
    xs, s_C, s_n, s_m = _sample_mixer(xs, ml_g, wqkvo, wg, bg, hg, wout,
                                      state_mlstm_C[0], state_mlstm_n[0], m0, LS)
    xs, ks, vs = _mlp(xs, mlp_g[0], w1, w2, 0, kv=(kv_g, wkv, kg))
    to_t = lambda c: c.transpose(0, 2, 3, 1).reshape(NS, ATT_KV, WINDOW)
    from_t = lambda c: c.reshape(NS, ATT_KVH, ATT_HD, WINDOW).transpose(0, 3, 1, 2)
    xs, s_wk, s_wv = _sample_attn(xs, ks, vs, to_t(cache_win_k), to_t(cache_win_v),
                                  sinks, att_g, wq, qg, wo, LS)
    y_sample = _mlp(xs, mlp_g[1], w1, w2, 1).reshape(NS, SAMPLE_PAD, D)[:, :LS]

    return (y_prompt, y_sample,
            p_C[None], p_n[None], p_m[None, :, :, 0], p_wk, p_wv,
            s_C[None], s_n[None], s_m[None, :, :, 0],
            from_t(s_wk), from_t(s_wv))
```

```python
import functools

import jax
import jax.numpy as jnp
from jax import lax
from jax.experimental import pallas as pl
from jax.experimental.pallas import tpu as pltpu

F32 = jnp.float32
BF16 = jnp.bfloat16

D_MODEL = 1024
ML_HEADS = 4
ML_DK = 128
ML_DV = 256
ML_QK = ML_HEADS * ML_DK
ML_VO = ML_HEADS * ML_DV
ML_QKVO = 2 * ML_QK + 2 * ML_VO
GATE_SOFTCAP = 15.0
ATT_HD = 64
ATT_QH = 16
ATT_KVH = 4
ATT_GROUP = 4
ATT_KV = ATT_KVH * ATT_HD
WINDOW = 128
D_FF = 4 * D_MODEL
EPS = 1e-6
LOG2E = 1.4426950408889634

LANES = 128
SUBLANES = 8
VMEM_LIMIT_CAP = 56 * 1024 * 1024

PROMPT_CHUNK = 256
PROMPT_ROWS = 2
PROJ_COLS = 512
SAMPLE_PAD = SUBLANES
SAMPLE_NB = 16
ATTN_ROWS = 1024
ROW_TILE = 512
FF_TILE = 1024


def _vmem_limit(*block_bytes):
    need = 4 * sum(block_bytes) + (8 << 20)
    return int(min(max(need, 32 << 20), VMEM_LIMIT_CAP))


def _nbytes(shape, dtype):
    n = 1
    for s in shape:
        n *= s
    return n * jnp.dtype(dtype).itemsize


def _rms(x, g):
    return x * lax.rsqrt(jnp.mean(x * x, axis=-1, keepdims=True) + EPS) * g


def _dot(a, b):
    return jnp.dot(a, b, preferred_element_type=F32)


def _split3(x):
    hi = x.astype(BF16)
    r1 = x - hi.astype(F32)
    mid = r1.astype(BF16)
    lo = (r1 - mid.astype(F32)).astype(BF16)
    return hi, mid, lo


def _dot_exactish(m01, x):
    hi, mid, lo = _split3(x)
    return _dot(m01, hi) + _dot(m01, mid) + _dot(m01, lo)


def _gate_act(z):
    cap = GATE_SOFTCAP * jnp.tanh(z * (1.0 / GATE_SOFTCAP))
    lsig = jnp.minimum(cap, 0.0) - jnp.log1p(jnp.exp(-jnp.abs(cap)))
    lane = lax.broadcasted_iota(jnp.int32, z.shape, 1)
    return jnp.where(lane < ML_HEADS, cap, lsig)


_DONE = object()


def _rounds(chains):
    live = [(c[0], c[1], c[2] if len(c) > 2 else 1) for c in chains]
    rnd = 0
    while live:
        for item in list(live):
            gen, start, stride = item
            if rnd >= start and (rnd - start) % stride == 0 and next(gen, _DONE) is _DONE:
                live.remove(item)
        rnd += 1
        yield


def _run(chains):
    for _ in _rounds(chains):
        pass


def _head_out(hh, po, hg):
    hn = hh * lax.rsqrt(jnp.mean(hh * hh, axis=-1, keepdims=True) + EPS) * hg
    return jax.nn.sigmoid(po) * hn


def _head_slices(p, h):
    q = p[:, h * ML_DK:(h + 1) * ML_DK]
    k = p[:, ML_QK + h * ML_DK:ML_QK + (h + 1) * ML_DK] * (ML_DK ** -0.5)
    v = p[:, 2 * ML_QK + h * ML_DV:2 * ML_QK + (h + 1) * ML_DV]
    po = p[:, 2 * ML_QK + ML_VO + h * ML_DV:2 * ML_QK + ML_VO + (h + 1) * ML_DV]
    return q, k, v, po


def _prompt_mixer_kernel(xc_ref, xnext_ref, g_ref, wqkvo_ref, wg_ref, bg_ref, hg_ref, wout_ref,
                         y_ref, C_ref, n_ref, m_ref, p_s, gz_s, *, chunks_per_seq):
    R, T = xc_ref.shape[0], xc_ref.shape[1]
    f = pl.program_id(0)
    slot = f % 2

    def project(x_ref, s, r):
        xn = _rms(x_ref[r], g_ref[...]).astype(BF16)
        yield
        for j in range(ML_QKVO // PROJ_COLS):
            cols = slice(j * PROJ_COLS, (j + 1) * PROJ_COLS)
            p_s[s, r, :, cols] = _dot(xn, wqkvo_ref[:, cols])
            yield
        gz_s[s, r] = _dot(xn, wg_ref[...])
        yield

    @pl.when(f == 0)
    def _():
        _run([(project(xc_ref, 0, r), 0) for r in range(R)])

    @pl.when(f % chunks_per_seq == 0)
    def _():
        C_ref[...] = jnp.zeros_like(C_ref)
        n_ref[...] = jnp.zeros_like(n_ref)
        m_ref[...] = jnp.zeros_like(m_ref)

    def recurrence(r):
        G = _gate_act(gz_s[slot, r] + bg_ref[...])
        row = lax.broadcasted_iota(jnp.int32, (T, T), 0)
        col = lax.broadcasted_iota(jnp.int32, (T, T), 1)
        causal = col <= row
        Bc = _dot_exactish(jnp.where(causal, 1.0, 0.0).astype(BF16), G)
        yield
        Gt = G.T
        Bt = Bc.T
        yield
        parts = [None] * ML_HEADS

        def head(h):
            def cols(base, width):
                return p_s[slot, r, :, base + h * width:base + (h + 1) * width]

            q = cols(0, ML_DK)
            k = cols(ML_QK, ML_DK) * (ML_DK ** -0.5)
            v = cols(2 * ML_QK, ML_DV)
            qc, kc, vc = q.astype(BF16), k.astype(BF16), v.astype(BF16)
            qk = lax.dot_general(qc, kc, (((1,), (1,)), ((), ())), preferred_element_type=F32)
            yield
            li_col = G[:, h:h + 1]
            b_col = Bc[:, ML_HEADS + h:ML_HEADS + h + 1]
            dmat = jnp.where(causal, b_col - Bt[ML_HEADS + h:ML_HEADS + h + 1, :] + Gt[h:h + 1, :], -jnp.inf)
            m_prev = m_ref[r, h:h + 1, 0:1]
            inter = b_col + m_prev
            m_t = jnp.maximum(inter, jnp.max(dmat, axis=1, keepdims=True))
            yield
            s = qk * jnp.exp(dmat - m_t)
            a_inter = jnp.exp(inter - m_t)
            den = (jnp.sum(s, axis=1, keepdims=True)
                   + a_inter * jnp.sum(q * n_ref[r, h:h + 1, :], axis=1, keepdims=True))
            yield
            C = C_ref[r, h]
            num = _dot(s.astype(BF16), vc) + a_inter * _dot(qc, C.astype(BF16))
            yield
            hh = num * (1.0 / jnp.maximum(jnp.abs(den), jnp.exp(-m_t)))
            hs = _head_out(hh, cols(2 * ML_QK + ML_VO, ML_DV), hg_ref[:, h * ML_DV:(h + 1) * ML_DV])
            yield
            parts[h] = _dot(hs.astype(BF16), wout_ref[h * ML_DV:(h + 1) * ML_DV, :])
            yield
            m_new = m_t[T - 1:T, :]
            b_last = b_col[T - 1:T, :]
            decay = jnp.exp(b_last + m_prev - m_new)
            kw = k * jnp.exp(b_last - b_col + li_col - m_new)
            C_ref[r, h] = decay * C + lax.dot_general(kw.astype(BF16), vc, (((0,), (0,)), ((), ())),
                                                      preferred_element_type=F32)
            n_ref[r, h:h + 1, :] = decay * n_ref[r, h:h + 1, :] + jnp.sum(kw, axis=0, keepdims=True)
            m_ref[r, h:h + 1, :] = jnp.broadcast_to(m_new, (1, LANES))
            yield

        yield from _rounds([(head(h), 2 * h) for h in range(ML_HEADS)])
        y_ref[r] = xc_ref[r] + ((parts[0] + parts[1]) + (parts[2] + parts[3]))

    _run([(project(xnext_ref, 1 - slot, r), 0) for r in range(R)] + [(recurrence(r), r) for r in range(R)])


def _prompt_mixer(x, g, wqkvo, wg, bg, hg, wout):
    B, L, D = x.shape
    T = PROMPT_CHUNK
    assert L % T == 0
    R = PROMPT_ROWS
    assert B % R == 0
    nc = L // T
    steps = (B // R) * nc
    const = lambda f: (0, 0)
    cur = lambda f: (f // nc, f % nc, 0)
    nxt = lambda f: (jnp.minimum(f + 1, steps - 1) // nc, jnp.minimum(f + 1, steps - 1) % nc, 0)
    once = dict(pipeline_mode=pl.Buffered(1))
    vmem = _vmem_limit(3 * _nbytes((R, T, D), F32), _nbytes(wqkvo.shape, BF16) // 2,
                       _nbytes(wout.shape, BF16) // 2, _nbytes((R, ML_HEADS, ML_DK, ML_DV), F32),
                       2 * _nbytes((R, T, ML_QKVO), F32))
    y, C, n, m = pl.pallas_call(
        functools.partial(_prompt_mixer_kernel, chunks_per_seq=nc),
        grid=(steps,),
        in_specs=[
            pl.BlockSpec((R, T, D), cur),
            pl.BlockSpec((R, T, D), nxt),
            pl.BlockSpec((1, D), const),
            pl.BlockSpec((D, ML_QKVO), const, **once),
            pl.BlockSpec((D, LANES), const, **once),
            pl.BlockSpec((1, LANES), const),
            pl.BlockSpec((1, ML_VO), const),
            pl.BlockSpec((ML_VO, D), const, **once),
        ],
        out_specs=[
            pl.BlockSpec((R, T, D), cur),
            pl.BlockSpec((R, ML_HEADS, ML_DK, ML_DV), lambda f: (f // nc, 0, 0, 0)),
            pl.BlockSpec((R, ML_HEADS, ML_DK), lambda f: (f // nc, 0, 0)),
            pl.BlockSpec((R, ML_HEADS, LANES), lambda f: (f // nc, 0, 0)),
        ],
        out_shape=[
            jax.ShapeDtypeStruct((B, L, D), F32),
            jax.ShapeDtypeStruct((B, ML_HEADS, ML_DK, ML_DV), F32),
            jax.ShapeDtypeStruct((B, ML_HEADS, ML_DK), F32),
            jax.ShapeDtypeStruct((B, ML_HEADS, LANES), F32),
        ],
        scratch_shapes=[pltpu.VMEM((2, R, T, ML_QKVO), F32), pltpu.VMEM((2, R, T, LANES), F32)],
        compiler_params=pltpu.CompilerParams(
            dimension_semantics=("arbitrary",), vmem_limit_bytes=vmem),
        name="prompt_mlstm_mixer",
    )(x, x, g, wqkvo, wg, bg, hg, wout)
    return y, C, n, m


def _sample_mixer_kernel(x_ref, g_ref, wqkvo_ref, wg_ref, bg_ref, hg_ref, wout_ref,
                         C0_ref, n0_ref, m0_ref,
                         y_ref, C_ref, n_ref, m_ref, hs_s, *, n_valid):
    T = SAMPLE_PAD
    nb = C0_ref.shape[0]
    R = nb * T
    x = x_ref[...]
    xn = _rms(x, g_ref[...]).astype(BF16)
    p = _dot(xn, wqkvo_ref[...])

    rowt = lax.broadcasted_iota(jnp.int32, (R, LANES), 0) % T
    lane = lax.broadcasted_iota(jnp.int32, (R, LANES), 1)
    G = jnp.where(rowt < n_valid, _gate_act(_dot(xn, wg_ref[...]) + bg_ref[...]),
                  jnp.where(lane < ML_HEADS, -jnp.inf, 0.0))
    row = lax.broadcasted_iota(jnp.int32, (R, R), 0)
    col = lax.broadcasted_iota(jnp.int32, (R, R), 1)
    causal = (row // T == col // T) & (col <= row)
    Bc = _dot_exactish(jnp.where(causal, 1.0, 0.0).astype(BF16), jnp.where(lane < ML_HEADS, 0.0, G))
    Gt = G.T
    Bt = Bc.T

    def per_seq(fn):
        return jnp.concatenate([fn(a) for a in range(nb)], axis=0)

    def seq_last(colvec):
        return per_seq(lambda a: jnp.broadcast_to(colvec[T * a + T - 1:T * a + T, :], (T, 1)))

    def head(h):
        q, k, v, po = _head_slices(p, h)
        li_col = G[:, h:h + 1]
        b_col = Bc[:, ML_HEADS + h:ML_HEADS + h + 1]
        li_row = Gt[h:h + 1, :]
        b_row = Bt[ML_HEADS + h:ML_HEADS + h + 1, :]
        dmat = jnp.where(causal, b_col - (b_row - li_row), -jnp.inf)
        m_prev = per_seq(lambda a: jnp.broadcast_to(m0_ref[a, h:h + 1, 0:1], (T, 1)))
        n_rows = per_seq(lambda a: jnp.broadcast_to(n0_ref[a, h:h + 1, :], (T, ML_DK)))
        qc, kc, vc = q.astype(BF16), k.astype(BF16), v.astype(BF16)
        qk = lax.dot_general(qc, kc, (((1,), (1,)), ((), ())), preferred_element_type=F32)
        yield
        inter = b_col + m_prev
        m_t = jnp.maximum(inter, jnp.max(dmat, axis=1, keepdims=True))
        s = qk * jnp.exp(dmat - m_t)
        a_inter = jnp.exp(inter - m_t)
        yield
        qC = per_seq(lambda a: _dot(q[T * a:T * a + T, :], C0_ref[a, h]))
        yield
        num = _dot(s.astype(BF16), vc) + a_inter * qC
        den = jnp.sum(s, axis=1, keepdims=True) + a_inter * jnp.sum(q * n_rows, axis=1, keepdims=True)
        hh = num * (1.0 / jnp.maximum(jnp.abs(den), jnp.exp(-m_t)))
        hs_s[:, h * ML_DV:(h + 1) * ML_DV] = _head_out(
            hh, po, hg_ref[:, h * ML_DV:(h + 1) * ML_DV]).astype(BF16)
        yield
        m_new = seq_last(m_t)
        b_last = seq_last(b_col)
        decay = jnp.exp(b_last + m_prev - m_new)
        kw = k * jnp.exp(b_last - b_col + li_col - m_new)
        for a in range(nb):
            dec = decay[T * a:T * a + 1, :]
            kw_a = kw[T * a:T * a + T, :]
            upd = lax.dot_general(kw_a, v[T * a:T * a + T, :], (((0,), (0,)), ((), ())),
                                  preferred_element_type=F32)
            C_ref[a, h] = dec * C0_ref[a, h] + upd
            n_ref[a, h:h + 1, :] = dec * n0_ref[a, h:h + 1, :] + jnp.sum(kw_a, axis=0, keepdims=True)
            m_ref[a, h:h + 1, :] = jnp.broadcast_to(m_new[T * a:T * a + 1, :], (1, LANES))
            if a % 4 == 3:
                yield

    _run([(head(h), h) for h in range(ML_HEADS)])
    y_ref[...] = x + _dot(hs_s[...], wout_ref[...])


def _sample_mixer(x, g, wqkvo, wg, bg, hg, wout, C0, n0, m0, n_valid):
    NBT, D = x.shape
    nseq = C0.shape[0]
    nb = SAMPLE_NB
    R = nb * SAMPLE_PAD
    assert nseq % nb == 0 and NBT == nseq * SAMPLE_PAD
    const = lambda i: (0, 0)
    state_specs = [
        pl.BlockSpec((nb, ML_HEADS, ML_DK, ML_DV), lambda i: (i, 0, 0, 0)),
        pl.BlockSpec((nb, ML_HEADS, ML_DK), lambda i: (i, 0, 0)),
        pl.BlockSpec((nb, ML_HEADS, LANES), lambda i: (i, 0, 0)),
    ]
    vmem = _vmem_limit(2 * _nbytes((R, D), F32), _nbytes(wqkvo.shape, BF16), _nbytes(wout.shape, BF16),
                       2 * _nbytes((nb, ML_HEADS, ML_DK, ML_DV), F32), _nbytes((R, ML_QKVO), F32))
    return pl.pallas_call(
        functools.partial(_sample_mixer_kernel, n_valid=n_valid),
        grid=(nseq // nb,),
        in_specs=[
            pl.BlockSpec((R, D), lambda i: (i, 0)),
            pl.BlockSpec((1, D), const),
            pl.BlockSpec((D, ML_QKVO), const),
            pl.BlockSpec((D, LANES), const),
            pl.BlockSpec((1, LANES), const),
            pl.BlockSpec((1, ML_VO), const),
            pl.BlockSpec((ML_VO, D), const),
        ] + state_specs,
        out_specs=[pl.BlockSpec((R, D), lambda i: (i, 0))] + state_specs,
        out_shape=[
            jax.ShapeDtypeStruct((NBT, D), F32),
            jax.ShapeDtypeStruct(C0.shape, F32),
            jax.ShapeDtypeStruct(n0.shape, F32),
            jax.ShapeDtypeStruct(m0.shape, F32),
        ],
        scratch_shapes=[pltpu.VMEM((R, ML_VO), BF16)],
        compiler_params=pltpu.CompilerParams(
            dimension_semantics=("arbitrary",), vmem_limit_bytes=vmem),
        name="sample_mlstm_mixer",
    )(x, g, wqkvo, wg, bg, hg, wout, C0, n0, m0)


def _dot_exactish_right(x, m01):
    hi, mid, lo = _split3(x)
    return _dot(hi, m01) + _dot(mid, m01) + _dot(lo, m01)


def _mlp_kernel(x_ref, g_ref, w1_ref, w2_ref, *rest):
    x = x_ref[...]
    xn = _rms(x, g_ref[...]).astype(BF16)
    acc = x
    for c in range(D_FF // FF_TILE):
        hcol = _dot(xn, w1_ref[:, c * FF_TILE:(c + 1) * FF_TILE])
        hcol = jnp.square(jnp.maximum(hcol, 0.0)).astype(BF16)
        acc = acc + _dot(hcol, w2_ref[c * FF_TILE:(c + 1) * FF_TILE, :])
    if len(rest) == 1:
        (y_ref,) = rest
        y_ref[...] = acc
        return
    gkv_ref, wkv_ref, kg_ref, y_ref, k_ref, v_ref = rest
    y_ref[...] = acc
    kv = _dot(_rms(acc, gkv_ref[...]).astype(BF16), wkv_ref[...])
    kraw = kv[:, :ATT_KV]
    v_ref[...] = kv[:, ATT_KV:]
    r = lax.broadcasted_iota(jnp.int32, (ATT_KV, ATT_KV), 0) // ATT_HD
    c = lax.broadcasted_iota(jnp.int32, (ATT_KV, ATT_KV), 1) // ATT_HD
    seg = jnp.where(r == c, 1.0, 0.0).astype(BF16)
    ss = _dot_exactish_right(kraw * kraw, seg)
    k_ref[...] = kraw * lax.rsqrt(ss * (1.0 / ATT_HD) + EPS) * kg_ref[...]


def _mlp(x, g, w1, w2, layer, kv=None):
    N, D = x.shape
    tm = min(ROW_TILE, N)
    assert N % tm == 0
    const = lambda i: (0, 0)
    rows = lambda i: (i, 0)
    once = dict(pipeline_mode=pl.Buffered(1))
    in_specs = [
        pl.BlockSpec((tm, D), rows),
        pl.BlockSpec((1, D), const),
        pl.BlockSpec((None, D, D_FF), lambda i: (layer, 0, 0), **once),
        pl.BlockSpec((None, D_FF, D), lambda i: (layer, 0, 0), **once),
    ]
    out_specs = [pl.BlockSpec((tm, D), rows)]
    out_shape = [jax.ShapeDtypeStruct((N, D), F32)]
    args = [x, g, w1, w2]
    if kv is not None:
        in_specs += [pl.BlockSpec((1, D), const), pl.BlockSpec((D, 2 * ATT_KV), const, **once),
                     pl.BlockSpec((1, ATT_KV), const)]
        out_specs += [pl.BlockSpec((tm, ATT_KV), rows)] * 2
        out_shape += [jax.ShapeDtypeStruct((N, ATT_KV), F32)] * 2
        args += list(kv)
    vmem = _vmem_limit(2 * _nbytes((tm, D), F32), _nbytes(w1.shape[1:], BF16) // 2,
                       _nbytes(w2.shape[1:], BF16) // 2, 2 * _nbytes((tm, FF_TILE), F32))
    out = pl.pallas_call(
        _mlp_kernel,
        grid=(N // tm,),
        in_specs=in_specs,
        out_specs=out_specs,
        out_shape=out_shape,
        compiler_params=pltpu.CompilerParams(
            dimension_semantics=("arbitrary",), vmem_limit_bytes=vmem),
        name="sqrelu_mlp",
    )(*args)
    return out[0] if kv is None else out


def _pair_queries(qraw, pr, qscale, col0=0):
    TQ = qraw.shape[0]
    lo = lax.broadcasted_iota(jnp.int32, (TQ, LANES), 1) < ATT_HD
    out = []
    for e in range(2):
        kvh = 2 * pr + e
        for g in range(ATT_GROUP):
            cc, half = divmod(g, 2)
            c0 = (2 * kvh + cc) * LANES - col0
            q2 = qraw[:, c0:c0 + LANES]
            qm = jnp.where(lo, q2, 0.0) if half == 0 else jnp.where(lo, 0.0, q2)
            ss = jnp.sum(qm * qm, axis=1, keepdims=True)
            qn = qm * lax.rsqrt(ss * (1.0 / ATT_HD) + EPS)
            if qscale is not None:
                qn = qn * qscale
            out.append(qn if half == e else pltpu.roll(qn, ATT_HD, 1))
    return out


def _pair_outputs(o_heads, rden, pr, store):
    TQ = o_heads[0].shape[0]
    lo = lax.broadcasted_iota(jnp.int32, (TQ, LANES), 1) < ATT_HD
    for e in range(2):
        kvh = 2 * pr + e
        for cc in range(2):
            tiles = []
            for half in range(2):
                o = o_heads[e * ATT_GROUP + 2 * cc + half] * rden[e * ATT_GROUP + 2 * cc + half]
                tiles.append(o if half == e else pltpu.roll(o, ATT_HD, 1))
            store((2 * kvh + cc) * LANES, jnp.where(lo, tiles[0], tiles[1]))


def _softmax_with_sink(parts, sink):
    M = sink
    for s in parts:
        M = jnp.maximum(M, jnp.max(s, axis=1, keepdims=True))
    ps = [jnp.exp2(s - M) for s in parts]
    den = jnp.exp2(sink - M)
    for p in ps:
        den = den + jnp.sum(p, axis=1, keepdims=True)
    return ps, 1.0 / den


def _prompt_attn_kernel(sinks_ref, x_ref, g_ref, wq_ref, qg_ref, kp_ref, kc_ref, vp_ref, vc_ref, wo_ref,
                        y_ref):
    TQ = WINDOW
    TR = x_ref.shape[0]
    kj = lax.broadcasted_iota(jnp.int32, (WINDOW + TQ, TQ), 0)
    qi = lax.broadcasted_iota(jnp.int32, (WINDOW + TQ, TQ), 1)
    band = (qi + WINDOW - kj >= 0) & (qi - kj <= 0)
    band_first = band & ((kj >= WINDOW) | (pl.program_id(1) > 0))
    kscale = qg_ref[...] * (ATT_HD ** -0.5 * LOG2E)
    slabs = []
    for pr in range(ATT_KVH // 2):
        sl = slice(pr * LANES, (pr + 1) * LANES)
        slabs.append(((jnp.concatenate([kp_ref[:, sl], kc_ref[:, sl]], axis=0) * kscale).astype(BF16),
                      jnp.concatenate([vp_ref[:, sl], vc_ref[:, sl]], axis=0).T))

    PW = 2 * ATT_GROUP * ATT_HD
    zeros_hd = jnp.zeros((ATT_HD, TQ), F32)
    ones_rows = jnp.ones((2 * SUBLANES, WINDOW + TQ), F32)

    def chain(sb):
        rows = slice(sb * TQ, (sb + 1) * TQ)
        keys = slice(sb * TQ, sb * TQ + WINDOW + TQ)
        mask = band_first if sb == 0 else band
        x = x_ref[rows, :]
        xn = _rms(x, g_ref[...]).astype(BF16)
        acc = x
        for pr in range(ATT_KVH // 2):
            kslab, vslab_t = slabs[pr]
            qt = _dot(xn, wq_ref[:, pr * PW:(pr + 1) * PW]).T
            yield
            tiles, sink_rows = [], []
            for e in range(2):
                for g in range(ATT_GROUP):
                    blk = qt[(e * ATT_GROUP + g) * ATT_HD:(e * ATT_GROUP + g + 1) * ATT_HD, :]
                    qn = blk * lax.rsqrt(jnp.sum(blk * blk, axis=0, keepdims=True) * (1.0 / ATT_HD) + EPS)
                    tiles.append(jnp.concatenate([qn, zeros_hd] if e == 0 else [zeros_hd, qn], axis=0))
                    sink_rows.append(jnp.full((1, TQ), sinks_ref[(2 * pr + e) * ATT_GROUP + g] * LOG2E, F32))
            qmat = jnp.concatenate(tiles, axis=1).astype(BF16)
            st = _dot(kslab[keys, :], qmat)
            yield
            parts = []
            for e in range(2):
                ps, ms = [], []
                for g in range(ATT_GROUP):
                    c0 = (e * ATT_GROUP + g) * TQ
                    s_h = jnp.where(mask, st[:, c0:c0 + TQ], -jnp.inf)
                    m_h = jnp.maximum(jnp.max(s_h, axis=0, keepdims=True), sink_rows[e * ATT_GROUP + g])
                    ps.append(jnp.exp2(s_h - m_h).astype(BF16))
                    ms.append(m_h)
                p_e = jnp.concatenate(ps, axis=1)
                v_aug = jnp.concatenate([vslab_t[e * ATT_HD:(e + 1) * ATT_HD, keys], ones_rows],
                                        axis=0).astype(BF16)
                ot = _dot(v_aug, p_e)
                yield
                sink_term = jnp.exp2(jnp.concatenate(sink_rows[e * ATT_GROUP:(e + 1) * ATT_GROUP], axis=1)
                                     - jnp.concatenate(ms, axis=1))
                on = ot[0:ATT_HD, :] * (1.0 / (ot[ATT_HD:ATT_HD + 1, :] + sink_term))
                parts += [on[:, g * TQ:(g + 1) * TQ] for g in range(ATT_GROUP)]
            o_pair = jnp.concatenate(parts, axis=0).T.astype(BF16)
            acc = acc + _dot(o_pair, wo_ref[pr * PW:(pr + 1) * PW, :])
            yield
        y_ref[rows, :] = acc

    _run([(chain(sb), 0) for sb in range(TR // TQ)])


def _prompt_attn(x, k, v, sinks, g, wq, qg, wo):
    B, L, D = x.shape
    TR = ATTN_ROWS
    per = TR // WINDOW
    assert L % TR == 0 and TR % WINDOW == 0
    const = lambda b, i: (0, 0)
    cur = lambda b, i: (b, i, 0)
    prev = lambda b, i: (b, jnp.maximum(i * per - 1, 0), 0)
    vmem = _vmem_limit(2 * _nbytes((TR, D), F32), _nbytes(wq.shape, BF16), _nbytes(wo.shape, BF16),
                       4 * _nbytes((TR, ATT_KV), F32), 8 * _nbytes((8 * WINDOW, 2 * WINDOW), F32))
    return pl.pallas_call(
        _prompt_attn_kernel,
        grid=(B, L // TR),
        in_specs=[
            pl.BlockSpec(memory_space=pltpu.SMEM),
            pl.BlockSpec((None, TR, D), cur),
            pl.BlockSpec((1, D), const),
            pl.BlockSpec((D, D), const),
            pl.BlockSpec((1, LANES), const),
            pl.BlockSpec((None, WINDOW, ATT_KV), prev),
            pl.BlockSpec((None, TR, ATT_KV), cur),
            pl.BlockSpec((None, WINDOW, ATT_KV), prev),
            pl.BlockSpec((None, TR, ATT_KV), cur),
            pl.BlockSpec((D, D), const),
        ],
        out_specs=pl.BlockSpec((None, TR, D), cur),
        out_shape=jax.ShapeDtypeStruct((B, L, D), F32),
        compiler_params=pltpu.CompilerParams(
            dimension_semantics=("arbitrary", "arbitrary"), vmem_limit_bytes=vmem),
        name="prompt_window_attention",
    )(sinks, x, g, wq, qg, k, k, v, v, wo)


def _sample_attn_kernel(sinks_ref, x_ref, g_ref, wq_ref, qg_ref, kc_ref, vc_ref, wk_ref, wv_ref, wo_ref,
                        y_ref, nk_ref, nv_ref, o_s, *, n_valid):
    T = SAMPLE_PAD
    nb = wk_ref.shape[0]
    R = nb * T
    assert R == WINDOW
    x = x_ref[...]
    q = _dot(_rms(x, g_ref[...]).astype(BF16), wq_ref[...])
    qscale = qg_ref[...] * (ATT_HD ** -0.5 * LOG2E)
    knew = kc_ref[...]
    vnew = vc_ref[...]

    cmask = (lax.broadcasted_iota(jnp.int32, (R, WINDOW), 1)
             >= lax.broadcasted_iota(jnp.int32, (R, WINDOW), 0) % T)
    row = lax.broadcasted_iota(jnp.int32, (R, R), 0)
    col = lax.broadcasted_iota(jnp.int32, (R, R), 1)
    nmask = (row // T == col // T) & (col % T <= row % T) & (col % T < n_valid)

    def store(off, val):
        o_s[:, off:off + LANES] = val.astype(BF16)

    def regroup(per_seq, i):
        return jnp.concatenate([per_seq[a][i * T:(i + 1) * T, :] for a in range(nb)], axis=0)

    def pair(pr):
        sl = slice(pr * LANES, (pr + 1) * LANES)
        qs = _pair_queries(q, pr, qscale)
        S_new = lax.dot_general(jnp.concatenate(qs, axis=0).astype(BF16), knew[:, sl].astype(BF16),
                                (((1,), (1,)), ((), ())), preferred_element_type=F32)
        yield
        sc = []
        for a in range(nb):
            q_a = jnp.concatenate([qi[T * a:T * a + T, :] for qi in qs], axis=0).astype(BF16)
            sc.append(_dot(q_a, wk_ref[a, sl, :].astype(BF16)))
            if a % 4 == 3:
                yield
        pcs, pns, rden = [], [], []
        for i in range(2 * ATT_GROUP):
            s_c = jnp.where(cmask, regroup(sc, i), -jnp.inf)
            s_n = jnp.where(nmask, S_new[i * R:(i + 1) * R, :], -jnp.inf)
            (p_c, p_n), r = _softmax_with_sink([s_c, s_n], sinks_ref[pr * 2 * ATT_GROUP + i] * LOG2E)
            pcs.append(p_c)
            pns.append(p_n)
            rden.append(r)
            if i % 2 == 1:
                yield
        O_new = _dot(jnp.concatenate(pns, axis=0).astype(BF16), vnew[:, sl].astype(BF16))
        yield
        oc = []
        for a in range(nb):
            p_a = jnp.concatenate([pc[T * a:T * a + T, :] for pc in pcs], axis=0).astype(BF16)
            oc.append(lax.dot_general(p_a, wv_ref[a, sl, :].astype(BF16), (((1,), (1,)), ((), ())),
                                      preferred_element_type=F32))
            if a % 4 == 3:
                yield
        _pair_outputs([O_new[i * R:(i + 1) * R, :] + regroup(oc, i) for i in range(2 * ATT_GROUP)],
                      rden, pr, store)
        yield

    def roll_cache():
        keep = lax.broadcasted_iota(jnp.int32, (ATT_KV, WINDOW), 1) < WINDOW - n_valid
        knew_t = knew.T
        vnew_t = vnew.T
        yield
        for a in range(nb):
            shift = (WINDOW - n_valid - T * a) % WINDOW
            nk_ref[a] = jnp.where(keep, pltpu.roll(wk_ref[a], WINDOW - n_valid, 1),
                                  pltpu.roll(knew_t, shift, 1) if shift else knew_t)
            nv_ref[a] = jnp.where(keep, pltpu.roll(wv_ref[a], WINDOW - n_valid, 1),
                                  pltpu.roll(vnew_t, shift, 1) if shift else vnew_t)
            yield

    _run([(pair(0), 0), (pair(1), 2), (roll_cache(), 0)])
    y_ref[...] = x + _dot(o_s[...], wo_ref[...])


def _sample_attn(x, k, v, win_k, win_v, sinks, g, wq, qg, wo, n_valid):
    NBT, D = x.shape
    nseq = win_k.shape[0]
    nb = SAMPLE_NB
    R = nb * SAMPLE_PAD
    assert nseq % nb == 0 and NBT == nseq * SAMPLE_PAD
    const = lambda i: (0, 0)
    rows = lambda i: (i, 0)
    cache = pl.BlockSpec((nb, ATT_KV, WINDOW), lambda i: (i, 0, 0))
    vmem = _vmem_limit(2 * _nbytes((R, D), F32), _nbytes(wq.shape, BF16), _nbytes(wo.shape, BF16),
                       4 * _nbytes((nb, ATT_KV, WINDOW), F32), 2 * _nbytes((R, D), F32))
    return pl.pallas_call(
        functools.partial(_sample_attn_kernel, n_valid=n_valid),
        grid=(nseq // nb,),
        in_specs=[
            pl.BlockSpec(memory_space=pltpu.SMEM),
            pl.BlockSpec((R, D), rows),
            pl.BlockSpec((1, D), const),
            pl.BlockSpec((D, D), const),
            pl.BlockSpec((1, LANES), const),
            pl.BlockSpec((R, ATT_KV), rows),
            pl.BlockSpec((R, ATT_KV), rows),
            cache, cache,
            pl.BlockSpec((D, D), const),
        ],
        out_specs=[pl.BlockSpec((R, D), rows), cache, cache],
        out_shape=[jax.ShapeDtypeStruct((NBT, D), F32),
                   jax.ShapeDtypeStruct(win_k.shape, F32), jax.ShapeDtypeStruct(win_v.shape, F32)],
        scratch_shapes=[pltpu.VMEM((R, D), BF16)],
        compiler_params=pltpu.CompilerParams(
            dimension_semantics=("arbitrary",), vmem_limit_bytes=vmem),
        name="sample_window_attention",
    )(sinks, x, g, wq, qg, k, v, win_k, win_v, wo)


def kernel(x_prompt, x_sample, state_mlstm_C, state_mlstm_n, state_mlstm_m, cache_win_k, cache_win_v,
           ml_norm_g, ml_w_in, ml_b_i, ml_b_f, ml_head_g, ml_w_out, kv_norm_g, w_kv, k_norm_g,
           att_norm_g, att_w_q, q_norm_g, att_sinks, att_w_o, mlp_norm_g, mlp_w1, mlp_w2):
    B, L, D = x_prompt.shape
    NS, LS, _ = x_sample.shape
    assert ml_w_in.shape[0] == 1 and att_w_q.shape[0] == 1 and mlp_w1.shape[0] == 2

    w_in = ml_w_in[0]
    wqkvo = w_in.astype(BF16)
    wg = jnp.pad(w_in[:, ML_QKVO:], ((0, 0), (0, LANES - 2 * ML_HEADS))).astype(BF16)
    bg = jnp.pad(jnp.concatenate([ml_b_i[0], ml_b_f[0]]), (0, LANES - 2 * ML_HEADS)).reshape(1, LANES)
    ml_g = ml_norm_g[0].reshape(1, D)
    hg = ml_head_g[0].reshape(1, ML_VO)
    wout = ml_w_out[0].astype(BF16)
    w1 = mlp_w1.astype(BF16)
    w2 = mlp_w2.astype(BF16)
    mlp_g = mlp_norm_g.reshape(2, 1, D)
    wkv = w_kv.astype(BF16)
    kv_g = kv_norm_g.reshape(1, D)
    kg = jnp.tile(k_norm_g, ATT_KVH).reshape(1, ATT_KV)
    att_g = att_norm_g[0].reshape(1, D)
    wq = att_w_q[0].astype(BF16)
    qg = jnp.tile(q_norm_g[0], 2).reshape(1, LANES)
    sinks = att_sinks[0]
    wo = att_w_o[0].astype(BF16)

    xp, p_C, p_n, p_m = _prompt_mixer(x_prompt, ml_g, wqkvo, wg, bg, hg, wout)
    xp, kp, vp = _mlp(xp.reshape(B * L, D), mlp_g[0], w1, w2, 0, kv=(kv_g, wkv, kg))
    xp = _prompt_attn(xp.reshape(B, L, D), kp.reshape(B, L, ATT_KV), vp.reshape(B, L, ATT_KV),
                      sinks, att_g, wq, qg, wo)
    y_prompt = _mlp(xp.reshape(B * L, D), mlp_g[1], w1, w2, 1).reshape(B, L, D)
    p_wk = kp.reshape(B, L, ATT_KV)[:, L - WINDOW:].reshape(B, WINDOW, ATT_KVH, ATT_HD)
    p_wv = vp.reshape(B, L, ATT_KV)[:, L - WINDOW:].reshape(B, WINDOW, ATT_KVH, ATT_HD)

    def pad8(a):
        return jnp.pad(a.reshape(NS, LS, -1), ((0, 0), (0, SAMPLE_PAD - LS), (0, 0))).reshape(NS * SAMPLE_PAD, -1)

    def unpad8(a):
        return a.reshape(NS, SAMPLE_PAD, -1)[:, :LS].reshape(NS * LS, -1)

    m0 = jnp.broadcast_to(state_mlstm_m[0][:, :, None], (NS, ML_HEADS, LANES))
    xs, s_C, s_n, s_m = _sample_mixer(pad8(x_sample), ml_g, wqkvo, wg, bg, hg, wout,
                                      state_mlstm_C[0], state_mlstm_n[0], m0, LS)
    xs, ks, vs = _mlp(unpad8(xs), mlp_g[0], w1, w2, 0, kv=(kv_g, wkv, kg))
    to_t = lambda c: c.transpose(0, 2, 3, 1).reshape(NS, ATT_KV, WINDOW)
    from_t = lambda c: c.reshape(NS, ATT_KVH, ATT_HD, WINDOW).transpose(0, 3, 1, 2)
    xs, s_wk, s_wv = _sample_attn(pad8(xs), pad8(ks), pad8(vs), to_t(cache_win_k), to_t(cache_win_v),
                                  sinks, att_g, wq, qg, wo, LS)
    y_sample = _mlp(unpad8(xs), mlp_g[1], w1, w2, 1).reshape(NS, LS, D)

    return (y_prompt, y_sample,
            p_C[None], p_n[None], p_m[None, :, :, 0], p_wk, p_wv,
            s_C[None], s_n[None], s_m[None, :, :, 0],
            from_t(s_wk), from_t(s_wv))
```

```python
import functools

import jax
import jax.numpy as jnp
from jax import lax
from jax.experimental import pallas as pl
from jax.experimental.pallas import tpu as pltpu

F32 = jnp.float32
BF16 = jnp.bfloat16

D_MODEL = 1024
ML_HEADS = 4
ML_DK = 128
ML_DV = 256
ML_QK = ML_HEADS * ML_DK
ML_VO = ML_HEADS * ML_DV
ML_QKVO = 2 * ML_QK + 2 * ML_VO
GATE_SOFTCAP = 15.0
ATT_HD = 64
ATT_QH = 16
ATT_KVH = 4
ATT_GROUP = 4
ATT_KV = ATT_KVH * ATT_HD
WINDOW = 128
D_FF = 4 * D_MODEL
EPS = 1e-6
LOG2E = 1.4426950408889634

LANES = 128
SUBLANES = 8
VMEM_LIMIT_CAP = 56 * 1024 * 1024

PROMPT_CHUNK = 256
PROMPT_ROWS = 2
PROJ_COLS = 512
SAMPLE_PAD = SUBLANES
SAMPLE_NB = 16
ATTN_ROWS = 1024
ROW_TILE = 512
FF_TILE = 1024


def _vmem_limit(*block_bytes):
    need = 4 * sum(block_bytes) + (8 << 20)
    return int(min(max(need, 32 << 20), VMEM_LIMIT_CAP))


def _nbytes(shape, dtype):
    n = 1
    for s in shape:
        n *= s
    return n * jnp.dtype(dtype).itemsize


def _rms(x, g):
    return x * lax.rsqrt(jnp.mean(x * x, axis=-1, keepdims=True) + EPS) * g


def _dot(a, b):
    return jnp.dot(a, b, preferred_element_type=F32)


def _split3(x):
    hi = x.astype(BF16)
    r1 = x - hi.astype(F32)
    mid = r1.astype(BF16)
    lo = (r1 - mid.astype(F32)).astype(BF16)
    return hi, mid, lo


def _dot_exactish(m01, x):
    hi, mid, lo = _split3(x)
    return _dot(m01, hi) + _dot(m01, mid) + _dot(m01, lo)


def _gate_act(z):
    cap = GATE_SOFTCAP * jnp.tanh(z * (1.0 / GATE_SOFTCAP))
    lsig = jnp.minimum(cap, 0.0) - jnp.log1p(jnp.exp(-jnp.abs(cap)))
    lane = lax.broadcasted_iota(jnp.int32, z.shape, 1)
    return jnp.where(lane < ML_HEADS, cap, lsig)


_DONE = object()


def _rounds(chains):
    live = [(c[0], c[1], c[2] if len(c) > 2 else 1) for c in chains]
    rnd = 0
    while live:
        for item in list(live):
            gen, start, stride = item
            if rnd >= start and (rnd - start) % stride == 0 and next(gen, _DONE) is _DONE:
                live.remove(item)
        rnd += 1
        yield


def _run(chains):
    for _ in _rounds(chains):
        pass


def _head_out(hh, po, hg):
    hn = hh * lax.rsqrt(jnp.mean(hh * hh, axis=-1, keepdims=True) + EPS) * hg
    return jax.nn.sigmoid(po) * hn


def _head_slices(p, h):
    q = p[:, h * ML_DK:(h + 1) * ML_DK]
    k = p[:, ML_QK + h * ML_DK:ML_QK + (h + 1) * ML_DK] * (ML_DK ** -0.5)
    v = p[:, 2 * ML_QK + h * ML_DV:2 * ML_QK + (h + 1) * ML_DV]
    po = p[:, 2 * ML_QK + ML_VO + h * ML_DV:2 * ML_QK + ML_VO + (h + 1) * ML_DV]
    return q, k, v, po


def _prompt_mixer_kernel(xc_ref, xnext_ref, g_ref, wqkvo_ref, wg_ref, bg_ref, hg_ref, wout_ref, *rest,
                         chunks_per_seq, n_cast):
    cast_in = rest[:n_cast]
    y_ref, C_ref, n_ref, m_ref = rest[n_cast:n_cast + 4]
    cast_out = rest[n_cast + 4:2 * n_cast + 4]
    p_s, gz_s = rest[2 * n_cast + 4:]
    R, T = xc_ref.shape[0], xc_ref.shape[1]
    f = pl.program_id(0)
    slot = f % 2

    def project(x_ref, s, r):
        xn = _rms(x_ref[r], g_ref[...]).astype(BF16)
        yield
        for j in range(ML_QKVO // PROJ_COLS):
            cols = slice(j * PROJ_COLS, (j + 1) * PROJ_COLS)
            p_s[s, r, :, cols] = _dot(xn, wqkvo_ref[:, cols])
            yield
        gz_s[s, r] = _dot(xn, wg_ref[...])
        yield

    @pl.when(f == 0)
    def _():
        _run([(project(xc_ref, 0, r), 0) for r in range(R)])

    @pl.when(f % chunks_per_seq == 0)
    def _():
        C_ref[...] = jnp.zeros_like(C_ref)
        n_ref[...] = jnp.zeros_like(n_ref)
        m_ref[...] = jnp.zeros_like(m_ref)

    def recurrence(r):
        G = _gate_act(gz_s[slot, r] + bg_ref[...])
        row = lax.broadcasted_iota(jnp.int32, (T, T), 0)
        col = lax.broadcasted_iota(jnp.int32, (T, T), 1)
        causal = col <= row
        Bc = _dot_exactish(jnp.where(causal, 1.0, 0.0).astype(BF16), G)
        yield
        Gt = G.T
        Bt = Bc.T
        yield
        parts = [None] * ML_HEADS

        def head(h):
            def cols(base, width):
                return p_s[slot, r, :, base + h * width:base + (h + 1) * width]

            q = cols(0, ML_DK)
            k = cols(ML_QK, ML_DK) * (ML_DK ** -0.5)
            v = cols(2 * ML_QK, ML_DV)
            qc, kc, vc = q.astype(BF16), k.astype(BF16), v.astype(BF16)
            qk = lax.dot_general(qc, kc, (((1,), (1,)), ((), ())), preferred_element_type=F32)
            yield
            li_col = G[:, h:h + 1]
            b_col = Bc[:, ML_HEADS + h:ML_HEADS + h + 1]
            dmat = jnp.where(causal, b_col - Bt[ML_HEADS + h:ML_HEADS + h + 1, :] + Gt[h:h + 1, :], -jnp.inf)
            m_prev = m_ref[r, h:h + 1, 0:1]
            inter = b_col + m_prev
            m_t = jnp.maximum(inter, jnp.max(dmat, axis=1, keepdims=True))
            yield
            s = qk * jnp.exp(dmat - m_t)
            a_inter = jnp.exp(inter - m_t)
            den = (jnp.sum(s, axis=1, keepdims=True)
                   + a_inter * jnp.sum(q * n_ref[r, h:h + 1, :], axis=1, keepdims=True))
            yield
            C = C_ref[r, h]
            num = _dot(s.astype(BF16), vc) + a_inter * _dot(qc, C.astype(BF16))
            yield
            hh = num * (1.0 / jnp.maximum(jnp.abs(den), jnp.exp(-m_t)))
            hs = _head_out(hh, cols(2 * ML_QK + ML_VO, ML_DV), hg_ref[:, h * ML_DV:(h + 1) * ML_DV])
            yield
            parts[h] = _dot(hs.astype(BF16), wout_ref[h * ML_DV:(h + 1) * ML_DV, :])
            yield
            m_new = m_t[T - 1:T, :]
            b_last = b_col[T - 1:T, :]
            decay = jnp.exp(b_last + m_prev - m_new)
            kw = k * jnp.exp(b_last - b_col + li_col - m_new)
            C_ref[r, h] = decay * C + lax.dot_general(kw.astype(BF16), vc, (((0,), (0,)), ((), ())),
                                                      preferred_element_type=F32)
            n_ref[r, h:h + 1, :] = decay * n_ref[r, h:h + 1, :] + jnp.sum(kw, axis=0, keepdims=True)
            m_ref[r, h:h + 1, :] = jnp.broadcast_to(m_new, (1, LANES))
            yield

        yield from _rounds([(head(h), 2 * h) for h in range(ML_HEADS)])
        y_ref[r] = xc_ref[r] + ((parts[0] + parts[1]) + (parts[2] + parts[3]))

    def casts():
        for src, dst in zip(cast_in, cast_out):
            dst[...] = src[...].astype(BF16)
            yield

    _run([(project(xnext_ref, 1 - slot, r), 0) for r in range(R)] + [(recurrence(r), r) for r in range(R)]
         + [(casts(), 0)])


def _prompt_mixer(x, g, wqkvo, wg, bg, hg, wout, to_cast):
    B, L, D = x.shape
    T = PROMPT_CHUNK
    assert L % T == 0
    R = PROMPT_ROWS
    assert B % R == 0
    nc = L // T
    steps = (B // R) * nc
    const = lambda f: (0, 0)
    cur = lambda f: (f // nc, f % nc, 0)
    nxt = lambda f: (jnp.minimum(f + 1, steps - 1) // nc, jnp.minimum(f + 1, steps - 1) % nc, 0)
    once = dict(pipeline_mode=pl.Buffered(1))
    vmem = _vmem_limit(3 * _nbytes((R, T, D), F32), _nbytes(wqkvo.shape, BF16) // 2,
                       _nbytes(wout.shape, BF16) // 2, _nbytes((R, ML_HEADS, ML_DK, ML_DV), F32),
                       2 * _nbytes((R, T, ML_QKVO), F32))
    for w in to_cast:
        assert w.shape[0] % (steps * 2 * SUBLANES) == 0
    cast_specs = [pl.BlockSpec((w.shape[0] // steps, w.shape[1]), lambda f: (f, 0)) for w in to_cast]
    outs = pl.pallas_call(
        functools.partial(_prompt_mixer_kernel, chunks_per_seq=nc, n_cast=len(to_cast)),
        grid=(steps,),
        in_specs=[
            pl.BlockSpec((R, T, D), cur),
            pl.BlockSpec((R, T, D), nxt),
            pl.BlockSpec((1, D), const),
            pl.BlockSpec((D, ML_QKVO), const, **once),
            pl.BlockSpec((D, LANES), const, **once),
            pl.BlockSpec((1, LANES), const),
            pl.BlockSpec((1, ML_VO), const),
            pl.BlockSpec((ML_VO, D), const, **once),
        ] + cast_specs,
        out_specs=[
            pl.BlockSpec((R, T, D), cur),
            pl.BlockSpec((R, ML_HEADS, ML_DK, ML_DV), lambda f: (f // nc, 0, 0, 0)),
            pl.BlockSpec((R, ML_HEADS, ML_DK), lambda f: (f // nc, 0, 0)),
            pl.BlockSpec((R, ML_HEADS, LANES), lambda f: (f // nc, 0, 0)),
        ] + cast_specs,
        out_shape=[
            jax.ShapeDtypeStruct((B, L, D), F32),
            jax.ShapeDtypeStruct((B, ML_HEADS, ML_DK, ML_DV), F32),
            jax.ShapeDtypeStruct((B, ML_HEADS, ML_DK), F32),
            jax.ShapeDtypeStruct((B, ML_HEADS, LANES), F32),
        ] + [jax.ShapeDtypeStruct(w.shape, BF16) for w in to_cast],
        scratch_shapes=[pltpu.VMEM((2, R, T, ML_QKVO), F32), pltpu.VMEM((2, R, T, LANES), F32)],
        compiler_params=pltpu.CompilerParams(
            dimension_semantics=("arbitrary",), vmem_limit_bytes=vmem),
        name="prompt_mlstm_mixer",
    )(x, x, g, wqkvo, wg, bg, hg, wout, *to_cast)
    return outs


def _sample_mixer_kernel(x_ref, g_ref, wqkvo_ref, wg_ref, bg_ref, hg_ref, wout_ref,
                         C0_ref, n0_ref, m0_ref,
                         y_ref, C_ref, n_ref, m_ref, hs_s, *, n_valid):
    T = SAMPLE_PAD
    nb = C0_ref.shape[0]
    R = nb * T
    x = x_ref[...]
    xn = _rms(x, g_ref[...]).astype(BF16)
    p = _dot(xn, wqkvo_ref[...])

    rowt = lax.broadcasted_iota(jnp.int32, (R, LANES), 0) % T
    lane = lax.broadcasted_iota(jnp.int32, (R, LANES), 1)
    G = jnp.where(rowt < n_valid, _gate_act(_dot(xn, wg_ref[...]) + bg_ref[...]),
                  jnp.where(lane < ML_HEADS, -jnp.inf, 0.0))
    row = lax.broadcasted_iota(jnp.int32, (R, R), 0)
    col = lax.broadcasted_iota(jnp.int32, (R, R), 1)
    causal = (row // T == col // T) & (col <= row)
    Bc = _dot_exactish(jnp.where(causal, 1.0, 0.0).astype(BF16), jnp.where(lane < ML_HEADS, 0.0, G))
    Gt = G.T
    Bt = Bc.T

    def per_seq(fn):
        return jnp.concatenate([fn(a) for a in range(nb)], axis=0)

    def seq_last(colvec):
        return per_seq(lambda a: jnp.broadcast_to(colvec[T * a + T - 1:T * a + T, :], (T, 1)))

    def head(h):
        q, k, v, po = _head_slices(p, h)
        li_col = G[:, h:h + 1]
        b_col = Bc[:, ML_HEADS + h:ML_HEADS + h + 1]
        li_row = Gt[h:h + 1, :]
        b_row = Bt[ML_HEADS + h:ML_HEADS + h + 1, :]
        dmat = jnp.where(causal, b_col - (b_row - li_row), -jnp.inf)
        m_prev = per_seq(lambda a: jnp.broadcast_to(m0_ref[a, h:h + 1, 0:1], (T, 1)))
        n_rows = per_seq(lambda a: jnp.broadcast_to(n0_ref[a, h:h + 1, :], (T, ML_DK)))
        qc, kc, vc = q.astype(BF16), k.astype(BF16), v.astype(BF16)
        qk = lax.dot_general(qc, kc, (((1,), (1,)), ((), ())), preferred_element_type=F32)
        yield
        inter = b_col + m_prev
        m_t = jnp.maximum(inter, jnp.max(dmat, axis=1, keepdims=True))
        s = qk * jnp.exp(dmat - m_t)
        a_inter = jnp.exp(inter - m_t)
        yield
        qC = per_seq(lambda a: _dot(q[T * a:T * a + T, :], C0_ref[a, h]))
        yield
        num = _dot(s.astype(BF16), vc) + a_inter * qC
        den = jnp.sum(s, axis=1, keepdims=True) + a_inter * jnp.sum(q * n_rows, axis=1, keepdims=True)
        hh = num * (1.0 / jnp.maximum(jnp.abs(den), jnp.exp(-m_t)))
        hs_s[:, h * ML_DV:(h + 1) * ML_DV] = _head_out(
            hh, po, hg_ref[:, h * ML_DV:(h + 1) * ML_DV]).astype(BF16)
        yield
        m_new = seq_last(m_t)
        b_last = seq_last(b_col)
        decay = jnp.exp(b_last + m_prev - m_new)
        kw = k * jnp.exp(b_last - b_col + li_col - m_new)
        for a in range(nb):
            dec = decay[T * a:T * a + 1, :]
            kw_a = kw[T * a:T * a + T, :]
            upd = lax.dot_general(kw_a, v[T * a:T * a + T, :], (((0,), (0,)), ((), ())),
                                  preferred_element_type=F32)
            C_ref[a, h] = dec * C0_ref[a, h] + upd
            n_ref[a, h:h + 1, :] = dec * n0_ref[a, h:h + 1, :] + jnp.sum(kw_a, axis=0, keepdims=True)
            m_ref[a, h:h + 1, :] = jnp.broadcast_to(m_new[T * a:T * a + 1, :], (1, LANES))
            if a % 4 == 3:
                yield

    _run([(head(h), h) for h in range(ML_HEADS)])
    y_ref[...] = x + _dot(hs_s[...], wout_ref[...])


def _sample_mixer(x, g, wqkvo, wg, bg, hg, wout, C0, n0, m0, n_valid):
    NBT, D = x.shape
    nseq = C0.shape[0]
    nb = SAMPLE_NB
    R = nb * SAMPLE_PAD
    assert nseq % nb == 0 and NBT == nseq * SAMPLE_PAD
    const = lambda i: (0, 0)
    state_specs = [
        pl.BlockSpec((nb, ML_HEADS, ML_DK, ML_DV), lambda i: (i, 0, 0, 0)),
        pl.BlockSpec((nb, ML_HEADS, ML_DK), lambda i: (i, 0, 0)),
        pl.BlockSpec((nb, ML_HEADS, LANES), lambda i: (i, 0, 0)),
    ]
    vmem = _vmem_limit(2 * _nbytes((R, D), F32), _nbytes(wqkvo.shape, BF16), _nbytes(wout.shape, BF16),
                       2 * _nbytes((nb, ML_HEADS, ML_DK, ML_DV), F32), _nbytes((R, ML_QKVO), F32))
    return pl.pallas_call(
        functools.partial(_sample_mixer_kernel, n_valid=n_valid),
        grid=(nseq // nb,),
        in_specs=[
            pl.BlockSpec((R, D), lambda i: (i, 0)),
            pl.BlockSpec((1, D), const),
            pl.BlockSpec((D, ML_QKVO), const),
            pl.BlockSpec((D, LANES), const),
            pl.BlockSpec((1, LANES), const),
            pl.BlockSpec((1, ML_VO), const),
            pl.BlockSpec((ML_VO, D), const),
        ] + state_specs,
        out_specs=[pl.BlockSpec((R, D), lambda i: (i, 0))] + state_specs,
        out_shape=[
            jax.ShapeDtypeStruct((NBT, D), F32),
            jax.ShapeDtypeStruct(C0.shape, F32),
            jax.ShapeDtypeStruct(n0.shape, F32),
            jax.ShapeDtypeStruct(m0.shape, F32),
        ],
        scratch_shapes=[pltpu.VMEM((R, ML_VO), BF16)],
        compiler_params=pltpu.CompilerParams(
            dimension_semantics=("arbitrary",), vmem_limit_bytes=vmem),
        name="sample_mlstm_mixer",
    )(x, g, wqkvo, wg, bg, hg, wout, C0, n0, m0)


def _dot_exactish_right(x, m01):
    hi, mid, lo = _split3(x)
    return _dot(hi, m01) + _dot(mid, m01) + _dot(lo, m01)


def _mlp_kernel(x_ref, g_ref, w1_ref, w2_ref, *rest):
    x = x_ref[...]
    xn = _rms(x, g_ref[...]).astype(BF16)
    acc = x
    for c in range(D_FF // FF_TILE):
        hcol = _dot(xn, w1_ref[:, c * FF_TILE:(c + 1) * FF_TILE])
        hcol = jnp.square(jnp.maximum(hcol, 0.0)).astype(BF16)
        acc = acc + _dot(hcol, w2_ref[c * FF_TILE:(c + 1) * FF_TILE, :])
    if len(rest) == 1:
        (y_ref,) = rest
        y_ref[...] = acc
        return
    gkv_ref, wkv_ref, kg_ref, y_ref, k_ref, v_ref = rest
    y_ref[...] = acc
    kv = _dot(_rms(acc, gkv_ref[...]).astype(BF16), wkv_ref[...])
    kraw = kv[:, :ATT_KV]
    v_ref[...] = kv[:, ATT_KV:]
    r = lax.broadcasted_iota(jnp.int32, (ATT_KV, ATT_KV), 0) // ATT_HD
    c = lax.broadcasted_iota(jnp.int32, (ATT_KV, ATT_KV), 1) // ATT_HD
    seg = jnp.where(r == c, 1.0, 0.0).astype(BF16)
    ss = _dot_exactish_right(kraw * kraw, seg)
    k_ref[...] = kraw * lax.rsqrt(ss * (1.0 / ATT_HD) + EPS) * kg_ref[...]


def _mlp(x, g, w1, w2, layer, kv=None):
    N, D = x.shape
    tm = min(ROW_TILE, N)
    assert N % tm == 0
    const = lambda i: (0, 0)
    rows = lambda i: (i, 0)
    once = dict(pipeline_mode=pl.Buffered(1))
    in_specs = [
        pl.BlockSpec((tm, D), rows),
        pl.BlockSpec((1, D), const),
        pl.BlockSpec((None, D, D_FF), lambda i: (layer, 0, 0), **once),
        pl.BlockSpec((None, D_FF, D), lambda i: (layer, 0, 0), **once),
    ]
    out_specs = [pl.BlockSpec((tm, D), rows)]
    out_shape = [jax.ShapeDtypeStruct((N, D), F32)]
    args = [x, g, w1, w2]
    if kv is not None:
        in_specs += [pl.BlockSpec((1, D), const), pl.BlockSpec((D, 2 * ATT_KV), const, **once),
                     pl.BlockSpec((1, ATT_KV), const)]
        out_specs += [pl.BlockSpec((tm, ATT_KV), rows)] * 2
        out_shape += [jax.ShapeDtypeStruct((N, ATT_KV), F32)] * 2
        args += list(kv)
    vmem = _vmem_limit(2 * _nbytes((tm, D), F32), _nbytes(w1.shape[1:], BF16) // 2,
                       _nbytes(w2.shape[1:], BF16) // 2, 2 * _nbytes((tm, FF_TILE), F32))
    out = pl.pallas_call(
        _mlp_kernel,
        grid=(N // tm,),
        in_specs=in_specs,
        out_specs=out_specs,
        out_shape=out_shape,
        compiler_params=pltpu.CompilerParams(
            dimension_semantics=("arbitrary",), vmem_limit_bytes=vmem),
        name="sqrelu_mlp",
    )(*args)
    return out[0] if kv is None else out


def _pair_queries(qraw, pr, qscale, col0=0):
    TQ = qraw.shape[0]
    lo = lax.broadcasted_iota(jnp.int32, (TQ, LANES), 1) < ATT_HD
    out = []
    for e in range(2):
        kvh = 2 * pr + e
        for g in range(ATT_GROUP):
            cc, half = divmod(g, 2)
            c0 = (2 * kvh + cc) * LANES - col0
            q2 = qraw[:, c0:c0 + LANES]
            qm = jnp.where(lo, q2, 0.0) if half == 0 else jnp.where(lo, 0.0, q2)
            ss = jnp.sum(qm * qm, axis=1, keepdims=True)
            qn = qm * lax.rsqrt(ss * (1.0 / ATT_HD) + EPS)
            if qscale is not None:
                qn = qn * qscale
            out.append(qn if half == e else pltpu.roll(qn, ATT_HD, 1))
    return out


def _pair_outputs(o_heads, rden, pr, store):
    TQ = o_heads[0].shape[0]
    lo = lax.broadcasted_iota(jnp.int32, (TQ, LANES), 1) < ATT_HD
    for e in range(2):
        kvh = 2 * pr + e
        for cc in range(2):
            tiles = []
            for half in range(2):
                o = o_heads[e * ATT_GROUP + 2 * cc + half] * rden[e * ATT_GROUP + 2 * cc + half]
                tiles.append(o if half == e else pltpu.roll(o, ATT_HD, 1))
            store((2 * kvh + cc) * LANES, jnp.where(lo, tiles[0], tiles[1]))


def _softmax_with_sink(parts, sink):
    M = sink
    for s in parts:
        M = jnp.maximum(M, jnp.max(s, axis=1, keepdims=True))
    ps = [jnp.exp2(s - M) for s in parts]
    den = jnp.exp2(sink - M)
    for p in ps:
        den = den + jnp.sum(p, axis=1, keepdims=True)
    return ps, 1.0 / den


def _prompt_attn_kernel(sinks_ref, x_ref, g_ref, wq_ref, qg_ref, kp_ref, kc_ref, vp_ref, vc_ref, wo_ref,
                        y_ref):
    TQ = WINDOW
    TR = x_ref.shape[0]
    kj = lax.broadcasted_iota(jnp.int32, (WINDOW + TQ, TQ), 0)
    qi = lax.broadcasted_iota(jnp.int32, (WINDOW + TQ, TQ), 1)
    band = (qi + WINDOW - kj >= 0) & (qi - kj <= 0)
    band_first = band & ((kj >= WINDOW) | (pl.program_id(1) > 0))
    kscale = qg_ref[...] * (ATT_HD ** -0.5 * LOG2E)
    slabs = []
    for pr in range(ATT_KVH // 2):
        sl = slice(pr * LANES, (pr + 1) * LANES)
        slabs.append(((jnp.concatenate([kp_ref[:, sl], kc_ref[:, sl]], axis=0) * kscale).astype(BF16),
                      jnp.concatenate([vp_ref[:, sl], vc_ref[:, sl]], axis=0).T))

    PW = 2 * ATT_GROUP * ATT_HD
    zeros_hd = jnp.zeros((ATT_HD, TQ), F32)
    ones_rows = jnp.ones((2 * SUBLANES, WINDOW + TQ), F32)

    def chain(sb):
        rows = slice(sb * TQ, (sb + 1) * TQ)
        keys = slice(sb * TQ, sb * TQ + WINDOW + TQ)
        mask = band_first if sb == 0 else band
        x = x_ref[rows, :]
        xn = _rms(x, g_ref[...]).astype(BF16)
        acc = x
        for pr in range(ATT_KVH // 2):
            kslab, vslab_t = slabs[pr]
            qt = _dot(xn, wq_ref[:, pr * PW:(pr + 1) * PW]).T
            yield
            tiles, sink_rows = [], []
            for e in range(2):
                for g in range(ATT_GROUP):
                    blk = qt[(e * ATT_GROUP + g) * ATT_HD:(e * ATT_GROUP + g + 1) * ATT_HD, :]
                    qn = blk * lax.rsqrt(jnp.sum(blk * blk, axis=0, keepdims=True) * (1.0 / ATT_HD) + EPS)
                    tiles.append(jnp.concatenate([qn, zeros_hd] if e == 0 else [zeros_hd, qn], axis=0))
                    sink_rows.append(jnp.full((1, TQ), sinks_ref[(2 * pr + e) * ATT_GROUP + g] * LOG2E, F32))
            qmat = jnp.concatenate(tiles, axis=1).astype(BF16)
            st = _dot(kslab[keys, :], qmat)
            yield
            parts = []
            for e in range(2):
                ps, ms = [], []
                for g in range(ATT_GROUP):
                    c0 = (e * ATT_GROUP + g) * TQ
                    s_h = jnp.where(mask, st[:, c0:c0 + TQ], -jnp.inf)
                    m_h = jnp.maximum(jnp.max(s_h, axis=0, keepdims=True), sink_rows[e * ATT_GROUP + g])
                    ps.append(jnp.exp2(s_h - m_h).astype(BF16))
                    ms.append(m_h)
                p_e = jnp.concatenate(ps, axis=1)
                v_aug = jnp.concatenate([vslab_t[e * ATT_HD:(e + 1) * ATT_HD, keys], ones_rows],
                                        axis=0).astype(BF16)
                ot = _dot(v_aug, p_e)
                yield
                sink_term = jnp.exp2(jnp.concatenate(sink_rows[e * ATT_GROUP:(e + 1) * ATT_GROUP], axis=1)
                                     - jnp.concatenate(ms, axis=1))
                on = ot[0:ATT_HD, :] * (1.0 / (ot[ATT_HD:ATT_HD + 1, :] + sink_term))
                parts += [on[:, g * TQ:(g + 1) * TQ] for g in range(ATT_GROUP)]
            o_pair = jnp.concatenate(parts, axis=0).T.astype(BF16)
            acc = acc + _dot(o_pair, wo_ref[pr * PW:(pr + 1) * PW, :])
            yield
        y_ref[rows, :] = acc

    _run([(chain(sb), 0) for sb in range(TR // TQ)])


def _prompt_attn(x, k, v, sinks, g, wq, qg, wo):
    B, L, D = x.shape
    TR = ATTN_ROWS
    per = TR // WINDOW
    assert L % TR == 0 and TR % WINDOW == 0
    const = lambda b, i: (0, 0)
    cur = lambda b, i: (b, i, 0)
    prev = lambda b, i: (b, jnp.maximum(i * per - 1, 0), 0)
    vmem = _vmem_limit(2 * _nbytes((TR, D), F32), _nbytes(wq.shape, BF16), _nbytes(wo.shape, BF16),
                       4 * _nbytes((TR, ATT_KV), F32), 8 * _nbytes((8 * WINDOW, 2 * WINDOW), F32))
    return pl.pallas_call(
        _prompt_attn_kernel,
        grid=(B, L // TR),
        in_specs=[
            pl.BlockSpec(memory_space=pltpu.SMEM),
            pl.BlockSpec((None, TR, D), cur),
            pl.BlockSpec((1, D), const),
            pl.BlockSpec((D, D), const),
            pl.BlockSpec((1, LANES), const),
            pl.BlockSpec((None, WINDOW, ATT_KV), prev),
            pl.BlockSpec((None, TR, ATT_KV), cur),
            pl.BlockSpec((None, WINDOW, ATT_KV), prev),
            pl.BlockSpec((None, TR, ATT_KV), cur),
            pl.BlockSpec((D, D), const),
        ],
        out_specs=pl.BlockSpec((None, TR, D), cur),
        out_shape=jax.ShapeDtypeStruct((B, L, D), F32),
        compiler_params=pltpu.CompilerParams(
            dimension_semantics=("arbitrary", "arbitrary"), vmem_limit_bytes=vmem),
        name="prompt_window_attention",
    )(sinks, x, g, wq, qg, k, k, v, v, wo)


def _sample_attn_kernel(sinks_ref, x_ref, g_ref, wq_ref, qg_ref, kc_ref, vc_ref, wk_ref, wv_ref, wo_ref,
                        y_ref, nk_ref, nv_ref, o_s, *, n_valid):
    T = SAMPLE_PAD
    nb = wk_ref.shape[0]
    R = nb * T
    assert R == WINDOW
    x = x_ref[...]
    q = _dot(_rms(x, g_ref[...]).astype(BF16), wq_ref[...])
    qscale = qg_ref[...] * (ATT_HD ** -0.5 * LOG2E)
    knew = kc_ref[...]
    vnew = vc_ref[...]

    cmask = (lax.broadcasted_iota(jnp.int32, (R, WINDOW), 1)
             >= lax.broadcasted_iota(jnp.int32, (R, WINDOW), 0) % T)
    row = lax.broadcasted_iota(jnp.int32, (R, R), 0)
    col = lax.broadcasted_iota(jnp.int32, (R, R), 1)
    nmask = (row // T == col // T) & (col % T <= row % T) & (col % T < n_valid)

    def store(off, val):
        o_s[:, off:off + LANES] = val.astype(BF16)

    def regroup(per_seq, i):
        return jnp.concatenate([per_seq[a][i * T:(i + 1) * T, :] for a in range(nb)], axis=0)

    def pair(pr):
        sl = slice(pr * LANES, (pr + 1) * LANES)
        qs = _pair_queries(q, pr, qscale)
        S_new = lax.dot_general(jnp.concatenate(qs, axis=0).astype(BF16), knew[:, sl].astype(BF16),
                                (((1,), (1,)), ((), ())), preferred_element_type=F32)
        yield
        sc = []
        for a in range(nb):
            q_a = jnp.concatenate([qi[T * a:T * a + T, :] for qi in qs], axis=0).astype(BF16)
            sc.append(_dot(q_a, wk_ref[a, sl, :].astype(BF16)))
            if a % 4 == 3:
                yield
        pcs, pns, rden = [], [], []
        for i in range(2 * ATT_GROUP):
            s_c = jnp.where(cmask, regroup(sc, i), -jnp.inf)
            s_n = jnp.where(nmask, S_new[i * R:(i + 1) * R, :], -jnp.inf)
            (p_c, p_n), r = _softmax_with_sink([s_c, s_n], sinks_ref[pr * 2 * ATT_GROUP + i] * LOG2E)
            pcs.append(p_c)
            pns.append(p_n)
            rden.append(r)
            if i % 2 == 1:
                yield
        O_new = _dot(jnp.concatenate(pns, axis=0).astype(BF16), vnew[:, sl].astype(BF16))
        yield
        oc = []
        for a in range(nb):
            p_a = jnp.concatenate([pc[T * a:T * a + T, :] for pc in pcs], axis=0).astype(BF16)
            oc.append(lax.dot_general(p_a, wv_ref[a, sl, :].astype(BF16), (((1,), (1,)), ((), ())),
                                      preferred_element_type=F32))
            if a % 4 == 3:
                yield
        _pair_outputs([O_new[i * R:(i + 1) * R, :] + regroup(oc, i) for i in range(2 * ATT_GROUP)],
                      rden, pr, store)
        yield

    def roll_cache():
        keep = lax.broadcasted_iota(jnp.int32, (ATT_KV, WINDOW), 1) < WINDOW - n_valid
        knew_t = knew.T
        vnew_t = vnew.T
        yield
        for a in range(nb):
            shift = (WINDOW - n_valid - T * a) % WINDOW
            nk_ref[a] = jnp.where(keep, pltpu.roll(wk_ref[a], WINDOW - n_valid, 1),
                                  pltpu.roll(knew_t, shift, 1) if shift else knew_t)
            nv_ref[a] = jnp.where(keep, pltpu.roll(wv_ref[a], WINDOW - n_valid, 1),
                                  pltpu.roll(vnew_t, shift, 1) if shift else vnew_t)
            yield

    _run([(pair(0), 0), (pair(1), 2), (roll_cache(), 0)])
    y_ref[...] = x + _dot(o_s[...], wo_ref[...])


def _sample_attn(x, k, v, win_k, win_v, sinks, g, wq, qg, wo, n_valid):
    NBT, D = x.shape
    nseq = win_k.shape[0]
    nb = SAMPLE_NB
    R = nb * SAMPLE_PAD
    assert nseq % nb == 0 and NBT == nseq * SAMPLE_PAD
    const = lambda i: (0, 0)
    rows = lambda i: (i, 0)
    cache = pl.BlockSpec((nb, ATT_KV, WINDOW), lambda i: (i, 0, 0))
    vmem = _vmem_limit(2 * _nbytes((R, D), F32), _nbytes(wq.shape, BF16), _nbytes(wo.shape, BF16),
                       4 * _nbytes((nb, ATT_KV, WINDOW), F32), 2 * _nbytes((R, D), F32))
    return pl.pallas_call(
        functools.partial(_sample_attn_kernel, n_valid=n_valid),
        grid=(nseq // nb,),
        in_specs=[
            pl.BlockSpec(memory_space=pltpu.SMEM),
            pl.BlockSpec((R, D), rows),
            pl.BlockSpec((1, D), const),
            pl.BlockSpec((D, D), const),
            pl.BlockSpec((1, LANES), const),
            pl.BlockSpec((R, ATT_KV), rows),
            pl.BlockSpec((R, ATT_KV), rows),
            cache, cache,
            pl.BlockSpec((D, D), const),
        ],
        out_specs=[pl.BlockSpec((R, D), rows), cache, cache],
        out_shape=[jax.ShapeDtypeStruct((NBT, D), F32),
                   jax.ShapeDtypeStruct(win_k.shape, F32), jax.ShapeDtypeStruct(win_v.shape, F32)],
        scratch_shapes=[pltpu.VMEM((R, D), BF16)],
        compiler_params=pltpu.CompilerParams(
            dimension_semantics=("arbitrary",), vmem_limit_bytes=vmem),
        name="sample_window_attention",
    )(sinks, x, g, wq, qg, k, v, win_k, win_v, wo)


def kernel(x_prompt, x_sample, state_mlstm_C, state_mlstm_n, state_mlstm_m, cache_win_k, cache_win_v,
           ml_norm_g, ml_w_in, ml_b_i, ml_b_f, ml_head_g, ml_w_out, kv_norm_g, w_kv, k_norm_g,
           att_norm_g, att_w_q, q_norm_g, att_sinks, att_w_o, mlp_norm_g, mlp_w1, mlp_w2):
    B, L, D = x_prompt.shape
    NS, LS, _ = x_sample.shape
    assert ml_w_in.shape[0] == 1 and att_w_q.shape[0] == 1 and mlp_w1.shape[0] == 2

    w_in = ml_w_in[0]
    wqkvo = w_in.astype(BF16)
    wg = jnp.pad(w_in[:, ML_QKVO:], ((0, 0), (0, LANES - 2 * ML_HEADS))).astype(BF16)
    bg = jnp.pad(jnp.concatenate([ml_b_i[0], ml_b_f[0]]), (0, LANES - 2 * ML_HEADS)).reshape(1, LANES)
    ml_g = ml_norm_g[0].reshape(1, D)
    hg = ml_head_g[0].reshape(1, ML_VO)
    wout = ml_w_out[0].astype(BF16)
    mlp_g = mlp_norm_g.reshape(2, 1, D)
    kv_g = kv_norm_g.reshape(1, D)
    kg = jnp.tile(k_norm_g, ATT_KVH).reshape(1, ATT_KV)
    att_g = att_norm_g[0].reshape(1, D)
    qg = jnp.tile(q_norm_g[0], 2).reshape(1, LANES)
    sinks = att_sinks[0]

    xp, p_C, p_n, p_m, w1, w2, wkv, wq, wo = _prompt_mixer(
        x_prompt, ml_g, wqkvo, wg, bg, hg, wout,
        [mlp_w1.reshape(2 * D, D_FF), mlp_w2.reshape(2 * D_FF, D), w_kv, att_w_q[0], att_w_o[0]])
    w1 = w1.reshape(2, D, D_FF)
    w2 = w2.reshape(2, D_FF, D)
    xp, kp, vp = _mlp(xp.reshape(B * L, D), mlp_g[0], w1, w2, 0, kv=(kv_g, wkv, kg))
    xp = _prompt_attn(xp.reshape(B, L, D), kp.reshape(B, L, ATT_KV), vp.reshape(B, L, ATT_KV),
                      sinks, att_g, wq, qg, wo)
    y_prompt = _mlp(xp.reshape(B * L, D), mlp_g[1], w1, w2, 1).reshape(B, L, D)
    p_wk = kp.reshape(B, L, ATT_KV)[:, L - WINDOW:].reshape(B, WINDOW, ATT_KVH, ATT_HD)
    p_wv = vp.reshape(B, L, ATT_KV)[:, L - WINDOW:].reshape(B, WINDOW, ATT_KVH, ATT_HD)

    def pad8(a):
        return jnp.pad(a.reshape(NS, LS, -1), ((0, 0), (0, SAMPLE_PAD - LS), (0, 0))).reshape(NS * SAMPLE_PAD, -1)

    def unpad8(a):
        return a.reshape(NS, SAMPLE_PAD, -1)[:, :LS].reshape(NS * LS, -1)

    m0 = jnp.broadcast_to(state_mlstm_m[0][:, :, None], (NS, ML_HEADS, LANES))
    xs, s_C, s_n, s_m = _sample_mixer(pad8(x_sample), ml_g, wqkvo, wg, bg, hg, wout,
                                      state_mlstm_C[0], state_mlstm_n[0], m0, LS)
    xs, ks, vs = _mlp(unpad8(xs), mlp_g[0], w1, w2, 0, kv=(kv_g, wkv, kg))
    to_t = lambda c: c.transpose(0, 2, 3, 1).reshape(NS, ATT_KV, WINDOW)
    from_t = lambda c: c.reshape(NS, ATT_KVH, ATT_HD, WINDOW).transpose(0, 3, 1, 2)
    xs, s_wk, s_wv = _sample_attn(pad8(xs), pad8(ks), pad8(vs), to_t(cache_win_k), to_t(cache_win_v),
                                  sinks, att_g, wq, qg, wo, LS)
    y_sample = _mlp(unpad8(xs), mlp_g[1], w1, w2, 1).reshape(NS, LS, D)

    return (y_prompt, y_sample,
            p_C[None], p_n[None], p_m[None, :, :, 0], p_wk, p_wv,
            s_C[None], s_n[None], s_m[None, :, :, 0],
            from_t(s_wk), from_t(s_wv))
```

```python
import functools

import jax
import jax.numpy as jnp
from jax import lax
from jax.experimental import pallas as pl
from jax.experimental.pallas import tpu as pltpu

F32 = jnp.float32
BF16 = jnp.bfloat16

D_MODEL = 1024
ML_HEADS = 4
ML_DK = 128
ML_DV = 256
ML_QK = ML_HEADS * ML_DK
ML_VO = ML_HEADS * ML_DV
ML_QKVO = 2 * ML_QK + 2 * ML_VO
GATE_SOFTCAP = 15.0
ATT_HD = 64
ATT_QH = 16
ATT_KVH = 4
ATT_GROUP = 4
ATT_KV = ATT_KVH * ATT_HD
WINDOW = 128
D_FF = 4 * D_MODEL
EPS = 1e-6
LOG2E = 1.4426950408889634

LANES = 128
SUBLANES = 8
VMEM_LIMIT_CAP = 56 * 1024 * 1024

PROMPT_CHUNK = 256
PROMPT_ROWS = 2
PROJ_COLS = 512
SAMPLE_PAD = SUBLANES
SAMPLE_NB = 16
ATTN_ROWS = 1024
ROW_TILE = 1024
FF_TILE = 1024


def _vmem_limit(*block_bytes):
    need = 4 * sum(block_bytes) + (8 << 20)
    return int(min(max(need, 32 << 20), VMEM_LIMIT_CAP))


def _nbytes(shape, dtype):
    n = 1
    for s in shape:
        n *= s
    return n * jnp.dtype(dtype).itemsize


def _rms(x, g):
    return x * lax.rsqrt(jnp.mean(x * x, axis=-1, keepdims=True) + EPS) * g


def _dot(a, b):
    return jnp.dot(a, b, preferred_element_type=F32)


def _split3(x):
    hi = x.astype(BF16)
    r1 = x - hi.astype(F32)
    mid = r1.astype(BF16)
    lo = (r1 - mid.astype(F32)).astype(BF16)
    return hi, mid, lo


def _dot_exactish(m01, x):
    hi, mid, lo = _split3(x)
    return _dot(m01, hi) + _dot(m01, mid) + _dot(m01, lo)


def _gate_act(z):
    cap = GATE_SOFTCAP * jnp.tanh(z * (1.0 / GATE_SOFTCAP))
    lsig = jnp.minimum(cap, 0.0) - jnp.log1p(jnp.exp(-jnp.abs(cap)))
    lane = lax.broadcasted_iota(jnp.int32, z.shape, 1)
    return jnp.where(lane < ML_HEADS, cap, lsig)


_DONE = object()


def _rounds(chains):
    live = [(c[0], c[1], c[2] if len(c) > 2 else 1) for c in chains]
    rnd = 0
    while live:
        for item in list(live):
            gen, start, stride = item
            if rnd >= start and (rnd - start) % stride == 0 and next(gen, _DONE) is _DONE:
                live.remove(item)
        rnd += 1
        yield


def _run(chains):
    for _ in _rounds(chains):
        pass


def _head_out(hh, po, hg):
    hn = hh * lax.rsqrt(jnp.mean(hh * hh, axis=-1, keepdims=True) + EPS) * hg
    return jax.nn.sigmoid(po) * hn


def _head_slices(p, h):
    q = p[:, h * ML_DK:(h + 1) * ML_DK]
    k = p[:, ML_QK + h * ML_DK:ML_QK + (h + 1) * ML_DK] * (ML_DK ** -0.5)
    v = p[:, 2 * ML_QK + h * ML_DV:2 * ML_QK + (h + 1) * ML_DV]
    po = p[:, 2 * ML_QK + ML_VO + h * ML_DV:2 * ML_QK + ML_VO + (h + 1) * ML_DV]
    return q, k, v, po


def _prompt_mixer_kernel(xc_ref, xnext_ref, g_ref, wqkvo_ref, wg_ref, bg_ref, hg_ref, wout_ref, *rest,
                         chunks_per_seq, n_cast):
    cast_in = rest[:n_cast]
    y_ref, C_ref, n_ref, m_ref = rest[n_cast:n_cast + 4]
    cast_out = rest[n_cast + 4:2 * n_cast + 4]
    p_s, gz_s = rest[2 * n_cast + 4:]
    R, T = xc_ref.shape[0], xc_ref.shape[1]
    f = pl.program_id(0)
    slot = f % 2

    def project(x_ref, s, r):
        xn = _rms(x_ref[r], g_ref[...]).astype(BF16)
        yield
        for j in range(ML_QKVO // PROJ_COLS):
            cols = slice(j * PROJ_COLS, (j + 1) * PROJ_COLS)
            p_s[s, r, :, cols] = _dot(xn, wqkvo_ref[:, cols])
            yield
        gz_s[s, r] = _dot(xn, wg_ref[...])
        yield

    @pl.when(f == 0)
    def _():
        _run([(project(xc_ref, 0, r), 0) for r in range(R)])

    @pl.when(f % chunks_per_seq == 0)
    def _():
        C_ref[...] = jnp.zeros_like(C_ref)
        n_ref[...] = jnp.zeros_like(n_ref)
        m_ref[...] = jnp.zeros_like(m_ref)

    def recurrence(r):
        G = _gate_act(gz_s[slot, r] + bg_ref[...])
        row = lax.broadcasted_iota(jnp.int32, (T, T), 0)
        col = lax.broadcasted_iota(jnp.int32, (T, T), 1)
        causal = col <= row
        Bc = _dot_exactish(jnp.where(causal, 1.0, 0.0).astype(BF16), G)
        yield
        Gt = G.T
        Bt = Bc.T
        yield
        parts = [None] * ML_HEADS

        def head(h):
            def cols(base, width):
                return p_s[slot, r, :, base + h * width:base + (h + 1) * width]

            q = cols(0, ML_DK)
            k = cols(ML_QK, ML_DK) * (ML_DK ** -0.5)
            v = cols(2 * ML_QK, ML_DV)
            qc, kc, vc = q.astype(BF16), k.astype(BF16), v.astype(BF16)
            qk = lax.dot_general(qc, kc, (((1,), (1,)), ((), ())), preferred_element_type=F32)
            yield
            li_col = G[:, h:h + 1]
            b_col = Bc[:, ML_HEADS + h:ML_HEADS + h + 1]
            dmat = jnp.where(causal, b_col - Bt[ML_HEADS + h:ML_HEADS + h + 1, :] + Gt[h:h + 1, :], -jnp.inf)
            m_prev = m_ref[r, h:h + 1, 0:1]
            inter = b_col + m_prev
            m_t = jnp.maximum(inter, jnp.max(dmat, axis=1, keepdims=True))
            s = qk * jnp.exp(dmat - m_t)
            a_inter = jnp.exp(inter - m_t)
            den = (jnp.sum(s, axis=1, keepdims=True)
                   + a_inter * jnp.sum(q * n_ref[r, h:h + 1, :], axis=1, keepdims=True))
            yield
            C = C_ref[r, h]
            num = _dot(s.astype(BF16), vc) + a_inter * _dot(qc, C.astype(BF16))
            yield
            hh = num * (1.0 / jnp.maximum(jnp.abs(den), jnp.exp(-m_t)))
            hs = _head_out(hh, cols(2 * ML_QK + ML_VO, ML_DV), hg_ref[:, h * ML_DV:(h + 1) * ML_DV])
            yield
            parts[h] = _dot(hs.astype(BF16), wout_ref[h * ML_DV:(h + 1) * ML_DV, :])
            yield
            m_new = m_t[T - 1:T, :]
            b_last = b_col[T - 1:T, :]
            decay = jnp.exp(b_last + m_prev - m_new)
            kw = k * jnp.exp(b_last - b_col + li_col - m_new)
            C_ref[r, h] = decay * C + lax.dot_general(kw.astype(BF16), vc, (((0,), (0,)), ((), ())),
                                                      preferred_element_type=F32)
            n_ref[r, h:h + 1, :] = decay * n_ref[r, h:h + 1, :] + jnp.sum(kw, axis=0, keepdims=True)
            m_ref[r, h:h + 1, :] = jnp.broadcast_to(m_new, (1, LANES))
            yield

        yield from _rounds([(head(h), 2 * h) for h in range(ML_HEADS)])
        y_ref[r] = xc_ref[r] + ((parts[0] + parts[1]) + (parts[2] + parts[3]))

    def casts():
        for src, dst in zip(cast_in, cast_out):
            dst[...] = src[...].astype(BF16)
            yield

    _run([(project(xnext_ref, 1 - slot, r), 0) for r in range(R)] + [(recurrence(r), r) for r in range(R)]
         + [(casts(), 0)])


def _prompt_mixer(x, g, wqkvo, wg, bg, hg, wout, to_cast):
    B, L, D = x.shape
    T = PROMPT_CHUNK
    assert L % T == 0
    R = PROMPT_ROWS
    assert B % R == 0
    nc = L // T
    steps = (B // R) * nc
    const = lambda f: (0, 0)
    cur = lambda f: (f // nc, f % nc, 0)
    nxt = lambda f: (jnp.minimum(f + 1, steps - 1) // nc, jnp.minimum(f + 1, steps - 1) % nc, 0)
    once = dict(pipeline_mode=pl.Buffered(1))
    vmem = _vmem_limit(3 * _nbytes((R, T, D), F32), _nbytes(wqkvo.shape, BF16) // 2,
                       _nbytes(wout.shape, BF16) // 2, _nbytes((R, ML_HEADS, ML_DK, ML_DV), F32),
                       2 * _nbytes((R, T, ML_QKVO), F32))
    for w in to_cast:
        assert w.shape[0] % (steps * 2 * SUBLANES) == 0
    cast_specs = [pl.BlockSpec((w.shape[0] // steps, w.shape[1]), lambda f: (f, 0)) for w in to_cast]
    outs = pl.pallas_call(
        functools.partial(_prompt_mixer_kernel, chunks_per_seq=nc, n_cast=len(to_cast)),
        grid=(steps,),
        in_specs=[
            pl.BlockSpec((R, T, D), cur),
            pl.BlockSpec((R, T, D), nxt),
            pl.BlockSpec((1, D), const),
            pl.BlockSpec((D, ML_QKVO), const, **once),
            pl.BlockSpec((D, LANES), const, **once),
            pl.BlockSpec((1, LANES), const),
            pl.BlockSpec((1, ML_VO), const),
            pl.BlockSpec((ML_VO, D), const, **once),
        ] + cast_specs,
        out_specs=[
            pl.BlockSpec((R, T, D), cur),
            pl.BlockSpec((R, ML_HEADS, ML_DK, ML_DV), lambda f: (f // nc, 0, 0, 0)),
            pl.BlockSpec((R, ML_HEADS, ML_DK), lambda f: (f // nc, 0, 0)),
            pl.BlockSpec((R, ML_HEADS, LANES), lambda f: (f // nc, 0, 0)),
        ] + cast_specs,
        out_shape=[
            jax.ShapeDtypeStruct((B, L, D), F32),
            jax.ShapeDtypeStruct((B, ML_HEADS, ML_DK, ML_DV), F32),
            jax.ShapeDtypeStruct((B, ML_HEADS, ML_DK), F32),
            jax.ShapeDtypeStruct((B, ML_HEADS, LANES), F32),
        ] + [jax.ShapeDtypeStruct(w.shape, BF16) for w in to_cast],
        scratch_shapes=[pltpu.VMEM((2, R, T, ML_QKVO), F32), pltpu.VMEM((2, R, T, LANES), F32)],
        compiler_params=pltpu.CompilerParams(
            dimension_semantics=("arbitrary",), vmem_limit_bytes=vmem),
        name="prompt_mlstm_mixer",
    )(x, x, g, wqkvo, wg, bg, hg, wout, *to_cast)
    return outs


def _sample_mixer_kernel(x_ref, g_ref, wqkvo_ref, wg_ref, bg_ref, hg_ref, wout_ref,
                         C0_ref, n0_ref, m0_ref,
                         y_ref, C_ref, n_ref, m_ref, hs_s, *, n_valid):
    T = SAMPLE_PAD
    nb = C0_ref.shape[0]
    R = nb * T
    x = x_ref[...]
    xn = _rms(x, g_ref[...]).astype(BF16)
    p = _dot(xn, wqkvo_ref[...])

    rowt = lax.broadcasted_iota(jnp.int32, (R, LANES), 0) % T
    lane = lax.broadcasted_iota(jnp.int32, (R, LANES), 1)
    G = jnp.where(rowt < n_valid, _gate_act(_dot(xn, wg_ref[...]) + bg_ref[...]),
                  jnp.where(lane < ML_HEADS, -jnp.inf, 0.0))
    row = lax.broadcasted_iota(jnp.int32, (R, R), 0)
    col = lax.broadcasted_iota(jnp.int32, (R, R), 1)
    causal = (row // T == col // T) & (col <= row)
    Bc = _dot_exactish(jnp.where(causal, 1.0, 0.0).astype(BF16), jnp.where(lane < ML_HEADS, 0.0, G))
    Gt = G.T
    Bt = Bc.T

    def per_seq(fn):
        return jnp.concatenate([fn(a) for a in range(nb)], axis=0)

    def seq_last(colvec):
        return per_seq(lambda a: jnp.broadcast_to(colvec[T * a + T - 1:T * a + T, :], (T, 1)))

    def head(h):
        q, k, v, po = _head_slices(p, h)
        li_col = G[:, h:h + 1]
        b_col = Bc[:, ML_HEADS + h:ML_HEADS + h + 1]
        li_row = Gt[h:h + 1, :]
        b_row = Bt[ML_HEADS + h:ML_HEADS + h + 1, :]
        dmat = jnp.where(causal, b_col - (b_row - li_row), -jnp.inf)
        m_prev = per_seq(lambda a: jnp.broadcast_to(m0_ref[a, h:h + 1, 0:1], (T, 1)))
        n_rows = per_seq(lambda a: jnp.broadcast_to(n0_ref[a, h:h + 1, :], (T, ML_DK)))
        qc, kc, vc = q.astype(BF16), k.astype(BF16), v.astype(BF16)
        qk = lax.dot_general(qc, kc, (((1,), (1,)), ((), ())), preferred_element_type=F32)
        yield
        inter = b_col + m_prev
        m_t = jnp.maximum(inter, jnp.max(dmat, axis=1, keepdims=True))
        s = qk * jnp.exp(dmat - m_t)
        a_inter = jnp.exp(inter - m_t)
        yield
        qC = per_seq(lambda a: _dot(q[T * a:T * a + T, :], C0_ref[a, h]))
        yield
        num = _dot(s.astype(BF16), vc) + a_inter * qC
        den = jnp.sum(s, axis=1, keepdims=True) + a_inter * jnp.sum(q * n_rows, axis=1, keepdims=True)
        hh = num * (1.0 / jnp.maximum(jnp.abs(den), jnp.exp(-m_t)))
        hs_s[:, h * ML_DV:(h + 1) * ML_DV] = _head_out(
            hh, po, hg_ref[:, h * ML_DV:(h + 1) * ML_DV]).astype(BF16)
        yield
        m_new = seq_last(m_t)
        b_last = seq_last(b_col)
        decay = jnp.exp(b_last + m_prev - m_new)
        kw = k * jnp.exp(b_last - b_col + li_col - m_new)
        for a in range(nb):
            dec = decay[T * a:T * a + 1, :]
            kw_a = kw[T * a:T * a + T, :]
            upd = lax.dot_general(kw_a, v[T * a:T * a + T, :], (((0,), (0,)), ((), ())),
                                  preferred_element_type=F32)
            C_ref[a, h] = dec * C0_ref[a, h] + upd
            n_ref[a, h:h + 1, :] = dec * n0_ref[a, h:h + 1, :] + jnp.sum(kw_a, axis=0, keepdims=True)
            m_ref[a, h:h + 1, :] = jnp.broadcast_to(m_new[T * a:T * a + 1, :], (1, LANES))
            if a % 4 == 3:
                yield

    _run([(head(h), h) for h in range(ML_HEADS)])
    y_ref[...] = x + _dot(hs_s[...], wout_ref[...])


def _sample_mixer(x, g, wqkvo, wg, bg, hg, wout, C0, n0, m0, n_valid):
    NBT, D = x.shape
    nseq = C0.shape[0]
    nb = SAMPLE_NB
    R = nb * SAMPLE_PAD
    assert nseq % nb == 0 and NBT == nseq * SAMPLE_PAD
    const = lambda i: (0, 0)
    state_specs = [
        pl.BlockSpec((nb, ML_HEADS, ML_DK, ML_DV), lambda i: (i, 0, 0, 0)),
        pl.BlockSpec((nb, ML_HEADS, ML_DK), lambda i: (i, 0, 0)),
        pl.BlockSpec((nb, ML_HEADS, LANES), lambda i: (i, 0, 0)),
    ]
    vmem = _vmem_limit(2 * _nbytes((R, D), F32), _nbytes(wqkvo.shape, BF16), _nbytes(wout.shape, BF16),
                       2 * _nbytes((nb, ML_HEADS, ML_DK, ML_DV), F32), _nbytes((R, ML_QKVO), F32))
    return pl.pallas_call(
        functools.partial(_sample_mixer_kernel, n_valid=n_valid),
        grid=(nseq // nb,),
        in_specs=[
            pl.BlockSpec((R, D), lambda i: (i, 0)),
            pl.BlockSpec((1, D), const),
            pl.BlockSpec((D, ML_QKVO), const),
            pl.BlockSpec((D, LANES), const),
            pl.BlockSpec((1, LANES), const),
            pl.BlockSpec((1, ML_VO), const),
            pl.BlockSpec((ML_VO, D), const),
        ] + state_specs,
        out_specs=[pl.BlockSpec((R, D), lambda i: (i, 0))] + state_specs,
        out_shape=[
            jax.ShapeDtypeStruct((NBT, D), F32),
            jax.ShapeDtypeStruct(C0.shape, F32),
            jax.ShapeDtypeStruct(n0.shape, F32),
            jax.ShapeDtypeStruct(m0.shape, F32),
        ],
        scratch_shapes=[pltpu.VMEM((R, ML_VO), BF16)],
        compiler_params=pltpu.CompilerParams(
            dimension_semantics=("arbitrary",), vmem_limit_bytes=vmem),
        name="sample_mlstm_mixer",
    )(x, g, wqkvo, wg, bg, hg, wout, C0, n0, m0)


def _dot_exactish_right(x, m01):
    hi, mid, lo = _split3(x)
    return _dot(hi, m01) + _dot(mid, m01) + _dot(lo, m01)


def _mlp_kernel(x_ref, g_ref, w1_ref, w2_ref, *rest):
    x = x_ref[...]
    xn = _rms(x, g_ref[...]).astype(BF16)
    acc = x
    for c in range(D_FF // FF_TILE):
        hcol = _dot(xn, w1_ref[:, c * FF_TILE:(c + 1) * FF_TILE])
        hcol = jnp.square(jnp.maximum(hcol, 0.0)).astype(BF16)
        acc = acc + _dot(hcol, w2_ref[c * FF_TILE:(c + 1) * FF_TILE, :])
    if len(rest) == 1:
        (y_ref,) = rest
        y_ref[...] = acc
        return
    gkv_ref, wkv_ref, kg_ref, y_ref, k_ref, v_ref = rest
    y_ref[...] = acc
    kv = _dot(_rms(acc, gkv_ref[...]).astype(BF16), wkv_ref[...])
    kraw = kv[:, :ATT_KV]
    v_ref[...] = kv[:, ATT_KV:]
    r = lax.broadcasted_iota(jnp.int32, (ATT_KV, ATT_KV), 0) // ATT_HD
    c = lax.broadcasted_iota(jnp.int32, (ATT_KV, ATT_KV), 1) // ATT_HD
    seg = jnp.where(r == c, 1.0, 0.0).astype(BF16)
    ss = _dot_exactish_right(kraw * kraw, seg)
    k_ref[...] = kraw * lax.rsqrt(ss * (1.0 / ATT_HD) + EPS) * kg_ref[...]


def _mlp(x, g, w1, w2, layer, kv=None):
    N, D = x.shape
    tm = min(ROW_TILE, N)
    assert N % tm == 0
    const = lambda i: (0, 0)
    rows = lambda i: (i, 0)
    once = dict(pipeline_mode=pl.Buffered(1))
    in_specs = [
        pl.BlockSpec((tm, D), rows),
        pl.BlockSpec((1, D), const),
        pl.BlockSpec((None, D, D_FF), lambda i: (layer, 0, 0), **once),
        pl.BlockSpec((None, D_FF, D), lambda i: (layer, 0, 0), **once),
    ]
    out_specs = [pl.BlockSpec((tm, D), rows)]
    out_shape = [jax.ShapeDtypeStruct((N, D), F32)]
    args = [x, g, w1, w2]
    if kv is not None:
        in_specs += [pl.BlockSpec((1, D), const), pl.BlockSpec((D, 2 * ATT_KV), const, **once),
                     pl.BlockSpec((1, ATT_KV), const)]
        out_specs += [pl.BlockSpec((tm, ATT_KV), rows)] * 2
        out_shape += [jax.ShapeDtypeStruct((N, ATT_KV), F32)] * 2
        args += list(kv)
    vmem = _vmem_limit(2 * _nbytes((tm, D), F32), _nbytes(w1.shape[1:], BF16) // 2,
                       _nbytes(w2.shape[1:], BF16) // 2, 2 * _nbytes((tm, FF_TILE), F32))
    out = pl.pallas_call(
        _mlp_kernel,
        grid=(N // tm,),
        in_specs=in_specs,
        out_specs=out_specs,
        out_shape=out_shape,
        compiler_params=pltpu.CompilerParams(
            dimension_semantics=("arbitrary",), vmem_limit_bytes=vmem),
        name="sqrelu_mlp",
    )(*args)
    return out[0] if kv is None else out


def _pair_queries(qraw, pr, qscale, col0=0):
    TQ = qraw.shape[0]
    lo = lax.broadcasted_iota(jnp.int32, (TQ, LANES), 1) < ATT_HD
    out = []
    for e in range(2):
        kvh = 2 * pr + e
        for g in range(ATT_GROUP):
            cc, half = divmod(g, 2)
            c0 = (2 * kvh + cc) * LANES - col0
            q2 = qraw[:, c0:c0 + LANES]
            qm = jnp.where(lo, q2, 0.0) if half == 0 else jnp.where(lo, 0.0, q2)
            ss = jnp.sum(qm * qm, axis=1, keepdims=True)
            qn = qm * lax.rsqrt(ss * (1.0 / ATT_HD) + EPS)
            if qscale is not None:
                qn = qn * qscale
            out.append(qn if half == e else pltpu.roll(qn, ATT_HD, 1))
    return out


def _pair_outputs(o_heads, rden, pr, store):
    TQ = o_heads[0].shape[0]
    lo = lax.broadcasted_iota(jnp.int32, (TQ, LANES), 1) < ATT_HD
    for e in range(2):
        kvh = 2 * pr + e
        for cc in range(2):
            tiles = []
            for half in range(2):
                o = o_heads[e * ATT_GROUP + 2 * cc + half] * rden[e * ATT_GROUP + 2 * cc + half]
                tiles.append(o if half == e else pltpu.roll(o, ATT_HD, 1))
            store((2 * kvh + cc) * LANES, jnp.where(lo, tiles[0], tiles[1]))


def _softmax_with_sink(parts, sink):
    M = sink
    for s in parts:
        M = jnp.maximum(M, jnp.max(s, axis=1, keepdims=True))
    ps = [jnp.exp2(s - M) for s in parts]
    den = jnp.exp2(sink - M)
    for p in ps:
        den = den + jnp.sum(p, axis=1, keepdims=True)
    return ps, 1.0 / den


def _prompt_attn_kernel(sinks_ref, x_ref, g_ref, wq_ref, qg_ref, kp_ref, kc_ref, vp_ref, vc_ref, wo_ref,
                        y_ref):
    TQ = WINDOW
    TR = x_ref.shape[0]
    kj = lax.broadcasted_iota(jnp.int32, (WINDOW + TQ, TQ), 0)
    qi = lax.broadcasted_iota(jnp.int32, (WINDOW + TQ, TQ), 1)
    band = (qi + WINDOW - kj >= 0) & (qi - kj <= 0)
    band_first = band & ((kj >= WINDOW) | (pl.program_id(1) > 0))
    kscale = qg_ref[...] * (ATT_HD ** -0.5 * LOG2E)
    slabs = []
    for pr in range(ATT_KVH // 2):
        sl = slice(pr * LANES, (pr + 1) * LANES)
        slabs.append(((jnp.concatenate([kp_ref[:, sl], kc_ref[:, sl]], axis=0) * kscale).astype(BF16),
                      jnp.concatenate([vp_ref[:, sl], vc_ref[:, sl]], axis=0).T))

    PW = 2 * ATT_GROUP * ATT_HD
    zeros_hd = jnp.zeros((ATT_HD, TQ), F32)
    ones_rows = jnp.ones((2 * SUBLANES, WINDOW + TQ), F32)

    def chain(sb):
        rows = slice(sb * TQ, (sb + 1) * TQ)
        keys = slice(sb * TQ, sb * TQ + WINDOW + TQ)
        mask = band_first if sb == 0 else band
        x = x_ref[rows, :]
        xn = _rms(x, g_ref[...]).astype(BF16)
        acc = x
        for pr in range(ATT_KVH // 2):
            kslab, vslab_t = slabs[pr]
            qt = _dot(xn, wq_ref[:, pr * PW:(pr + 1) * PW]).T
            yield
            tiles, sink_rows = [], []
            for e in range(2):
                for g in range(ATT_GROUP):
                    blk = qt[(e * ATT_GROUP + g) * ATT_HD:(e * ATT_GROUP + g + 1) * ATT_HD, :]
                    qn = blk * lax.rsqrt(jnp.sum(blk * blk, axis=0, keepdims=True) * (1.0 / ATT_HD) + EPS)
                    tiles.append(jnp.concatenate([qn, zeros_hd] if e == 0 else [zeros_hd, qn], axis=0))
                    sink_rows.append(jnp.full((1, TQ), sinks_ref[(2 * pr + e) * ATT_GROUP + g] * LOG2E, F32))
            qmat = jnp.concatenate(tiles, axis=1).astype(BF16)
            st = _dot(kslab[keys, :], qmat)
            yield
            parts = []
            for e in range(2):
                ps, ms = [], []
                for g in range(ATT_GROUP):
                    c0 = (e * ATT_GROUP + g) * TQ
                    s_h = jnp.where(mask, st[:, c0:c0 + TQ], -jnp.inf)
                    m_h = jnp.maximum(jnp.max(s_h, axis=0, keepdims=True), sink_rows[e * ATT_GROUP + g])
                    ps.append(jnp.exp2(s_h - m_h).astype(BF16))
                    ms.append(m_h)
                p_e = jnp.concatenate(ps, axis=1)
                v_aug = jnp.concatenate([vslab_t[e * ATT_HD:(e + 1) * ATT_HD, keys], ones_rows],
                                        axis=0).astype(BF16)
                ot = _dot(v_aug, p_e)
                yield
                sink_term = jnp.exp2(jnp.concatenate(sink_rows[e * ATT_GROUP:(e + 1) * ATT_GROUP], axis=1)
                                     - jnp.concatenate(ms, axis=1))
                on = ot[0:ATT_HD, :] * (1.0 / (ot[ATT_HD:ATT_HD + 1, :] + sink_term))
                parts += [on[:, g * TQ:(g + 1) * TQ] for g in range(ATT_GROUP)]
            o_pair = jnp.concatenate(parts, axis=0).T.astype(BF16)
            acc = acc + _dot(o_pair, wo_ref[pr * PW:(pr + 1) * PW, :])
            yield
        y_ref[rows, :] = acc

    _run([(chain(sb), 0) for sb in range(TR // TQ)])


def _prompt_attn(x, k, v, sinks, g, wq, qg, wo):
    B, L, D = x.shape
    TR = ATTN_ROWS
    per = TR // WINDOW
    assert L % TR == 0 and TR % WINDOW == 0
    const = lambda b, i: (0, 0)
    cur = lambda b, i: (b, i, 0)
    prev = lambda b, i: (b, jnp.maximum(i * per - 1, 0), 0)
    vmem = _vmem_limit(2 * _nbytes((TR, D), F32), _nbytes(wq.shape, BF16), _nbytes(wo.shape, BF16),
                       4 * _nbytes((TR, ATT_KV), F32), 8 * _nbytes((8 * WINDOW, 2 * WINDOW), F32))
    return pl.pallas_call(
        _prompt_attn_kernel,
        grid=(B, L // TR),
        in_specs=[
            pl.BlockSpec(memory_space=pltpu.SMEM),
            pl.BlockSpec((None, TR, D), cur),
            pl.BlockSpec((1, D), const),
            pl.BlockSpec((D, D), const),
            pl.BlockSpec((1, LANES), const),
            pl.BlockSpec((None, WINDOW, ATT_KV), prev),
            pl.BlockSpec((None, TR, ATT_KV), cur),
            pl.BlockSpec((None, WINDOW, ATT_KV), prev),
            pl.BlockSpec((None, TR, ATT_KV), cur),
            pl.BlockSpec((D, D), const),
        ],
        out_specs=pl.BlockSpec((None, TR, D), cur),
        out_shape=jax.ShapeDtypeStruct((B, L, D), F32),
        compiler_params=pltpu.CompilerParams(
            dimension_semantics=("arbitrary", "arbitrary"), vmem_limit_bytes=vmem),
        name="prompt_window_attention",
    )(sinks, x, g, wq, qg, k, k, v, v, wo)


def _sample_attn_kernel(sinks_ref, x_ref, g_ref, wq_ref, qg_ref, kc_ref, vc_ref, wk_ref, wv_ref, wo_ref,
                        y_ref, nk_ref, nv_ref, o_s, *, n_valid):
    T = SAMPLE_PAD
    nb = wk_ref.shape[0]
    R = nb * T
    assert R == WINDOW
    x = x_ref[...]
    q = _dot(_rms(x, g_ref[...]).astype(BF16), wq_ref[...])
    qscale = qg_ref[...] * (ATT_HD ** -0.5 * LOG2E)
    knew = kc_ref[...]
    vnew = vc_ref[...]

    cmask = (lax.broadcasted_iota(jnp.int32, (R, WINDOW), 1)
             >= lax.broadcasted_iota(jnp.int32, (R, WINDOW), 0) % T)
    row = lax.broadcasted_iota(jnp.int32, (R, R), 0)
    col = lax.broadcasted_iota(jnp.int32, (R, R), 1)
    nmask = (row // T == col // T) & (col % T <= row % T) & (col % T < n_valid)

    def store(off, val):
        o_s[:, off:off + LANES] = val.astype(BF16)

    def regroup(per_seq, i):
        return jnp.concatenate([per_seq[a][i * T:(i + 1) * T, :] for a in range(nb)], axis=0)

    def pair(pr):
        sl = slice(pr * LANES, (pr + 1) * LANES)
        qs = _pair_queries(q, pr, qscale)
        S_new = lax.dot_general(jnp.concatenate(qs, axis=0).astype(BF16), knew[:, sl].astype(BF16),
                                (((1,), (1,)), ((), ())), preferred_element_type=F32)
        yield
        sc = []
        for a in range(nb):
            q_a = jnp.concatenate([qi[T * a:T * a + T, :] for qi in qs], axis=0).astype(BF16)
            sc.append(_dot(q_a, wk_ref[a, sl, :].astype(BF16)))
            if a % 4 == 3:
                yield
        pcs, pns, rden = [], [], []
        for i in range(2 * ATT_GROUP):
            s_c = jnp.where(cmask, regroup(sc, i), -jnp.inf)
            s_n = jnp.where(nmask, S_new[i * R:(i + 1) * R, :], -jnp.inf)
            (p_c, p_n), r = _softmax_with_sink([s_c, s_n], sinks_ref[pr * 2 * ATT_GROUP + i] * LOG2E)
            pcs.append(p_c)
            pns.append(p_n)
            rden.append(r)
            if i % 2 == 1:
                yield
        O_new = _dot(jnp.concatenate(pns, axis=0).astype(BF16), vnew[:, sl].astype(BF16))
        yield
        oc = []
        for a in range(nb):
            p_a = jnp.concatenate([pc[T * a:T * a + T, :] for pc in pcs], axis=0).astype(BF16)
            oc.append(lax.dot_general(p_a, wv_ref[a, sl, :].astype(BF16), (((1,), (1,)), ((), ())),
                                      preferred_element_type=F32))
            if a % 4 == 3:
                yield
        _pair_outputs([O_new[i * R:(i + 1) * R, :] + regroup(oc, i) for i in range(2 * ATT_GROUP)],
                      rden, pr, store)
        yield

    def roll_cache():
        keep = lax.broadcasted_iota(jnp.int32, (ATT_KV, WINDOW), 1) < WINDOW - n_valid
        knew_t = knew.T
        vnew_t = vnew.T
        yield
        for a in range(nb):
            shift = (WINDOW - n_valid - T * a) % WINDOW
            nk_ref[a] = jnp.where(keep, pltpu.roll(wk_ref[a], WINDOW - n_valid, 1),
                                  pltpu.roll(knew_t, shift, 1) if shift else knew_t)
            nv_ref[a] = jnp.where(keep, pltpu.roll(wv_ref[a], WINDOW - n_valid, 1),
                                  pltpu.roll(vnew_t, shift, 1) if shift else vnew_t)
            yield

    _run([(pair(0), 0), (pair(1), 2), (roll_cache(), 0)])
    y_ref[...] = x + _dot(o_s[...], wo_ref[...])


def _sample_attn(x, k, v, win_k, win_v, sinks, g, wq, qg, wo, n_valid):
    NBT, D = x.shape
    nseq = win_k.shape[0]
    nb = SAMPLE_NB
    R = nb * SAMPLE_PAD
    assert nseq % nb == 0 and NBT == nseq * SAMPLE_PAD
    const = lambda i: (0, 0)
    rows = lambda i: (i, 0)
    cache = pl.BlockSpec((nb, ATT_KV, WINDOW), lambda i: (i, 0, 0))
    vmem = _vmem_limit(2 * _nbytes((R, D), F32), _nbytes(wq.shape, BF16), _nbytes(wo.shape, BF16),
                       4 * _nbytes((nb, ATT_KV, WINDOW), F32), 2 * _nbytes((R, D), F32))
    return pl.pallas_call(
        functools.partial(_sample_attn_kernel, n_valid=n_valid),
        grid=(nseq // nb,),
        in_specs=[
            pl.BlockSpec(memory_space=pltpu.SMEM),
            pl.BlockSpec((R, D), rows),
            pl.BlockSpec((1, D), const),
            pl.BlockSpec((D, D), const),
            pl.BlockSpec((1, LANES), const),
            pl.BlockSpec((R, ATT_KV), rows),
            pl.BlockSpec((R, ATT_KV), rows),
            cache, cache,
            pl.BlockSpec((D, D), const),
        ],
        out_specs=[pl.BlockSpec((R, D), rows), cache, cache],
        out_shape=[jax.ShapeDtypeStruct((NBT, D), F32),
                   jax.ShapeDtypeStruct(win_k.shape, F32), jax.ShapeDtypeStruct(win_v.shape, F32)],
        scratch_shapes=[pltpu.VMEM((R, D), BF16)],
        compiler_params=pltpu.CompilerParams(
            dimension_semantics=("arbitrary",), vmem_limit_bytes=vmem),
        name="sample_window_attention",
    )(sinks, x, g, wq, qg, k, v, win_k, win_v, wo)


def kernel(x_prompt, x_sample, state_mlstm_C, state_mlstm_n, state_mlstm_m, cache_win_k, cache_win_v,
           ml_norm_g, ml_w_in, ml_b_i, ml_b_f, ml_head_g, ml_w_out, kv_norm_g, w_kv, k_norm_g,
           att_norm_g, att_w_q, q_norm_g, att_sinks, att_w_o, mlp_norm_g, mlp_w1, mlp_w2):
    B, L, D = x_prompt.shape
    NS, LS, _ = x_sample.shape
    assert ml_w_in.shape[0] == 1 and att_w_q.shape[0] == 1 and mlp_w1.shape[0] == 2

    w_in = ml_w_in[0]
    wqkvo = w_in.astype(BF16)
    wg = jnp.pad(w_in[:, ML_QKVO:], ((0, 0), (0, LANES - 2 * ML_HEADS))).astype(BF16)
    bg = jnp.pad(jnp.concatenate([ml_b_i[0], ml_b_f[0]]), (0, LANES - 2 * ML_HEADS)).reshape(1, LANES)
    ml_g = ml_norm_g[0].reshape(1, D)
    hg = ml_head_g[0].reshape(1, ML_VO)
    wout = ml_w_out[0].astype(BF16)
    mlp_g = mlp_norm_g.reshape(2, 1, D)
    kv_g = kv_norm_g.reshape(1, D)
    kg = jnp.tile(k_norm_g, ATT_KVH).reshape(1, ATT_KV)
    att_g = att_norm_g[0].reshape(1, D)
    qg = jnp.tile(q_norm_g[0], 2).reshape(1, LANES)
    sinks = att_sinks[0]

    xp, p_C, p_n, p_m, w1, w2, wkv, wq, wo = _prompt_mixer(
        x_prompt, ml_g, wqkvo, wg, bg, hg, wout,
        [mlp_w1.reshape(2 * D, D_FF), mlp_w2.reshape(2 * D_FF, D), w_kv, att_w_q[0], att_w_o[0]])
    w1 = w1.reshape(2, D, D_FF)
    w2 = w2.reshape(2, D_FF, D)
    xp, kp, vp = _mlp(xp.reshape(B * L, D), mlp_g[0], w1, w2, 0, kv=(kv_g, wkv, kg))
    xp = _prompt_attn(xp.reshape(B, L, D), kp.reshape(B, L, ATT_KV), vp.reshape(B, L, ATT_KV),
                      sinks, att_g, wq, qg, wo)
    y_prompt = _mlp(xp.reshape(B * L, D), mlp_g[1], w1, w2, 1).reshape(B, L, D)
    p_wk = kp.reshape(B, L, ATT_KV)[:, L - WINDOW:].reshape(B, WINDOW, ATT_KVH, ATT_HD)
    p_wv = vp.reshape(B, L, ATT_KV)[:, L - WINDOW:].reshape(B, WINDOW, ATT_KVH, ATT_HD)

    def pad8(a):
        return jnp.pad(a.reshape(NS, LS, -1), ((0, 0), (0, SAMPLE_PAD - LS), (0, 0))).reshape(NS * SAMPLE_PAD, -1)

    def unpad8(a):
        return a.reshape(NS, SAMPLE_PAD, -1)[:, :LS].reshape(NS * LS, -1)

    m0 = jnp.broadcast_to(state_mlstm_m[0][:, :, None], (NS, ML_HEADS, LANES))
    xs, s_C, s_n, s_m = _sample_mixer(pad8(x_sample), ml_g, wqkvo, wg, bg, hg, wout,
                                      state_mlstm_C[0], state_mlstm_n[0], m0, LS)
    xs, ks, vs = _mlp(unpad8(xs), mlp_g[0], w1, w2, 0, kv=(kv_g, wkv, kg))
    to_t = lambda c: c.transpose(0, 2, 3, 1).reshape(NS, ATT_KV, WINDOW)
    from_t = lambda c: c.reshape(NS, ATT_KVH, ATT_HD, WINDOW).transpose(0, 3, 1, 2)
    xs, s_wk, s_wv = _sample_attn(pad8(xs), pad8(ks), pad8(vs), to_t(cache_win_k), to_t(cache_win_v),
                                  sinks, att_g, wq, qg, wo, LS)
    y_sample = _mlp(unpad8(xs), mlp_g[1], w1, w2, 1).reshape(NS, LS, D)

    return (y_prompt, y_sample,
            p_C[None], p_n[None], p_m[None, :, :, 0], p_wk, p_wv,
            s_C[None], s_n[None], s_m[None, :, :, 0],
            from_t(s_wk), from_t(s_wv))
```

```python
import functools

import jax
import jax.numpy as jnp
from jax import lax
from jax.experimental import pallas as pl
from jax.experimental.pallas import tpu as pltpu

F32 = jnp.float32
BF16 = jnp.bfloat16

D_MODEL = 1024
ML_HEADS = 4
ML_DK = 128
ML_DV = 256
ML_QK = ML_HEADS * ML_DK
ML_VO = ML_HEADS * ML_DV
ML_QKVO = 2 * ML_QK + 2 * ML_VO
GATE_SOFTCAP = 15.0
ATT_HD = 64
ATT_QH = 16
ATT_KVH = 4
ATT_GROUP = 4
ATT_KV = ATT_KVH * ATT_HD
WINDOW = 128
D_FF = 4 * D_MODEL
EPS = 1e-6
LOG2E = 1.4426950408889634

LANES = 128
SUBLANES = 8
VMEM_LIMIT_CAP = 56 * 1024 * 1024

PROMPT_CHUNK = 256
PROMPT_ROWS = 2
PROJ_COLS = 512
SAMPLE_PAD = SUBLANES
SAMPLE_NB = 16
ATTN_ROWS = 1024
ROW_TILE = 1024
FF_TILE = 1024


def _vmem_limit(*block_bytes):
    need = 4 * sum(block_bytes) + (8 << 20)
    return int(min(max(need, 32 << 20), VMEM_LIMIT_CAP))


def _nbytes(shape, dtype):
    n = 1
    for s in shape:
        n *= s
    return n * jnp.dtype(dtype).itemsize


def _rms(x, g):
    return x * lax.rsqrt(jnp.mean(x * x, axis=-1, keepdims=True) + EPS) * g


def _dot(a, b):
    return jnp.dot(a, b, preferred_element_type=F32)


def _split3(x):
    hi = x.astype(BF16)
    r1 = x - hi.astype(F32)
    mid = r1.astype(BF16)
    lo = (r1 - mid.astype(F32)).astype(BF16)
    return hi, mid, lo


def _dot_exactish(m01, x):
    hi, mid, lo = _split3(x)
    return _dot(m01, hi) + _dot(m01, mid) + _dot(m01, lo)


def _gate_act(z):
    cap = GATE_SOFTCAP * jnp.tanh(z * (1.0 / GATE_SOFTCAP))
    lsig = jnp.minimum(cap, 0.0) - jnp.log1p(jnp.exp(-jnp.abs(cap)))
    lane = lax.broadcasted_iota(jnp.int32, z.shape, 1)
    return jnp.where(lane < ML_HEADS, cap, lsig)


_DONE = object()


def _rounds(chains):
    live = [(c[0], c[1], c[2] if len(c) > 2 else 1) for c in chains]
    rnd = 0
    while live:
        for item in list(live):
            gen, start, stride = item
            if rnd >= start and (rnd - start) % stride == 0 and next(gen, _DONE) is _DONE:
                live.remove(item)
        rnd += 1
        yield


def _run(chains):
    for _ in _rounds(chains):
        pass


def _head_out(hh, po, hg):
    hn = hh * lax.rsqrt(jnp.mean(hh * hh, axis=-1, keepdims=True) + EPS) * hg
    return jax.nn.sigmoid(po) * hn


def _head_slices(p, h):
    q = p[:, h * ML_DK:(h + 1) * ML_DK]
    k = p[:, ML_QK + h * ML_DK:ML_QK + (h + 1) * ML_DK] * (ML_DK ** -0.5)
    v = p[:, 2 * ML_QK + h * ML_DV:2 * ML_QK + (h + 1) * ML_DV]
    po = p[:, 2 * ML_QK + ML_VO + h * ML_DV:2 * ML_QK + ML_VO + (h + 1) * ML_DV]
    return q, k, v, po


def _prompt_mixer_kernel(xc_ref, xnext_ref, g_ref, wqkvo_ref, wg_ref, bg_ref, hg_ref, wout_ref, *rest,
                         chunks_per_seq, n_cast):
    cast_in = rest[:n_cast]
    y_ref, C_ref, n_ref, m_ref = rest[n_cast:n_cast + 4]
    cast_out = rest[n_cast + 4:2 * n_cast + 4]
    p_s, gz_s = rest[2 * n_cast + 4:]
    R, T = xc_ref.shape[0], xc_ref.shape[1]
    f = pl.program_id(0)
    slot = f % 2

    def project(x_ref, s, r):
        xn = _rms(x_ref[r], g_ref[...]).astype(BF16)
        yield
        for j in range(ML_QKVO // PROJ_COLS):
            cols = slice(j * PROJ_COLS, (j + 1) * PROJ_COLS)
            p_s[s, r, :, cols] = _dot(xn, wqkvo_ref[:, cols])
            yield
        gz_s[s, r] = _dot(xn, wg_ref[...])
        yield

    @pl.when(f == 0)
    def _():
        _run([(project(xc_ref, 0, r), 0) for r in range(R)])

    @pl.when(f % chunks_per_seq == 0)
    def _():
        C_ref[...] = jnp.zeros_like(C_ref)
        n_ref[...] = jnp.zeros_like(n_ref)
        m_ref[...] = jnp.zeros_like(m_ref)

    def recurrence(r):
        G = _gate_act(gz_s[slot, r] + bg_ref[...])
        row = lax.broadcasted_iota(jnp.int32, (T, T), 0)
        col = lax.broadcasted_iota(jnp.int32, (T, T), 1)
        causal = col <= row
        Bc = _dot_exactish(jnp.where(causal, 1.0, 0.0).astype(BF16), G)
        yield
        Gt = G.T
        Bt = Bc.T
        yield
        parts = [None] * ML_HEADS

        def head(h):
            def cols(base, width):
                return p_s[slot, r, :, base + h * width:base + (h + 1) * width]

            q = cols(0, ML_DK)
            k = cols(ML_QK, ML_DK) * (ML_DK ** -0.5)
            v = cols(2 * ML_QK, ML_DV)
            qc, kc, vc = q.astype(BF16), k.astype(BF16), v.astype(BF16)
            qk = lax.dot_general(qc, kc, (((1,), (1,)), ((), ())), preferred_element_type=F32)
            yield
            li_col = G[:, h:h + 1]
            b_col = Bc[:, ML_HEADS + h:ML_HEADS + h + 1]
            dmat = jnp.where(causal, b_col - Bt[ML_HEADS + h:ML_HEADS + h + 1, :] + Gt[h:h + 1, :], -jnp.inf)
            m_prev = m_ref[r, h:h + 1, 0:1]
            inter = b_col + m_prev
            m_t = jnp.maximum(inter, jnp.max(dmat, axis=1, keepdims=True))
            s = qk * jnp.exp(dmat - m_t)
            a_inter = jnp.exp(inter - m_t)
            den = (jnp.sum(s, axis=1, keepdims=True)
                   + a_inter * jnp.sum(q * n_ref[r, h:h + 1, :], axis=1, keepdims=True))
            yield
            C = C_ref[r, h]
            num = _dot(s.astype(BF16), vc) + a_inter * _dot(qc, C.astype(BF16))
            yield
            hh = num * (1.0 / jnp.maximum(jnp.abs(den), jnp.exp(-m_t)))
            hs = _head_out(hh, cols(2 * ML_QK + ML_VO, ML_DV), hg_ref[:, h * ML_DV:(h + 1) * ML_DV])
            yield
            parts[h] = _dot(hs.astype(BF16), wout_ref[h * ML_DV:(h + 1) * ML_DV, :])
            yield
            m_new = m_t[T - 1:T, :]
            b_last = b_col[T - 1:T, :]
            decay = jnp.exp(b_last + m_prev - m_new)
            kw = k * jnp.exp(b_last - b_col + li_col - m_new)
            C_ref[r, h] = decay * C + lax.dot_general(kw.astype(BF16), vc, (((0,), (0,)), ((), ())),
                                                      preferred_element_type=F32)
            n_ref[r, h:h + 1, :] = decay * n_ref[r, h:h + 1, :] + jnp.sum(kw, axis=0, keepdims=True)
            m_ref[r, h:h + 1, :] = jnp.broadcast_to(m_new, (1, LANES))
            yield

        yield from _rounds([(head(h), 2 * h) for h in range(ML_HEADS)])
        y_ref[r] = xc_ref[r] + ((parts[0] + parts[1]) + (parts[2] + parts[3]))

    def casts():
        for src, dst in zip(cast_in, cast_out):
            dst[...] = src[...].astype(BF16)
            yield

    _run([(project(xnext_ref, 1 - slot, r), 0) for r in range(R)] + [(recurrence(r), r) for r in range(R)]
         + [(casts(), 0)])


def _prompt_mixer(x, g, wqkvo, wg, bg, hg, wout, to_cast):
    B, L, D = x.shape
    T = PROMPT_CHUNK
    assert L % T == 0
    R = PROMPT_ROWS
    assert B % R == 0
    nc = L // T
    steps = (B // R) * nc
    const = lambda f: (0, 0)
    cur = lambda f: (f // nc, f % nc, 0)
    nxt = lambda f: (jnp.minimum(f + 1, steps - 1) // nc, jnp.minimum(f + 1, steps - 1) % nc, 0)
    once = dict(pipeline_mode=pl.Buffered(1))
    vmem = _vmem_limit(3 * _nbytes((R, T, D), F32), _nbytes(wqkvo.shape, BF16) // 2,
                       _nbytes(wout.shape, BF16) // 2, _nbytes((R, ML_HEADS, ML_DK, ML_DV), F32),
                       2 * _nbytes((R, T, ML_QKVO), F32))
    for w in to_cast:
        assert w.shape[0] % (steps * 2 * SUBLANES) == 0
    cast_specs = [pl.BlockSpec((w.shape[0] // steps, w.shape[1]), lambda f: (f, 0)) for w in to_cast]
    outs = pl.pallas_call(
        functools.partial(_prompt_mixer_kernel, chunks_per_seq=nc, n_cast=len(to_cast)),
        grid=(steps,),
        in_specs=[
            pl.BlockSpec((R, T, D), cur),
            pl.BlockSpec((R, T, D), nxt),
            pl.BlockSpec((1, D), const),
            pl.BlockSpec((D, ML_QKVO), const, **once),
            pl.BlockSpec((D, LANES), const, **once),
            pl.BlockSpec((1, LANES), const),
            pl.BlockSpec((1, ML_VO), const),
            pl.BlockSpec((ML_VO, D), const, **once),
        ] + cast_specs,
        out_specs=[
            pl.BlockSpec((R, T, D), cur),
            pl.BlockSpec((R, ML_HEADS, ML_DK, ML_DV), lambda f: (f // nc, 0, 0, 0)),
            pl.BlockSpec((R, ML_HEADS, ML_DK), lambda f: (f // nc, 0, 0)),
            pl.BlockSpec((R, ML_HEADS, LANES), lambda f: (f // nc, 0, 0)),
        ] + cast_specs,
        out_shape=[
            jax.ShapeDtypeStruct((B, L, D), F32),
            jax.ShapeDtypeStruct((B, ML_HEADS, ML_DK, ML_DV), F32),
            jax.ShapeDtypeStruct((B, ML_HEADS, ML_DK), F32),
            jax.ShapeDtypeStruct((B, ML_HEADS, LANES), F32),
        ] + [jax.ShapeDtypeStruct(w.shape, BF16) for w in to_cast],
        scratch_shapes=[pltpu.VMEM((2, R, T, ML_QKVO), F32), pltpu.VMEM((2, R, T, LANES), F32)],
        compiler_params=pltpu.CompilerParams(
            dimension_semantics=("arbitrary",), vmem_limit_bytes=vmem),
        name="prompt_mlstm_mixer",
    )(x, x, g, wqkvo, wg, bg, hg, wout, *to_cast)
    return outs


def _sample_mixer_kernel(x_ref, g_ref, wqkvo_ref, wg_ref, bg_ref, hg_ref, wout_ref,
                         C0_ref, n0_ref, m0_ref,
                         y_ref, C_ref, n_ref, m_ref, hs_s, *, n_valid):
    T = SAMPLE_PAD
    nb = C0_ref.shape[0]
    R = nb * T
    x = x_ref[...]
    xn = _rms(x, g_ref[...]).astype(BF16)
    p = _dot(xn, wqkvo_ref[...])

    rowt = lax.broadcasted_iota(jnp.int32, (R, LANES), 0) % T
    lane = lax.broadcasted_iota(jnp.int32, (R, LANES), 1)
    G = jnp.where(rowt < n_valid, _gate_act(_dot(xn, wg_ref[...]) + bg_ref[...]),
                  jnp.where(lane < ML_HEADS, -jnp.inf, 0.0))
    row = lax.broadcasted_iota(jnp.int32, (R, R), 0)
    col = lax.broadcasted_iota(jnp.int32, (R, R), 1)
    causal = (row // T == col // T) & (col <= row)
    Bc = _dot_exactish(jnp.where(causal, 1.0, 0.0).astype(BF16), jnp.where(lane < ML_HEADS, 0.0, G))
    Gt = G.T
    Bt = Bc.T

    def per_seq(fn):
        return jnp.concatenate([fn(a) for a in range(nb)], axis=0)

    def seq_last(colvec):
        return per_seq(lambda a: jnp.broadcast_to(colvec[T * a + T - 1:T * a + T, :], (T, 1)))

    def head(h):
        q, k, v, po = _head_slices(p, h)
        li_col = G[:, h:h + 1]
        b_col = Bc[:, ML_HEADS + h:ML_HEADS + h + 1]
        li_row = Gt[h:h + 1, :]
        b_row = Bt[ML_HEADS + h:ML_HEADS + h + 1, :]
        dmat = jnp.where(causal, b_col - (b_row - li_row), -jnp.inf)
        m_prev = per_seq(lambda a: jnp.broadcast_to(m0_ref[a, h:h + 1, 0:1], (T, 1)))
        n_rows = per_seq(lambda a: jnp.broadcast_to(n0_ref[a, h:h + 1, :], (T, ML_DK)))
        qc, kc, vc = q.astype(BF16), k.astype(BF16), v.astype(BF16)
        qk = lax.dot_general(qc, kc, (((1,), (1,)), ((), ())), preferred_element_type=F32)
        yield
        inter = b_col + m_prev
        m_t = jnp.maximum(inter, jnp.max(dmat, axis=1, keepdims=True))
        s = qk * jnp.exp(dmat - m_t)
        a_inter = jnp.exp(inter - m_t)
        yield
        qC = per_seq(lambda a: _dot(q[T * a:T * a + T, :], C0_ref[a, h]))
        yield
        num = _dot(s.astype(BF16), vc) + a_inter * qC
        den = jnp.sum(s, axis=1, keepdims=True) + a_inter * jnp.sum(q * n_rows, axis=1, keepdims=True)
        hh = num * (1.0 / jnp.maximum(jnp.abs(den), jnp.exp(-m_t)))
        hs_s[:, h * ML_DV:(h + 1) * ML_DV] = _head_out(
            hh, po, hg_ref[:, h * ML_DV:(h + 1) * ML_DV]).astype(BF16)
        yield
        m_new = seq_last(m_t)
        b_last = seq_last(b_col)
        decay = jnp.exp(b_last + m_prev - m_new)
        kw = k * jnp.exp(b_last - b_col + li_col - m_new)
        for a in range(nb):
            dec = decay[T * a:T * a + 1, :]
            kw_a = kw[T * a:T * a + T, :]
            upd = lax.dot_general(kw_a, v[T * a:T * a + T, :], (((0,), (0,)), ((), ())),
                                  preferred_element_type=F32)
            C_ref[a, h] = dec * C0_ref[a, h] + upd
            n_ref[a, h:h + 1, :] = dec * n0_ref[a, h:h + 1, :] + jnp.sum(kw_a, axis=0, keepdims=True)
            m_ref[a, h:h + 1, :] = jnp.broadcast_to(m_new[T * a:T * a + 1, :], (1, LANES))
            if a % 4 == 3:
                yield

    _run([(head(h), h) for h in range(ML_HEADS)])
    y_ref[...] = x + _dot(hs_s[...], wout_ref[...])


def _sample_mixer(x, g, wqkvo, wg, bg, hg, wout, C0, n0, m0, n_valid):
    NBT, D = x.shape
    nseq = C0.shape[0]
    nb = SAMPLE_NB
    R = nb * SAMPLE_PAD
    assert nseq % nb == 0 and NBT == nseq * SAMPLE_PAD
    const = lambda i: (0, 0)
    state_specs = [
        pl.BlockSpec((nb, ML_HEADS, ML_DK, ML_DV), lambda i: (i, 0, 0, 0)),
        pl.BlockSpec((nb, ML_HEADS, ML_DK), lambda i: (i, 0, 0)),
        pl.BlockSpec((nb, ML_HEADS, LANES), lambda i: (i, 0, 0)),
    ]
    vmem = _vmem_limit(2 * _nbytes((R, D), F32), _nbytes(wqkvo.shape, BF16), _nbytes(wout.shape, BF16),
                       2 * _nbytes((nb, ML_HEADS, ML_DK, ML_DV), F32), _nbytes((R, ML_QKVO), F32))
    return pl.pallas_call(
        functools.partial(_sample_mixer_kernel, n_valid=n_valid),
        grid=(nseq // nb,),
        in_specs=[
            pl.BlockSpec((R, D), lambda i: (i, 0)),
            pl.BlockSpec((1, D), const),
            pl.BlockSpec((D, ML_QKVO), const),
            pl.BlockSpec((D, LANES), const),
            pl.BlockSpec((1, LANES), const),
            pl.BlockSpec((1, ML_VO), const),
            pl.BlockSpec((ML_VO, D), const),
        ] + state_specs,
        out_specs=[pl.BlockSpec((R, D), lambda i: (i, 0))] + state_specs,
        out_shape=[
            jax.ShapeDtypeStruct((NBT, D), F32),
            jax.ShapeDtypeStruct(C0.shape, F32),
            jax.ShapeDtypeStruct(n0.shape, F32),
            jax.ShapeDtypeStruct(m0.shape, F32),
        ],
        scratch_shapes=[pltpu.VMEM((R, ML_VO), BF16)],
        compiler_params=pltpu.CompilerParams(
            dimension_semantics=("arbitrary",), vmem_limit_bytes=vmem),
        name="sample_mlstm_mixer",
    )(x, g, wqkvo, wg, bg, hg, wout, C0, n0, m0)


def _dot_exactish_right(x, m01):
    hi, mid, lo = _split3(x)
    return _dot(hi, m01) + _dot(mid, m01) + _dot(lo, m01)


def _mlp_kernel(x_ref, g_ref, w1_ref, w2_ref, *rest):
    x = x_ref[...]
    xn = _rms(x, g_ref[...]).astype(BF16)
    acc = x
    for c in range(D_FF // FF_TILE):
        hcol = _dot(xn, w1_ref[:, c * FF_TILE:(c + 1) * FF_TILE])
        hcol = jnp.square(jnp.maximum(hcol, 0.0)).astype(BF16)
        acc = acc + _dot(hcol, w2_ref[c * FF_TILE:(c + 1) * FF_TILE, :])
    if len(rest) == 1:
        (y_ref,) = rest
        y_ref[...] = acc
        return
    gkv_ref, wkv_ref, kg_ref, y_ref, k_ref, v_ref = rest
    y_ref[...] = acc
    kv = _dot(_rms(acc, gkv_ref[...]).astype(BF16), wkv_ref[...])
    kraw = kv[:, :ATT_KV]
    v_ref[...] = kv[:, ATT_KV:]
    r = lax.broadcasted_iota(jnp.int32, (ATT_KV, ATT_KV), 0) // ATT_HD
    c = lax.broadcasted_iota(jnp.int32, (ATT_KV, ATT_KV), 1) // ATT_HD
    seg = jnp.where(r == c, 1.0, 0.0).astype(BF16)
    ss = _dot_exactish_right(kraw * kraw, seg)
    k_ref[...] = kraw * lax.rsqrt(ss * (1.0 / ATT_HD) + EPS) * kg_ref[...]


def _mlp(x, g, w1, w2, layer, kv=None):
    N, D = x.shape
    tm = min(ROW_TILE, N)
    assert N % tm == 0
    const = lambda i: (0, 0)
    rows = lambda i: (i, 0)
    once = dict(pipeline_mode=pl.Buffered(1))
    in_specs = [
        pl.BlockSpec((tm, D), rows),
        pl.BlockSpec((1, D), const),
        pl.BlockSpec((None, D, D_FF), lambda i: (layer, 0, 0), **once),
        pl.BlockSpec((None, D_FF, D), lambda i: (layer, 0, 0), **once),
    ]
    out_specs = [pl.BlockSpec((tm, D), rows)]
    out_shape = [jax.ShapeDtypeStruct((N, D), F32)]
    args = [x, g, w1, w2]
    if kv is not None:
        in_specs += [pl.BlockSpec((1, D), const), pl.BlockSpec((D, 2 * ATT_KV), const, **once),
                     pl.BlockSpec((1, ATT_KV), const)]
        out_specs += [pl.BlockSpec((tm, ATT_KV), rows)] * 2
        out_shape += [jax.ShapeDtypeStruct((N, ATT_KV), F32)] * 2
        args += list(kv)
    vmem = _vmem_limit(2 * _nbytes((tm, D), F32), _nbytes(w1.shape[1:], BF16) // 2,
                       _nbytes(w2.shape[1:], BF16) // 2, 2 * _nbytes((tm, FF_TILE), F32))
    out = pl.pallas_call(
        _mlp_kernel,
        grid=(N // tm,),
        in_specs=in_specs,
        out_specs=out_specs,
        out_shape=out_shape,
        compiler_params=pltpu.CompilerParams(
            dimension_semantics=("arbitrary",), vmem_limit_bytes=vmem),
        name="sqrelu_mlp",
    )(*args)
    return out[0] if kv is None else out


def _pair_queries(qraw, pr, qscale, col0=0):
    TQ = qraw.shape[0]
    lo = lax.broadcasted_iota(jnp.int32, (TQ, LANES), 1) < ATT_HD
    out = []
    for e in range(2):
        kvh = 2 * pr + e
        for g in range(ATT_GROUP):
            cc, half = divmod(g, 2)
            c0 = (2 * kvh + cc) * LANES - col0
            q2 = qraw[:, c0:c0 + LANES]
            qm = jnp.where(lo, q2, 0.0) if half == 0 else jnp.where(lo, 0.0, q2)
            ss = jnp.sum(qm * qm, axis=1, keepdims=True)
            qn = qm * lax.rsqrt(ss * (1.0 / ATT_HD) + EPS)
            if qscale is not None:
                qn = qn * qscale
            out.append(qn if half == e else pltpu.roll(qn, ATT_HD, 1))
    return out


def _pair_outputs(o_heads, rden, pr, store):
    TQ = o_heads[0].shape[0]
    lo = lax.broadcasted_iota(jnp.int32, (TQ, LANES), 1) < ATT_HD
    for e in range(2):
        kvh = 2 * pr + e
        for cc in range(2):
            tiles = []
            for half in range(2):
                o = o_heads[e * ATT_GROUP + 2 * cc + half] * rden[e * ATT_GROUP + 2 * cc + half]
                tiles.append(o if half == e else pltpu.roll(o, ATT_HD, 1))
            store((2 * kvh + cc) * LANES, jnp.where(lo, tiles[0], tiles[1]))


def _softmax_with_sink(parts, sink):
    assert all(s.shape == parts[0].shape for s in parts)
    M = jnp.maximum(sink, jnp.max(functools.reduce(jnp.maximum, parts), axis=1, keepdims=True))
    ps = [jnp.exp2(s - M) for s in parts]
    den = jnp.exp2(sink - M) + jnp.sum(functools.reduce(jnp.add, ps), axis=1, keepdims=True)
    return ps, 1.0 / den


def _prompt_attn_kernel(sinks_ref, x_ref, g_ref, wq_ref, qg_ref, kp_ref, kc_ref, vp_ref, vc_ref, wo_ref,
                        y_ref):
    TQ = WINDOW
    TR = x_ref.shape[0]
    kj = lax.broadcasted_iota(jnp.int32, (WINDOW + TQ, TQ), 0)
    qi = lax.broadcasted_iota(jnp.int32, (WINDOW + TQ, TQ), 1)
    band = (qi + WINDOW - kj >= 0) & (qi - kj <= 0)
    band_first = band & ((kj >= WINDOW) | (pl.program_id(1) > 0))
    kscale = qg_ref[...] * (ATT_HD ** -0.5 * LOG2E)
    slabs = []
    for pr in range(ATT_KVH // 2):
        sl = slice(pr * LANES, (pr + 1) * LANES)
        slabs.append(((jnp.concatenate([kp_ref[:, sl], kc_ref[:, sl]], axis=0) * kscale).astype(BF16),
                      jnp.concatenate([vp_ref[:, sl], vc_ref[:, sl]], axis=0).T))

    PW = 2 * ATT_GROUP * ATT_HD
    zeros_hd = jnp.zeros((ATT_HD, TQ), F32)
    ones_rows = jnp.ones((2 * SUBLANES, WINDOW + TQ), F32)

    def chain(sb):
        rows = slice(sb * TQ, (sb + 1) * TQ)
        keys = slice(sb * TQ, sb * TQ + WINDOW + TQ)
        mask = band_first if sb == 0 else band
        x = x_ref[rows, :]
        xn = _rms(x, g_ref[...]).astype(BF16)
        acc = x
        for pr in range(ATT_KVH // 2):
            kslab, vslab_t = slabs[pr]
            qt = _dot(xn, wq_ref[:, pr * PW:(pr + 1) * PW]).T
            yield
            tiles, sink_rows = [], []
            for e in range(2):
                for g in range(ATT_GROUP):
                    blk = qt[(e * ATT_GROUP + g) * ATT_HD:(e * ATT_GROUP + g + 1) * ATT_HD, :]
                    qn = blk * lax.rsqrt(jnp.sum(blk * blk, axis=0, keepdims=True) * (1.0 / ATT_HD) + EPS)
                    tiles.append(jnp.concatenate([qn, zeros_hd] if e == 0 else [zeros_hd, qn], axis=0))
                    sink_rows.append(jnp.full((1, TQ), sinks_ref[(2 * pr + e) * ATT_GROUP + g] * LOG2E, F32))
            qmat = jnp.concatenate(tiles, axis=1).astype(BF16)
            st = _dot(kslab[keys, :], qmat)
            yield
            parts = []
            for e in range(2):
                ps, ms = [], []
                for g in range(ATT_GROUP):
                    c0 = (e * ATT_GROUP + g) * TQ
                    s_h = jnp.where(mask, st[:, c0:c0 + TQ], -jnp.inf)
                    m_h = jnp.maximum(jnp.max(s_h, axis=0, keepdims=True), sink_rows[e * ATT_GROUP + g])
                    ps.append(jnp.exp2(s_h - m_h).astype(BF16))
                    ms.append(m_h)
                p_e = jnp.concatenate(ps, axis=1)
                v_aug = jnp.concatenate([vslab_t[e * ATT_HD:(e + 1) * ATT_HD, keys], ones_rows],
                                        axis=0).astype(BF16)
                ot = _dot(v_aug, p_e)
                yield
                sink_term = jnp.exp2(jnp.concatenate(sink_rows[e * ATT_GROUP:(e + 1) * ATT_GROUP], axis=1)
                                     - jnp.concatenate(ms, axis=1))
                on = ot[0:ATT_HD, :] * (1.0 / (ot[ATT_HD:ATT_HD + 1, :] + sink_term))
                parts += [on[:, g * TQ:(g + 1) * TQ] for g in range(ATT_GROUP)]
            o_pair = jnp.concatenate(parts, axis=0).T.astype(BF16)
            acc = acc + _dot(o_pair, wo_ref[pr * PW:(pr + 1) * PW, :])
            yield
        y_ref[rows, :] = acc

    _run([(chain(sb), 0) for sb in range(TR // TQ)])


def _prompt_attn(x, k, v, sinks, g, wq, qg, wo):
    B, L, D = x.shape
    TR = ATTN_ROWS
    per = TR // WINDOW
    assert L % TR == 0 and TR % WINDOW == 0
    const = lambda b, i: (0, 0)
    cur = lambda b, i: (b, i, 0)
    prev = lambda b, i: (b, jnp.maximum(i * per - 1, 0), 0)
    vmem = _vmem_limit(2 * _nbytes((TR, D), F32), _nbytes(wq.shape, BF16), _nbytes(wo.shape, BF16),
                       4 * _nbytes((TR, ATT_KV), F32), 8 * _nbytes((8 * WINDOW, 2 * WINDOW), F32))
    return pl.pallas_call(
        _prompt_attn_kernel,
        grid=(B, L // TR),
        in_specs=[
            pl.BlockSpec(memory_space=pltpu.SMEM),
            pl.BlockSpec((None, TR, D), cur),
            pl.BlockSpec((1, D), const),
            pl.BlockSpec((D, D), const),
            pl.BlockSpec((1, LANES), const),
            pl.BlockSpec((None, WINDOW, ATT_KV), prev),
            pl.BlockSpec((None, TR, ATT_KV), cur),
            pl.BlockSpec((None, WINDOW, ATT_KV), prev),
            pl.BlockSpec((None, TR, ATT_KV), cur),
            pl.BlockSpec((D, D), const),
        ],
        out_specs=pl.BlockSpec((None, TR, D), cur),
        out_shape=jax.ShapeDtypeStruct((B, L, D), F32),
        compiler_params=pltpu.CompilerParams(
            dimension_semantics=("arbitrary", "arbitrary"), vmem_limit_bytes=vmem),
        name="prompt_window_attention",
    )(sinks, x, g, wq, qg, k, k, v, v, wo)


def _sample_attn_kernel(sinks_ref, x_ref, g_ref, wq_ref, qg_ref, kc_ref, vc_ref, wk_ref, wv_ref, wo_ref,
                        y_ref, nk_ref, nv_ref, o_s, *, n_valid):
    T = SAMPLE_PAD
    nb = wk_ref.shape[0]
    R = nb * T
    assert R == WINDOW
    x = x_ref[...]
    q = _dot(_rms(x, g_ref[...]).astype(BF16), wq_ref[...])
    qscale = qg_ref[...] * (ATT_HD ** -0.5 * LOG2E)
    knew = kc_ref[...]
    vnew = vc_ref[...]

    cmask = (lax.broadcasted_iota(jnp.int32, (R, WINDOW), 1)
             >= lax.broadcasted_iota(jnp.int32, (R, WINDOW), 0) % T)
    row = lax.broadcasted_iota(jnp.int32, (R, R), 0)
    col = lax.broadcasted_iota(jnp.int32, (R, R), 1)
    nmask = (row // T == col // T) & (col % T <= row % T) & (col % T < n_valid)

    def store(off, val):
        o_s[:, off:off + LANES] = val.astype(BF16)

    def regroup(per_seq, i):
        return jnp.concatenate([per_seq[a][i * T:(i + 1) * T, :] for a in range(nb)], axis=0)

    def pair(pr):
        sl = slice(pr * LANES, (pr + 1) * LANES)
        qs = _pair_queries(q, pr, qscale)
        S_new = lax.dot_general(jnp.concatenate(qs, axis=0).astype(BF16), knew[:, sl].astype(BF16),
                                (((1,), (1,)), ((), ())), preferred_element_type=F32)
        yield
        sc = []
        for a in range(nb):
            q_a = jnp.concatenate([qi[T * a:T * a + T, :] for qi in qs], axis=0).astype(BF16)
            sc.append(_dot(q_a, wk_ref[a, sl, :].astype(BF16)))
            if a % 4 == 3:
                yield
        pcs, pns, rden = [], [], []
        for i in range(2 * ATT_GROUP):
            s_c = jnp.where(cmask, regroup(sc, i), -jnp.inf)
            s_n = jnp.where(nmask, S_new[i * R:(i + 1) * R, :], -jnp.inf)
            (p_c, p_n), r = _softmax_with_sink([s_c, s_n], sinks_ref[pr * 2 * ATT_GROUP + i] * LOG2E)
            pcs.append(p_c)
            pns.append(p_n)
            rden.append(r)
            if i % 2 == 1:
                yield
        O_new = _dot(jnp.concatenate(pns, axis=0).astype(BF16), vnew[:, sl].astype(BF16))
        yield
        oc = []
        for a in range(nb):
            p_a = jnp.concatenate([pc[T * a:T * a + T, :] for pc in pcs], axis=0).astype(BF16)
            oc.append(lax.dot_general(p_a, wv_ref[a, sl, :].astype(BF16), (((1,), (1,)), ((), ())),
                                      preferred_element_type=F32))
            if a % 4 == 3:
                yield
        _pair_outputs([O_new[i * R:(i + 1) * R, :] + regroup(oc, i) for i in range(2 * ATT_GROUP)],
                      rden, pr, store)
        yield

    def roll_cache():
        keep = lax.broadcasted_iota(jnp.int32, (ATT_KV, WINDOW), 1) < WINDOW - n_valid
        k_parts = _split3(knew.T)
        v_parts = _split3(vnew.T)
        group = 4
        rr = lax.broadcasted_iota(jnp.int32, (R, group * WINDOW), 0)
        cc = lax.broadcasted_iota(jnp.int32, (R, group * WINDOW), 1)
        hit = (cc % WINDOW == WINDOW - n_valid + rr % T) & (rr % T < n_valid)
        seq_off = rr // T - cc // WINDOW
        yield
        for g0 in range(0, nb, group):
            place = jnp.where(hit & (seq_off == g0), 1.0, 0.0).astype(BF16)
            new_k = _dot(k_parts[0], place) + _dot(k_parts[1], place) + _dot(k_parts[2], place)
            new_v = _dot(v_parts[0], place) + _dot(v_parts[1], place) + _dot(v_parts[2], place)
            for j in range(group):
                lanes = slice(j * WINDOW, (j + 1) * WINDOW)
                nk_ref[g0 + j] = jnp.where(keep, pltpu.roll(wk_ref[g0 + j], WINDOW - n_valid, 1), new_k[:, lanes])
                nv_ref[g0 + j] = jnp.where(keep, pltpu.roll(wv_ref[g0 + j], WINDOW - n_valid, 1), new_v[:, lanes])
            yield

    _run([(pair(0), 0), (pair(1), 2), (roll_cache(), 0)])
    y_ref[...] = x + _dot(o_s[...], wo_ref[...])


def _sample_attn(x, k, v, win_k, win_v, sinks, g, wq, qg, wo, n_valid):
    NBT, D = x.shape
    nseq = win_k.shape[0]
    nb = SAMPLE_NB
    R = nb * SAMPLE_PAD
    assert nseq % nb == 0 and NBT == nseq * SAMPLE_PAD
    const = lambda i: (0, 0)
    rows = lambda i: (i, 0)
    cache = pl.BlockSpec((nb, ATT_KV, WINDOW), lambda i: (i, 0, 0))
    vmem = _vmem_limit(2 * _nbytes((R, D), F32), _nbytes(wq.shape, BF16), _nbytes(wo.shape, BF16),
                       4 * _nbytes((nb, ATT_KV, WINDOW), F32), 2 * _nbytes((R, D), F32))
    return pl.pallas_call(
        functools.partial(_sample_attn_kernel, n_valid=n_valid),
        grid=(nseq // nb,),
        in_specs=[
            pl.BlockSpec(memory_space=pltpu.SMEM),
            pl.BlockSpec((R, D), rows),
            pl.BlockSpec((1, D), const),
            pl.BlockSpec((D, D), const),
            pl.BlockSpec((1, LANES), const),
            pl.BlockSpec((R, ATT_KV), rows),
            pl.BlockSpec((R, ATT_KV), rows),
            cache, cache,
            pl.BlockSpec((D, D), const),
        ],
        out_specs=[pl.BlockSpec((R, D), rows), cache, cache],
        out_shape=[jax.ShapeDtypeStruct((NBT, D), F32),
                   jax.ShapeDtypeStruct(win_k.shape, F32), jax.ShapeDtypeStruct(win_v.shape, F32)],
        scratch_shapes=[pltpu.VMEM((R, D), BF16)],
        compiler_params=pltpu.CompilerParams(
            dimension_semantics=("arbitrary",), vmem_limit_bytes=vmem),
        name="sample_window_attention",
    )(sinks, x, g, wq, qg, k, v, win_k, win_v, wo)


def kernel(x_prompt, x_sample, state_mlstm_C, state_mlstm_n, state_mlstm_m, cache_win_k, cache_win_v,
           ml_norm_g, ml_w_in, ml_b_i, ml_b_f, ml_head_g, ml_w_out, kv_norm_g, w_kv, k_norm_g,
           att_norm_g, att_w_q, q_norm_g, att_sinks, att_w_o, mlp_norm_g, mlp_w1, mlp_w2):
    B, L, D = x_prompt.shape
    NS, LS, _ = x_sample.shape
    assert ml_w_in.shape[0] == 1 and att_w_q.shape[0] == 1 and mlp_w1.shape[0] == 2

    w_in = ml_w_in[0]
    wqkvo = w_in.astype(BF16)
    wg = jnp.pad(w_in[:, ML_QKVO:], ((0, 0), (0, LANES - 2 * ML_HEADS))).astype(BF16)
    bg = jnp.pad(jnp.concatenate([ml_b_i[0], ml_b_f[0]]), (0, LANES - 2 * ML_HEADS)).reshape(1, LANES)
    ml_g = ml_norm_g[0].reshape(1, D)
    hg = ml_head_g[0].reshape(1, ML_VO)
    wout = ml_w_out[0].astype(BF16)
    mlp_g = mlp_norm_g.reshape(2, 1, D)
    kv_g = kv_norm_g.reshape(1, D)
    kg = jnp.tile(k_norm_g, ATT_KVH).reshape(1, ATT_KV)
    att_g = att_norm_g[0].reshape(1, D)
    qg = jnp.tile(q_norm_g[0], 2).reshape(1, LANES)
    sinks = att_sinks[0]

    xp, p_C, p_n, p_m, w1, w2, wkv, wq, wo = _prompt_mixer(
        x_prompt, ml_g, wqkvo, wg, bg, hg, wout,
        [mlp_w1.reshape(2 * D, D_FF), mlp_w2.reshape(2 * D_FF, D), w_kv, att_w_q[0], att_w_o[0]])
    w1 = w1.reshape(2, D, D_FF)
    w2 = w2.reshape(2, D_FF, D)
    xp, kp, vp = _mlp(xp.reshape(B * L, D), mlp_g[0], w1, w2, 0, kv=(kv_g, wkv, kg))
    xp = _prompt_attn(xp.reshape(B, L, D), kp.reshape(B, L, ATT_KV), vp.reshape(B, L, ATT_KV),
                      sinks, att_g, wq, qg, wo)
    y_prompt = _mlp(xp.reshape(B * L, D), mlp_g[1], w1, w2, 1).reshape(B, L, D)
    p_wk = kp.reshape(B, L, ATT_KV)[:, L - WINDOW:].reshape(B, WINDOW, ATT_KVH, ATT_HD)
    p_wv = vp.reshape(B, L, ATT_KV)[:, L - WINDOW:].reshape(B, WINDOW, ATT_KVH, ATT_HD)

    def pad8(a):
        return jnp.pad(a.reshape(NS, LS, -1), ((0, 0), (0, SAMPLE_PAD - LS), (0, 0))).reshape(NS * SAMPLE_PAD, -1)

    def unpad8(a):
        return a.reshape(NS, SAMPLE_PAD, -1)[:, :LS].reshape(NS * LS, -1)

    m0 = jnp.broadcast_to(state_mlstm_m[0][:, :, None], (NS, ML_HEADS, LANES))
    xs, s_C, s_n, s_m = _sample_mixer(pad8(x_sample), ml_g, wqkvo, wg, bg, hg, wout,
                                      state_mlstm_C[0], state_mlstm_n[0], m0, LS)
    xs, ks, vs = _mlp(unpad8(xs), mlp_g[0], w1, w2, 0, kv=(kv_g, wkv, kg))
    to_t = lambda c: c.transpose(0, 2, 3, 1).reshape(NS, ATT_KV, WINDOW)
    from_t = lambda c: c.reshape(NS, ATT_KVH, ATT_HD, WINDOW).transpose(0, 3, 1, 2)
    xs, s_wk, s_wv = _sample_attn(pad8(xs), pad8(ks), pad8(vs), to_t(cache_win_k), to_t(cache_win_v),
                                  sinks, att_g, wq, qg, wo, LS)
    y_sample = _mlp(unpad8(xs), mlp_g[1], w1, w2, 1).reshape(NS, LS, D)

    return (y_prompt, y_sample,
            p_C[None], p_n[None], p_m[None, :, :, 0], p_wk, p_wv,
            s_C[None], s_n[None], s_m[None, :, :, 0],
            from_t(s_wk), from_t(s_wv))
```

```python
import functools

import jax
import jax.numpy as jnp
from jax import lax
from jax.experimental import pallas as pl
from jax.experimental.pallas import tpu as pltpu

F32 = jnp.float32
BF16 = jnp.bfloat16

D_MODEL = 1024
ML_HEADS = 4
ML_DK = 128
ML_DV = 256
ML_QK = ML_HEADS * ML_DK
ML_VO = ML_HEADS * ML_DV
ML_QKVO = 2 * ML_QK + 2 * ML_VO
GATE_SOFTCAP = 15.0
ATT_HD = 64
ATT_QH = 16
ATT_KVH = 4
ATT_GROUP = 4
ATT_KV = ATT_KVH * ATT_HD
WINDOW = 128
D_FF = 4 * D_MODEL
EPS = 1e-6
LOG2E = 1.4426950408889634

LANES = 128
SUBLANES = 8
VMEM_LIMIT_CAP = 56 * 1024 * 1024

PROMPT_CHUNK = 256
PROMPT_ROWS = 2
PROJ_COLS = 512
SAMPLE_PAD = SUBLANES
SAMPLE_NB = 16
ATTN_ROWS = 1024
ATTN_CHAIN_BLOCKS = 2
ROW_TILE = 1024
FF_TILE = 1024


def _vmem_limit(*block_bytes):
    need = 4 * sum(block_bytes) + (8 << 20)
    return int(min(max(need, 32 << 20), VMEM_LIMIT_CAP))


def _nbytes(shape, dtype):
    n = 1
    for s in shape:
        n *= s
    return n * jnp.dtype(dtype).itemsize


def _rms(x, g):
    return x * lax.rsqrt(jnp.mean(x * x, axis=-1, keepdims=True) + EPS) * g


def _dot(a, b):
    return jnp.dot(a, b, preferred_element_type=F32)


def _split3(x):
    hi = x.astype(BF16)
    r1 = x - hi.astype(F32)
    mid = r1.astype(BF16)
    lo = (r1 - mid.astype(F32)).astype(BF16)
    return hi, mid, lo


def _dot_exactish(m01, x):
    hi, mid, lo = _split3(x)
    return _dot(m01, hi) + _dot(m01, mid) + _dot(m01, lo)


def _gate_act(z):
    cap = GATE_SOFTCAP * jnp.tanh(z * (1.0 / GATE_SOFTCAP))
    lsig = jnp.minimum(cap, 0.0) - jnp.log1p(jnp.exp(-jnp.abs(cap)))
    lane = lax.broadcasted_iota(jnp.int32, z.shape, 1)
    return jnp.where(lane < ML_HEADS, cap, lsig)


_DONE = object()


def _rounds(chains):
    live = [(c[0], c[1], c[2] if len(c) > 2 else 1) for c in chains]
    rnd = 0
    while live:
        for item in list(live):
            gen, start, stride = item
            if rnd >= start and (rnd - start) % stride == 0 and next(gen, _DONE) is _DONE:
                live.remove(item)
        rnd += 1
        yield


def _run(chains):
    for _ in _rounds(chains):
        pass


def _head_out(hh, po, hg):
    hn = hh * lax.rsqrt(jnp.mean(hh * hh, axis=-1, keepdims=True) + EPS) * hg
    return jax.nn.sigmoid(po) * hn


def _head_slices(p, h):
    q = p[:, h * ML_DK:(h + 1) * ML_DK]
    k = p[:, ML_QK + h * ML_DK:ML_QK + (h + 1) * ML_DK] * (ML_DK ** -0.5)
    v = p[:, 2 * ML_QK + h * ML_DV:2 * ML_QK + (h + 1) * ML_DV]
    po = p[:, 2 * ML_QK + ML_VO + h * ML_DV:2 * ML_QK + ML_VO + (h + 1) * ML_DV]
    return q, k, v, po


def _prompt_mixer_kernel(xc_ref, xnext_ref, g_ref, wqkvo_ref, wg_ref, bg_ref, hg_ref, wout_ref, *rest,
                         chunks_per_seq, n_cast):
    cast_in = rest[:n_cast]
    y_ref, C_ref, n_ref, m_ref = rest[n_cast:n_cast + 4]
    cast_out = rest[n_cast + 4:2 * n_cast + 4]
    p_s, gz_s = rest[2 * n_cast + 4:]
    R, T = xc_ref.shape[0], xc_ref.shape[1]
    f = pl.program_id(0)
    slot = f % 2

    def project(x_ref, s, r):
        xn = _rms(x_ref[r], g_ref[...]).astype(BF16)
        yield
        for j in range(ML_QKVO // PROJ_COLS):
            cols = slice(j * PROJ_COLS, (j + 1) * PROJ_COLS)
            p_s[s, r, :, cols] = _dot(xn, wqkvo_ref[:, cols])
            yield
        gz_s[s, r] = _dot(xn, wg_ref[...])
        yield

    @pl.when(f == 0)
    def _():
        _run([(project(xc_ref, 0, r), 0) for r in range(R)])

    @pl.when(f % chunks_per_seq == 0)
    def _():
        C_ref[...] = jnp.zeros_like(C_ref)
        n_ref[...] = jnp.zeros_like(n_ref)
        m_ref[...] = jnp.zeros_like(m_ref)

    def recurrence(r):
        G = _gate_act(gz_s[slot, r] + bg_ref[...])
        row = lax.broadcasted_iota(jnp.int32, (T, T), 0)
        col = lax.broadcasted_iota(jnp.int32, (T, T), 1)
        causal = col <= row
        Bc = _dot_exactish(jnp.where(causal, 1.0, 0.0).astype(BF16), G)
        yield
        Gt = G.T
        Bt = Bc.T
        yield
        parts = [None] * ML_HEADS

        def head(h):
            def cols(base, width):
                return p_s[slot, r, :, base + h * width:base + (h + 1) * width]

            q = cols(0, ML_DK)
            k = cols(ML_QK, ML_DK) * (ML_DK ** -0.5)
            v = cols(2 * ML_QK, ML_DV)
            qc, kc, vc = q.astype(BF16), k.astype(BF16), v.astype(BF16)
            qk = lax.dot_general(qc, kc, (((1,), (1,)), ((), ())), preferred_element_type=F32)
            yield
            li_col = G[:, h:h + 1]
            b_col = Bc[:, ML_HEADS + h:ML_HEADS + h + 1]
            dmat = jnp.where(causal, b_col - Bt[ML_HEADS + h:ML_HEADS + h + 1, :] + Gt[h:h + 1, :], -jnp.inf)
            m_prev = m_ref[r, h:h + 1, 0:1]
            inter = b_col + m_prev
            m_t = jnp.maximum(inter, jnp.max(dmat, axis=1, keepdims=True))
            s = qk * jnp.exp(dmat - m_t)
            a_inter = jnp.exp(inter - m_t)
            den = (jnp.sum(s, axis=1, keepdims=True)
                   + a_inter * jnp.sum(q * n_ref[r, h:h + 1, :], axis=1, keepdims=True))
            yield
            C = C_ref[r, h]
            num = _dot(s.astype(BF16), vc) + a_inter * _dot(qc, C.astype(BF16))
            yield
            hh = num * (1.0 / jnp.maximum(jnp.abs(den), jnp.exp(-m_t)))
            hs = _head_out(hh, cols(2 * ML_QK + ML_VO, ML_DV), hg_ref[:, h * ML_DV:(h + 1) * ML_DV])
            yield
            parts[h] = _dot(hs.astype(BF16), wout_ref[h * ML_DV:(h + 1) * ML_DV, :])
            yield
            m_new = m_t[T - 1:T, :]
            b_last = b_col[T - 1:T, :]
            decay = jnp.exp(b_last + m_prev - m_new)
            kw = k * jnp.exp(b_last - b_col + li_col - m_new)
            C_ref[r, h] = decay * C + lax.dot_general(kw.astype(BF16), vc, (((0,), (0,)), ((), ())),
                                                      preferred_element_type=F32)
            n_ref[r, h:h + 1, :] = decay * n_ref[r, h:h + 1, :] + jnp.sum(kw, axis=0, keepdims=True)
            m_ref[r, h:h + 1, :] = jnp.broadcast_to(m_new, (1, LANES))
            yield

        yield from _rounds([(head(h), 2 * h) for h in range(ML_HEADS)])
        y_ref[r] = xc_ref[r] + ((parts[0] + parts[1]) + (parts[2] + parts[3]))

    def casts():
        for src, dst in zip(cast_in, cast_out):
            dst[...] = src[...].astype(BF16)
            yield

    _run([(project(xnext_ref, 1 - slot, r), 0) for r in range(R)] + [(recurrence(r), r) for r in range(R)]
         + [(casts(), 0)])


def _prompt_mixer(x, g, wqkvo, wg, bg, hg, wout, to_cast):
    B, L, D = x.shape
    T = PROMPT_CHUNK
    assert L % T == 0
    R = PROMPT_ROWS
    assert B % R == 0
    nc = L // T
    steps = (B // R) * nc
    const = lambda f: (0, 0)
    cur = lambda f: (f // nc, f % nc, 0)
    nxt = lambda f: (jnp.minimum(f + 1, steps - 1) // nc, jnp.minimum(f + 1, steps - 1) % nc, 0)
    once = dict(pipeline_mode=pl.Buffered(1))
    vmem = _vmem_limit(3 * _nbytes((R, T, D), F32), _nbytes(wqkvo.shape, BF16) // 2,
                       _nbytes(wout.shape, BF16) // 2, _nbytes((R, ML_HEADS, ML_DK, ML_DV), F32),
                       2 * _nbytes((R, T, ML_QKVO), F32))
    for w in to_cast:
        assert w.shape[0] % (steps * 2 * SUBLANES) == 0
    cast_specs = [pl.BlockSpec((w.shape[0] // steps, w.shape[1]), lambda f: (f, 0)) for w in to_cast]
    outs = pl.pallas_call(
        functools.partial(_prompt_mixer_kernel, chunks_per_seq=nc, n_cast=len(to_cast)),
        grid=(steps,),
        in_specs=[
            pl.BlockSpec((R, T, D), cur),
            pl.BlockSpec((R, T, D), nxt),
            pl.BlockSpec((1, D), const),
            pl.BlockSpec((D, ML_QKVO), const, **once),
            pl.BlockSpec((D, LANES), const, **once),
            pl.BlockSpec((1, LANES), const),
            pl.BlockSpec((1, ML_VO), const),
            pl.BlockSpec((ML_VO, D), const, **once),
        ] + cast_specs,
        out_specs=[
            pl.BlockSpec((R, T, D), cur),
            pl.BlockSpec((R, ML_HEADS, ML_DK, ML_DV), lambda f: (f // nc, 0, 0, 0)),
            pl.BlockSpec((R, ML_HEADS, ML_DK), lambda f: (f // nc, 0, 0)),
            pl.BlockSpec((R, ML_HEADS, LANES), lambda f: (f // nc, 0, 0)),
        ] + cast_specs,
        out_shape=[
            jax.ShapeDtypeStruct((B, L, D), F32),
            jax.ShapeDtypeStruct((B, ML_HEADS, ML_DK, ML_DV), F32),
            jax.ShapeDtypeStruct((B, ML_HEADS, ML_DK), F32),
            jax.ShapeDtypeStruct((B, ML_HEADS, LANES), F32),
        ] + [jax.ShapeDtypeStruct(w.shape, BF16) for w in to_cast],
        scratch_shapes=[pltpu.VMEM((2, R, T, ML_QKVO), F32), pltpu.VMEM((2, R, T, LANES), F32)],
        compiler_params=pltpu.CompilerParams(
            dimension_semantics=("arbitrary",), vmem_limit_bytes=vmem),
        name="prompt_mlstm_mixer",
    )(x, x, g, wqkvo, wg, bg, hg, wout, *to_cast)
    return outs


def _sample_mixer_kernel(x_ref, g_ref, wqkvo_ref, wg_ref, bg_ref, hg_ref, wout_ref,
                         C0_ref, n0_ref, m0_ref,
                         y_ref, C_ref, n_ref, m_ref, hs_s, *, n_valid):
    T = SAMPLE_PAD
    nb = C0_ref.shape[0]
    R = nb * T
    x = x_ref[...]
    xn = _rms(x, g_ref[...]).astype(BF16)
    p = _dot(xn, wqkvo_ref[...])

    rowt = lax.broadcasted_iota(jnp.int32, (R, LANES), 0) % T
    lane = lax.broadcasted_iota(jnp.int32, (R, LANES), 1)
    G = jnp.where(rowt < n_valid, _gate_act(_dot(xn, wg_ref[...]) + bg_ref[...]),
                  jnp.where(lane < ML_HEADS, -jnp.inf, 0.0))
    row = lax.broadcasted_iota(jnp.int32, (R, R), 0)
    col = lax.broadcasted_iota(jnp.int32, (R, R), 1)
    causal = (row // T == col // T) & (col <= row)
    Bc = _dot_exactish(jnp.where(causal, 1.0, 0.0).astype(BF16), jnp.where(lane < ML_HEADS, 0.0, G))
    Gt = G.T
    Bt = Bc.T

    def per_seq(fn):
        return jnp.concatenate([fn(a) for a in range(nb)], axis=0)

    def seq_last(colvec):
        return per_seq(lambda a: jnp.broadcast_to(colvec[T * a + T - 1:T * a + T, :], (T, 1)))

    def head(h):
        q, k, v, po = _head_slices(p, h)
        li_col = G[:, h:h + 1]
        b_col = Bc[:, ML_HEADS + h:ML_HEADS + h + 1]
        li_row = Gt[h:h + 1, :]
        b_row = Bt[ML_HEADS + h:ML_HEADS + h + 1, :]
        dmat = jnp.where(causal, b_col - (b_row - li_row), -jnp.inf)
        m_prev = per_seq(lambda a: jnp.broadcast_to(m0_ref[a, h:h + 1, 0:1], (T, 1)))
        n_rows = per_seq(lambda a: jnp.broadcast_to(n0_ref[a, h:h + 1, :], (T, ML_DK)))
        qc, kc, vc = q.astype(BF16), k.astype(BF16), v.astype(BF16)
        qk = lax.dot_general(qc, kc, (((1,), (1,)), ((), ())), preferred_element_type=F32)
        yield
        inter = b_col + m_prev
        m_t = jnp.maximum(inter, jnp.max(dmat, axis=1, keepdims=True))
        s = qk * jnp.exp(dmat - m_t)
        a_inter = jnp.exp(inter - m_t)
        yield
        qC = per_seq(lambda a: _dot(q[T * a:T * a + T, :], C0_ref[a, h]))
        yield
        num = _dot(s.astype(BF16), vc) + a_inter * qC
        den = jnp.sum(s, axis=1, keepdims=True) + a_inter * jnp.sum(q * n_rows, axis=1, keepdims=True)
        hh = num * (1.0 / jnp.maximum(jnp.abs(den), jnp.exp(-m_t)))
        hs_s[:, h * ML_DV:(h + 1) * ML_DV] = _head_out(
            hh, po, hg_ref[:, h * ML_DV:(h + 1) * ML_DV]).astype(BF16)
        yield
        m_new = seq_last(m_t)
        b_last = seq_last(b_col)
        decay = jnp.exp(b_last + m_prev - m_new)
        kw = k * jnp.exp(b_last - b_col + li_col - m_new)
        for a in range(nb):
            dec = decay[T * a:T * a + 1, :]
            kw_a = kw[T * a:T * a + T, :]
            upd = lax.dot_general(kw_a, v[T * a:T * a + T, :], (((0,), (0,)), ((), ())),
                                  preferred_element_type=F32)
            C_ref[a, h] = dec * C0_ref[a, h] + upd
            n_ref[a, h:h + 1, :] = dec * n0_ref[a, h:h + 1, :] + jnp.sum(kw_a, axis=0, keepdims=True)
            m_ref[a, h:h + 1, :] = jnp.broadcast_to(m_new[T * a:T * a + 1, :], (1, LANES))
            if a % 4 == 3:
                yield

    _run([(head(h), h) for h in range(ML_HEADS)])
    y_ref[...] = x + _dot(hs_s[...], wout_ref[...])


def _sample_mixer(x, g, wqkvo, wg, bg, hg, wout, C0, n0, m0, n_valid):
    NBT, D = x.shape
    nseq = C0.shape[0]
    nb = SAMPLE_NB
    R = nb * SAMPLE_PAD
    assert nseq % nb == 0 and NBT == nseq * SAMPLE_PAD
    const = lambda i: (0, 0)
    state_specs = [
        pl.BlockSpec((nb, ML_HEADS, ML_DK, ML_DV), lambda i: (i, 0, 0, 0)),
        pl.BlockSpec((nb, ML_HEADS, ML_DK), lambda i: (i, 0, 0)),
        pl.BlockSpec((nb, ML_HEADS, LANES), lambda i: (i, 0, 0)),
    ]
    vmem = _vmem_limit(2 * _nbytes((R, D), F32), _nbytes(wqkvo.shape, BF16), _nbytes(wout.shape, BF16),
                       2 * _nbytes((nb, ML_HEADS, ML_DK, ML_DV), F32), _nbytes((R, ML_QKVO), F32))
    return pl.pallas_call(
        functools.partial(_sample_mixer_kernel, n_valid=n_valid),
        grid=(nseq // nb,),
        in_specs=[
            pl.BlockSpec((R, D), lambda i: (i, 0)),
            pl.BlockSpec((1, D), const),
            pl.BlockSpec((D, ML_QKVO), const),
            pl.BlockSpec((D, LANES), const),
            pl.BlockSpec((1, LANES), const),
            pl.BlockSpec((1, ML_VO), const),
            pl.BlockSpec((ML_VO, D), const),
        ] + state_specs,
        out_specs=[pl.BlockSpec((R, D), lambda i: (i, 0))] + state_specs,
        out_shape=[
            jax.ShapeDtypeStruct((NBT, D), F32),
            jax.ShapeDtypeStruct(C0.shape, F32),
            jax.ShapeDtypeStruct(n0.shape, F32),
            jax.ShapeDtypeStruct(m0.shape, F32),
        ],
        scratch_shapes=[pltpu.VMEM((R, ML_VO), BF16)],
        compiler_params=pltpu.CompilerParams(
            dimension_semantics=("arbitrary",), vmem_limit_bytes=vmem),
        name="sample_mlstm_mixer",
    )(x, g, wqkvo, wg, bg, hg, wout, C0, n0, m0)


def _dot_exactish_right(x, m01):
    hi, mid, lo = _split3(x)
    return _dot(hi, m01) + _dot(mid, m01) + _dot(lo, m01)


def _mlp_kernel(x_ref, g_ref, w1_ref, w2_ref, *rest):
    x = x_ref[...]
    xn = _rms(x, g_ref[...]).astype(BF16)
    acc = x
    for c in range(D_FF // FF_TILE):
        hcol = _dot(xn, w1_ref[:, c * FF_TILE:(c + 1) * FF_TILE])
        hcol = jnp.square(jnp.maximum(hcol, 0.0)).astype(BF16)
        acc = acc + _dot(hcol, w2_ref[c * FF_TILE:(c + 1) * FF_TILE, :])
    if len(rest) == 1:
        (y_ref,) = rest
        y_ref[...] = acc
        return
    gkv_ref, wkv_ref, kg_ref, y_ref, k_ref, v_ref = rest
    y_ref[...] = acc
    kv = _dot(_rms(acc, gkv_ref[...]).astype(BF16), wkv_ref[...])
    kraw = kv[:, :ATT_KV]
    v_ref[...] = kv[:, ATT_KV:]
    r = lax.broadcasted_iota(jnp.int32, (ATT_KV, ATT_KV), 0) // ATT_HD
    c = lax.broadcasted_iota(jnp.int32, (ATT_KV, ATT_KV), 1) // ATT_HD
    seg = jnp.where(r == c, 1.0, 0.0).astype(BF16)
    ss = _dot_exactish_right(kraw * kraw, seg)
    k_ref[...] = kraw * lax.rsqrt(ss * (1.0 / ATT_HD) + EPS) * kg_ref[...]


def _mlp(x, g, w1, w2, layer, kv=None):
    N, D = x.shape
    tm = min(ROW_TILE, N)
    assert N % tm == 0
    const = lambda i: (0, 0)
    rows = lambda i: (i, 0)
    once = dict(pipeline_mode=pl.Buffered(1))
    in_specs = [
        pl.BlockSpec((tm, D), rows),
        pl.BlockSpec((1, D), const),
        pl.BlockSpec((None, D, D_FF), lambda i: (layer, 0, 0), **once),
        pl.BlockSpec((None, D_FF, D), lambda i: (layer, 0, 0), **once),
    ]
    out_specs = [pl.BlockSpec((tm, D), rows)]
    out_shape = [jax.ShapeDtypeStruct((N, D), F32)]
    args = [x, g, w1, w2]
    if kv is not None:
        in_specs += [pl.BlockSpec((1, D), const), pl.BlockSpec((D, 2 * ATT_KV), const, **once),
                     pl.BlockSpec((1, ATT_KV), const)]
        out_specs += [pl.BlockSpec((tm, ATT_KV), rows)] * 2
        out_shape += [jax.ShapeDtypeStruct((N, ATT_KV), F32)] * 2
        args += list(kv)
    vmem = _vmem_limit(2 * _nbytes((tm, D), F32), _nbytes(w1.shape[1:], BF16) // 2,
                       _nbytes(w2.shape[1:], BF16) // 2, 2 * _nbytes((tm, FF_TILE), F32))
    out = pl.pallas_call(
        _mlp_kernel,
        grid=(N // tm,),
        in_specs=in_specs,
        out_specs=out_specs,
        out_shape=out_shape,
        compiler_params=pltpu.CompilerParams(
            dimension_semantics=("arbitrary",), vmem_limit_bytes=vmem),
        name="sqrelu_mlp",
    )(*args)
    return out[0] if kv is None else out


def _pair_queries(qraw, pr, qscale, col0=0):
    TQ = qraw.shape[0]
    lo = lax.broadcasted_iota(jnp.int32, (TQ, LANES), 1) < ATT_HD
    out = []
    for e in range(2):
        kvh = 2 * pr + e
        for g in range(ATT_GROUP):
            cc, half = divmod(g, 2)
            c0 = (2 * kvh + cc) * LANES - col0
            q2 = qraw[:, c0:c0 + LANES]
            qm = jnp.where(lo, q2, 0.0) if half == 0 else jnp.where(lo, 0.0, q2)
            ss = jnp.sum(qm * qm, axis=1, keepdims=True)
            qn = qm * lax.rsqrt(ss * (1.0 / ATT_HD) + EPS)
            if qscale is not None:
                qn = qn * qscale
            out.append(qn if half == e else pltpu.roll(qn, ATT_HD, 1))
    return out


def _pair_outputs(o_heads, rden, pr, store):
    TQ = o_heads[0].shape[0]
    lo = lax.broadcasted_iota(jnp.int32, (TQ, LANES), 1) < ATT_HD
    for e in range(2):
        kvh = 2 * pr + e
        for cc in range(2):
            tiles = []
            for half in range(2):
                o = o_heads[e * ATT_GROUP + 2 * cc + half] * rden[e * ATT_GROUP + 2 * cc + half]
                tiles.append(o if half == e else pltpu.roll(o, ATT_HD, 1))
            store((2 * kvh + cc) * LANES, jnp.where(lo, tiles[0], tiles[1]))


def _softmax_with_sink(parts, sink):
    assert all(s.shape == parts[0].shape for s in parts)
    M = jnp.maximum(sink, jnp.max(functools.reduce(jnp.maximum, parts), axis=1, keepdims=True))
    ps = [jnp.exp2(s - M) for s in parts]
    den = jnp.exp2(sink - M) + jnp.sum(functools.reduce(jnp.add, ps), axis=1, keepdims=True)
    return ps, 1.0 / den


def _prompt_attn_kernel(sinks_ref, x_ref, g_ref, wq_ref, qg_ref, kp_ref, kc_ref, vp_ref, vc_ref, wo_ref,
                        y_ref):
    TQ = WINDOW
    TR = x_ref.shape[0]
    kj = lax.broadcasted_iota(jnp.int32, (WINDOW + TQ, TQ), 0)
    qi = lax.broadcasted_iota(jnp.int32, (WINDOW + TQ, TQ), 1)
    band = (qi + WINDOW - kj >= 0) & (qi - kj <= 0)
    band_first = band & ((kj >= WINDOW) | (pl.program_id(1) > 0))
    kscale = qg_ref[...] * (ATT_HD ** -0.5 * LOG2E)
    slabs = []
    for pr in range(ATT_KVH // 2):
        sl = slice(pr * LANES, (pr + 1) * LANES)
        slabs.append(((jnp.concatenate([kp_ref[:, sl], kc_ref[:, sl]], axis=0) * kscale).astype(BF16),
                      jnp.concatenate([vp_ref[:, sl], vc_ref[:, sl]], axis=0).T))

    PW = 2 * ATT_GROUP * ATT_HD
    zeros_hd = jnp.zeros((ATT_HD, TQ), F32)
    ones_rows = jnp.ones((2 * SUBLANES, WINDOW + TQ), F32)

    def chain(cb):
        nsb = ATTN_CHAIN_BLOCKS
        rows = slice(cb * nsb * TQ, (cb + 1) * nsb * TQ)
        x = x_ref[rows, :]
        xn = _rms(x, g_ref[...]).astype(BF16)
        acc = x
        for pr in range(ATT_KVH // 2):
            kslab, vslab_t = slabs[pr]
            qt_all = _dot(xn, wq_ref[:, pr * PW:(pr + 1) * PW]).T
            yield
            outs = []
            for i in range(nsb):
                sb = cb * nsb + i
                keys = slice(sb * TQ, sb * TQ + WINDOW + TQ)
                mask = band_first if sb == 0 else band
                qt = qt_all[:, i * TQ:(i + 1) * TQ]
                tiles, sink_rows = [], []
                for e in range(2):
                    for g in range(ATT_GROUP):
                        blk = qt[(e * ATT_GROUP + g) * ATT_HD:(e * ATT_GROUP + g + 1) * ATT_HD, :]
                        qn = blk * lax.rsqrt(jnp.sum(blk * blk, axis=0, keepdims=True) * (1.0 / ATT_HD) + EPS)
                        tiles.append(jnp.concatenate([qn, zeros_hd] if e == 0 else [zeros_hd, qn], axis=0))
                        sink_rows.append(jnp.full((1, TQ), sinks_ref[(2 * pr + e) * ATT_GROUP + g] * LOG2E, F32))
                qmat = jnp.concatenate(tiles, axis=1).astype(BF16)
                st = _dot(kslab[keys, :], qmat)
                yield
                parts = []
                for e in range(2):
                    ps, ms = [], []
                    for g in range(ATT_GROUP):
                        c0 = (e * ATT_GROUP + g) * TQ
                        s_h = jnp.where(mask, st[:, c0:c0 + TQ], -jnp.inf)
                        m_h = jnp.maximum(jnp.max(s_h, axis=0, keepdims=True), sink_rows[e * ATT_GROUP + g])
                        ps.append(jnp.exp2(s_h - m_h).astype(BF16))
                        ms.append(m_h)
                    p_e = jnp.concatenate(ps, axis=1)
                    v_aug = jnp.concatenate([vslab_t[e * ATT_HD:(e + 1) * ATT_HD, keys], ones_rows],
                                            axis=0).astype(BF16)
                    ot = _dot(v_aug, p_e)
                    yield
                    sink_term = jnp.exp2(jnp.concatenate(sink_rows[e * ATT_GROUP:(e + 1) * ATT_GROUP], axis=1)
                                         - jnp.concatenate(ms, axis=1))
                    on = ot[0:ATT_HD, :] * (1.0 / (ot[ATT_HD:ATT_HD + 1, :] + sink_term))
                    parts += [on[:, g * TQ:(g + 1) * TQ] for g in range(ATT_GROUP)]
                outs.append(jnp.concatenate(parts, axis=0))
            o_pair = jnp.concatenate(outs, axis=1).T.astype(BF16)
            acc = acc + _dot(o_pair, wo_ref[pr * PW:(pr + 1) * PW, :])
            yield
        y_ref[rows, :] = acc

    _run([(chain(cb), 0) for cb in range(TR // (ATTN_CHAIN_BLOCKS * TQ))])


def _prompt_attn(x, k, v, sinks, g, wq, qg, wo):
    B, L, D = x.shape
    TR = ATTN_ROWS
    per = TR // WINDOW
    assert L % TR == 0 and TR % WINDOW == 0
    const = lambda b, i: (0, 0)
    cur = lambda b, i: (b, i, 0)
    prev = lambda b, i: (b, jnp.maximum(i * per - 1, 0), 0)
    vmem = _vmem_limit(2 * _nbytes((TR, D), F32), _nbytes(wq.shape, BF16), _nbytes(wo.shape, BF16),
                       4 * _nbytes((TR, ATT_KV), F32), 8 * _nbytes((8 * WINDOW, 2 * WINDOW), F32))
    return pl.pallas_call(
        _prompt_attn_kernel,
        grid=(B, L // TR),
        in_specs=[
            pl.BlockSpec(memory_space=pltpu.SMEM),
            pl.BlockSpec((None, TR, D), cur),
            pl.BlockSpec((1, D), const),
            pl.BlockSpec((D, D), const),
            pl.BlockSpec((1, LANES), const),
            pl.BlockSpec((None, WINDOW, ATT_KV), prev),
            pl.BlockSpec((None, TR, ATT_KV), cur),
            pl.BlockSpec((None, WINDOW, ATT_KV), prev),
            pl.BlockSpec((None, TR, ATT_KV), cur),
            pl.BlockSpec((D, D), const),
        ],
        out_specs=pl.BlockSpec((None, TR, D), cur),
        out_shape=jax.ShapeDtypeStruct((B, L, D), F32),
        compiler_params=pltpu.CompilerParams(
            dimension_semantics=("arbitrary", "arbitrary"), vmem_limit_bytes=vmem),
        name="prompt_window_attention",
    )(sinks, x, g, wq, qg, k, k, v, v, wo)


def _sample_attn_kernel(sinks_ref, x_ref, g_ref, wq_ref, qg_ref, kc_ref, vc_ref, wk_ref, wv_ref, wo_ref,
                        y_ref, nk_ref, nv_ref, o_s, *, n_valid):
    T = SAMPLE_PAD
    nb = wk_ref.shape[0]
    R = nb * T
    assert R == WINDOW
    x = x_ref[...]
    q = _dot(_rms(x, g_ref[...]).astype(BF16), wq_ref[...])
    qscale = qg_ref[...] * (ATT_HD ** -0.5 * LOG2E)
    knew = kc_ref[...]
    vnew = vc_ref[...]

    cmask = (lax.broadcasted_iota(jnp.int32, (R, WINDOW), 1)
             >= lax.broadcasted_iota(jnp.int32, (R, WINDOW), 0) % T)
    row = lax.broadcasted_iota(jnp.int32, (R, R), 0)
    col = lax.broadcasted_iota(jnp.int32, (R, R), 1)
    nmask = (row // T == col // T) & (col % T <= row % T) & (col % T < n_valid)

    def store(off, val):
        o_s[:, off:off + LANES] = val.astype(BF16)

    def regroup(per_seq, i):
        return jnp.concatenate([per_seq[a][i * T:(i + 1) * T, :] for a in range(nb)], axis=0)

    def pair(pr):
        sl = slice(pr * LANES, (pr + 1) * LANES)
        qs = _pair_queries(q, pr, qscale)
        S_new = lax.dot_general(jnp.concatenate(qs, axis=0).astype(BF16), knew[:, sl].astype(BF16),
                                (((1,), (1,)), ((), ())), preferred_element_type=F32)
        yield
        sc = []
        for a in range(nb):
            q_a = jnp.concatenate([qi[T * a:T * a + T, :] for qi in qs], axis=0).astype(BF16)
            sc.append(_dot(q_a, wk_ref[a, sl, :].astype(BF16)))
            if a % 4 == 3:
                yield
        pcs, pns, rden = [], [], []
        for i in range(2 * ATT_GROUP):
            s_c = jnp.where(cmask, regroup(sc, i), -jnp.inf)
            s_n = jnp.where(nmask, S_new[i * R:(i + 1) * R, :], -jnp.inf)
            (p_c, p_n), r = _softmax_with_sink([s_c, s_n], sinks_ref[pr * 2 * ATT_GROUP + i] * LOG2E)
            pcs.append(p_c)
            pns.append(p_n)
            rden.append(r)
            if i % 2 == 1:
                yield
        O_new = _dot(jnp.concatenate(pns, axis=0).astype(BF16), vnew[:, sl].astype(BF16))
        yield
        oc = []
        for a in range(nb):
            p_a = jnp.concatenate([pc[T * a:T * a + T, :] for pc in pcs], axis=0).astype(BF16)
            oc.append(lax.dot_general(p_a, wv_ref[a, sl, :].astype(BF16), (((1,), (1,)), ((), ())),
                                      preferred_element_type=F32))
            if a % 4 == 3:
                yield
        _pair_outputs([O_new[i * R:(i + 1) * R, :] + regroup(oc, i) for i in range(2 * ATT_GROUP)],
                      rden, pr, store)
        yield

    def roll_cache():
        keep = lax.broadcasted_iota(jnp.int32, (ATT_KV, WINDOW), 1) < WINDOW - n_valid
        k_parts = _split3(knew.T)
        v_parts = _split3(vnew.T)
        group = 4
        rr = lax.broadcasted_iota(jnp.int32, (R, group * WINDOW), 0)
        cc = lax.broadcasted_iota(jnp.int32, (R, group * WINDOW), 1)
        hit = (cc % WINDOW == WINDOW - n_valid + rr % T) & (rr % T < n_valid)
        seq_off = rr // T - cc // WINDOW
        yield
        for g0 in range(0, nb, group):
            place = jnp.where(hit & (seq_off == g0), 1.0, 0.0).astype(BF16)
            new_k = _dot(k_parts[0], place) + _dot(k_parts[1], place) + _dot(k_parts[2], place)
            new_v = _dot(v_parts[0], place) + _dot(v_parts[1], place) + _dot(v_parts[2], place)
            for j in range(group):
                lanes = slice(j * WINDOW, (j + 1) * WINDOW)
                nk_ref[g0 + j] = jnp.where(keep, pltpu.roll(wk_ref[g0 + j], WINDOW - n_valid, 1), new_k[:, lanes])
                nv_ref[g0 + j] = jnp.where(keep, pltpu.roll(wv_ref[g0 + j], WINDOW - n_valid, 1), new_v[:, lanes])
            yield

    _run([(pair(0), 0), (pair(1), 2), (roll_cache(), 0)])
    y_ref[...] = x + _dot(o_s[...], wo_ref[...])


def _sample_attn(x, k, v, win_k, win_v, sinks, g, wq, qg, wo, n_valid):
    NBT, D = x.shape
    nseq = win_k.shape[0]
    nb = SAMPLE_NB
    R = nb * SAMPLE_PAD
    assert nseq % nb == 0 and NBT == nseq * SAMPLE_PAD
    const = lambda i: (0, 0)
    rows = lambda i: (i, 0)
    cache = pl.BlockSpec((nb, ATT_KV, WINDOW), lambda i: (i, 0, 0))
    vmem = _vmem_limit(2 * _nbytes((R, D), F32), _nbytes(wq.shape, BF16), _nbytes(wo.shape, BF16),
                       4 * _nbytes((nb, ATT_KV, WINDOW), F32), 2 * _nbytes((R, D), F32))
    return pl.pallas_call(
        functools.partial(_sample_attn_kernel, n_valid=n_valid),
        grid=(nseq // nb,),
        in_specs=[
            pl.BlockSpec(memory_space=pltpu.SMEM),
            pl.BlockSpec((R, D), rows),
            pl.BlockSpec((1, D), const),
            pl.BlockSpec((D, D), const),
            pl.BlockSpec((1, LANES), const),
            pl.BlockSpec((R, ATT_KV), rows),
            pl.BlockSpec((R, ATT_KV), rows),
            cache, cache,
            pl.BlockSpec((D, D), const),
        ],
        out_specs=[pl.BlockSpec((R, D), rows), cache, cache],
        out_shape=[jax.ShapeDtypeStruct((NBT, D), F32),
                   jax.ShapeDtypeStruct(win_k.shape, F32), jax.ShapeDtypeStruct(win_v.shape, F32)],
        scratch_shapes=[pltpu.VMEM((R, D), BF16)],
        compiler_params=pltpu.CompilerParams(
            dimension_semantics=("arbitrary",), vmem_limit_bytes=vmem),
        name="sample_window_attention",
    )(sinks, x, g, wq, qg, k, v, win_k, win_v, wo)


def kernel(x_prompt, x_sample, state_mlstm_C, state_mlstm_n, state_mlstm_m, cache_win_k, cache_win_v,
           ml_norm_g, ml_w_in, ml_b_i, ml_b_f, ml_head_g, ml_w_out, kv_norm_g, w_kv, k_norm_g,
           att_norm_g, att_w_q, q_norm_g, att_sinks, att_w_o, mlp_norm_g, mlp_w1, mlp_w2):
    B, L, D = x_prompt.shape
    NS, LS, _ = x_sample.shape
    assert ml_w_in.shape[0] == 1 and att_w_q.shape[0] == 1 and mlp_w1.shape[0] == 2

    w_in = ml_w_in[0]
    wqkvo = w_in.astype(BF16)
    wg = jnp.pad(w_in[:, ML_QKVO:], ((0, 0), (0, LANES - 2 * ML_HEADS))).astype(BF16)
    bg = jnp.pad(jnp.concatenate([ml_b_i[0], ml_b_f[0]]), (0, LANES - 2 * ML_HEADS)).reshape(1, LANES)
    ml_g = ml_norm_g[0].reshape(1, D)
    hg = ml_head_g[0].reshape(1, ML_VO)
    wout = ml_w_out[0].astype(BF16)
    mlp_g = mlp_norm_g.reshape(2, 1, D)
    kv_g = kv_norm_g.reshape(1, D)
    kg = jnp.tile(k_norm_g, ATT_KVH).reshape(1, ATT_KV)
    att_g = att_norm_g[0].reshape(1, D)
    qg = jnp.tile(q_norm_g[0], 2).reshape(1, LANES)
    sinks = att_sinks[0]

    xp, p_C, p_n, p_m, w1, w2, wkv, wq, wo = _prompt_mixer(
        x_prompt, ml_g, wqkvo, wg, bg, hg, wout,
        [mlp_w1.reshape(2 * D, D_FF), mlp_w2.reshape(2 * D_FF, D), w_kv, att_w_q[0], att_w_o[0]])
    w1 = w1.reshape(2, D, D_FF)
    w2 = w2.reshape(2, D_FF, D)
    xp, kp, vp = _mlp(xp.reshape(B * L, D), mlp_g[0], w1, w2, 0, kv=(kv_g, wkv, kg))
    xp = _prompt_attn(xp.reshape(B, L, D), kp.reshape(B, L, ATT_KV), vp.reshape(B, L, ATT_KV),
                      sinks, att_g, wq, qg, wo)
    y_prompt = _mlp(xp.reshape(B * L, D), mlp_g[1], w1, w2, 1).reshape(B, L, D)
    p_wk = kp.reshape(B, L, ATT_KV)[:, L - WINDOW:].reshape(B, WINDOW, ATT_KVH, ATT_HD)
    p_wv = vp.reshape(B, L, ATT_KV)[:, L - WINDOW:].reshape(B, WINDOW, ATT_KVH, ATT_HD)

    def pad8(a):
        return jnp.pad(a.reshape(NS, LS, -1), ((0, 0), (0, SAMPLE_PAD - LS), (0, 0))).reshape(NS * SAMPLE_PAD, -1)

    def unpad8(a):
        return a.reshape(NS, SAMPLE_PAD, -1)[:, :LS].reshape(NS * LS, -1)

    m0 = jnp.broadcast_to(state_mlstm_m[0][:, :, None], (NS, ML_HEADS, LANES))
    xs, s_C, s_n, s_m = _sample_mixer(pad8(x_sample), ml_g, wqkvo, wg, bg, hg, wout,
                                      state_mlstm_C[0], state_mlstm_n[0], m0, LS)
    xs, ks, vs = _mlp(unpad8(xs), mlp_g[0], w1, w2, 0, kv=(kv_g, wkv, kg))
    to_t = lambda c: c.transpose(0, 2, 3, 1).reshape(NS, ATT_KV, WINDOW)
    from_t = lambda c: c.reshape(NS, ATT_KVH, ATT_HD, WINDOW).transpose(0, 3, 1, 2)
    xs, s_wk, s_wv = _sample_attn(pad8(xs), pad8(ks), pad8(vs), to_t(cache_win_k), to_t(cache_win_v),
                                  sinks, att_g, wq, qg, wo, LS)
    y_sample = _mlp(unpad8(xs), mlp_g[1], w1, w2, 1).reshape(NS, LS, D)

    return (y_prompt, y_sample,
            p_C[None], p_n[None], p_m[None, :, :, 0], p_wk, p_wv,
            s_C[None], s_n[None], s_m[None, :, :, 0],
            from_t(s_wk), from_t(s_wv))
```

```python
import functools

import jax
import jax.numpy as jnp
from jax import lax
from jax.experimental import pallas as pl
from jax.experimental.pallas import tpu as pltpu

F32 = jnp.float32
BF16 = jnp.bfloat16

D_MODEL = 1024
ML_HEADS = 4
ML_DK = 128
ML_DV = 256
ML_QK = ML_HEADS * ML_DK
ML_VO = ML_HEADS * ML_DV
ML_QKVO = 2 * ML_QK + 2 * ML_VO
GATE_SOFTCAP = 15.0
ATT_HD = 64
ATT_QH = 16
ATT_KVH = 4
ATT_GROUP = 4
ATT_KV = ATT_KVH * ATT_HD
WINDOW = 128
D_FF = 4 * D_MODEL
EPS = 1e-6
LOG2E = 1.4426950408889634

LANES = 128
SUBLANES = 8
VMEM_LIMIT_CAP = 56 * 1024 * 1024

PROMPT_CHUNK = 256
PROMPT_ROWS = 2
PROJ_COLS = 512
SAMPLE_PAD = SUBLANES
SAMPLE_NB = 16
ATTN_ROWS = 1024
ATTN_CHAIN_BLOCKS = 2
ROW_TILE = 1024
FF_TILE = 1024


def _vmem_limit(*block_bytes):
    need = 4 * sum(block_bytes) + (8 << 20)
    return int(min(max(need, 32 << 20), VMEM_LIMIT_CAP))


def _nbytes(shape, dtype):
    n = 1
    for s in shape:
        n *= s
    return n * jnp.dtype(dtype).itemsize


def _rms(x, g):
    return x * lax.rsqrt(jnp.mean(x * x, axis=-1, keepdims=True) + EPS) * g


def _dot(a, b):
    return jnp.dot(a, b, preferred_element_type=F32)


def _split3(x):
    hi = x.astype(BF16)
    r1 = x - hi.astype(F32)
    mid = r1.astype(BF16)
    lo = (r1 - mid.astype(F32)).astype(BF16)
    return hi, mid, lo


def _dot_exactish(m01, x):
    hi, mid, lo = _split3(x)
    return _dot(m01, hi) + _dot(m01, mid) + _dot(m01, lo)


def _gate_act(z):
    cap = GATE_SOFTCAP * jnp.tanh(z * (1.0 / GATE_SOFTCAP))
    lsig = jnp.minimum(cap, 0.0) - jnp.log1p(jnp.exp(-jnp.abs(cap)))
    lane = lax.broadcasted_iota(jnp.int32, z.shape, 1)
    return jnp.where(lane < ML_HEADS, cap, lsig)


_DONE = object()


def _rounds(chains):
    live = [(c[0], c[1], c[2] if len(c) > 2 else 1) for c in chains]
    rnd = 0
    while live:
        for item in list(live):
            gen, start, stride = item
            if rnd >= start and (rnd - start) % stride == 0 and next(gen, _DONE) is _DONE:
                live.remove(item)
        rnd += 1
        yield


def _run(chains):
    for _ in _rounds(chains):
        pass


def _head_out(hh, po, hg):
    hn = hh * lax.rsqrt(jnp.mean(hh * hh, axis=-1, keepdims=True) + EPS) * hg
    return jax.nn.sigmoid(po) * hn


def _prompt_mixer_kernel(xc_ref, xnext_ref, g_ref, wqkvo_ref, wg_ref, bg_ref, hg_ref, wout_ref, *rest,
                         chunks_per_seq, n_cast):
    cast_in = rest[:n_cast]
    y_ref, C_ref, n_ref, m_ref = rest[n_cast:n_cast + 4]
    cast_out = rest[n_cast + 4:2 * n_cast + 4]
    p_s, gz_s = rest[2 * n_cast + 4:]
    R, T = xc_ref.shape[0], xc_ref.shape[1]
    f = pl.program_id(0)
    slot = f % 2

    def project(x_ref, s, r):
        xn = _rms(x_ref[r], g_ref[...]).astype(BF16)
        yield
        for j in range(ML_QKVO // PROJ_COLS):
            cols = slice(j * PROJ_COLS, (j + 1) * PROJ_COLS)
            p_s[s, r, :, cols] = _dot(xn, wqkvo_ref[:, cols])
            yield
        gz_s[s, r] = _dot(xn, wg_ref[...])
        yield

    @pl.when(f == 0)
    def _():
        _run([(project(xc_ref, 0, r), 0) for r in range(R)])

    @pl.when(f % chunks_per_seq == 0)
    def _():
        C_ref[...] = jnp.zeros_like(C_ref)
        n_ref[...] = jnp.zeros_like(n_ref)
        m_ref[...] = jnp.zeros_like(m_ref)

    def recurrence(r):
        G = _gate_act(gz_s[slot, r] + bg_ref[...])
        row = lax.broadcasted_iota(jnp.int32, (T, T), 0)
        col = lax.broadcasted_iota(jnp.int32, (T, T), 1)
        causal = col <= row
        Bc = _dot_exactish(jnp.where(causal, 1.0, 0.0).astype(BF16), G)
        yield
        Gt = G.T
        Bt = Bc.T
        yield
        parts = [None] * ML_HEADS

        def head(h):
            def cols(base, width):
                return p_s[slot, r, :, base + h * width:base + (h + 1) * width]

            q = cols(0, ML_DK)
            k = cols(ML_QK, ML_DK) * (ML_DK ** -0.5)
            v = cols(2 * ML_QK, ML_DV)
            qc, kc, vc = q.astype(BF16), k.astype(BF16), v.astype(BF16)
            qk = lax.dot_general(qc, kc, (((1,), (1,)), ((), ())), preferred_element_type=F32)
            yield
            li_col = G[:, h:h + 1]
            b_col = Bc[:, ML_HEADS + h:ML_HEADS + h + 1]
            dmat = jnp.where(causal, b_col - Bt[ML_HEADS + h:ML_HEADS + h + 1, :] + Gt[h:h + 1, :], -jnp.inf)
            m_prev = m_ref[r, h:h + 1, 0:1]
            inter = b_col + m_prev
            m_t = jnp.maximum(inter, jnp.max(dmat, axis=1, keepdims=True))
            s = qk * jnp.exp(dmat - m_t)
            a_inter = jnp.exp(inter - m_t)
            den = (jnp.sum(s, axis=1, keepdims=True)
                   + a_inter * jnp.sum(q * n_ref[r, h:h + 1, :], axis=1, keepdims=True))
            yield
            C = C_ref[r, h]
            num = _dot(s.astype(BF16), vc) + a_inter * _dot(qc, C.astype(BF16))
            yield
            hh = num * (1.0 / jnp.maximum(jnp.abs(den), jnp.exp(-m_t)))
            hs = _head_out(hh, cols(2 * ML_QK + ML_VO, ML_DV), hg_ref[:, h * ML_DV:(h + 1) * ML_DV])
            yield
            parts[h] = _dot(hs.astype(BF16), wout_ref[h * ML_DV:(h + 1) * ML_DV, :])
            yield
            m_new = m_t[T - 1:T, :]
            b_last = b_col[T - 1:T, :]
            decay = jnp.exp(b_last + m_prev - m_new)
            kw = k * jnp.exp(b_last - b_col + li_col - m_new)
            C_ref[r, h] = decay * C + lax.dot_general(kw.astype(BF16), vc, (((0,), (0,)), ((), ())),
                                                      preferred_element_type=F32)
            n_ref[r, h:h + 1, :] = decay * n_ref[r, h:h + 1, :] + jnp.sum(kw, axis=0, keepdims=True)
            m_ref[r, h:h + 1, :] = jnp.broadcast_to(m_new, (1, LANES))
            yield

        yield from _rounds([(head(h), 2 * h) for h in range(ML_HEADS)])
        y_ref[r] = xc_ref[r] + ((parts[0] + parts[1]) + (parts[2] + parts[3]))

    def casts():
        for src, dst in zip(cast_in, cast_out):
            dst[...] = src[...].astype(BF16)
            yield

    _run([(project(xnext_ref, 1 - slot, r), 0) for r in range(R)] + [(recurrence(r), r) for r in range(R)]
         + [(casts(), 0)])


def _prompt_mixer(x, g, wqkvo, wg, bg, hg, wout, to_cast):
    B, L, D = x.shape
    T = PROMPT_CHUNK
    assert L % T == 0
    R = PROMPT_ROWS
    assert B % R == 0
    nc = L // T
    steps = (B // R) * nc
    const = lambda f: (0, 0)
    cur = lambda f: (f // nc, f % nc, 0)
    nxt = lambda f: (jnp.minimum(f + 1, steps - 1) // nc, jnp.minimum(f + 1, steps - 1) % nc, 0)
    once = dict(pipeline_mode=pl.Buffered(1))
    vmem = _vmem_limit(3 * _nbytes((R, T, D), F32), _nbytes(wqkvo.shape, BF16) // 2,
                       _nbytes(wout.shape, BF16) // 2, _nbytes((R, ML_HEADS, ML_DK, ML_DV), F32),
                       2 * _nbytes((R, T, ML_QKVO), F32))
    for w in to_cast:
        assert w.shape[0] % (steps * 2 * SUBLANES) == 0
    cast_specs = [pl.BlockSpec((w.shape[0] // steps, w.shape[1]), lambda f: (f, 0)) for w in to_cast]
    outs = pl.pallas_call(
        functools.partial(_prompt_mixer_kernel, chunks_per_seq=nc, n_cast=len(to_cast)),
        grid=(steps,),
        in_specs=[
            pl.BlockSpec((R, T, D), cur),
            pl.BlockSpec((R, T, D), nxt),
            pl.BlockSpec((1, D), const),
            pl.BlockSpec((D, ML_QKVO), const, **once),
            pl.BlockSpec((D, LANES), const, **once),
            pl.BlockSpec((1, LANES), const),
            pl.BlockSpec((1, ML_VO), const),
            pl.BlockSpec((ML_VO, D), const, **once),
        ] + cast_specs,
        out_specs=[
            pl.BlockSpec((R, T, D), cur),
            pl.BlockSpec((R, ML_HEADS, ML_DK, ML_DV), lambda f: (f // nc, 0, 0, 0)),
            pl.BlockSpec((R, ML_HEADS, ML_DK), lambda f: (f // nc, 0, 0)),
            pl.BlockSpec((R, ML_HEADS, LANES), lambda f: (f // nc, 0, 0)),
        ] + cast_specs,
        out_shape=[
            jax.ShapeDtypeStruct((B, L, D), F32),
            jax.ShapeDtypeStruct((B, ML_HEADS, ML_DK, ML_DV), F32),
            jax.ShapeDtypeStruct((B, ML_HEADS, ML_DK), F32),
            jax.ShapeDtypeStruct((B, ML_HEADS, LANES), F32),
        ] + [jax.ShapeDtypeStruct(w.shape, BF16) for w in to_cast],
        scratch_shapes=[pltpu.VMEM((2, R, T, ML_QKVO), F32), pltpu.VMEM((2, R, T, LANES), F32)],
        compiler_params=pltpu.CompilerParams(
            dimension_semantics=("arbitrary",), vmem_limit_bytes=vmem),
        name="prompt_mlstm_mixer",
    )(x, x, g, wqkvo, wg, bg, hg, wout, *to_cast)
    return outs


def _sample_mixer_kernel(x_ref, xnext_ref, g_ref, wqkvo_ref, wg_ref, bg_ref, hg_ref, wout_ref,
                         C0_ref, n0_ref, m0_ref,
                         y_ref, C_ref, n_ref, m_ref, hs_s, p_s, gz_s, *, n_valid):
    T = SAMPLE_PAD
    nb = C0_ref.shape[0]
    R = nb * T
    step = pl.program_id(0)
    slot = step % 2

    def project(xr, s):
        xn = _rms(xr[...], g_ref[...]).astype(BF16)
        yield
        for j in range(ML_QKVO // PROJ_COLS):
            cols = slice(j * PROJ_COLS, (j + 1) * PROJ_COLS)
            p_s[s, :, cols] = _dot(xn, wqkvo_ref[:, cols])
            yield
        gz_s[s] = _dot(xn, wg_ref[...])
        yield

    @pl.when(step == 0)
    def _():
        _run([(project(x_ref, 0), 0)])

    x = x_ref[...]
    rowt = lax.broadcasted_iota(jnp.int32, (R, LANES), 0) % T
    lane = lax.broadcasted_iota(jnp.int32, (R, LANES), 1)
    G = jnp.where(rowt < n_valid, _gate_act(gz_s[slot] + bg_ref[...]),
                  jnp.where(lane < ML_HEADS, -jnp.inf, 0.0))
    row = lax.broadcasted_iota(jnp.int32, (R, R), 0)
    col = lax.broadcasted_iota(jnp.int32, (R, R), 1)
    causal = (row // T == col // T) & (col <= row)
    Bc = _dot_exactish(jnp.where(causal, 1.0, 0.0).astype(BF16), jnp.where(lane < ML_HEADS, 0.0, G))
    Gt = G.T
    Bt = Bc.T

    def per_seq(fn):
        return jnp.concatenate([fn(a) for a in range(nb)], axis=0)

    def seq_last(colvec):
        return per_seq(lambda a: jnp.broadcast_to(colvec[T * a + T - 1:T * a + T, :], (T, 1)))

    def head(h):
        def cols(base, width):
            return p_s[slot, :, base + h * width:base + (h + 1) * width]

        q = cols(0, ML_DK)
        k = cols(ML_QK, ML_DK) * (ML_DK ** -0.5)
        v = cols(2 * ML_QK, ML_DV)
        po = cols(2 * ML_QK + ML_VO, ML_DV)
        li_col = G[:, h:h + 1]
        b_col = Bc[:, ML_HEADS + h:ML_HEADS + h + 1]
        li_row = Gt[h:h + 1, :]
        b_row = Bt[ML_HEADS + h:ML_HEADS + h + 1, :]
        dmat = jnp.where(causal, b_col - (b_row - li_row), -jnp.inf)
        m_prev = per_seq(lambda a: jnp.broadcast_to(m0_ref[a, h:h + 1, 0:1], (T, 1)))
        n_rows = per_seq(lambda a: jnp.broadcast_to(n0_ref[a, h:h + 1, :], (T, ML_DK)))
        qc, kc, vc = q.astype(BF16), k.astype(BF16), v.astype(BF16)
        qk = lax.dot_general(qc, kc, (((1,), (1,)), ((), ())), preferred_element_type=F32)
        yield
        inter = b_col + m_prev
        m_t = jnp.maximum(inter, jnp.max(dmat, axis=1, keepdims=True))
        s = qk * jnp.exp(dmat - m_t)
        a_inter = jnp.exp(inter - m_t)
        yield
        qC = per_seq(lambda a: _dot(q[T * a:T * a + T, :], C0_ref[a, h]))
        yield
        num = _dot(s.astype(BF16), vc) + a_inter * qC
        den = jnp.sum(s, axis=1, keepdims=True) + a_inter * jnp.sum(q * n_rows, axis=1, keepdims=True)
        hh = num * (1.0 / jnp.maximum(jnp.abs(den), jnp.exp(-m_t)))
        hs_s[:, h * ML_DV:(h + 1) * ML_DV] = _head_out(
            hh, po, hg_ref[:, h * ML_DV:(h + 1) * ML_DV]).astype(BF16)
        yield
        m_new = seq_last(m_t)
        b_last = seq_last(b_col)
        decay = jnp.exp(b_last + m_prev - m_new)
        kw = k * jnp.exp(b_last - b_col + li_col - m_new)
        for a in range(nb):
            dec = decay[T * a:T * a + 1, :]
            kw_a = kw[T * a:T * a + T, :]
            upd = lax.dot_general(kw_a, v[T * a:T * a + T, :], (((0,), (0,)), ((), ())),
                                  preferred_element_type=F32)
            C_ref[a, h] = dec * C0_ref[a, h] + upd
            n_ref[a, h:h + 1, :] = dec * n0_ref[a, h:h + 1, :] + jnp.sum(kw_a, axis=0, keepdims=True)
            m_ref[a, h:h + 1, :] = jnp.broadcast_to(m_new[T * a:T * a + 1, :], (1, LANES))
            if a % 4 == 3:
                yield

    _run([(project(xnext_ref, 1 - slot), 0)] + [(head(h), h) for h in range(ML_HEADS)])
    y_ref[...] = x + _dot(hs_s[...], wout_ref[...])


def _sample_mixer(x, g, wqkvo, wg, bg, hg, wout, C0, n0, m0, n_valid):
    NBT, D = x.shape
    nseq = C0.shape[0]
    nb = SAMPLE_NB
    R = nb * SAMPLE_PAD
    assert nseq % nb == 0 and NBT == nseq * SAMPLE_PAD
    steps = nseq // nb
    const = lambda i: (0, 0)
    state_specs = [
        pl.BlockSpec((nb, ML_HEADS, ML_DK, ML_DV), lambda i: (i, 0, 0, 0)),
        pl.BlockSpec((nb, ML_HEADS, ML_DK), lambda i: (i, 0, 0)),
        pl.BlockSpec((nb, ML_HEADS, LANES), lambda i: (i, 0, 0)),
    ]
    vmem = _vmem_limit(2 * _nbytes((R, D), F32), _nbytes(wqkvo.shape, BF16), _nbytes(wout.shape, BF16),
                       2 * _nbytes((nb, ML_HEADS, ML_DK, ML_DV), F32), _nbytes((R, ML_QKVO), F32))
    return pl.pallas_call(
        functools.partial(_sample_mixer_kernel, n_valid=n_valid),
        grid=(steps,),
        in_specs=[
            pl.BlockSpec((R, D), lambda i: (i, 0)),
            pl.BlockSpec((R, D), lambda i: (jnp.minimum(i + 1, steps - 1), 0)),
            pl.BlockSpec((1, D), const),
            pl.BlockSpec((D, ML_QKVO), const, pipeline_mode=pl.Buffered(1)),
            pl.BlockSpec((D, LANES), const, pipeline_mode=pl.Buffered(1)),
            pl.BlockSpec((1, LANES), const),
            pl.BlockSpec((1, ML_VO), const),
            pl.BlockSpec((ML_VO, D), const, pipeline_mode=pl.Buffered(1)),
        ] + state_specs,
        out_specs=[pl.BlockSpec((R, D), lambda i: (i, 0))] + state_specs,
        out_shape=[
            jax.ShapeDtypeStruct((NBT, D), F32),
            jax.ShapeDtypeStruct(C0.shape, F32),
            jax.ShapeDtypeStruct(n0.shape, F32),
            jax.ShapeDtypeStruct(m0.shape, F32),
        ],
        scratch_shapes=[pltpu.VMEM((R, ML_VO), BF16), pltpu.VMEM((2, R, ML_QKVO), F32),
                        pltpu.VMEM((2, R, LANES), F32)],
        compiler_params=pltpu.CompilerParams(
            dimension_semantics=("arbitrary",), vmem_limit_bytes=vmem),
        name="sample_mlstm_mixer",
    )(x, x, g, wqkvo, wg, bg, hg, wout, C0, n0, m0)


def _dot_exactish_right(x, m01):
    hi, mid, lo = _split3(x)
    return _dot(hi, m01) + _dot(mid, m01) + _dot(lo, m01)


def _mlp_kernel(x_ref, g_ref, w1_ref, w2_ref, *rest):
    x = x_ref[...]
    xn = _rms(x, g_ref[...]).astype(BF16)
    acc = x
    for c in range(D_FF // FF_TILE):
        hcol = _dot(xn, w1_ref[:, c * FF_TILE:(c + 1) * FF_TILE])
        hcol = jnp.square(jnp.maximum(hcol, 0.0)).astype(BF16)
        acc = acc + _dot(hcol, w2_ref[c * FF_TILE:(c + 1) * FF_TILE, :])
    if len(rest) == 1:
        (y_ref,) = rest
        y_ref[...] = acc
        return
    gkv_ref, wkv_ref, kg_ref, y_ref, k_ref, v_ref = rest
    y_ref[...] = acc
    kv = _dot(_rms(acc, gkv_ref[...]).astype(BF16), wkv_ref[...])
    kraw = kv[:, :ATT_KV]
    v_ref[...] = kv[:, ATT_KV:]
    r = lax.broadcasted_iota(jnp.int32, (ATT_KV, ATT_KV), 0) // ATT_HD
    c = lax.broadcasted_iota(jnp.int32, (ATT_KV, ATT_KV), 1) // ATT_HD
    seg = jnp.where(r == c, 1.0, 0.0).astype(BF16)
    ss = _dot_exactish_right(kraw * kraw, seg)
    k_ref[...] = kraw * lax.rsqrt(ss * (1.0 / ATT_HD) + EPS) * kg_ref[...]


def _mlp(x, g, w1, w2, layer, kv=None):
    N, D = x.shape
    tm = min(ROW_TILE, N)
    assert N % tm == 0
    const = lambda i: (0, 0)
    rows = lambda i: (i, 0)
    once = dict(pipeline_mode=pl.Buffered(1))
    in_specs = [
        pl.BlockSpec((tm, D), rows),
        pl.BlockSpec((1, D), const),
        pl.BlockSpec((None, D, D_FF), lambda i: (layer, 0, 0), **once),
        pl.BlockSpec((None, D_FF, D), lambda i: (layer, 0, 0), **once),
    ]
    out_specs = [pl.BlockSpec((tm, D), rows)]
    out_shape = [jax.ShapeDtypeStruct((N, D), F32)]
    args = [x, g, w1, w2]
    if kv is not None:
        in_specs += [pl.BlockSpec((1, D), const), pl.BlockSpec((D, 2 * ATT_KV), const, **once),
                     pl.BlockSpec((1, ATT_KV), const)]
        out_specs += [pl.BlockSpec((tm, ATT_KV), rows)] * 2
        out_shape += [jax.ShapeDtypeStruct((N, ATT_KV), F32)] * 2
        args += list(kv)
    vmem = _vmem_limit(2 * _nbytes((tm, D), F32), _nbytes(w1.shape[1:], BF16) // 2,
                       _nbytes(w2.shape[1:], BF16) // 2, 2 * _nbytes((tm, FF_TILE), F32))
    out = pl.pallas_call(
        _mlp_kernel,
        grid=(N // tm,),
        in_specs=in_specs,
        out_specs=out_specs,
        out_shape=out_shape,
        compiler_params=pltpu.CompilerParams(
            dimension_semantics=("arbitrary",), vmem_limit_bytes=vmem),
        name="sqrelu_mlp",
    )(*args)
    return out[0] if kv is None else out


def _pair_queries(qraw, pr, qscale, col0=0):
    TQ = qraw.shape[0]
    lo = lax.broadcasted_iota(jnp.int32, (TQ, LANES), 1) < ATT_HD
    out = []
    for e in range(2):
        kvh = 2 * pr + e
        for g in range(ATT_GROUP):
            cc, half = divmod(g, 2)
            c0 = (2 * kvh + cc) * LANES - col0
            q2 = qraw[:, c0:c0 + LANES]
            qm = jnp.where(lo, q2, 0.0) if half == 0 else jnp.where(lo, 0.0, q2)
            ss = jnp.sum(qm * qm, axis=1, keepdims=True)
            qn = qm * lax.rsqrt(ss * (1.0 / ATT_HD) + EPS)
            if qscale is not None:
                qn = qn * qscale
            out.append(qn if half == e else pltpu.roll(qn, ATT_HD, 1))
    return out


def _pair_outputs(o_heads, rden, pr, store):
    TQ = o_heads[0].shape[0]
    lo = lax.broadcasted_iota(jnp.int32, (TQ, LANES), 1) < ATT_HD
    for e in range(2):
        kvh = 2 * pr + e
        for cc in range(2):
            tiles = []
            for half in range(2):
                o = o_heads[e * ATT_GROUP + 2 * cc + half] * rden[e * ATT_GROUP + 2 * cc + half]
                tiles.append(o if half == e else pltpu.roll(o, ATT_HD, 1))
            store((2 * kvh + cc) * LANES, jnp.where(lo, tiles[0], tiles[1]))


def _softmax_with_sink(parts, sink):
    assert all(s.shape == parts[0].shape for s in parts)
    M = jnp.maximum(sink, jnp.max(functools.reduce(jnp.maximum, parts), axis=1, keepdims=True))
    ps = [jnp.exp2(s - M) for s in parts]
    den = jnp.exp2(sink - M) + jnp.sum(functools.reduce(jnp.add, ps), axis=1, keepdims=True)
    return ps, 1.0 / den


def _prompt_attn_kernel(sinks_ref, x_ref, g_ref, wq_ref, qg_ref, kp_ref, kc_ref, vp_ref, vc_ref, wo_ref,
                        y_ref):
    TQ = WINDOW
    TR = x_ref.shape[0]
    kj = lax.broadcasted_iota(jnp.int32, (WINDOW + TQ, TQ), 0)
    qi = lax.broadcasted_iota(jnp.int32, (WINDOW + TQ, TQ), 1)
    band = (qi + WINDOW - kj >= 0) & (qi - kj <= 0)
    band_first = band & ((kj >= WINDOW) | (pl.program_id(1) > 0))
    kscale = qg_ref[...] * (ATT_HD ** -0.5 * LOG2E)
    slabs = []
    for pr in range(ATT_KVH // 2):
        sl = slice(pr * LANES, (pr + 1) * LANES)
        slabs.append(((jnp.concatenate([kp_ref[:, sl], kc_ref[:, sl]], axis=0) * kscale).astype(BF16),
                      jnp.concatenate([vp_ref[:, sl], vc_ref[:, sl]], axis=0).T))

    PW = 2 * ATT_GROUP * ATT_HD
    zeros_hd = jnp.zeros((ATT_HD, TQ), F32)
    ones_rows = jnp.ones((2 * SUBLANES, WINDOW + TQ), F32)

    def chain(cb):
        nsb = ATTN_CHAIN_BLOCKS
        rows = slice(cb * nsb * TQ, (cb + 1) * nsb * TQ)
        x = x_ref[rows, :]
        xn = _rms(x, g_ref[...]).astype(BF16)
        acc = x
        for pr in range(ATT_KVH // 2):
            kslab, vslab_t = slabs[pr]
            qt_all = _dot(xn, wq_ref[:, pr * PW:(pr + 1) * PW]).T
            yield
            outs = []
            for i in range(nsb):
                sb = cb * nsb + i
                keys = slice(sb * TQ, sb * TQ + WINDOW + TQ)
                mask = band_first if sb == 0 else band
                qt = qt_all[:, i * TQ:(i + 1) * TQ]
                tiles, sink_rows = [], []
                for e in range(2):
                    for g in range(ATT_GROUP):
                        blk = qt[(e * ATT_GROUP + g) * ATT_HD:(e * ATT_GROUP + g + 1) * ATT_HD, :]
                        qn = blk * lax.rsqrt(jnp.sum(blk * blk, axis=0, keepdims=True) * (1.0 / ATT_HD) + EPS)
                        tiles.append(jnp.concatenate([qn, zeros_hd] if e == 0 else [zeros_hd, qn], axis=0))
                        sink_rows.append(jnp.full((1, TQ), sinks_ref[(2 * pr + e) * ATT_GROUP + g] * LOG2E, F32))
                qmat = jnp.concatenate(tiles, axis=1).astype(BF16)
                st = _dot(kslab[keys, :], qmat)
                yield
                parts = []
                for e in range(2):
                    ps, ms = [], []
                    for g in range(ATT_GROUP):
                        c0 = (e * ATT_GROUP + g) * TQ
                        s_h = jnp.where(mask, st[:, c0:c0 + TQ], -jnp.inf)
                        m_h = jnp.maximum(jnp.max(s_h, axis=0, keepdims=True), sink_rows[e * ATT_GROUP + g])
                        ps.append(jnp.exp2(s_h - m_h).astype(BF16))
                        ms.append(m_h)
                    p_e = jnp.concatenate(ps, axis=1)
                    v_aug = jnp.concatenate([vslab_t[e * ATT_HD:(e + 1) * ATT_HD, keys], ones_rows],
                                            axis=0).astype(BF16)
                    ot = _dot(v_aug, p_e)
                    yield
                    sink_term = jnp.exp2(jnp.concatenate(sink_rows[e * ATT_GROUP:(e + 1) * ATT_GROUP], axis=1)
                                         - jnp.concatenate(ms, axis=1))
                    on = ot[0:ATT_HD, :] * (1.0 / (ot[ATT_HD:ATT_HD + 1, :] + sink_term))
                    parts += [on[:, g * TQ:(g + 1) * TQ] for g in range(ATT_GROUP)]
                outs.append(jnp.concatenate(parts, axis=0))
            o_pair = jnp.concatenate(outs, axis=1).T.astype(BF16)
            acc = acc + _dot(o_pair, wo_ref[pr * PW:(pr + 1) * PW, :])
            yield
        y_ref[rows, :] = acc

    _run([(chain(cb), 0) for cb in range(TR // (ATTN_CHAIN_BLOCKS * TQ))])


def _prompt_attn(x, k, v, sinks, g, wq, qg, wo):
    B, L, D = x.shape
    TR = ATTN_ROWS
    per = TR // WINDOW
    assert L % TR == 0 and TR % WINDOW == 0
    const = lambda b, i: (0, 0)
    cur = lambda b, i: (b, i, 0)
    prev = lambda b, i: (b, jnp.maximum(i * per - 1, 0), 0)
    vmem = _vmem_limit(2 * _nbytes((TR, D), F32), _nbytes(wq.shape, BF16), _nbytes(wo.shape, BF16),
                       4 * _nbytes((TR, ATT_KV), F32), 8 * _nbytes((8 * WINDOW, 2 * WINDOW), F32))
    return pl.pallas_call(
        _prompt_attn_kernel,
        grid=(B, L // TR),
        in_specs=[
            pl.BlockSpec(memory_space=pltpu.SMEM),
            pl.BlockSpec((None, TR, D), cur),
            pl.BlockSpec((1, D), const),
            pl.BlockSpec((D, D), const),
            pl.BlockSpec((1, LANES), const),
            pl.BlockSpec((None, WINDOW, ATT_KV), prev),
            pl.BlockSpec((None, TR, ATT_KV), cur),
            pl.BlockSpec((None, WINDOW, ATT_KV), prev),
            pl.BlockSpec((None, TR, ATT_KV), cur),
            pl.BlockSpec((D, D), const),
        ],
        out_specs=pl.BlockSpec((None, TR, D), cur),
        out_shape=jax.ShapeDtypeStruct((B, L, D), F32),
        compiler_params=pltpu.CompilerParams(
            dimension_semantics=("arbitrary", "arbitrary"), vmem_limit_bytes=vmem),
        name="prompt_window_attention",
    )(sinks, x, g, wq, qg, k, k, v, v, wo)


def _sample_attn_kernel(sinks_ref, x_ref, g_ref, wq_ref, qg_ref, kc_ref, vc_ref, wk_ref, wv_ref, wo_ref,
                        y_ref, nk_ref, nv_ref, o_s, *, n_valid):
    T = SAMPLE_PAD
    nb = wk_ref.shape[0]
    R = nb * T
    assert R == WINDOW
    x = x_ref[...]
    q = _dot(_rms(x, g_ref[...]).astype(BF16), wq_ref[...])
    qscale = qg_ref[...] * (ATT_HD ** -0.5 * LOG2E)
    knew = kc_ref[...]
    vnew = vc_ref[...]

    cmask = (lax.broadcasted_iota(jnp.int32, (R, WINDOW), 1)
             >= lax.broadcasted_iota(jnp.int32, (R, WINDOW), 0) % T)
    row = lax.broadcasted_iota(jnp.int32, (R, R), 0)
    col = lax.broadcasted_iota(jnp.int32, (R, R), 1)
    nmask = (row // T == col // T) & (col % T <= row % T) & (col % T < n_valid)

    def store(off, val):
        o_s[:, off:off + LANES] = val.astype(BF16)

    def regroup(per_seq, i):
        return jnp.concatenate([per_seq[a][i * T:(i + 1) * T, :] for a in range(nb)], axis=0)

    def pair(pr):
        sl = slice(pr * LANES, (pr + 1) * LANES)
        qs = _pair_queries(q, pr, qscale)
        S_new = lax.dot_general(jnp.concatenate(qs, axis=0).astype(BF16), knew[:, sl].astype(BF16),
                                (((1,), (1,)), ((), ())), preferred_element_type=F32)
        yield
        sc = []
        for a in range(nb):
            q_a = jnp.concatenate([qi[T * a:T * a + T, :] for qi in qs], axis=0).astype(BF16)
            sc.append(_dot(q_a, wk_ref[a, sl, :].astype(BF16)))
            if a % 4 == 3:
                yield
        pcs, pns, rden = [], [], []
        for i in range(2 * ATT_GROUP):
            s_c = jnp.where(cmask, regroup(sc, i), -jnp.inf)
            s_n = jnp.where(nmask, S_new[i * R:(i + 1) * R, :], -jnp.inf)
            (p_c, p_n), r = _softmax_with_sink([s_c, s_n], sinks_ref[pr * 2 * ATT_GROUP + i] * LOG2E)
            pcs.append(p_c)
            pns.append(p_n)
            rden.append(r)
            if i % 2 == 1:
                yield
        O_new = _dot(jnp.concatenate(pns, axis=0).astype(BF16), vnew[:, sl].astype(BF16))
        yield
        oc = []
        for a in range(nb):
            p_a = jnp.concatenate([pc[T * a:T * a + T, :] for pc in pcs], axis=0).astype(BF16)
            oc.append(lax.dot_general(p_a, wv_ref[a, sl, :].astype(BF16), (((1,), (1,)), ((), ())),
                                      preferred_element_type=F32))
            if a % 4 == 3:
                yield
        _pair_outputs([O_new[i * R:(i + 1) * R, :] + regroup(oc, i) for i in range(2 * ATT_GROUP)],
                      rden, pr, store)
        yield

    def roll_cache():
        keep = lax.broadcasted_iota(jnp.int32, (ATT_KV, WINDOW), 1) < WINDOW - n_valid
        k_parts = _split3(knew.T)
        v_parts = _split3(vnew.T)
        group = 4
        rr = lax.broadcasted_iota(jnp.int32, (R, group * WINDOW), 0)
        cc = lax.broadcasted_iota(jnp.int32, (R, group * WINDOW), 1)
        hit = (cc % WINDOW == WINDOW - n_valid + rr % T) & (rr % T < n_valid)
        seq_off = rr // T - cc // WINDOW
        yield
        for g0 in range(0, nb, group):
            place = jnp.where(hit & (seq_off == g0), 1.0, 0.0).astype(BF16)
            new_k = _dot(k_parts[0], place) + _dot(k_parts[1], place) + _dot(k_parts[2], place)
            new_v = _dot(v_parts[0], place) + _dot(v_parts[1], place) + _dot(v_parts[2], place)
            for j in range(group):
                lanes = slice(j * WINDOW, (j + 1) * WINDOW)
                nk_ref[g0 + j] = jnp.where(keep, pltpu.roll(wk_ref[g0 + j], WINDOW - n_valid, 1), new_k[:, lanes])
                nv_ref[g0 + j] = jnp.where(keep, pltpu.roll(wv_ref[g0 + j], WINDOW - n_valid, 1), new_v[:, lanes])
            yield

    _run([(pair(0), 0), (pair(1), 2), (roll_cache(), 0)])
    y_ref[...] = x + _dot(o_s[...], wo_ref[...])


def _sample_attn(x, k, v, win_k, win_v, sinks, g, wq, qg, wo, n_valid):
    NBT, D = x.shape
    nseq = win_k.shape[0]
    nb = SAMPLE_NB
    R = nb * SAMPLE_PAD
    assert nseq % nb == 0 and NBT == nseq * SAMPLE_PAD
    const = lambda i: (0, 0)
    rows = lambda i: (i, 0)
    cache = pl.BlockSpec((nb, ATT_KV, WINDOW), lambda i: (i, 0, 0))
    vmem = _vmem_limit(2 * _nbytes((R, D), F32), _nbytes(wq.shape, BF16), _nbytes(wo.shape, BF16),
                       4 * _nbytes((nb, ATT_KV, WINDOW), F32), 2 * _nbytes((R, D), F32))
    return pl.pallas_call(
        functools.partial(_sample_attn_kernel, n_valid=n_valid),
        grid=(nseq // nb,),
        in_specs=[
            pl.BlockSpec(memory_space=pltpu.SMEM),
            pl.BlockSpec((R, D), rows),
            pl.BlockSpec((1, D), const),
            pl.BlockSpec((D, D), const),
            pl.BlockSpec((1, LANES), const),
            pl.BlockSpec((R, ATT_KV), rows),
            pl.BlockSpec((R, ATT_KV), rows),
            cache, cache,
            pl.BlockSpec((D, D), const),
        ],
        out_specs=[pl.BlockSpec((R, D), rows), cache, cache],
        out_shape=[jax.ShapeDtypeStruct((NBT, D), F32),
                   jax.ShapeDtypeStruct(win_k.shape, F32), jax.ShapeDtypeStruct(win_v.shape, F32)],
        scratch_shapes=[pltpu.VMEM((R, D), BF16)],
        compiler_params=pltpu.CompilerParams(
            dimension_semantics=("arbitrary",), vmem_limit_bytes=vmem),
        name="sample_window_attention",
    )(sinks, x, g, wq, qg, k, v, win_k, win_v, wo)


def kernel(x_prompt, x_sample, state_mlstm_C, state_mlstm_n, state_mlstm_m, cache_win_k, cache_win_v,
           ml_norm_g, ml_w_in, ml_b_i, ml_b_f, ml_head_g, ml_w_out, kv_norm_g, w_kv, k_norm_g,
           att_norm_g, att_w_q, q_norm_g, att_sinks, att_w_o, mlp_norm_g, mlp_w1, mlp_w2):
    B, L, D = x_prompt.shape
    NS, LS, _ = x_sample.shape
    assert ml_w_in.shape[0] == 1 and att_w_q.shape[0] == 1 and mlp_w1.shape[0] == 2

    w_in = ml_w_in[0]
    wqkvo = w_in.astype(BF16)
    wg = jnp.pad(w_in[:, ML_QKVO:], ((0, 0), (0, LANES - 2 * ML_HEADS))).astype(BF16)
    bg = jnp.pad(jnp.concatenate([ml_b_i[0], ml_b_f[0]]), (0, LANES - 2 * ML_HEADS)).reshape(1, LANES)
    ml_g = ml_norm_g[0].reshape(1, D)
    hg = ml_head_g[0].reshape(1, ML_VO)
    wout = ml_w_out[0].astype(BF16)
    mlp_g = mlp_norm_g.reshape(2, 1, D)
    kv_g = kv_norm_g.reshape(1, D)
    kg = jnp.tile(k_norm_g, ATT_KVH).reshape(1, ATT_KV)
    att_g = att_norm_g[0].reshape(1, D)
    qg = jnp.tile(q_norm_g[0], 2).reshape(1, LANES)
    sinks = att_sinks[0]

    xp, p_C, p_n, p_m, w1, w2, wkv, wq, wo = _prompt_mixer(
        x_prompt, ml_g, wqkvo, wg, bg, hg, wout,
        [mlp_w1.reshape(2 * D, D_FF), mlp_w2.reshape(2 * D_FF, D), w_kv, att_w_q[0], att_w_o[0]])
    w1 = w1.reshape(2, D, D_FF)
    w2 = w2.reshape(2, D_FF, D)
    xp, kp, vp = _mlp(xp.reshape(B * L, D), mlp_g[0], w1, w2, 0, kv=(kv_g, wkv, kg))
    xp = _prompt_attn(xp.reshape(B, L, D), kp.reshape(B, L, ATT_KV), vp.reshape(B, L, ATT_KV),
                      sinks, att_g, wq, qg, wo)
    y_prompt = _mlp(xp.reshape(B * L, D), mlp_g[1], w1, w2, 1).reshape(B, L, D)
    p_wk = kp.reshape(B, L, ATT_KV)[:, L - WINDOW:].reshape(B, WINDOW, ATT_KVH, ATT_HD)
    p_wv = vp.reshape(B, L, ATT_KV)[:, L - WINDOW:].reshape(B, WINDOW, ATT_KVH, ATT_HD)

    def pad8(a):
        return jnp.pad(a.reshape(NS, LS, -1), ((0, 0), (0, SAMPLE_PAD - LS), (0, 0))).reshape(NS * SAMPLE_PAD, -1)

    def unpad8(a):
        return a.reshape(NS, SAMPLE_PAD, -1)[:, :LS].reshape(NS * LS, -1)

    m0 = jnp.broadcast_to(state_mlstm_m[0][:, :, None], (NS, ML_HEADS, LANES))
    xs, s_C, s_n, s_m = _sample_mixer(pad8(x_sample), ml_g, wqkvo, wg, bg, hg, wout,
                                      state_mlstm_C[0], state_mlstm_n[0], m0, LS)
    xs, ks, vs = _mlp(unpad8(xs), mlp_g[0], w1, w2, 0, kv=(kv_g, wkv, kg))
    to_t = lambda c: c.transpose(0, 2, 3, 1).reshape(NS, ATT_KV, WINDOW)
    from_t = lambda c: c.reshape(NS, ATT_KVH, ATT_HD, WINDOW).transpose(0, 3, 1, 2)
    xs, s_wk, s_wv = _sample_attn(pad8(xs), pad8(ks), pad8(vs), to_t(cache_win_k), to_t(cache_win_v),
                                  sinks, att_g, wq, qg, wo, LS)
    y_sample = _mlp(unpad8(xs), mlp_g[1], w1, w2, 1).reshape(NS, LS, D)

    return (y_prompt, y_sample,
            p_C[None], p_n[None], p_m[None, :, :, 0], p_wk, p_wv,
            s_C[None], s_n[None], s_m[None, :, :, 0],
            from_t(s_wk), from_t(s_wv))
```

```python
import functools

import jax
import jax.numpy as jnp
from jax import lax
from jax.experimental import pallas as pl
from jax.experimental.pallas import tpu as pltpu

F32 = jnp.float32
BF16 = jnp.bfloat16

D_MODEL = 1024
ML_HEADS = 4
ML_DK = 128
ML_DV = 256
ML_QK = ML_HEADS * ML_DK
ML_VO = ML_HEADS * ML_DV
ML_QKVO = 2 * ML_QK + 2 * ML_VO
GATE_SOFTCAP = 15.0
ATT_HD = 64
ATT_QH = 16
ATT_KVH = 4
ATT_GROUP = 4
ATT_KV = ATT_KVH * ATT_HD
WINDOW = 128
D_FF = 4 * D_MODEL
EPS = 1e-6
LOG2E = 1.4426950408889634

LANES = 128
SUBLANES = 8
VMEM_LIMIT_CAP = 56 * 1024 * 1024

PROMPT_CHUNK = 256
PROMPT_ROWS = 2
PROJ_COLS = 512
SAMPLE_PAD = SUBLANES
SAMPLE_NB = 16
ATTN_ROWS = 512
ATTN_CHAIN_BLOCKS = 2
ROW_TILE = 1024
FF_TILE = 1024


def _vmem_limit(*block_bytes):
    need = 4 * sum(block_bytes) + (8 << 20)
    return int(min(max(need, 32 << 20), VMEM_LIMIT_CAP))


def _nbytes(shape, dtype):
    n = 1
    for s in shape:
        n *= s
    return n * jnp.dtype(dtype).itemsize


def _rms(x, g):
    return x * lax.rsqrt(jnp.mean(x * x, axis=-1, keepdims=True) + EPS) * g


def _dot(a, b):
    return jnp.dot(a, b, preferred_element_type=F32)


def _split3(x):
    hi = x.astype(BF16)
    r1 = x - hi.astype(F32)
    mid = r1.astype(BF16)
    lo = (r1 - mid.astype(F32)).astype(BF16)
    return hi, mid, lo


def _dot_exactish(m01, x):
    hi, mid, lo = _split3(x)
    return _dot(m01, hi) + _dot(m01, mid) + _dot(m01, lo)


def _gate_act(z):
    cap = GATE_SOFTCAP * jnp.tanh(z * (1.0 / GATE_SOFTCAP))
    lsig = jnp.minimum(cap, 0.0) - jnp.log1p(jnp.exp(-jnp.abs(cap)))
    lane = lax.broadcasted_iota(jnp.int32, z.shape, 1)
    return jnp.where(lane < ML_HEADS, cap, lsig)


_DONE = object()


def _rounds(chains):
    live = [(c[0], c[1], c[2] if len(c) > 2 else 1) for c in chains]
    rnd = 0
    while live:
        for item in list(live):
            gen, start, stride = item
            if rnd >= start and (rnd - start) % stride == 0 and next(gen, _DONE) is _DONE:
                live.remove(item)
        rnd += 1
        yield


def _run(chains):
    for _ in _rounds(chains):
        pass


def _head_out(hh, po, hg):
    hn = hh * lax.rsqrt(jnp.mean(hh * hh, axis=-1, keepdims=True) + EPS) * hg
    return jax.nn.sigmoid(po) * hn


def _head_slices(p, h):
    q = p[:, h * ML_DK:(h + 1) * ML_DK]
    k = p[:, ML_QK + h * ML_DK:ML_QK + (h + 1) * ML_DK] * (ML_DK ** -0.5)
    v = p[:, 2 * ML_QK + h * ML_DV:2 * ML_QK + (h + 1) * ML_DV]
    po = p[:, 2 * ML_QK + ML_VO + h * ML_DV:2 * ML_QK + ML_VO + (h + 1) * ML_DV]
    return q, k, v, po


def _prompt_mixer_kernel(xc_ref, xnext_ref, g_ref, wqkvo_ref, wg_ref, bg_ref, hg_ref, wout_ref, *rest,
                         chunks_per_seq, n_cast):
    cast_in = rest[:n_cast]
    y_ref, C_ref, n_ref, m_ref = rest[n_cast:n_cast + 4]
    cast_out = rest[n_cast + 4:2 * n_cast + 4]
    p_s, gz_s = rest[2 * n_cast + 4:]
    R, T = xc_ref.shape[0], xc_ref.shape[1]
    f = pl.program_id(0)
    slot = f % 2

    def project(x_ref, s, r):
        xn = _rms(x_ref[r], g_ref[...]).astype(BF16)
        yield
        for j in range(ML_QKVO // PROJ_COLS):
            cols = slice(j * PROJ_COLS, (j + 1) * PROJ_COLS)
            p_s[s, r, :, cols] = _dot(xn, wqkvo_ref[:, cols])
            yield
        gz_s[s, r] = _dot(xn, wg_ref[...])
        yield

    @pl.when(f == 0)
    def _():
        _run([(project(xc_ref, 0, r), 0) for r in range(R)])

    @pl.when(f % chunks_per_seq == 0)
    def _():
        C_ref[...] = jnp.zeros_like(C_ref)
        n_ref[...] = jnp.zeros_like(n_ref)
        m_ref[...] = jnp.zeros_like(m_ref)

    def recurrence(r):
        G = _gate_act(gz_s[slot, r] + bg_ref[...])
        row = lax.broadcasted_iota(jnp.int32, (T, T), 0)
        col = lax.broadcasted_iota(jnp.int32, (T, T), 1)
        causal = col <= row
        Bc = _dot_exactish(jnp.where(causal, 1.0, 0.0).astype(BF16), G)
        yield
        Gt = G.T
        Bt = Bc.T
        yield
        parts = [None] * ML_HEADS

        def head(h):
            def cols(base, width):
                return p_s[slot, r, :, base + h * width:base + (h + 1) * width]

            q = cols(0, ML_DK)
            k = cols(ML_QK, ML_DK) * (ML_DK ** -0.5)
            v = cols(2 * ML_QK, ML_DV)
            qc, kc, vc = q.astype(BF16), k.astype(BF16), v.astype(BF16)
            qk = lax.dot_general(qc, kc, (((1,), (1,)), ((), ())), preferred_element_type=F32)
            yield
            li_col = G[:, h:h + 1]
            b_col = Bc[:, ML_HEADS + h:ML_HEADS + h + 1]
            dmat = jnp.where(causal, b_col - Bt[ML_HEADS + h:ML_HEADS + h + 1, :] + Gt[h:h + 1, :], -jnp.inf)
            m_prev = m_ref[r, h:h + 1, 0:1]
            inter = b_col + m_prev
            m_t = jnp.maximum(inter, jnp.max(dmat, axis=1, keepdims=True))
            s = qk * jnp.exp(dmat - m_t)
            a_inter = jnp.exp(inter - m_t)
            den = (jnp.sum(s, axis=1, keepdims=True)
                   + a_inter * jnp.sum(q * n_ref[r, h:h + 1, :], axis=1, keepdims=True))
            yield
            C = C_ref[r, h]
            num = _dot(s.astype(BF16), vc) + a_inter * _dot(qc, C.astype(BF16))
            yield
            hh = num * (1.0 / jnp.maximum(jnp.abs(den), jnp.exp(-m_t)))
            hs = _head_out(hh, cols(2 * ML_QK + ML_VO, ML_DV), hg_ref[:, h * ML_DV:(h + 1) * ML_DV])
            yield
            parts[h] = _dot(hs.astype(BF16), wout_ref[h * ML_DV:(h + 1) * ML_DV, :])
            yield
            m_new = m_t[T - 1:T, :]
            b_last = b_col[T - 1:T, :]
            decay = jnp.exp(b_last + m_prev - m_new)
            kw = k * jnp.exp(b_last - b_col + li_col - m_new)
            C_ref[r, h] = decay * C + lax.dot_general(kw.astype(BF16), vc, (((0,), (0,)), ((), ())),
                                                      preferred_element_type=F32)
            n_ref[r, h:h + 1, :] = decay * n_ref[r, h:h + 1, :] + jnp.sum(kw, axis=0, keepdims=True)
            m_ref[r, h:h + 1, :] = jnp.broadcast_to(m_new, (1, LANES))
            yield

        yield from _rounds([(head(h), 2 * h) for h in range(ML_HEADS)])
        y_ref[r] = xc_ref[r] + ((parts[0] + parts[1]) + (parts[2] + parts[3]))

    def casts():
        for src, dst in zip(cast_in, cast_out):
            dst[...] = src[...].astype(BF16)
            yield

    _run([(project(xnext_ref, 1 - slot, r), 0) for r in range(R)] + [(recurrence(r), r) for r in range(R)]
         + [(casts(), 0)])


def _prompt_mixer(x, g, wqkvo, wg, bg, hg, wout, to_cast):
    B, L, D = x.shape
    T = PROMPT_CHUNK
    assert L % T == 0
    R = PROMPT_ROWS
    assert B % R == 0
    nc = L // T
    steps = (B // R) * nc
    const = lambda f: (0, 0)
    cur = lambda f: (f // nc, f % nc, 0)
    nxt = lambda f: (jnp.minimum(f + 1, steps - 1) // nc, jnp.minimum(f + 1, steps - 1) % nc, 0)
    once = dict(pipeline_mode=pl.Buffered(1))
    vmem = _vmem_limit(3 * _nbytes((R, T, D), F32), _nbytes(wqkvo.shape, BF16) // 2,
                       _nbytes(wout.shape, BF16) // 2, _nbytes((R, ML_HEADS, ML_DK, ML_DV), F32),
                       2 * _nbytes((R, T, ML_QKVO), F32))
    for w in to_cast:
        assert w.shape[0] % (steps * 2 * SUBLANES) == 0
    cast_specs = [pl.BlockSpec((w.shape[0] // steps, w.shape[1]), lambda f: (f, 0)) for w in to_cast]
    outs = pl.pallas_call(
        functools.partial(_prompt_mixer_kernel, chunks_per_seq=nc, n_cast=len(to_cast)),
        grid=(steps,),
        in_specs=[
            pl.BlockSpec((R, T, D), cur),
            pl.BlockSpec((R, T, D), nxt),
            pl.BlockSpec((1, D), const),
            pl.BlockSpec((D, ML_QKVO), const, **once),
            pl.BlockSpec((D, LANES), const, **once),
            pl.BlockSpec((1, LANES), const),
            pl.BlockSpec((1, ML_VO), const),
            pl.BlockSpec((ML_VO, D), const, **once),
        ] + cast_specs,
        out_specs=[
            pl.BlockSpec((R, T, D), cur),
            pl.BlockSpec((R, ML_HEADS, ML_DK, ML_DV), lambda f: (f // nc, 0, 0, 0)),
            pl.BlockSpec((R, ML_HEADS, ML_DK), lambda f: (f // nc, 0, 0)),
            pl.BlockSpec((R, ML_HEADS, LANES), lambda f: (f // nc, 0, 0)),
        ] + cast_specs,
        out_shape=[
            jax.ShapeDtypeStruct((B, L, D), F32),
            jax.ShapeDtypeStruct((B, ML_HEADS, ML_DK, ML_DV), F32),
            jax.ShapeDtypeStruct((B, ML_HEADS, ML_DK), F32),
            jax.ShapeDtypeStruct((B, ML_HEADS, LANES), F32),
        ] + [jax.ShapeDtypeStruct(w.shape, BF16) for w in to_cast],
        scratch_shapes=[pltpu.VMEM((2, R, T, ML_QKVO), F32), pltpu.VMEM((2, R, T, LANES), F32)],
        compiler_params=pltpu.CompilerParams(
            dimension_semantics=("arbitrary",), vmem_limit_bytes=vmem),
        name="prompt_mlstm_mixer",
    )(x, x, g, wqkvo, wg, bg, hg, wout, *to_cast)
    return outs


def _sample_mixer_kernel(x_ref, g_ref, wqkvo_ref, wg_ref, bg_ref, hg_ref, wout_ref,
                         C0_ref, n0_ref, m0_ref,
                         y_ref, C_ref, n_ref, m_ref, hs_s, *, n_valid):
    T = SAMPLE_PAD
    nb = C0_ref.shape[0]
    R = nb * T
    x = x_ref[...]
    xn = _rms(x, g_ref[...]).astype(BF16)
    p = _dot(xn, wqkvo_ref[...])

    rowt = lax.broadcasted_iota(jnp.int32, (R, LANES), 0) % T
    lane = lax.broadcasted_iota(jnp.int32, (R, LANES), 1)
    G = jnp.where(rowt < n_valid, _gate_act(_dot(xn, wg_ref[...]) + bg_ref[...]),
                  jnp.where(lane < ML_HEADS, -jnp.inf, 0.0))
    row = lax.broadcasted_iota(jnp.int32, (R, R), 0)
    col = lax.broadcasted_iota(jnp.int32, (R, R), 1)
    causal = (row // T == col // T) & (col <= row)
    Bc = _dot_exactish(jnp.where(causal, 1.0, 0.0).astype(BF16), jnp.where(lane < ML_HEADS, 0.0, G))
    Gt = G.T
    Bt = Bc.T

    def per_seq(fn):
        return jnp.concatenate([fn(a) for a in range(nb)], axis=0)

    def seq_last(colvec):
        return per_seq(lambda a: jnp.broadcast_to(colvec[T * a + T - 1:T * a + T, :], (T, 1)))

    def head(h):
        q, k, v, po = _head_slices(p, h)
        li_col = G[:, h:h + 1]
        b_col = Bc[:, ML_HEADS + h:ML_HEADS + h + 1]
        li_row = Gt[h:h + 1, :]
        b_row = Bt[ML_HEADS + h:ML_HEADS + h + 1, :]
        dmat = jnp.where(causal, b_col - (b_row - li_row), -jnp.inf)
        m_prev = per_seq(lambda a: jnp.broadcast_to(m0_ref[a, h:h + 1, 0:1], (T, 1)))
        n_rows = per_seq(lambda a: jnp.broadcast_to(n0_ref[a, h:h + 1, :], (T, ML_DK)))
        qc, kc, vc = q.astype(BF16), k.astype(BF16), v.astype(BF16)
        qk = lax.dot_general(qc, kc, (((1,), (1,)), ((), ())), preferred_element_type=F32)
        yield
        inter = b_col + m_prev
        m_t = jnp.maximum(inter, jnp.max(dmat, axis=1, keepdims=True))
        s = qk * jnp.exp(dmat - m_t)
        a_inter = jnp.exp(inter - m_t)
        yield
        qC = per_seq(lambda a: _dot(q[T * a:T * a + T, :], C0_ref[a, h]))
        yield
        num = _dot(s.astype(BF16), vc) + a_inter * qC
        den = jnp.sum(s, axis=1, keepdims=True) + a_inter * jnp.sum(q * n_rows, axis=1, keepdims=True)
        hh = num * (1.0 / jnp.maximum(jnp.abs(den), jnp.exp(-m_t)))
        hs_s[:, h * ML_DV:(h + 1) * ML_DV] = _head_out(
            hh, po, hg_ref[:, h * ML_DV:(h + 1) * ML_DV]).astype(BF16)
        yield
        m_new = seq_last(m_t)
        b_last = seq_last(b_col)
        decay = jnp.exp(b_last + m_prev - m_new)
        kw = k * jnp.exp(b_last - b_col + li_col - m_new)
        for a in range(nb):
            dec = decay[T * a:T * a + 1, :]
            kw_a = kw[T * a:T * a + T, :]
            upd = lax.dot_general(kw_a, v[T * a:T * a + T, :], (((0,), (0,)), ((), ())),
                                  preferred_element_type=F32)
            C_ref[a, h] = dec * C0_ref[a, h] + upd
            n_ref[a, h:h + 1, :] = dec * n0_ref[a, h:h + 1, :] + jnp.sum(kw_a, axis=0, keepdims=True)
            m_ref[a, h:h + 1, :] = jnp.broadcast_to(m_new[T * a:T * a + 1, :], (1, LANES))
            if a % 4 == 3:
                yield

    _run([(head(h), h) for h in range(ML_HEADS)])
    y_ref[...] = x + _dot(hs_s[...], wout_ref[...])


def _sample_mixer(x, g, wqkvo, wg, bg, hg, wout, C0, n0, m0, n_valid):
    NBT, D = x.shape
    nseq = C0.shape[0]
    nb = SAMPLE_NB
    R = nb * SAMPLE_PAD
    assert nseq % nb == 0 and NBT == nseq * SAMPLE_PAD
    const = lambda i: (0, 0)
    state_specs = [
        pl.BlockSpec((nb, ML_HEADS, ML_DK, ML_DV), lambda i: (i, 0, 0, 0)),
        pl.BlockSpec((nb, ML_HEADS, ML_DK), lambda i: (i, 0, 0)),
        pl.BlockSpec((nb, ML_HEADS, LANES), lambda i: (i, 0, 0)),
    ]
    vmem = _vmem_limit(2 * _nbytes((R, D), F32), _nbytes(wqkvo.shape, BF16), _nbytes(wout.shape, BF16),
                       2 * _nbytes((nb, ML_HEADS, ML_DK, ML_DV), F32), _nbytes((R, ML_QKVO), F32))
    return pl.pallas_call(
        functools.partial(_sample_mixer_kernel, n_valid=n_valid),
        grid=(nseq // nb,),
        in_specs=[
            pl.BlockSpec((R, D), lambda i: (i, 0)),
            pl.BlockSpec((1, D), const),
            pl.BlockSpec((D, ML_QKVO), const),
            pl.BlockSpec((D, LANES), const),
            pl.BlockSpec((1, LANES), const),
            pl.BlockSpec((1, ML_VO), const),
            pl.BlockSpec((ML_VO, D), const),
        ] + state_specs,
        out_specs=[pl.BlockSpec((R, D), lambda i: (i, 0))] + state_specs,
        out_shape=[
            jax.ShapeDtypeStruct((NBT, D), F32),
            jax.ShapeDtypeStruct(C0.shape, F32),
            jax.ShapeDtypeStruct(n0.shape, F32),
            jax.ShapeDtypeStruct(m0.shape, F32),
        ],
        scratch_shapes=[pltpu.VMEM((R, ML_VO), BF16)],
        compiler_params=pltpu.CompilerParams(
            dimension_semantics=("arbitrary",), vmem_limit_bytes=vmem),
        name="sample_mlstm_mixer",
    )(x, g, wqkvo, wg, bg, hg, wout, C0, n0, m0)


def _dot_exactish_right(x, m01):
    hi, mid, lo = _split3(x)
    return _dot(hi, m01) + _dot(mid, m01) + _dot(lo, m01)


def _mlp_kernel(x_ref, g_ref, w1_ref, w2_ref, *rest):
    x = x_ref[...]
    xn = _rms(x, g_ref[...]).astype(BF16)
    acc = x
    for c in range(D_FF // FF_TILE):
        hcol = _dot(xn, w1_ref[:, c * FF_TILE:(c + 1) * FF_TILE])
        hcol = jnp.square(jnp.maximum(hcol, 0.0)).astype(BF16)
        acc = acc + _dot(hcol, w2_ref[c * FF_TILE:(c + 1) * FF_TILE, :])
    if len(rest) == 1:
        (y_ref,) = rest
        y_ref[...] = acc
        return
    gkv_ref, wkv_ref, kg_ref, y_ref, k_ref, v_ref = rest
    y_ref[...] = acc
    kv = _dot(_rms(acc, gkv_ref[...]).astype(BF16), wkv_ref[...])
    kraw = kv[:, :ATT_KV]
    v_ref[...] = kv[:, ATT_KV:]
    r = lax.broadcasted_iota(jnp.int32, (ATT_KV, ATT_KV), 0) // ATT_HD
    c = lax.broadcasted_iota(jnp.int32, (ATT_KV, ATT_KV), 1) // ATT_HD
    seg = jnp.where(r == c, 1.0, 0.0).astype(BF16)
    ss = _dot_exactish_right(kraw * kraw, seg)
    k_ref[...] = kraw * lax.rsqrt(ss * (1.0 / ATT_HD) + EPS) * kg_ref[...]


def _mlp(x, g, w1, w2, layer, kv=None):
    N, D = x.shape
    tm = min(ROW_TILE, N)
    assert N % tm == 0
    const = lambda i: (0, 0)
    rows = lambda i: (i, 0)
    once = dict(pipeline_mode=pl.Buffered(1))
    in_specs = [
        pl.BlockSpec((tm, D), rows),
        pl.BlockSpec((1, D), const),
        pl.BlockSpec((None, D, D_FF), lambda i: (layer, 0, 0), **once),
        pl.BlockSpec((None, D_FF, D), lambda i: (layer, 0, 0), **once),
    ]
    out_specs = [pl.BlockSpec((tm, D), rows)]
    out_shape = [jax.ShapeDtypeStruct((N, D), F32)]
    args = [x, g, w1, w2]
    if kv is not None:
        in_specs += [pl.BlockSpec((1, D), const), pl.BlockSpec((D, 2 * ATT_KV), const, **once),
                     pl.BlockSpec((1, ATT_KV), const)]
        out_specs += [pl.BlockSpec((tm, ATT_KV), rows)] * 2
        out_shape += [jax.ShapeDtypeStruct((N, ATT_KV), F32)] * 2
        args += list(kv)
    vmem = _vmem_limit(2 * _nbytes((tm, D), F32), _nbytes(w1.shape[1:], BF16) // 2,
                       _nbytes(w2.shape[1:], BF16) // 2, 2 * _nbytes((tm, FF_TILE), F32))
    out = pl.pallas_call(
        _mlp_kernel,
        grid=(N // tm,),
        in_specs=in_specs,
        out_specs=out_specs,
        out_shape=out_shape,
        compiler_params=pltpu.CompilerParams(
            dimension_semantics=("arbitrary",), vmem_limit_bytes=vmem),
        name="sqrelu_mlp",
    )(*args)
    return out[0] if kv is None else out


def _pair_queries(qraw, pr, qscale, col0=0):
    TQ = qraw.shape[0]
    lo = lax.broadcasted_iota(jnp.int32, (TQ, LANES), 1) < ATT_HD
    out = []
    for e in range(2):
        kvh = 2 * pr + e
        for g in range(ATT_GROUP):
            cc, half = divmod(g, 2)
            c0 = (2 * kvh + cc) * LANES - col0
            q2 = qraw[:, c0:c0 + LANES]
            qm = jnp.where(lo, q2, 0.0) if half == 0 else jnp.where(lo, 0.0, q2)
            ss = jnp.sum(qm * qm, axis=1, keepdims=True)
            qn = qm * lax.rsqrt(ss * (1.0 / ATT_HD) + EPS)
            if qscale is not None:
                qn = qn * qscale
            out.append(qn if half == e else pltpu.roll(qn, ATT_HD, 1))
    return out


def _pair_outputs(o_heads, rden, pr, store):
    TQ = o_heads[0].shape[0]
    lo = lax.broadcasted_iota(jnp.int32, (TQ, LANES), 1) < ATT_HD
    for e in range(2):
        kvh = 2 * pr + e
        for cc in range(2):
            tiles = []
            for half in range(2):
                o = o_heads[e * ATT_GROUP + 2 * cc + half] * rden[e * ATT_GROUP + 2 * cc + half]
                tiles.append(o if half == e else pltpu.roll(o, ATT_HD, 1))
            store((2 * kvh + cc) * LANES, jnp.where(lo, tiles[0], tiles[1]))


def _softmax_with_sink(parts, sink):
    assert all(s.shape == parts[0].shape for s in parts)
    M = jnp.maximum(sink, jnp.max(functools.reduce(jnp.maximum, parts), axis=1, keepdims=True))
    ps = [jnp.exp2(s - M) for s in parts]
    den = jnp.exp2(sink - M) + jnp.sum(functools.reduce(jnp.add, ps), axis=1, keepdims=True)
    return ps, 1.0 / den


def _prompt_attn_kernel(sinks_ref, x_ref, g_ref, wq_ref, qg_ref, kp_ref, kc_ref, vp_ref, vc_ref, wo_ref,
                        y_ref):
    TQ = WINDOW
    TR = x_ref.shape[0]
    kj = lax.broadcasted_iota(jnp.int32, (WINDOW + TQ, TQ), 0)
    qi = lax.broadcasted_iota(jnp.int32, (WINDOW + TQ, TQ), 1)
    band = (qi + WINDOW - kj >= 0) & (qi - kj <= 0)
    band_first = band & ((kj >= WINDOW) | (pl.program_id(1) > 0))
    kscale = qg_ref[...] * (ATT_HD ** -0.5 * LOG2E)
    slabs = []
    for pr in range(ATT_KVH // 2):
        sl = slice(pr * LANES, (pr + 1) * LANES)
        slabs.append(((jnp.concatenate([kp_ref[:, sl], kc_ref[:, sl]], axis=0) * kscale).astype(BF16),
                      jnp.concatenate([vp_ref[:, sl], vc_ref[:, sl]], axis=0).T))

    PW = 2 * ATT_GROUP * ATT_HD
    zeros_hd = jnp.zeros((ATT_HD, TQ), F32)
    ones_rows = jnp.ones((2 * SUBLANES, WINDOW + TQ), F32)

    def chain(cb):
        nsb = ATTN_CHAIN_BLOCKS
        rows = slice(cb * nsb * TQ, (cb + 1) * nsb * TQ)
        x = x_ref[rows, :]
        xn = _rms(x, g_ref[...]).astype(BF16)
        acc = x
        for pr in range(ATT_KVH // 2):
            kslab, vslab_t = slabs[pr]
            qt_all = _dot(xn, wq_ref[:, pr * PW:(pr + 1) * PW]).T
            yield
            outs = []
            for i in range(nsb):
                sb = cb * nsb + i
                keys = slice(sb * TQ, sb * TQ + WINDOW + TQ)
                mask = band_first if sb == 0 else band
                qt = qt_all[:, i * TQ:(i + 1) * TQ]
                tiles, sink_rows = [], []
                for e in range(2):
                    for g in range(ATT_GROUP):
                        blk = qt[(e * ATT_GROUP + g) * ATT_HD:(e * ATT_GROUP + g + 1) * ATT_HD, :]
                        qn = blk * lax.rsqrt(jnp.sum(blk * blk, axis=0, keepdims=True) * (1.0 / ATT_HD) + EPS)
                        tiles.append(jnp.concatenate([qn, zeros_hd] if e == 0 else [zeros_hd, qn], axis=0))
                        sink_rows.append(jnp.full((1, TQ), sinks_ref[(2 * pr + e) * ATT_GROUP + g] * LOG2E, F32))
                qmat = jnp.concatenate(tiles, axis=1).astype(BF16)
                st = _dot(kslab[keys, :], qmat)
                yield
                parts = []
                for e in range(2):
                    ps, ms = [], []
                    for g in range(ATT_GROUP):
                        c0 = (e * ATT_GROUP + g) * TQ
                        s_h = jnp.where(mask, st[:, c0:c0 + TQ], -jnp.inf)
                        m_h = jnp.maximum(jnp.max(s_h, axis=0, keepdims=True), sink_rows[e * ATT_GROUP + g])
                        ps.append(jnp.exp2(s_h - m_h).astype(BF16))
                        ms.append(m_h)
                    p_e = jnp.concatenate(ps, axis=1)
                    v_aug = jnp.concatenate([vslab_t[e * ATT_HD:(e + 1) * ATT_HD, keys], ones_rows],
                                            axis=0).astype(BF16)
                    ot = _dot(v_aug, p_e)
                    yield
                    sink_term = jnp.exp2(jnp.concatenate(sink_rows[e * ATT_GROUP:(e + 1) * ATT_GROUP], axis=1)
                                         - jnp.concatenate(ms, axis=1))
                    on = ot[0:ATT_HD, :] * (1.0 / (ot[ATT_HD:ATT_HD + 1, :] + sink_term))
                    parts += [on[:, g * TQ:(g + 1) * TQ] for g in range(ATT_GROUP)]
                outs.append(jnp.concatenate(parts, axis=0))
            o_pair = jnp.concatenate(outs, axis=1).T.astype(BF16)
            acc = acc + _dot(o_pair, wo_ref[pr * PW:(pr + 1) * PW, :])
            yield
        y_ref[rows, :] = acc

    _run([(chain(cb), 0) for cb in range(TR // (ATTN_CHAIN_BLOCKS * TQ))])


def _prompt_attn(x, k, v, sinks, g, wq, qg, wo):
    B, L, D = x.shape
    TR = ATTN_ROWS
    per = TR // WINDOW
    assert L % TR == 0 and TR % WINDOW == 0
    const = lambda b, i: (0, 0)
    cur = lambda b, i: (b, i, 0)
    prev = lambda b, i: (b, jnp.maximum(i * per - 1, 0), 0)
    vmem = _vmem_limit(2 * _nbytes((TR, D), F32), _nbytes(wq.shape, BF16), _nbytes(wo.shape, BF16),
                       4 * _nbytes((TR, ATT_KV), F32), 8 * _nbytes((8 * WINDOW, 2 * WINDOW), F32))
    return pl.pallas_call(
        _prompt_attn_kernel,
        grid=(B, L // TR),
        in_specs=[
            pl.BlockSpec(memory_space=pltpu.SMEM),
            pl.BlockSpec((None, TR, D), cur),
            pl.BlockSpec((1, D), const),
            pl.BlockSpec((D, D), const),
            pl.BlockSpec((1, LANES), const),
            pl.BlockSpec((None, WINDOW, ATT_KV), prev),
            pl.BlockSpec((None, TR, ATT_KV), cur),
            pl.BlockSpec((None, WINDOW, ATT_KV), prev),
            pl.BlockSpec((None, TR, ATT_KV), cur),
            pl.BlockSpec((D, D), const),
        ],
        out_specs=pl.BlockSpec((None, TR, D), cur),
        out_shape=jax.ShapeDtypeStruct((B, L, D), F32),
        compiler_params=pltpu.CompilerParams(
            dimension_semantics=("arbitrary", "arbitrary"), vmem_limit_bytes=vmem),
        name="prompt_window_attention",
    )(sinks, x, g, wq, qg, k, k, v, v, wo)


def _sample_attn_kernel(sinks_ref, x_ref, g_ref, wq_ref, qg_ref, kc_ref, vc_ref, wk_ref, wv_ref, wo_ref,
                        y_ref, nk_ref, nv_ref, o_s, *, n_valid):
    T = SAMPLE_PAD
    nb = wk_ref.shape[0]
    R = nb * T
    assert R == WINDOW
    x = x_ref[...]
    q = _dot(_rms(x, g_ref[...]).astype(BF16), wq_ref[...])
    qscale = qg_ref[...] * (ATT_HD ** -0.5 * LOG2E)
    knew = kc_ref[...]
    vnew = vc_ref[...]

    cmask = (lax.broadcasted_iota(jnp.int32, (R, WINDOW), 1)
             >= lax.broadcasted_iota(jnp.int32, (R, WINDOW), 0) % T)
    row = lax.broadcasted_iota(jnp.int32, (R, R), 0)
    col = lax.broadcasted_iota(jnp.int32, (R, R), 1)
    nmask = (row // T == col // T) & (col % T <= row % T) & (col % T < n_valid)

    def store(off, val):
        o_s[:, off:off + LANES] = val.astype(BF16)

    def regroup(per_seq, i):
        return jnp.concatenate([per_seq[a][i * T:(i + 1) * T, :] for a in range(nb)], axis=0)

    def pair(pr):
        sl = slice(pr * LANES, (pr + 1) * LANES)
        qs = _pair_queries(q, pr, qscale)
        S_new = lax.dot_general(jnp.concatenate(qs, axis=0).astype(BF16), knew[:, sl].astype(BF16),
                                (((1,), (1,)), ((), ())), preferred_element_type=F32)
        yield
        sc = []
        for a in range(nb):
            q_a = jnp.concatenate([qi[T * a:T * a + T, :] for qi in qs], axis=0).astype(BF16)
            sc.append(_dot(q_a, wk_ref[a, sl, :].astype(BF16)))
            if a % 4 == 3:
                yield
        pcs, pns, rden = [], [], []
        for i in range(2 * ATT_GROUP):
            s_c = jnp.where(cmask, regroup(sc, i), -jnp.inf)
            s_n = jnp.where(nmask, S_new[i * R:(i + 1) * R, :], -jnp.inf)
            (p_c, p_n), r = _softmax_with_sink([s_c, s_n], sinks_ref[pr * 2 * ATT_GROUP + i] * LOG2E)
            pcs.append(p_c)
            pns.append(p_n)
            rden.append(r)
            if i % 2 == 1:
                yield
        O_new = _dot(jnp.concatenate(pns, axis=0).astype(BF16), vnew[:, sl].astype(BF16))
        yield
        oc = []
        for a in range(nb):
            p_a = jnp.concatenate([pc[T * a:T * a + T, :] for pc in pcs], axis=0).astype(BF16)
            oc.append(lax.dot_general(p_a, wv_ref[a, sl, :].astype(BF16), (((1,), (1,)), ((), ())),
                                      preferred_element_type=F32))
            if a % 4 == 3:
                yield
        _pair_outputs([O_new[i * R:(i + 1) * R, :] + regroup(oc, i) for i in range(2 * ATT_GROUP)],
                      rden, pr, store)
        yield

    def roll_cache():
        keep = lax.broadcasted_iota(jnp.int32, (ATT_KV, WINDOW), 1) < WINDOW - n_valid
        k_parts = _split3(knew.T)
        v_parts = _split3(vnew.T)
        group = 4
        rr = lax.broadcasted_iota(jnp.int32, (R, group * WINDOW), 0)
        cc = lax.broadcasted_iota(jnp.int32, (R, group * WINDOW), 1)
        hit = (cc % WINDOW == WINDOW - n_valid + rr % T) & (rr % T < n_valid)
        seq_off = rr // T - cc // WINDOW
        yield
        for g0 in range(0, nb, group):
            place = jnp.where(hit & (seq_off == g0), 1.0, 0.0).astype(BF16)
            new_k = _dot(k_parts[0], place) + _dot(k_parts[1], place) + _dot(k_parts[2], place)
            new_v = _dot(v_parts[0], place) + _dot(v_parts[1], place) + _dot(v_parts[2], place)
            for j in range(group):
                lanes = slice(j * WINDOW, (j + 1) * WINDOW)
                nk_ref[g0 + j] = jnp.where(keep, pltpu.roll(wk_ref[g0 + j], WINDOW - n_valid, 1), new_k[:, lanes])
                nv_ref[g0 + j] = jnp.where(keep, pltpu.roll(wv_ref[g0 + j], WINDOW - n_valid, 1), new_v[:, lanes])
            yield

    _run([(pair(0), 0), (pair(1), 2), (roll_cache(), 0)])
    y_ref[...] = x + _dot(o_s[...], wo_ref[...])


def _sample_attn(x, k, v, win_k, win_v, sinks, g, wq, qg, wo, n_valid):
    NBT, D = x.shape
    nseq = win_k.shape[0]
    nb = SAMPLE_NB
    R = nb * SAMPLE_PAD
    assert nseq % nb == 0 and NBT == nseq * SAMPLE_PAD
    const = lambda i: (0, 0)
    rows = lambda i: (i, 0)
    cache = pl.BlockSpec((nb, ATT_KV, WINDOW), lambda i: (i, 0, 0))
    vmem = _vmem_limit(2 * _nbytes((R, D), F32), _nbytes(wq.shape, BF16), _nbytes(wo.shape, BF16),
                       4 * _nbytes((nb, ATT_KV, WINDOW), F32), 2 * _nbytes((R, D), F32))
    return pl.pallas_call(
        functools.partial(_sample_attn_kernel, n_valid=n_valid),
        grid=(nseq // nb,),
        in_specs=[
            pl.BlockSpec(memory_space=pltpu.SMEM),
            pl.BlockSpec((R, D), rows),
            pl.BlockSpec((1, D), const),
            pl.BlockSpec((D, D), const),
            pl.BlockSpec((1, LANES), const),
            pl.BlockSpec((R, ATT_KV), rows),
            pl.BlockSpec((R, ATT_KV), rows),
            cache, cache,
            pl.BlockSpec((D, D), const),
        ],
        out_specs=[pl.BlockSpec((R, D), rows), cache, cache],
        out_shape=[jax.ShapeDtypeStruct((NBT, D), F32),
                   jax.ShapeDtypeStruct(win_k.shape, F32), jax.ShapeDtypeStruct(win_v.shape, F32)],
        scratch_shapes=[pltpu.VMEM((R, D), BF16)],
        compiler_params=pltpu.CompilerParams(
            dimension_semantics=("arbitrary",), vmem_limit_bytes=vmem),
        name="sample_window_attention",
    )(sinks, x, g, wq, qg, k, v, win_k, win_v, wo)


def kernel(x_prompt, x_sample, state_mlstm_C, state_mlstm_n, state_mlstm_m, cache_win_k, cache_win_v,
           ml_norm_g, ml_w_in, ml_b_i, ml_b_f, ml_head_g, ml_w_out, kv_norm_g, w_kv, k_norm_g,
           att_norm_g, att_w_q, q_norm_g, att_sinks, att_w_o, mlp_norm_g, mlp_w1, mlp_w2):
    B, L, D = x_prompt.shape
    NS, LS, _ = x_sample.shape
    assert ml_w_in.shape[0] == 1 and att_w_q.shape[0] == 1 and mlp_w1.shape[0] == 2

    w_in = ml_w_in[0]
    wqkvo = w_in.astype(BF16)
    wg = jnp.pad(w_in[:, ML_QKVO:], ((0, 0), (0, LANES - 2 * ML_HEADS))).astype(BF16)
    bg = jnp.pad(jnp.concatenate([ml_b_i[0], ml_b_f[0]]), (0, LANES - 2 * ML_HEADS)).reshape(1, LANES)
    ml_g = ml_norm_g[0].reshape(1, D)
    hg = ml_head_g[0].reshape(1, ML_VO)
    wout = ml_w_out[0].astype(BF16)
    mlp_g = mlp_norm_g.reshape(2, 1, D)
    kv_g = kv_norm_g.reshape(1, D)
    kg = jnp.tile(k_norm_g, ATT_KVH).reshape(1, ATT_KV)
    att_g = att_norm_g[0].reshape(1, D)
    qg = jnp.tile(q_norm_g[0], 2).reshape(1, LANES)
    sinks = att_sinks[0]

    xp, p_C, p_n, p_m, w1, w2, wkv, wq, wo = _prompt_mixer(
        x_prompt, ml_g, wqkvo, wg, bg, hg, wout,
        [mlp_w1.reshape(2 * D, D_FF), mlp_w2.reshape(2 * D_FF, D), w_kv, att_w_q[0], att_w_o[0]])
    w1 = w1.reshape(2, D, D_FF)
    w2 = w2.reshape(2, D_FF, D)
    xp, kp, vp = _mlp(xp.reshape(B * L, D), mlp_g[0], w1, w2, 0, kv=(kv_g, wkv, kg))
    xp = _prompt_attn(xp.reshape(B, L, D), kp.reshape(B, L, ATT_KV), vp.reshape(B, L, ATT_KV),
                      sinks, att_g, wq, qg, wo)
    y_prompt = _mlp(xp.reshape(B * L, D), mlp_g[1], w1, w2, 1).reshape(B, L, D)
    p_wk = kp.reshape(B, L, ATT_KV)[:, L - WINDOW:].reshape(B, WINDOW, ATT_KVH, ATT_HD)
    p_wv = vp.reshape(B, L, ATT_KV)[:, L - WINDOW:].reshape(B, WINDOW, ATT_KVH, ATT_HD)

    def pad8(a):
        return jnp.pad(a.reshape(NS, LS, -1), ((0, 0), (0, SAMPLE_PAD - LS), (0, 0))).reshape(NS * SAMPLE_PAD, -1)

    def unpad8(a):
        return a.reshape(NS, SAMPLE_PAD, -1)[:, :LS].reshape(NS * LS, -1)

    m0 = jnp.broadcast_to(state_mlstm_m[0][:, :, None], (NS, ML_HEADS, LANES))
    xs, s_C, s_n, s_m = _sample_mixer(pad8(x_sample), ml_g, wqkvo, wg, bg, hg, wout,
                                      state_mlstm_C[0], state_mlstm_n[0], m0, LS)
    xs, ks, vs = _mlp(unpad8(xs), mlp_g[0], w1, w2, 0, kv=(kv_g, wkv, kg))
    to_t = lambda c: c.transpose(0, 2, 3, 1).reshape(NS, ATT_KV, WINDOW)
    from_t = lambda c: c.reshape(NS, ATT_KVH, ATT_HD, WINDOW).transpose(0, 3, 1, 2)
    xs, s_wk, s_wv = _sample_attn(pad8(xs), pad8(ks), pad8(vs), to_t(cache_win_k), to_t(cache_win_v),
                                  sinks, att_g, wq, qg, wo, LS)
    y_sample = _mlp(unpad8(xs), mlp_g[1], w1, w2, 1).reshape(NS, LS, D)

    return (y_prompt, y_sample,
            p_C[None], p_n[None], p_m[None, :, :, 0], p_wk, p_wv,
            s_C[None], s_n[None], s_m[None, :, :, 0],
            from_t(s_wk), from_t(s_wv))
```

```python
import functools

import jax
import jax.numpy as jnp
from jax import lax
from jax.experimental import pallas as pl
from jax.experimental.pallas import tpu as pltpu

F32 = jnp.float32
BF16 = jnp.bfloat16

D_MODEL = 1024
ML_HEADS = 4
ML_DK = 128
ML_DV = 256
ML_QK = ML_HEADS * ML_DK
ML_VO = ML_HEADS * ML_DV
ML_QKVO = 2 * ML_QK + 2 * ML_VO
GATE_SOFTCAP = 15.0
ATT_HD = 64
ATT_QH = 16
ATT_KVH = 4
ATT_GROUP = 4
ATT_KV = ATT_KVH * ATT_HD
WINDOW = 128
D_FF = 4 * D_MODEL
EPS = 1e-6
LOG2E = 1.4426950408889634

LANES = 128
SUBLANES = 8
VMEM_LIMIT_CAP = 56 * 1024 * 1024

PROMPT_CHUNK = 256
PROMPT_ROWS = 2
PROJ_COLS = 512
SAMPLE_PAD = SUBLANES
SAMPLE_NB = 16
ATTN_ROWS = 1024
ATTN_CHAIN_BLOCKS = 2
ROW_TILE = 1024
FF_TILE = 1024


def _vmem_limit(*block_bytes):
    need = 4 * sum(block_bytes) + (8 << 20)
    return int(min(max(need, 32 << 20), VMEM_LIMIT_CAP))


def _nbytes(shape, dtype):
    n = 1
    for s in shape:
        n *= s
    return n * jnp.dtype(dtype).itemsize


def _rms(x, g):
    return x * lax.rsqrt(jnp.mean(x * x, axis=-1, keepdims=True) + EPS) * g


def _dot(a, b):
    return jnp.dot(a, b, preferred_element_type=F32)


def _split3(x):
    hi = x.astype(BF16)
    r1 = x - hi.astype(F32)
    mid = r1.astype(BF16)
    lo = (r1 - mid.astype(F32)).astype(BF16)
    return hi, mid, lo


def _dot_exactish(m01, x):
    hi, mid, lo = _split3(x)
    return _dot(m01, hi) + _dot(m01, mid) + _dot(m01, lo)


def _gate_act(z):
    cap = GATE_SOFTCAP * jnp.tanh(z * (1.0 / GATE_SOFTCAP))
    lsig = jnp.minimum(cap, 0.0) - jnp.log1p(jnp.exp(-jnp.abs(cap)))
    lane = lax.broadcasted_iota(jnp.int32, z.shape, 1)
    return jnp.where(lane < ML_HEADS, cap, lsig)


_DONE = object()


def _rounds(chains):
    live = [(c[0], c[1], c[2] if len(c) > 2 else 1) for c in chains]
    rnd = 0
    while live:
        for item in list(live):
            gen, start, stride = item
            if rnd >= start and (rnd - start) % stride == 0 and next(gen, _DONE) is _DONE:
                live.remove(item)
        rnd += 1
        yield


def _run(chains):
    for _ in _rounds(chains):
        pass


def _head_out(hh, po, hg):
    hn = hh * lax.rsqrt(jnp.mean(hh * hh, axis=-1, keepdims=True) + EPS) * hg
    return jax.nn.sigmoid(po) * hn


def _head_slices(p, h):
    q = p[:, h * ML_DK:(h + 1) * ML_DK]
    k = p[:, ML_QK + h * ML_DK:ML_QK + (h + 1) * ML_DK] * (ML_DK ** -0.5)
    v = p[:, 2 * ML_QK + h * ML_DV:2 * ML_QK + (h + 1) * ML_DV]
    po = p[:, 2 * ML_QK + ML_VO + h * ML_DV:2 * ML_QK + ML_VO + (h + 1) * ML_DV]
    return q, k, v, po


def _prompt_mixer_kernel(xc_ref, xnext_ref, g_ref, wqkvo_ref, wg_ref, bg_ref, hg_ref, wout_ref, *rest,
                         chunks_per_seq, n_cast):
    cast_in = rest[:n_cast]
    y_ref, C_ref, n_ref, m_ref = rest[n_cast:n_cast + 4]
    cast_out = rest[n_cast + 4:2 * n_cast + 4]
    p_s, gz_s = rest[2 * n_cast + 4:]
    R, T = xc_ref.shape[0], xc_ref.shape[1]
    f = pl.program_id(0)
    slot = f % 2

    def project(x_ref, s, r):
        xn = _rms(x_ref[r], g_ref[...]).astype(BF16)
        yield
        for j in range(ML_QKVO // PROJ_COLS):
            cols = slice(j * PROJ_COLS, (j + 1) * PROJ_COLS)
            p_s[s, r, :, cols] = _dot(xn, wqkvo_ref[:, cols])
            yield
        gz_s[s, r] = _dot(xn, wg_ref[...])
        yield

    @pl.when(f == 0)
    def _():
        _run([(project(xc_ref, 0, r), 0) for r in range(R)])

    @pl.when(f % chunks_per_seq == 0)
    def _():
        C_ref[...] = jnp.zeros_like(C_ref)
        n_ref[...] = jnp.zeros_like(n_ref)
        m_ref[...] = jnp.zeros_like(m_ref)

    def recurrence(r):
        G = _gate_act(gz_s[slot, r] + bg_ref[...])
        row = lax.broadcasted_iota(jnp.int32, (T, T), 0)
        col = lax.broadcasted_iota(jnp.int32, (T, T), 1)
        causal = col <= row
        Bc = _dot_exactish(jnp.where(causal, 1.0, 0.0).astype(BF16), G)
        yield
        Gt = G.T
        Bt = Bc.T
        yield
        parts = [None] * ML_HEADS

        def head(h):
            def cols(base, width):
                return p_s[slot, r, :, base + h * width:base + (h + 1) * width]

            q = cols(0, ML_DK)
            k = cols(ML_QK, ML_DK) * (ML_DK ** -0.5)
            v = cols(2 * ML_QK, ML_DV)
            qc, kc, vc = q.astype(BF16), k.astype(BF16), v.astype(BF16)
            qk = lax.dot_general(qc, kc, (((1,), (1,)), ((), ())), preferred_element_type=F32)
            yield
            li_col = G[:, h:h + 1]
            b_col = Bc[:, ML_HEADS + h:ML_HEADS + h + 1]
            dmat = jnp.where(causal, b_col - Bt[ML_HEADS + h:ML_HEADS + h + 1, :] + Gt[h:h + 1, :], -jnp.inf)
            m_prev = m_ref[r, h:h + 1, 0:1]
            inter = b_col + m_prev
            m_t = jnp.maximum(inter, jnp.max(dmat, axis=1, keepdims=True))
            s = qk * jnp.exp(dmat - m_t)
            a_inter = jnp.exp(inter - m_t)
            den = (jnp.sum(s, axis=1, keepdims=True)
                   + a_inter * jnp.sum(q * n_ref[r, h:h + 1, :], axis=1, keepdims=True))
            yield
            C = C_ref[r, h]
            num = _dot(s.astype(BF16), vc) + a_inter * _dot(qc, C.astype(BF16))
            yield
            hh = num * (1.0 / jnp.maximum(jnp.abs(den), jnp.exp(-m_t)))
            hs = _head_out(hh, cols(2 * ML_QK + ML_VO, ML_DV), hg_ref[:, h * ML_DV:(h + 1) * ML_DV])
            yield
            parts[h] = _dot(hs.astype(BF16), wout_ref[h * ML_DV:(h + 1) * ML_DV, :])
            yield
            m_new = m_t[T - 1:T, :]
            b_last = b_col[T - 1:T, :]
            decay = jnp.exp(b_last + m_prev - m_new)
            kw = k * jnp.exp(b_last - b_col + li_col - m_new)
            C_ref[r, h] = decay * C + lax.dot_general(kw.astype(BF16), vc, (((0,), (0,)), ((), ())),
                                                      preferred_element_type=F32)
            n_ref[r, h:h + 1, :] = decay * n_ref[r, h:h + 1, :] + jnp.sum(kw, axis=0, keepdims=True)
            m_ref[r, h:h + 1, :] = jnp.broadcast_to(m_new, (1, LANES))
            yield

        yield from _rounds([(head(h), 2 * h) for h in range(ML_HEADS)])
        y_ref[r] = xc_ref[r] + ((parts[0] + parts[1]) + (parts[2] + parts[3]))

    def casts():
        for src, dst in zip(cast_in, cast_out):
            dst[...] = src[...].astype(BF16)
            yield

    _run([(project(xnext_ref, 1 - slot, r), 0) for r in range(R)] + [(recurrence(r), r) for r in range(R)]
         + [(casts(), 0)])


def _prompt_mixer(x, g, wqkvo, wg, bg, hg, wout, to_cast):
    B, L, D = x.shape
    T = PROMPT_CHUNK
    assert L % T == 0
    R = PROMPT_ROWS
    assert B % R == 0
    nc = L // T
    steps = (B // R) * nc
    const = lambda f: (0, 0)
    cur = lambda f: (f // nc, f % nc, 0)
    nxt = lambda f: (jnp.minimum(f + 1, steps - 1) // nc, jnp.minimum(f + 1, steps - 1) % nc, 0)
    once = dict(pipeline_mode=pl.Buffered(1))
    vmem = _vmem_limit(3 * _nbytes((R, T, D), F32), _nbytes(wqkvo.shape, BF16) // 2,
                       _nbytes(wout.shape, BF16) // 2, _nbytes((R, ML_HEADS, ML_DK, ML_DV), F32),
                       2 * _nbytes((R, T, ML_QKVO), F32))
    for w in to_cast:
        assert w.shape[0] % (steps * 2 * SUBLANES) == 0
    cast_specs = [pl.BlockSpec((w.shape[0] // steps, w.shape[1]), lambda f: (f, 0)) for w in to_cast]
    outs = pl.pallas_call(
        functools.partial(_prompt_mixer_kernel, chunks_per_seq=nc, n_cast=len(to_cast)),
        grid=(steps,),
        in_specs=[
            pl.BlockSpec((R, T, D), cur),
            pl.BlockSpec((R, T, D), nxt),
            pl.BlockSpec((1, D), const),
            pl.BlockSpec((D, ML_QKVO), const, **once),
            pl.BlockSpec((D, LANES), const, **once),
            pl.BlockSpec((1, LANES), const),
            pl.BlockSpec((1, ML_VO), const),
            pl.BlockSpec((ML_VO, D), const, **once),
        ] + cast_specs,
        out_specs=[
            pl.BlockSpec((R, T, D), cur),
            pl.BlockSpec((R, ML_HEADS, ML_DK, ML_DV), lambda f: (f // nc, 0, 0, 0)),
            pl.BlockSpec((R, ML_HEADS, ML_DK), lambda f: (f // nc, 0, 0)),
            pl.BlockSpec((R, ML_HEADS, LANES), lambda f: (f // nc, 0, 0)),
        ] + cast_specs,
        out_shape=[
            jax.ShapeDtypeStruct((B, L, D), F32),
            jax.ShapeDtypeStruct((B, ML_HEADS, ML_DK, ML_DV), F32),
            jax.ShapeDtypeStruct((B, ML_HEADS, ML_DK), F32),
            jax.ShapeDtypeStruct((B, ML_HEADS, LANES), F32),
        ] + [jax.ShapeDtypeStruct(w.shape, BF16) for w in to_cast],
        scratch_shapes=[pltpu.VMEM((2, R, T, ML_QKVO), F32), pltpu.VMEM((2, R, T, LANES), F32)],
        compiler_params=pltpu.CompilerParams(
            dimension_semantics=("arbitrary",), vmem_limit_bytes=vmem),
        name="prompt_mlstm_mixer",
    )(x, x, g, wqkvo, wg, bg, hg, wout, *to_cast)
    return outs


def _sample_mixer_kernel(x_ref, g_ref, wqkvo_ref, wg_ref, bg_ref, hg_ref, wout_ref,
                         C0_ref, n0_ref, m0_ref,
                         y_ref, C_ref, n_ref, m_ref, hs_s, *, n_valid):
    T = SAMPLE_PAD
    nb = C0_ref.shape[0]
    R = nb * T
    x = x_ref[...]
    xn = _rms(x, g_ref[...]).astype(BF16)
    p = _dot(xn, wqkvo_ref[...])

    rowt = lax.broadcasted_iota(jnp.int32, (R, LANES), 0) % T
    lane = lax.broadcasted_iota(jnp.int32, (R, LANES), 1)
    G = jnp.where(rowt < n_valid, _gate_act(_dot(xn, wg_ref[...]) + bg_ref[...]),
                  jnp.where(lane < ML_HEADS, -jnp.inf, 0.0))
    row = lax.broadcasted_iota(jnp.int32, (R, R), 0)
    col = lax.broadcasted_iota(jnp.int32, (R, R), 1)
    causal = (row // T == col // T) & (col <= row)
    Bc = _dot_exactish(jnp.where(causal, 1.0, 0.0).astype(BF16), jnp.where(lane < ML_HEADS, 0.0, G))
    Gt = G.T
    Bt = Bc.T

    def per_seq(fn):
        return jnp.concatenate([fn(a) for a in range(nb)], axis=0)

    def seq_last(colvec):
        return per_seq(lambda a: jnp.broadcast_to(colvec[T * a + T - 1:T * a + T, :], (T, 1)))

    def head(h):
        q, k, v, po = _head_slices(p, h)
        li_col = G[:, h:h + 1]
        b_col = Bc[:, ML_HEADS + h:ML_HEADS + h + 1]
        li_row = Gt[h:h + 1, :]
        b_row = Bt[ML_HEADS + h:ML_HEADS + h + 1, :]
        dmat = jnp.where(causal, b_col - (b_row - li_row), -jnp.inf)
        m_prev = per_seq(lambda a: jnp.broadcast_to(m0_ref[a, h:h + 1, 0:1], (T, 1)))
        n_rows = per_seq(lambda a: jnp.broadcast_to(n0_ref[a, h:h + 1, :], (T, ML_DK)))
        qc, kc, vc = q.astype(BF16), k.astype(BF16), v.astype(BF16)
        qk = lax.dot_general(qc, kc, (((1,), (1,)), ((), ())), preferred_element_type=F32)
        yield
        inter = b_col + m_prev
        m_t = jnp.maximum(inter, jnp.max(dmat, axis=1, keepdims=True))
        s = qk * jnp.exp(dmat - m_t)
        a_inter = jnp.exp(inter - m_t)
        yield
        qC = per_seq(lambda a: _dot(q[T * a:T * a + T, :], C0_ref[a, h]))
        yield
        num = _dot(s.astype(BF16), vc) + a_inter * qC
        den = jnp.sum(s, axis=1, keepdims=True) + a_inter * jnp.sum(q * n_rows, axis=1, keepdims=True)
        hh = num * (1.0 / jnp.maximum(jnp.abs(den), jnp.exp(-m_t)))
        hs_s[:, h * ML_DV:(h + 1) * ML_DV] = _head_out(
            hh, po, hg_ref[:, h * ML_DV:(h + 1) * ML_DV]).astype(BF16)
        yield
        m_new = seq_last(m_t)
        b_last = seq_last(b_col)
        decay = jnp.exp(b_last + m_prev - m_new)
        kw = k * jnp.exp(b_last - b_col + li_col - m_new)
        for a in range(nb):
            dec = decay[T * a:T * a + 1, :]
            kw_a = kw[T * a:T * a + T, :]
            upd = lax.dot_general(kw_a, v[T * a:T * a + T, :], (((0,), (0,)), ((), ())),
                                  preferred_element_type=F32)
            C_ref[a, h] = dec * C0_ref[a, h] + upd
            n_ref[a, h:h + 1, :] = dec * n0_ref[a, h:h + 1, :] + jnp.sum(kw_a, axis=0, keepdims=True)
            m_ref[a, h:h + 1, :] = jnp.broadcast_to(m_new[T * a:T * a + 1, :], (1, LANES))
            if a % 4 == 3:
                yield

    _run([(head(h), h) for h in range(ML_HEADS)])
    y_ref[...] = x + _dot(hs_s[...], wout_ref[...])


def _sample_mixer(x, g, wqkvo, wg, bg, hg, wout, C0, n0, m0, n_valid):
    NBT, D = x.shape
    nseq = C0.shape[0]
    nb = SAMPLE_NB
    R = nb * SAMPLE_PAD
    assert nseq % nb == 0 and NBT == nseq * SAMPLE_PAD
    const = lambda i: (0, 0)
    state_specs = [
        pl.BlockSpec((nb, ML_HEADS, ML_DK, ML_DV), lambda i: (i, 0, 0, 0)),
        pl.BlockSpec((nb, ML_HEADS, ML_DK), lambda i: (i, 0, 0)),
        pl.BlockSpec((nb, ML_HEADS, LANES), lambda i: (i, 0, 0)),
    ]
    vmem = _vmem_limit(2 * _nbytes((R, D), F32), _nbytes(wqkvo.shape, BF16), _nbytes(wout.shape, BF16),
                       2 * _nbytes((nb, ML_HEADS, ML_DK, ML_DV), F32), _nbytes((R, ML_QKVO), F32))
    return pl.pallas_call(
        functools.partial(_sample_mixer_kernel, n_valid=n_valid),
        grid=(nseq // nb,),
        in_specs=[
            pl.BlockSpec((R, D), lambda i: (i, 0)),
            pl.BlockSpec((1, D), const),
            pl.BlockSpec((D, ML_QKVO), const),
            pl.BlockSpec((D, LANES), const),
            pl.BlockSpec((1, LANES), const),
            pl.BlockSpec((1, ML_VO), const),
            pl.BlockSpec((ML_VO, D), const),
        ] + state_specs,
        out_specs=[pl.BlockSpec((R, D), lambda i: (i, 0))] + state_specs,
        out_shape=[
            jax.ShapeDtypeStruct((NBT, D), F32),
            jax.ShapeDtypeStruct(C0.shape, F32),
            jax.ShapeDtypeStruct(n0.shape, F32),
            jax.ShapeDtypeStruct(m0.shape, F32),
        ],
        scratch_shapes=[pltpu.VMEM((R, ML_VO), BF16)],
        compiler_params=pltpu.CompilerParams(
            dimension_semantics=("arbitrary",), vmem_limit_bytes=vmem),
        name="sample_mlstm_mixer",
    )(x, g, wqkvo, wg, bg, hg, wout, C0, n0, m0)


def _dot_exactish_right(x, m01):
    hi, mid, lo = _split3(x)
    return _dot(hi, m01) + _dot(mid, m01) + _dot(lo, m01)


def _mlp_kernel(x_ref, g_ref, w1_ref, w2_ref, *rest):
    with_kv = len(rest) > 1
    if with_kv:
        gkv_ref, wkv_ref, kg_ref, y_ref, k_ref, v_ref = rest
    else:
        (y_ref,) = rest
    half = x_ref.shape[0] // 2

    def chain(i):
        rows = slice(i * half, (i + 1) * half)
        x = x_ref[rows, :]
        xn = _rms(x, g_ref[...]).astype(BF16)
        acc = x
        for c in range(D_FF // FF_TILE):
            hcol = _dot(xn, w1_ref[:, c * FF_TILE:(c + 1) * FF_TILE])
            hcol = jnp.square(jnp.maximum(hcol, 0.0)).astype(BF16)
            acc = acc + _dot(hcol, w2_ref[c * FF_TILE:(c + 1) * FF_TILE, :])
            yield
        y_ref[rows, :] = acc
        if not with_kv:
            return
        kv = _dot(_rms(acc, gkv_ref[...]).astype(BF16), wkv_ref[...])
        kraw = kv[:, :ATT_KV]
        v_ref[rows, :] = kv[:, ATT_KV:]
        yield
        r = lax.broadcasted_iota(jnp.int32, (ATT_KV, ATT_KV), 0) // ATT_HD
        c = lax.broadcasted_iota(jnp.int32, (ATT_KV, ATT_KV), 1) // ATT_HD
        seg = jnp.where(r == c, 1.0, 0.0).astype(BF16)
        ss = _dot_exactish_right(kraw * kraw, seg)
        k_ref[rows, :] = kraw * lax.rsqrt(ss * (1.0 / ATT_HD) + EPS) * kg_ref[...]

    _run([(chain(0), 0), (chain(1), 2)])


def _mlp(x, g, w1, w2, layer, kv=None):
    N, D = x.shape
    tm = min(ROW_TILE, N)
    assert N % tm == 0
    const = lambda i: (0, 0)
    rows = lambda i: (i, 0)
    once = dict(pipeline_mode=pl.Buffered(1))
    in_specs = [
        pl.BlockSpec((tm, D), rows),
        pl.BlockSpec((1, D), const),
        pl.BlockSpec((None, D, D_FF), lambda i: (layer, 0, 0), **once),
        pl.BlockSpec((None, D_FF, D), lambda i: (layer, 0, 0), **once),
    ]
    out_specs = [pl.BlockSpec((tm, D), rows)]
    out_shape = [jax.ShapeDtypeStruct((N, D), F32)]
    args = [x, g, w1, w2]
    if kv is not None:
        in_specs += [pl.BlockSpec((1, D), const), pl.BlockSpec((D, 2 * ATT_KV), const, **once),
                     pl.BlockSpec((1, ATT_KV), const)]
        out_specs += [pl.BlockSpec((tm, ATT_KV), rows)] * 2
        out_shape += [jax.ShapeDtypeStruct((N, ATT_KV), F32)] * 2
        args += list(kv)
    vmem = _vmem_limit(2 * _nbytes((tm, D), F32), _nbytes(w1.shape[1:], BF16) // 2,
                       _nbytes(w2.shape[1:], BF16) // 2, 2 * _nbytes((tm, FF_TILE), F32))
    out = pl.pallas_call(
        _mlp_kernel,
        grid=(N // tm,),
        in_specs=in_specs,
        out_specs=out_specs,
        out_shape=out_shape,
        compiler_params=pltpu.CompilerParams(
            dimension_semantics=("arbitrary",), vmem_limit_bytes=vmem),
        name="sqrelu_mlp",
    )(*args)
    return out[0] if kv is None else out


def _pair_queries(qraw, pr, qscale, col0=0):
    TQ = qraw.shape[0]
    lo = lax.broadcasted_iota(jnp.int32, (TQ, LANES), 1) < ATT_HD
    out = []
    for e in range(2):
        kvh = 2 * pr + e
        for g in range(ATT_GROUP):
            cc, half = divmod(g, 2)
            c0 = (2 * kvh + cc) * LANES - col0
            q2 = qraw[:, c0:c0 + LANES]
            qm = jnp.where(lo, q2, 0.0) if half == 0 else jnp.where(lo, 0.0, q2)
            ss = jnp.sum(qm * qm, axis=1, keepdims=True)
            qn = qm * lax.rsqrt(ss * (1.0 / ATT_HD) + EPS)
            if qscale is not None:
                qn = qn * qscale
            out.append(qn if half == e else pltpu.roll(qn, ATT_HD, 1))
    return out


def _pair_outputs(o_heads, rden, pr, store):
    TQ = o_heads[0].shape[0]
    lo = lax.broadcasted_iota(jnp.int32, (TQ, LANES), 1) < ATT_HD
    for e in range(2):
        kvh = 2 * pr + e
        for cc in range(2):
            tiles = []
            for half in range(2):
                o = o_heads[e * ATT_GROUP + 2 * cc + half] * rden[e * ATT_GROUP + 2 * cc + half]
                tiles.append(o if half == e else pltpu.roll(o, ATT_HD, 1))
            store((2 * kvh + cc) * LANES, jnp.where(lo, tiles[0], tiles[1]))


def _softmax_with_sink(parts, sink):
    assert all(s.shape == parts[0].shape for s in parts)
    M = jnp.maximum(sink, jnp.max(functools.reduce(jnp.maximum, parts), axis=1, keepdims=True))
    ps = [jnp.exp2(s - M) for s in parts]
    den = jnp.exp2(sink - M) + jnp.sum(functools.reduce(jnp.add, ps), axis=1, keepdims=True)
    return ps, 1.0 / den


def _prompt_attn_kernel(sinks_ref, x_ref, g_ref, wq_ref, qg_ref, kp_ref, kc_ref, vp_ref, vc_ref, wo_ref,
                        y_ref):
    TQ = WINDOW
    TR = x_ref.shape[0]
    kj = lax.broadcasted_iota(jnp.int32, (WINDOW + TQ, TQ), 0)
    qi = lax.broadcasted_iota(jnp.int32, (WINDOW + TQ, TQ), 1)
    band = (qi + WINDOW - kj >= 0) & (qi - kj <= 0)
    band_first = band & ((kj >= WINDOW) | (pl.program_id(1) > 0))
    kscale = qg_ref[...] * (ATT_HD ** -0.5 * LOG2E)
    slabs = []
    for pr in range(ATT_KVH // 2):
        sl = slice(pr * LANES, (pr + 1) * LANES)
        slabs.append(((jnp.concatenate([kp_ref[:, sl], kc_ref[:, sl]], axis=0) * kscale).astype(BF16),
                      jnp.concatenate([vp_ref[:, sl], vc_ref[:, sl]], axis=0).T))

    PW = 2 * ATT_GROUP * ATT_HD
    zeros_hd = jnp.zeros((ATT_HD, TQ), F32)
    ones_rows = jnp.ones((2 * SUBLANES, WINDOW + TQ), F32)

    def chain(cb):
        nsb = ATTN_CHAIN_BLOCKS
        rows = slice(cb * nsb * TQ, (cb + 1) * nsb * TQ)
        x = x_ref[rows, :]
        xn = _rms(x, g_ref[...]).astype(BF16)
        acc = x
        for pr in range(ATT_KVH // 2):
            kslab, vslab_t = slabs[pr]
            qt_all = _dot(xn, wq_ref[:, pr * PW:(pr + 1) * PW]).T
            yield
            outs = []
            for i in range(nsb):
                sb = cb * nsb + i
                keys = slice(sb * TQ, sb * TQ + WINDOW + TQ)
                mask = band_first if sb == 0 else band
                qt = qt_all[:, i * TQ:(i + 1) * TQ]
                tiles, sink_rows = [], []
                for e in range(2):
                    for g in range(ATT_GROUP):
                        blk = qt[(e * ATT_GROUP + g) * ATT_HD:(e * ATT_GROUP + g + 1) * ATT_HD, :]
                        qn = blk * lax.rsqrt(jnp.sum(blk * blk, axis=0, keepdims=True) * (1.0 / ATT_HD) + EPS)
                        tiles.append(jnp.concatenate([qn, zeros_hd] if e == 0 else [zeros_hd, qn], axis=0))
                        sink_rows.append(jnp.full((1, TQ), sinks_ref[(2 * pr + e) * ATT_GROUP + g] * LOG2E, F32))
                qmat = jnp.concatenate(tiles, axis=1).astype(BF16)
                st = _dot(kslab[keys, :], qmat)
                yield
                parts = []
                for e in range(2):
                    ps, ms = [], []
                    for g in range(ATT_GROUP):
                        c0 = (e * ATT_GROUP + g) * TQ
                        s_h = jnp.where(mask, st[:, c0:c0 + TQ], -jnp.inf)
                        m_h = jnp.maximum(jnp.max(s_h, axis=0, keepdims=True), sink_rows[e * ATT_GROUP + g])
                        ps.append(jnp.exp2(s_h - m_h).astype(BF16))
                        ms.append(m_h)
                    p_e = jnp.concatenate(ps, axis=1)
                    v_aug = jnp.concatenate([vslab_t[e * ATT_HD:(e + 1) * ATT_HD, keys], ones_rows],
                                            axis=0).astype(BF16)
                    ot = _dot(v_aug, p_e)
                    yield
                    sink_term = jnp.exp2(jnp.concatenate(sink_rows[e * ATT_GROUP:(e + 1) * ATT_GROUP], axis=1)
                                         - jnp.concatenate(ms, axis=1))
                    on = ot[0:ATT_HD, :] * (1.0 / (ot[ATT_HD:ATT_HD + 1, :] + sink_term))
                    parts += [on[:, g * TQ:(g + 1) * TQ] for g in range(ATT_GROUP)]
                outs.append(jnp.concatenate(parts, axis=0))
            o_pair = jnp.concatenate(outs, axis=1).T.astype(BF16)
            acc = acc + _dot(o_pair, wo_ref[pr * PW:(pr + 1) * PW, :])
            yield
        y_ref[rows, :] = acc

    _run([(chain(cb), 0) for cb in range(TR // (ATTN_CHAIN_BLOCKS * TQ))])


def _prompt_attn(x, k, v, sinks, g, wq, qg, wo):
    B, L, D = x.shape
    TR = ATTN_ROWS
    per = TR // WINDOW
    assert L % TR == 0 and TR % WINDOW == 0
    const = lambda b, i: (0, 0)
    cur = lambda b, i: (b, i, 0)
    prev = lambda b, i: (b, jnp.maximum(i * per - 1, 0), 0)
    vmem = _vmem_limit(2 * _nbytes((TR, D), F32), _nbytes(wq.shape, BF16), _nbytes(wo.shape, BF16),
                       4 * _nbytes((TR, ATT_KV), F32), 8 * _nbytes((8 * WINDOW, 2 * WINDOW), F32))
    return pl.pallas_call(
        _prompt_attn_kernel,
        grid=(B, L // TR),
        in_specs=[
            pl.BlockSpec(memory_space=pltpu.SMEM),
            pl.BlockSpec((None, TR, D), cur),
            pl.BlockSpec((1, D), const),
            pl.BlockSpec((D, D), const),
            pl.BlockSpec((1, LANES), const),
            pl.BlockSpec((None, WINDOW, ATT_KV), prev),
            pl.BlockSpec((None, TR, ATT_KV), cur),
            pl.BlockSpec((None, WINDOW, ATT_KV), prev),
            pl.BlockSpec((None, TR, ATT_KV), cur),
            pl.BlockSpec((D, D), const),
        ],
        out_specs=pl.BlockSpec((None, TR, D), cur),
        out_shape=jax.ShapeDtypeStruct((B, L, D), F32),
        compiler_params=pltpu.CompilerParams(
            dimension_semantics=("arbitrary", "arbitrary"), vmem_limit_bytes=vmem),
        name="prompt_window_attention",
    )(sinks, x, g, wq, qg, k, k, v, v, wo)


def _sample_attn_kernel(sinks_ref, x_ref, g_ref, wq_ref, qg_ref, kc_ref, vc_ref, wk_ref, wv_ref, wo_ref,
                        y_ref, nk_ref, nv_ref, o_s, *, n_valid):
    T = SAMPLE_PAD
    nb = wk_ref.shape[0]
    R = nb * T
    assert R == WINDOW
    x = x_ref[...]
    q = _dot(_rms(x, g_ref[...]).astype(BF16), wq_ref[...])
    qscale = qg_ref[...] * (ATT_HD ** -0.5 * LOG2E)
    knew = kc_ref[...]
    vnew = vc_ref[...]

    cmask = (lax.broadcasted_iota(jnp.int32, (R, WINDOW), 1)
             >= lax.broadcasted_iota(jnp.int32, (R, WINDOW), 0) % T)
    row = lax.broadcasted_iota(jnp.int32, (R, R), 0)
    col = lax.broadcasted_iota(jnp.int32, (R, R), 1)
    nmask = (row // T == col // T) & (col % T <= row % T) & (col % T < n_valid)

    def store(off, val):
        o_s[:, off:off + LANES] = val.astype(BF16)

    def regroup(per_seq, i):
        return jnp.concatenate([per_seq[a][i * T:(i + 1) * T, :] for a in range(nb)], axis=0)

    def pair(pr):
        sl = slice(pr * LANES, (pr + 1) * LANES)
        qs = _pair_queries(q, pr, qscale)
        S_new = lax.dot_general(jnp.concatenate(qs, axis=0).astype(BF16), knew[:, sl].astype(BF16),
                                (((1,), (1,)), ((), ())), preferred_element_type=F32)
        yield
        sc = []
        for a in range(nb):
            q_a = jnp.concatenate([qi[T * a:T * a + T, :] for qi in qs], axis=0).astype(BF16)
            sc.append(_dot(q_a, wk_ref[a, sl, :].astype(BF16)))
            if a % 4 == 3:
                yield
        pcs, pns, rden = [], [], []
        for i in range(2 * ATT_GROUP):
            s_c = jnp.where(cmask, regroup(sc, i), -jnp.inf)
            s_n = jnp.where(nmask, S_new[i * R:(i + 1) * R, :], -jnp.inf)
            (p_c, p_n), r = _softmax_with_sink([s_c, s_n], sinks_ref[pr * 2 * ATT_GROUP + i] * LOG2E)
            pcs.append(p_c)
            pns.append(p_n)
            rden.append(r)
            if i % 2 == 1:
                yield
        O_new = _dot(jnp.concatenate(pns, axis=0).astype(BF16), vnew[:, sl].astype(BF16))
        yield
        oc = []
        for a in range(nb):
            p_a = jnp.concatenate([pc[T * a:T * a + T, :] for pc in pcs], axis=0).astype(BF16)
            oc.append(lax.dot_general(p_a, wv_ref[a, sl, :].astype(BF16), (((1,), (1,)), ((), ())),
                                      preferred_element_type=F32))
            if a % 4 == 3:
                yield
        _pair_outputs([O_new[i * R:(i + 1) * R, :] + regroup(oc, i) for i in range(2 * ATT_GROUP)],
                      rden, pr, store)
        yield

    def roll_cache():
        keep = lax.broadcasted_iota(jnp.int32, (ATT_KV, WINDOW), 1) < WINDOW - n_valid
        k_parts = _split3(knew.T)
        v_parts = _split3(vnew.T)
        group = 4
        rr = lax.broadcasted_iota(jnp.int32, (R, group * WINDOW), 0)
        cc = lax.broadcasted_iota(jnp.int32, (R, group * WINDOW), 1)
        hit = (cc % WINDOW == WINDOW - n_valid + rr % T) & (rr % T < n_valid)
        seq_off = rr // T - cc // WINDOW
        yield
        for g0 in range(0, nb, group):
            place = jnp.where(hit & (seq_off == g0), 1.0, 0.0).astype(BF16)
            new_k = _dot(k_parts[0], place) + _dot(k_parts[1], place) + _dot(k_parts[2], place)
            new_v = _dot(v_parts[0], place) + _dot(v_parts[1], place) + _dot(v_parts[2], place)
            for j in range(group):
                lanes = slice(j * WINDOW, (j + 1) * WINDOW)
                nk_ref[g0 + j] = jnp.where(keep, pltpu.roll(wk_ref[g0 + j], WINDOW - n_valid, 1), new_k[:, lanes])
                nv_ref[g0 + j] = jnp.where(keep, pltpu.roll(wv_ref[g0 + j], WINDOW - n_valid, 1), new_v[:, lanes])
            yield

    _run([(pair(0), 0), (pair(1), 2), (roll_cache(), 0)])
    y_ref[...] = x + _dot(o_s[...], wo_ref[...])


def _sample_attn(x, k, v, win_k, win_v, sinks, g, wq, qg, wo, n_valid):
    NBT, D = x.shape
    nseq = win_k.shape[0]
    nb = SAMPLE_NB
    R = nb * SAMPLE_PAD
    assert nseq % nb == 0 and NBT == nseq * SAMPLE_PAD
    const = lambda i: (0, 0)
    rows = lambda i: (i, 0)
    cache = pl.BlockSpec((nb, ATT_KV, WINDOW), lambda i: (i, 0, 0))
    vmem = _vmem_limit(2 * _nbytes((R, D), F32), _nbytes(wq.shape, BF16), _nbytes(wo.shape, BF16),
                       4 * _nbytes((nb, ATT_KV, WINDOW), F32), 2 * _nbytes((R, D), F32))
    return pl.pallas_call(
        functools.partial(_sample_attn_kernel, n_valid=n_valid),
        grid=(nseq // nb,),
        in_specs=[
            pl.BlockSpec(memory_space=pltpu.SMEM),
            pl.BlockSpec((R, D), rows),
            pl.BlockSpec((1, D), const),
            pl.BlockSpec((D, D), const),
            pl.BlockSpec((1, LANES), const),
            pl.BlockSpec((R, ATT_KV), rows),
            pl.BlockSpec((R, ATT_KV), rows),
            cache, cache,
            pl.BlockSpec((D, D), const),
        ],
        out_specs=[pl.BlockSpec((R, D), rows), cache, cache],
        out_shape=[jax.ShapeDtypeStruct((NBT, D), F32),
                   jax.ShapeDtypeStruct(win_k.shape, F32), jax.ShapeDtypeStruct(win_v.shape, F32)],
        scratch_shapes=[pltpu.VMEM((R, D), BF16)],
        compiler_params=pltpu.CompilerParams(
            dimension_semantics=("arbitrary",), vmem_limit_bytes=vmem),
        name="sample_window_attention",
    )(sinks, x, g, wq, qg, k, v, win_k, win_v, wo)


def kernel(x_prompt, x_sample, state_mlstm_C, state_mlstm_n, state_mlstm_m, cache_win_k, cache_win_v,
           ml_norm_g, ml_w_in, ml_b_i, ml_b_f, ml_head_g, ml_w_out, kv_norm_g, w_kv, k_norm_g,
           att_norm_g, att_w_q, q_norm_g, att_sinks, att_w_o, mlp_norm_g, mlp_w1, mlp_w2):
    B, L, D = x_prompt.shape
    NS, LS, _ = x_sample.shape
    assert ml_w_in.shape[0] == 1 and att_w_q.shape[0] == 1 and mlp_w1.shape[0] == 2

    w_in = ml_w_in[0]
    wqkvo = w_in.astype(BF16)
    wg = jnp.pad(w_in[:, ML_QKVO:], ((0, 0), (0, LANES - 2 * ML_HEADS))).astype(BF16)
    bg = jnp.pad(jnp.concatenate([ml_b_i[0], ml_b_f[0]]), (0, LANES - 2 * ML_HEADS)).reshape(1, LANES)
    ml_g = ml_norm_g[0].reshape(1, D)
    hg = ml_head_g[0].reshape(1, ML_VO)
    wout = ml_w_out[0].astype(BF16)
    mlp_g = mlp_norm_g.reshape(2, 1, D)
    kv_g = kv_norm_g.reshape(1, D)
    kg = jnp.tile(k_norm_g, ATT_KVH).reshape(1, ATT_KV)
    att_g = att_norm_g[0].reshape(1, D)
    qg = jnp.tile(q_norm_g[0], 2).reshape(1, LANES)
    sinks = att_sinks[0]

    xp, p_C, p_n, p_m, w1, w2, wkv, wq, wo = _prompt_mixer(
        x_prompt, ml_g, wqkvo, wg, bg, hg, wout,
        [mlp_w1.reshape(2 * D, D_FF), mlp_w2.reshape(2 * D_FF, D), w_kv, att_w_q[0], att_w_o[0]])
    w1 = w1.reshape(2, D, D_FF)
    w2 = w2.reshape(2, D_FF, D)
    xp, kp, vp = _mlp(xp.reshape(B * L, D), mlp_g[0], w1, w2, 0, kv=(kv_g, wkv, kg))
    xp = _prompt_attn(xp.reshape(B, L, D), kp.reshape(B, L, ATT_KV), vp.reshape(B, L, ATT_KV),
                      sinks, att_g, wq, qg, wo)
    y_prompt = _mlp(xp.reshape(B * L, D), mlp_g[1], w1, w2, 1).reshape(B, L, D)
    p_wk = kp.reshape(B, L, ATT_KV)[:, L - WINDOW:].reshape(B, WINDOW, ATT_KVH, ATT_HD)
    p_wv = vp.reshape(B, L, ATT_KV)[:, L - WINDOW:].reshape(B, WINDOW, ATT_KVH, ATT_HD)

    def pad8(a):
        return jnp.pad(a.reshape(NS, LS, -1), ((0, 0), (0, SAMPLE_PAD - LS), (0, 0))).reshape(NS * SAMPLE_PAD, -1)

    def unpad8(a):
        return a.reshape(NS, SAMPLE_PAD, -1)[:, :LS].reshape(NS * LS, -1)

    m0 = jnp.broadcast_to(state_mlstm_m[0][:, :, None], (NS, ML_HEADS, LANES))
    xs, s_C, s_n, s_m = _sample_mixer(pad8(x_sample), ml_g, wqkvo, wg, bg, hg, wout,
                                      state_mlstm_C[0], state_mlstm_n[0], m0, LS)
    xs, ks, vs = _mlp(unpad8(xs), mlp_g[0], w1, w2, 0, kv=(kv_g, wkv, kg))
    to_t = lambda c: c.transpose(0, 2, 3, 1).reshape(NS, ATT_KV, WINDOW)
    from_t = lambda c: c.reshape(NS, ATT_KVH, ATT_HD, WINDOW).transpose(0, 3, 1, 2)
    xs, s_wk, s_wv = _sample_attn(pad8(xs), pad8(ks), pad8(vs), to_t(cache_win_k), to_t(cache_win_v),
                                  sinks, att_g, wq, qg, wo, LS)
    y_sample = _mlp(unpad8(xs), mlp_g[1], w1, w2, 1).reshape(NS, LS, D)

    return (y_prompt, y_sample,
            p_C[None], p_n[None], p_m[None, :, :, 0], p_wk, p_wv,
            s_C[None], s_n[None], s_m[None, :, :, 0],
            from_t(s_wk), from_t(s_wv))
```

```python
import functools

import jax
import jax.numpy as jnp
from jax import lax
from jax.experimental import pallas as pl
from jax.experimental.pallas import tpu as pltpu

F32 = jnp.float32
BF16 = jnp.bfloat16

D_MODEL = 1024
ML_HEADS = 4
ML_DK = 128
ML_DV = 256
ML_QK = ML_HEADS * ML_DK
ML_VO = ML_HEADS * ML_DV
ML_QKVO = 2 * ML_QK + 2 * ML_VO
GATE_SOFTCAP = 15.0
ATT_HD = 64
ATT_QH = 16
ATT_KVH = 4
ATT_GROUP = 4
ATT_KV = ATT_KVH * ATT_HD
WINDOW = 128
D_FF = 4 * D_MODEL
EPS = 1e-6
LOG2E = 1.4426950408889634

LANES = 128
SUBLANES = 8
VMEM_LIMIT_CAP = 56 * 1024 * 1024

PROMPT_CHUNK = 256
PROMPT_ROWS = 2
PROJ_COLS = 512
SAMPLE_PAD = SUBLANES
SAMPLE_NB = 16
ATTN_ROWS = 1024
ATTN_CHAIN_BLOCKS = 2
ROW_TILE = 1024
FF_TILE = 1024


def _vmem_limit(*block_bytes):
    need = 4 * sum(block_bytes) + (8 << 20)
    return int(min(max(need, 32 << 20), VMEM_LIMIT_CAP))


def _nbytes(shape, dtype):
    n = 1
    for s in shape:
        n *= s
    return n * jnp.dtype(dtype).itemsize


def _rms(x, g):
    return x * lax.rsqrt(jnp.mean(x * x, axis=-1, keepdims=True) + EPS) * g


def _dot(a, b):
    return jnp.dot(a, b, preferred_element_type=F32)


def _split3(x):
    hi = x.astype(BF16)
    r1 = x - hi.astype(F32)
    mid = r1.astype(BF16)
    lo = (r1 - mid.astype(F32)).astype(BF16)
    return hi, mid, lo


def _dot_exactish(m01, x):
    hi, mid, lo = _split3(x)
    return _dot(m01, hi) + _dot(m01, mid) + _dot(m01, lo)


def _pad_rows(x_c, nb, n_valid):
    rows, cols = nb * SAMPLE_PAD, nb * n_valid
    r = lax.broadcasted_iota(jnp.int32, (rows, cols), 0)
    c = lax.broadcasted_iota(jnp.int32, (rows, cols), 1)
    sel = (r % SAMPLE_PAD < n_valid) & (c == (r // SAMPLE_PAD) * n_valid + r % SAMPLE_PAD)
    return _dot_exactish(jnp.where(sel, 1.0, 0.0).astype(BF16), x_c)


def _unpad_rows(x_p, nb, n_valid):
    rows, cols = nb * n_valid, nb * SAMPLE_PAD
    r = lax.broadcasted_iota(jnp.int32, (rows, cols), 0)
    c = lax.broadcasted_iota(jnp.int32, (rows, cols), 1)
    sel = c == (r // n_valid) * SAMPLE_PAD + r % n_valid
    return _dot_exactish(jnp.where(sel, 1.0, 0.0).astype(BF16), x_p)


def _gate_act(z):
    cap = GATE_SOFTCAP * jnp.tanh(z * (1.0 / GATE_SOFTCAP))
    lsig = jnp.minimum(cap, 0.0) - jnp.log1p(jnp.exp(-jnp.abs(cap)))
    lane = lax.broadcasted_iota(jnp.int32, z.shape, 1)
    return jnp.where(lane < ML_HEADS, cap, lsig)


_DONE = object()


def _rounds(chains):
    live = [(c[0], c[1], c[2] if len(c) > 2 else 1) for c in chains]
    rnd = 0
    while live:
        for item in list(live):
            gen, start, stride = item
            if rnd >= start and (rnd - start) % stride == 0 and next(gen, _DONE) is _DONE:
                live.remove(item)
        rnd += 1
        yield


def _run(chains):
    for _ in _rounds(chains):
        pass


def _head_out(hh, po, hg):
    hn = hh * lax.rsqrt(jnp.mean(hh * hh, axis=-1, keepdims=True) + EPS) * hg
    return jax.nn.sigmoid(po) * hn


def _head_slices(p, h):
    q = p[:, h * ML_DK:(h + 1) * ML_DK]
    k = p[:, ML_QK + h * ML_DK:ML_QK + (h + 1) * ML_DK] * (ML_DK ** -0.5)
    v = p[:, 2 * ML_QK + h * ML_DV:2 * ML_QK + (h + 1) * ML_DV]
    po = p[:, 2 * ML_QK + ML_VO + h * ML_DV:2 * ML_QK + ML_VO + (h + 1) * ML_DV]
    return q, k, v, po


def _prompt_mixer_kernel(xc_ref, xnext_ref, g_ref, wqkvo_ref, wg_ref, bg_ref, hg_ref, wout_ref, *rest,
                         chunks_per_seq, n_cast):
    cast_in = rest[:n_cast]
    y_ref, C_ref, n_ref, m_ref = rest[n_cast:n_cast + 4]
    cast_out = rest[n_cast + 4:2 * n_cast + 4]
    p_s, gz_s = rest[2 * n_cast + 4:]
    R, T = xc_ref.shape[0], xc_ref.shape[1]
    f = pl.program_id(0)
    slot = f % 2

    def project(x_ref, s, r):
        xn = _rms(x_ref[r], g_ref[...]).astype(BF16)
        yield
        for j in range(ML_QKVO // PROJ_COLS):
            cols = slice(j * PROJ_COLS, (j + 1) * PROJ_COLS)
            p_s[s, r, :, cols] = _dot(xn, wqkvo_ref[:, cols])
            yield
        gz_s[s, r] = _dot(xn, wg_ref[...])
        yield

    @pl.when(f == 0)
    def _():
        _run([(project(xc_ref, 0, r), 0) for r in range(R)])

    @pl.when(f % chunks_per_seq == 0)
    def _():
        C_ref[...] = jnp.zeros_like(C_ref)
        n_ref[...] = jnp.zeros_like(n_ref)
        m_ref[...] = jnp.zeros_like(m_ref)

    def recurrence(r):
        G = _gate_act(gz_s[slot, r] + bg_ref[...])
        row = lax.broadcasted_iota(jnp.int32, (T, T), 0)
        col = lax.broadcasted_iota(jnp.int32, (T, T), 1)
        causal = col <= row
        Bc = _dot_exactish(jnp.where(causal, 1.0, 0.0).astype(BF16), G)
        yield
        Gt = G.T
        Bt = Bc.T
        yield
        parts = [None] * ML_HEADS

        def head(h):
            def cols(base, width):
                return p_s[slot, r, :, base + h * width:base + (h + 1) * width]

            q = cols(0, ML_DK)
            k = cols(ML_QK, ML_DK) * (ML_DK ** -0.5)
            v = cols(2 * ML_QK, ML_DV)
            qc, kc, vc = q.astype(BF16), k.astype(BF16), v.astype(BF16)
            qk = lax.dot_general(qc, kc, (((1,), (1,)), ((), ())), preferred_element_type=F32)
            yield
            li_col = G[:, h:h + 1]
            b_col = Bc[:, ML_HEADS + h:ML_HEADS + h + 1]
            dmat = jnp.where(causal, b_col - Bt[ML_HEADS + h:ML_HEADS + h + 1, :] + Gt[h:h + 1, :], -jnp.inf)
            m_prev = m_ref[r, h:h + 1, 0:1]
            inter = b_col + m_prev
            m_t = jnp.maximum(inter, jnp.max(dmat, axis=1, keepdims=True))
            s = qk * jnp.exp(dmat - m_t)
            a_inter = jnp.exp(inter - m_t)
            den = (jnp.sum(s, axis=1, keepdims=True)
                   + a_inter * jnp.sum(q * n_ref[r, h:h + 1, :], axis=1, keepdims=True))
            yield
            C = C_ref[r, h]
            num = _dot(s.astype(BF16), vc) + a_inter * _dot(qc, C.astype(BF16))
            yield
            hh = num * (1.0 / jnp.maximum(jnp.abs(den), jnp.exp(-m_t)))
            hs = _head_out(hh, cols(2 * ML_QK + ML_VO, ML_DV), hg_ref[:, h * ML_DV:(h + 1) * ML_DV])
            yield
            parts[h] = _dot(hs.astype(BF16), wout_ref[h * ML_DV:(h + 1) * ML_DV, :])
            yield
            m_new = m_t[T - 1:T, :]
            b_last = b_col[T - 1:T, :]
            decay = jnp.exp(b_last + m_prev - m_new)
            kw = k * jnp.exp(b_last - b_col + li_col - m_new)
            C_ref[r, h] = decay * C + lax.dot_general(kw.astype(BF16), vc, (((0,), (0,)), ((), ())),
                                                      preferred_element_type=F32)
            n_ref[r, h:h + 1, :] = decay * n_ref[r, h:h + 1, :] + jnp.sum(kw, axis=0, keepdims=True)
            m_ref[r, h:h + 1, :] = jnp.broadcast_to(m_new, (1, LANES))
            yield

        yield from _rounds([(head(h), 2 * h) for h in range(ML_HEADS)])
        y_ref[r] = xc_ref[r] + ((parts[0] + parts[1]) + (parts[2] + parts[3]))

    def casts():
        for src, dst in zip(cast_in, cast_out):
            dst[...] = src[...].astype(BF16)
            yield

    _run([(project(xnext_ref, 1 - slot, r), 0) for r in range(R)] + [(recurrence(r), r) for r in range(R)]
         + [(casts(), 0)])


def _prompt_mixer(x, g, wqkvo, wg, bg, hg, wout, to_cast):
    B, L, D = x.shape
    T = PROMPT_CHUNK
    assert L % T == 0
    R = PROMPT_ROWS
    assert B % R == 0
    nc = L // T
    steps = (B // R) * nc
    const = lambda f: (0, 0)
    cur = lambda f: (f // nc, f % nc, 0)
    nxt = lambda f: (jnp.minimum(f + 1, steps - 1) // nc, jnp.minimum(f + 1, steps - 1) % nc, 0)
    once = dict(pipeline_mode=pl.Buffered(1))
    vmem = _vmem_limit(3 * _nbytes((R, T, D), F32), _nbytes(wqkvo.shape, BF16) // 2,
                       _nbytes(wout.shape, BF16) // 2, _nbytes((R, ML_HEADS, ML_DK, ML_DV), F32),
                       2 * _nbytes((R, T, ML_QKVO), F32))
    for w in to_cast:
        assert w.shape[0] % (steps * 2 * SUBLANES) == 0
    cast_specs = [pl.BlockSpec((w.shape[0] // steps, w.shape[1]), lambda f: (f, 0)) for w in to_cast]
    outs = pl.pallas_call(
        functools.partial(_prompt_mixer_kernel, chunks_per_seq=nc, n_cast=len(to_cast)),
        grid=(steps,),
        in_specs=[
            pl.BlockSpec((R, T, D), cur),
            pl.BlockSpec((R, T, D), nxt),
            pl.BlockSpec((1, D), const),
            pl.BlockSpec((D, ML_QKVO), const, **once),
            pl.BlockSpec((D, LANES), const, **once),
            pl.BlockSpec((1, LANES), const),
            pl.BlockSpec((1, ML_VO), const),
            pl.BlockSpec((ML_VO, D), const, **once),
        ] + cast_specs,
        out_specs=[
            pl.BlockSpec((R, T, D), cur),
            pl.BlockSpec((R, ML_HEADS, ML_DK, ML_DV), lambda f: (f // nc, 0, 0, 0)),
            pl.BlockSpec((R, ML_HEADS, ML_DK), lambda f: (f // nc, 0, 0)),
            pl.BlockSpec((R, ML_HEADS, LANES), lambda f: (f // nc, 0, 0)),
        ] + cast_specs,
        out_shape=[
            jax.ShapeDtypeStruct((B, L, D), F32),
            jax.ShapeDtypeStruct((B, ML_HEADS, ML_DK, ML_DV), F32),
            jax.ShapeDtypeStruct((B, ML_HEADS, ML_DK), F32),
            jax.ShapeDtypeStruct((B, ML_HEADS, LANES), F32),
        ] + [jax.ShapeDtypeStruct(w.shape, BF16) for w in to_cast],
        scratch_shapes=[pltpu.VMEM((2, R, T, ML_QKVO), F32), pltpu.VMEM((2, R, T, LANES), F32)],
        compiler_params=pltpu.CompilerParams(
            dimension_semantics=("arbitrary",), vmem_limit_bytes=vmem),
        name="prompt_mlstm_mixer",
    )(x, x, g, wqkvo, wg, bg, hg, wout, *to_cast)
    return outs


def _sample_mixer_kernel(x_ref, g_ref, wqkvo_ref, wg_ref, bg_ref, hg_ref, wout_ref,
                         C0_ref, n0_ref, m0_ref,
                         y_ref, C_ref, n_ref, m_ref, hs_s, *, n_valid):
    T = SAMPLE_PAD
    nb = C0_ref.shape[0]
    R = nb * T
    x = _pad_rows(x_ref[...], nb, n_valid)
    xn = _rms(x, g_ref[...]).astype(BF16)
    p = _dot(xn, wqkvo_ref[...])

    rowt = lax.broadcasted_iota(jnp.int32, (R, LANES), 0) % T
    lane = lax.broadcasted_iota(jnp.int32, (R, LANES), 1)
    G = jnp.where(rowt < n_valid, _gate_act(_dot(xn, wg_ref[...]) + bg_ref[...]),
                  jnp.where(lane < ML_HEADS, -jnp.inf, 0.0))
    row = lax.broadcasted_iota(jnp.int32, (R, R), 0)
    col = lax.broadcasted_iota(jnp.int32, (R, R), 1)
    causal = (row // T == col // T) & (col <= row)
    Bc = _dot_exactish(jnp.where(causal, 1.0, 0.0).astype(BF16), jnp.where(lane < ML_HEADS, 0.0, G))
    Gt = G.T
    Bt = Bc.T

    def per_seq(fn):
        return jnp.concatenate([fn(a) for a in range(nb)], axis=0)

    def seq_last(colvec):
        return per_seq(lambda a: jnp.broadcast_to(colvec[T * a + T - 1:T * a + T, :], (T, 1)))

    def head(h):
        q, k, v, po = _head_slices(p, h)
        li_col = G[:, h:h + 1]
        b_col = Bc[:, ML_HEADS + h:ML_HEADS + h + 1]
        li_row = Gt[h:h + 1, :]
        b_row = Bt[ML_HEADS + h:ML_HEADS + h + 1, :]
        dmat = jnp.where(causal, b_col - (b_row - li_row), -jnp.inf)
        m_prev = per_seq(lambda a: jnp.broadcast_to(m0_ref[a, h:h + 1, 0:1], (T, 1)))
        n_rows = per_seq(lambda a: jnp.broadcast_to(n0_ref[a, h:h + 1, :], (T, ML_DK)))
        qc, kc, vc = q.astype(BF16), k.astype(BF16), v.astype(BF16)
        qk = lax.dot_general(qc, kc, (((1,), (1,)), ((), ())), preferred_element_type=F32)
        yield
        inter = b_col + m_prev
        m_t = jnp.maximum(inter, jnp.max(dmat, axis=1, keepdims=True))
        s = qk * jnp.exp(dmat - m_t)
        a_inter = jnp.exp(inter - m_t)
        yield
        qC = per_seq(lambda a: _dot(q[T * a:T * a + T, :], C0_ref[a, h]))
        yield
        num = _dot(s.astype(BF16), vc) + a_inter * qC
        den = jnp.sum(s, axis=1, keepdims=True) + a_inter * jnp.sum(q * n_rows, axis=1, keepdims=True)
        hh = num * (1.0 / jnp.maximum(jnp.abs(den), jnp.exp(-m_t)))
        hs_s[:, h * ML_DV:(h + 1) * ML_DV] = _head_out(
            hh, po, hg_ref[:, h * ML_DV:(h + 1) * ML_DV]).astype(BF16)
        yield
        m_new = seq_last(m_t)
        b_last = seq_last(b_col)
        decay = jnp.exp(b_last + m_prev - m_new)
        kw = k * jnp.exp(b_last - b_col + li_col - m_new)
        for a in range(nb):
            dec = decay[T * a:T * a + 1, :]
            kw_a = kw[T * a:T * a + T, :]
            upd = lax.dot_general(kw_a, v[T * a:T * a + T, :], (((0,), (0,)), ((), ())),
                                  preferred_element_type=F32)
            C_ref[a, h] = dec * C0_ref[a, h] + upd
            n_ref[a, h:h + 1, :] = dec * n0_ref[a, h:h + 1, :] + jnp.sum(kw_a, axis=0, keepdims=True)
            m_ref[a, h:h + 1, :] = jnp.broadcast_to(m_new[T * a:T * a + 1, :], (1, LANES))
            if a % 4 == 3:
                yield

    _run([(head(h), h) for h in range(ML_HEADS)])
    y_ref[...] = _unpad_rows(x + _dot(hs_s[...], wout_ref[...]), nb, n_valid)


def _sample_mixer(x, g, wqkvo, wg, bg, hg, wout, C0, n0, m0, n_valid):
    NT, D = x.shape
    nseq = C0.shape[0]
    nb = SAMPLE_NB
    R = nb * SAMPLE_PAD
    RC = nb * n_valid
    assert nseq % nb == 0 and NT == nseq * n_valid and RC % SUBLANES == 0
    const = lambda i: (0, 0)
    state_specs = [
        pl.BlockSpec((nb, ML_HEADS, ML_DK, ML_DV), lambda i: (i, 0, 0, 0)),
        pl.BlockSpec((nb, ML_HEADS, ML_DK), lambda i: (i, 0, 0)),
        pl.BlockSpec((nb, ML_HEADS, LANES), lambda i: (i, 0, 0)),
    ]
    vmem = _vmem_limit(2 * _nbytes((R, D), F32), _nbytes(wqkvo.shape, BF16), _nbytes(wout.shape, BF16),
                       2 * _nbytes((nb, ML_HEADS, ML_DK, ML_DV), F32), _nbytes((R, ML_QKVO), F32))
    return pl.pallas_call(
        functools.partial(_sample_mixer_kernel, n_valid=n_valid),
        grid=(nseq // nb,),
        in_specs=[
            pl.BlockSpec((RC, D), lambda i: (i, 0)),
            pl.BlockSpec((1, D), const),
            pl.BlockSpec((D, ML_QKVO), const),
            pl.BlockSpec((D, LANES), const),
            pl.BlockSpec((1, LANES), const),
            pl.BlockSpec((1, ML_VO), const),
            pl.BlockSpec((ML_VO, D), const),
        ] + state_specs,
        out_specs=[pl.BlockSpec((RC, D), lambda i: (i, 0))] + state_specs,
        out_shape=[
            jax.ShapeDtypeStruct((NT, D), F32),
            jax.ShapeDtypeStruct(C0.shape, F32),
            jax.ShapeDtypeStruct(n0.shape, F32),
            jax.ShapeDtypeStruct(m0.shape, F32),
        ],
        scratch_shapes=[pltpu.VMEM((R, ML_VO), BF16)],
        compiler_params=pltpu.CompilerParams(
            dimension_semantics=("arbitrary",), vmem_limit_bytes=vmem),
        name="sample_mlstm_mixer",
    )(x, g, wqkvo, wg, bg, hg, wout, C0, n0, m0)


def _dot_exactish_right(x, m01):
    hi, mid, lo = _split3(x)
    return _dot(hi, m01) + _dot(mid, m01) + _dot(lo, m01)


def _mlp_kernel(x_ref, g_ref, w1_ref, w2_ref, *rest):
    x = x_ref[...]
    xn = _rms(x, g_ref[...]).astype(BF16)
    acc = x
    for c in range(D_FF // FF_TILE):
        hcol = _dot(xn, w1_ref[:, c * FF_TILE:(c + 1) * FF_TILE])
        hcol = jnp.square(jnp.maximum(hcol, 0.0)).astype(BF16)
        acc = acc + _dot(hcol, w2_ref[c * FF_TILE:(c + 1) * FF_TILE, :])
    if len(rest) == 1:
        (y_ref,) = rest
        y_ref[...] = acc
        return
    gkv_ref, wkv_ref, kg_ref, y_ref, k_ref, v_ref = rest
    y_ref[...] = acc
    kv = _dot(_rms(acc, gkv_ref[...]).astype(BF16), wkv_ref[...])
    kraw = kv[:, :ATT_KV]
    v_ref[...] = kv[:, ATT_KV:]
    r = lax.broadcasted_iota(jnp.int32, (ATT_KV, ATT_KV), 0) // ATT_HD
    c = lax.broadcasted_iota(jnp.int32, (ATT_KV, ATT_KV), 1) // ATT_HD
    seg = jnp.where(r == c, 1.0, 0.0).astype(BF16)
    ss = _dot_exactish_right(kraw * kraw, seg)
    k_ref[...] = kraw * lax.rsqrt(ss * (1.0 / ATT_HD) + EPS) * kg_ref[...]


def _mlp(x, g, w1, w2, layer, kv=None):
    N, D = x.shape
    tm = min(ROW_TILE, N)
    assert N % tm == 0
    const = lambda i: (0, 0)
    rows = lambda i: (i, 0)
    once = dict(pipeline_mode=pl.Buffered(1))
    in_specs = [
        pl.BlockSpec((tm, D), rows),
        pl.BlockSpec((1, D), const),
        pl.BlockSpec((None, D, D_FF), lambda i: (layer, 0, 0), **once),
        pl.BlockSpec((None, D_FF, D), lambda i: (layer, 0, 0), **once),
    ]
    out_specs = [pl.BlockSpec((tm, D), rows)]
    out_shape = [jax.ShapeDtypeStruct((N, D), F32)]
    args = [x, g, w1, w2]
    if kv is not None:
        in_specs += [pl.BlockSpec((1, D), const), pl.BlockSpec((D, 2 * ATT_KV), const, **once),
                     pl.BlockSpec((1, ATT_KV), const)]
        out_specs += [pl.BlockSpec((tm, ATT_KV), rows)] * 2
        out_shape += [jax.ShapeDtypeStruct((N, ATT_KV), F32)] * 2
        args += list(kv)
    vmem = _vmem_limit(2 * _nbytes((tm, D), F32), _nbytes(w1.shape[1:], BF16) // 2,
                       _nbytes(w2.shape[1:], BF16) // 2, 2 * _nbytes((tm, FF_TILE), F32))
    out = pl.pallas_call(
        _mlp_kernel,
        grid=(N // tm,),
        in_specs=in_specs,
        out_specs=out_specs,
        out_shape=out_shape,
        compiler_params=pltpu.CompilerParams(
            dimension_semantics=("arbitrary",), vmem_limit_bytes=vmem),
        name="sqrelu_mlp",
    )(*args)
    return out[0] if kv is None else out


def _pair_queries(qraw, pr, qscale, col0=0):
    TQ = qraw.shape[0]
    lo = lax.broadcasted_iota(jnp.int32, (TQ, LANES), 1) < ATT_HD
    out = []
    for e in range(2):
        kvh = 2 * pr + e
        for g in range(ATT_GROUP):
            cc, half = divmod(g, 2)
            c0 = (2 * kvh + cc) * LANES - col0
            q2 = qraw[:, c0:c0 + LANES]
            qm = jnp.where(lo, q2, 0.0) if half == 0 else jnp.where(lo, 0.0, q2)
            ss = jnp.sum(qm * qm, axis=1, keepdims=True)
            qn = qm * lax.rsqrt(ss * (1.0 / ATT_HD) + EPS)
            if qscale is not None:
                qn = qn * qscale
            out.append(qn if half == e else pltpu.roll(qn, ATT_HD, 1))
    return out


def _pair_outputs(o_heads, rden, pr, store):
    TQ = o_heads[0].shape[0]
    lo = lax.broadcasted_iota(jnp.int32, (TQ, LANES), 1) < ATT_HD
    for e in range(2):
        kvh = 2 * pr + e
        for cc in range(2):
            tiles = []
            for half in range(2):
                o = o_heads[e * ATT_GROUP + 2 * cc + half] * rden[e * ATT_GROUP + 2 * cc + half]
                tiles.append(o if half == e else pltpu.roll(o, ATT_HD, 1))
            store((2 * kvh + cc) * LANES, jnp.where(lo, tiles[0], tiles[1]))


def _softmax_with_sink(parts, sink):
    assert all(s.shape == parts[0].shape for s in parts)
    M = jnp.maximum(sink, jnp.max(functools.reduce(jnp.maximum, parts), axis=1, keepdims=True))
    ps = [jnp.exp2(s - M) for s in parts]
    den = jnp.exp2(sink - M) + jnp.sum(functools.reduce(jnp.add, ps), axis=1, keepdims=True)
    return ps, 1.0 / den


def _prompt_attn_kernel(sinks_ref, x_ref, g_ref, wq_ref, qg_ref, kp_ref, kc_ref, vp_ref, vc_ref, wo_ref,
                        y_ref):
    TQ = WINDOW
    TR = x_ref.shape[0]
    kj = lax.broadcasted_iota(jnp.int32, (WINDOW + TQ, TQ), 0)
    qi = lax.broadcasted_iota(jnp.int32, (WINDOW + TQ, TQ), 1)
    band = (qi + WINDOW - kj >= 0) & (qi - kj <= 0)
    band_first = band & ((kj >= WINDOW) | (pl.program_id(1) > 0))
    kscale = qg_ref[...] * (ATT_HD ** -0.5 * LOG2E)
    slabs = []
    for pr in range(ATT_KVH // 2):
        sl = slice(pr * LANES, (pr + 1) * LANES)
        slabs.append(((jnp.concatenate([kp_ref[:, sl], kc_ref[:, sl]], axis=0) * kscale).astype(BF16),
                      jnp.concatenate([vp_ref[:, sl], vc_ref[:, sl]], axis=0).T))

    PW = 2 * ATT_GROUP * ATT_HD
    zeros_hd = jnp.zeros((ATT_HD, TQ), F32)
    ones_rows = jnp.ones((2 * SUBLANES, WINDOW + TQ), F32)

    def chain(cb):
        nsb = ATTN_CHAIN_BLOCKS
        rows = slice(cb * nsb * TQ, (cb + 1) * nsb * TQ)
        x = x_ref[rows, :]
        xn = _rms(x, g_ref[...]).astype(BF16)
        acc = x
        for pr in range(ATT_KVH // 2):
            kslab, vslab_t = slabs[pr]
            qt_all = _dot(xn, wq_ref[:, pr * PW:(pr + 1) * PW]).T
            yield
            outs = []
            for i in range(nsb):
                sb = cb * nsb + i
                keys = slice(sb * TQ, sb * TQ + WINDOW + TQ)
                mask = band_first if sb == 0 else band
                qt = qt_all[:, i * TQ:(i + 1) * TQ]
                tiles, sink_rows = [], []
                for e in range(2):
                    for g in range(ATT_GROUP):
                        blk = qt[(e * ATT_GROUP + g) * ATT_HD:(e * ATT_GROUP + g + 1) * ATT_HD, :]
                        qn = blk * lax.rsqrt(jnp.sum(blk * blk, axis=0, keepdims=True) * (1.0 / ATT_HD) + EPS)
                        tiles.append(jnp.concatenate([qn, zeros_hd] if e == 0 else [zeros_hd, qn], axis=0))
                        sink_rows.append(jnp.full((1, TQ), sinks_ref[(2 * pr + e) * ATT_GROUP + g] * LOG2E, F32))
                qmat = jnp.concatenate(tiles, axis=1).astype(BF16)
                st = _dot(kslab[keys, :], qmat)
                yield
                parts = []
                for e in range(2):
                    ps, ms = [], []
                    for g in range(ATT_GROUP):
                        c0 = (e * ATT_GROUP + g) * TQ
                        s_h = jnp.where(mask, st[:, c0:c0 + TQ], -jnp.inf)
                        m_h = jnp.maximum(jnp.max(s_h, axis=0, keepdims=True), sink_rows[e * ATT_GROUP + g])
                        ps.append(jnp.exp2(s_h - m_h).astype(BF16))
                        ms.append(m_h)
                    p_e = jnp.concatenate(ps, axis=1)
                    v_aug = jnp.concatenate([vslab_t[e * ATT_HD:(e + 1) * ATT_HD, keys], ones_rows],
                                            axis=0).astype(BF16)
                    ot = _dot(v_aug, p_e)
                    yield
                    sink_term = jnp.exp2(jnp.concatenate(sink_rows[e * ATT_GROUP:(e + 1) * ATT_GROUP], axis=1)
                                         - jnp.concatenate(ms, axis=1))
                    on = ot[0:ATT_HD, :] * (1.0 / (ot[ATT_HD:ATT_HD + 1, :] + sink_term))
                    parts += [on[:, g * TQ:(g + 1) * TQ] for g in range(ATT_GROUP)]
                outs.append(jnp.concatenate(parts, axis=0))
            o_pair = jnp.concatenate(outs, axis=1).T.astype(BF16)
            acc = acc + _dot(o_pair, wo_ref[pr * PW:(pr + 1) * PW, :])
            yield
        y_ref[rows, :] = acc

    _run([(chain(cb), 0) for cb in range(TR // (ATTN_CHAIN_BLOCKS * TQ))])


def _prompt_attn(x, k, v, sinks, g, wq, qg, wo):
    B, L, D = x.shape
    TR = ATTN_ROWS
    per = TR // WINDOW
    assert L % TR == 0 and TR % WINDOW == 0
    const = lambda b, i: (0, 0)
    cur = lambda b, i: (b, i, 0)
    prev = lambda b, i: (b, jnp.maximum(i * per - 1, 0), 0)
    vmem = _vmem_limit(2 * _nbytes((TR, D), F32), _nbytes(wq.shape, BF16), _nbytes(wo.shape, BF16),
                       4 * _nbytes((TR, ATT_KV), F32), 8 * _nbytes((8 * WINDOW, 2 * WINDOW), F32))
    return pl.pallas_call(
        _prompt_attn_kernel,
        grid=(B, L // TR),
        in_specs=[
            pl.BlockSpec(memory_space=pltpu.SMEM),
            pl.BlockSpec((None, TR, D), cur),
            pl.BlockSpec((1, D), const),
            pl.BlockSpec((D, D), const),
            pl.BlockSpec((1, LANES), const),
            pl.BlockSpec((None, WINDOW, ATT_KV), prev),
            pl.BlockSpec((None, TR, ATT_KV), cur),
            pl.BlockSpec((None, WINDOW, ATT_KV), prev),
            pl.BlockSpec((None, TR, ATT_KV), cur),
            pl.BlockSpec((D, D), const),
        ],
        out_specs=pl.BlockSpec((None, TR, D), cur),
        out_shape=jax.ShapeDtypeStruct((B, L, D), F32),
        compiler_params=pltpu.CompilerParams(
            dimension_semantics=("arbitrary", "arbitrary"), vmem_limit_bytes=vmem),
        name="prompt_window_attention",
    )(sinks, x, g, wq, qg, k, k, v, v, wo)


def _sample_attn_kernel(sinks_ref, x_ref, g_ref, wq_ref, qg_ref, kc_ref, vc_ref, wk_ref, wv_ref, wo_ref,
                        y_ref, nk_ref, nv_ref, o_s, *, n_valid):
    T = SAMPLE_PAD
    nb = wk_ref.shape[0]
    R = nb * T
    assert R == WINDOW
    x = _pad_rows(x_ref[...], nb, n_valid)
    q = _dot(_rms(x, g_ref[...]).astype(BF16), wq_ref[...])
    qscale = qg_ref[...] * (ATT_HD ** -0.5 * LOG2E)
    knew = _pad_rows(kc_ref[...], nb, n_valid)
    vnew = _pad_rows(vc_ref[...], nb, n_valid)

    cmask = (lax.broadcasted_iota(jnp.int32, (R, WINDOW), 1)
             >= lax.broadcasted_iota(jnp.int32, (R, WINDOW), 0) % T)
    row = lax.broadcasted_iota(jnp.int32, (R, R), 0)
    col = lax.broadcasted_iota(jnp.int32, (R, R), 1)
    nmask = (row // T == col // T) & (col % T <= row % T) & (col % T < n_valid)

    def store(off, val):
        o_s[:, off:off + LANES] = val.astype(BF16)

    def regroup(per_seq, i):
        return jnp.concatenate([per_seq[a][i * T:(i + 1) * T, :] for a in range(nb)], axis=0)

    def pair(pr):
        sl = slice(pr * LANES, (pr + 1) * LANES)
        qs = _pair_queries(q, pr, qscale)
        S_new = lax.dot_general(jnp.concatenate(qs, axis=0).astype(BF16), knew[:, sl].astype(BF16),
                                (((1,), (1,)), ((), ())), preferred_element_type=F32)
        yield
        sc = []
        for a in range(nb):
            q_a = jnp.concatenate([qi[T * a:T * a + T, :] for qi in qs], axis=0).astype(BF16)
            sc.append(_dot(q_a, wk_ref[a, sl, :].astype(BF16)))
            if a % 4 == 3:
                yield
        pcs, pns, rden = [], [], []
        for i in range(2 * ATT_GROUP):
            s_c = jnp.where(cmask, regroup(sc, i), -jnp.inf)
            s_n = jnp.where(nmask, S_new[i * R:(i + 1) * R, :], -jnp.inf)
            (p_c, p_n), r = _softmax_with_sink([s_c, s_n], sinks_ref[pr * 2 * ATT_GROUP + i] * LOG2E)
            pcs.append(p_c)
            pns.append(p_n)
            rden.append(r)
            if i % 2 == 1:
                yield
        O_new = _dot(jnp.concatenate(pns, axis=0).astype(BF16), vnew[:, sl].astype(BF16))
        yield
        oc = []
        for a in range(nb):
            p_a = jnp.concatenate([pc[T * a:T * a + T, :] for pc in pcs], axis=0).astype(BF16)
            oc.append(lax.dot_general(p_a, wv_ref[a, sl, :].astype(BF16), (((1,), (1,)), ((), ())),
                                      preferred_element_type=F32))
            if a % 4 == 3:
                yield
        _pair_outputs([O_new[i * R:(i + 1) * R, :] + regroup(oc, i) for i in range(2 * ATT_GROUP)],
                      rden, pr, store)
        yield

    def roll_cache():
        keep = lax.broadcasted_iota(jnp.int32, (ATT_KV, WINDOW), 1) < WINDOW - n_valid
        k_parts = _split3(knew.T)
        v_parts = _split3(vnew.T)
        group = 4
        rr = lax.broadcasted_iota(jnp.int32, (R, group * WINDOW), 0)
        cc = lax.broadcasted_iota(jnp.int32, (R, group * WINDOW), 1)
        hit = (cc % WINDOW == WINDOW - n_valid + rr % T) & (rr % T < n_valid)
        seq_off = rr // T - cc // WINDOW
        yield
        for g0 in range(0, nb, group):
            place = jnp.where(hit & (seq_off == g0), 1.0, 0.0).astype(BF16)
            new_k = _dot(k_parts[0], place) + _dot(k_parts[1], place) + _dot(k_parts[2], place)
            new_v = _dot(v_parts[0], place) + _dot(v_parts[1], place) + _dot(v_parts[2], place)
            for j in range(group):
                lanes = slice(j * WINDOW, (j + 1) * WINDOW)
                nk_ref[g0 + j] = jnp.where(keep, pltpu.roll(wk_ref[g0 + j], WINDOW - n_valid, 1), new_k[:, lanes])
                nv_ref[g0 + j] = jnp.where(keep, pltpu.roll(wv_ref[g0 + j], WINDOW - n_valid, 1), new_v[:, lanes])
            yield

    _run([(pair(0), 0), (pair(1), 2), (roll_cache(), 0)])
    y_ref[...] = _unpad_rows(x + _dot(o_s[...], wo_ref[...]), nb, n_valid)


def _sample_attn(x, k, v, win_k, win_v, sinks, g, wq, qg, wo, n_valid):
    NT, D = x.shape
    nseq = win_k.shape[0]
    nb = SAMPLE_NB
    R = nb * SAMPLE_PAD
    RC = nb * n_valid
    assert nseq % nb == 0 and NT == nseq * n_valid and RC % SUBLANES == 0
    const = lambda i: (0, 0)
    rows = lambda i: (i, 0)
    cache = pl.BlockSpec((nb, ATT_KV, WINDOW), lambda i: (i, 0, 0))
    vmem = _vmem_limit(2 * _nbytes((R, D), F32), _nbytes(wq.shape, BF16), _nbytes(wo.shape, BF16),
                       4 * _nbytes((nb, ATT_KV, WINDOW), F32), 2 * _nbytes((R, D), F32))
    return pl.pallas_call(
        functools.partial(_sample_attn_kernel, n_valid=n_valid),
        grid=(nseq // nb,),
        in_specs=[
            pl.BlockSpec(memory_space=pltpu.SMEM),
            pl.BlockSpec((RC, D), rows),
            pl.BlockSpec((1, D), const),
            pl.BlockSpec((D, D), const),
            pl.BlockSpec((1, LANES), const),
            pl.BlockSpec((RC, ATT_KV), rows),
            pl.BlockSpec((RC, ATT_KV), rows),
            cache, cache,
            pl.BlockSpec((D, D), const),
        ],
        out_specs=[pl.BlockSpec((RC, D), rows), cache, cache],
        out_shape=[jax.ShapeDtypeStruct((NT, D), F32),
                   jax.ShapeDtypeStruct(win_k.shape, F32), jax.ShapeDtypeStruct(win_v.shape, F32)],
        scratch_shapes=[pltpu.VMEM((R, D), BF16)],
        compiler_params=pltpu.CompilerParams(
            dimension_semantics=("arbitrary",), vmem_limit_bytes=vmem),
        name="sample_window_attention",
    )(sinks, x, g, wq, qg, k, v, win_k, win_v, wo)


def kernel(x_prompt, x_sample, state_mlstm_C, state_mlstm_n, state_mlstm_m, cache_win_k, cache_win_v,
           ml_norm_g, ml_w_in, ml_b_i, ml_b_f, ml_head_g, ml_w_out, kv_norm_g, w_kv, k_norm_g,
           att_norm_g, att_w_q, q_norm_g, att_sinks, att_w_o, mlp_norm_g, mlp_w1, mlp_w2):
    B, L, D = x_prompt.shape
    NS, LS, _ = x_sample.shape
    assert ml_w_in.shape[0] == 1 and att_w_q.shape[0] == 1 and mlp_w1.shape[0] == 2

    w_in = ml_w_in[0]
    wqkvo = w_in.astype(BF16)
    wg = jnp.pad(w_in[:, ML_QKVO:], ((0, 0), (0, LANES - 2 * ML_HEADS))).astype(BF16)
    bg = jnp.pad(jnp.concatenate([ml_b_i[0], ml_b_f[0]]), (0, LANES - 2 * ML_HEADS)).reshape(1, LANES)
    ml_g = ml_norm_g[0].reshape(1, D)
    hg = ml_head_g[0].reshape(1, ML_VO)
    wout = ml_w_out[0].astype(BF16)
    mlp_g = mlp_norm_g.reshape(2, 1, D)
    kv_g = kv_norm_g.reshape(1, D)
    kg = jnp.tile(k_norm_g, ATT_KVH).reshape(1, ATT_KV)
    att_g = att_norm_g[0].reshape(1, D)
    qg = jnp.tile(q_norm_g[0], 2).reshape(1, LANES)
    sinks = att_sinks[0]

    xp, p_C, p_n, p_m, w1, w2, wkv, wq, wo = _prompt_mixer(
        x_prompt, ml_g, wqkvo, wg, bg, hg, wout,
        [mlp_w1.reshape(2 * D, D_FF), mlp_w2.reshape(2 * D_FF, D), w_kv, att_w_q[0], att_w_o[0]])
    w1 = w1.reshape(2, D, D_FF)
    w2 = w2.reshape(2, D_FF, D)
    xp, kp, vp = _mlp(xp.reshape(B * L, D), mlp_g[0], w1, w2, 0, kv=(kv_g, wkv, kg))
    xp = _prompt_attn(xp.reshape(B, L, D), kp.reshape(B, L, ATT_KV), vp.reshape(B, L, ATT_KV),
                      sinks, att_g, wq, qg, wo)
    y_prompt = _mlp(xp.reshape(B * L, D), mlp_g[1], w1, w2, 1).reshape(B, L, D)
    p_wk = kp.reshape(B, L, ATT_KV)[:, L - WINDOW:].reshape(B, WINDOW, ATT_KVH, ATT_HD)
    p_wv = vp.reshape(B, L, ATT_KV)[:, L - WINDOW:].reshape(B, WINDOW, ATT_KVH, ATT_HD)

    m0 = jnp.broadcast_to(state_mlstm_m[0][:, :, None], (NS, ML_HEADS, LANES))
    xs, s_C, s_n, s_m = _sample_mixer(x_sample.reshape(NS * LS, D), ml_g, wqkvo, wg, bg, hg, wout,
                                      state_mlstm_C[0], state_mlstm_n[0], m0, LS)
    xs, ks, vs = _mlp(xs, mlp_g[0], w1, w2, 0, kv=(kv_g, wkv, kg))
    to_t = lambda c: c.transpose(0, 2, 3, 1).reshape(NS, ATT_KV, WINDOW)
    from_t = lambda c: c.reshape(NS, ATT_KVH, ATT_HD, WINDOW).transpose(0, 3, 1, 2)
    xs, s_wk, s_wv = _sample_attn(xs, ks, vs, to_t(cache_win_k), to_t(cache_win_v),
                                  sinks, att_g, wq, qg, wo, LS)
    y_sample = _mlp(xs, mlp_g[1], w1, w2, 1).reshape(NS, LS, D)

    return (y_prompt, y_sample,
            p_C[None], p_n[None], p_m[None, :, :, 0], p_wk, p_wv,
            s_C[None], s_n[None], s_m[None, :, :, 0],
            from_t(s_wk), from_t(s_wv))
```

```python
import functools

import jax
import jax.numpy as jnp
from jax import lax
from jax.experimental import pallas as pl
from jax.experimental.pallas import tpu as pltpu

F32 = jnp.float32
BF16 = jnp.bfloat16

D_MODEL = 1024
ML_HEADS = 4
ML_DK = 128
ML_DV = 256
ML_QK = ML_HEADS * ML_DK
ML_VO = ML_HEADS * ML_DV
ML_QKVO = 2 * ML_QK + 2 * ML_VO
GATE_SOFTCAP = 15.0
ATT_HD = 64
ATT_QH = 16
ATT_KVH = 4
ATT_GROUP = 4
ATT_KV = ATT_KVH * ATT_HD
WINDOW = 128
D_FF = 4 * D_MODEL
EPS = 1e-6
LOG2E = 1.4426950408889634

LANES = 128
SUBLANES = 8
VMEM_LIMIT_CAP = 56 * 1024 * 1024

PROMPT_CHUNK = 256
PROMPT_ROWS = 2
PROJ_COLS = 512
SAMPLE_PAD = SUBLANES
SAMPLE_NB = 16
ATTN_ROWS = 1024
ATTN_CHAIN_BLOCKS = 2
ROW_TILE = 1024
FF_TILE = 1024


def _vmem_limit(*block_bytes):
    need = 4 * sum(block_bytes) + (8 << 20)
    return int(min(max(need, 32 << 20), VMEM_LIMIT_CAP))


def _nbytes(shape, dtype):
    n = 1
    for s in shape:
        n *= s
    return n * jnp.dtype(dtype).itemsize


def _rms(x, g):
    return x * lax.rsqrt(jnp.mean(x * x, axis=-1, keepdims=True) + EPS) * g


def _dot(a, b):
    return jnp.dot(a, b, preferred_element_type=F32)


def _split3(x):
    hi = x.astype(BF16)
    r1 = x - hi.astype(F32)
    mid = r1.astype(BF16)
    lo = (r1 - mid.astype(F32)).astype(BF16)
    return hi, mid, lo


def _dot_exactish(m01, x):
    hi, mid, lo = _split3(x)
    return _dot(m01, hi) + _dot(m01, mid) + _dot(m01, lo)


def _pad_rows(x_c, nb, n_valid):
    rows, cols = nb * SAMPLE_PAD, nb * n_valid
    r = lax.broadcasted_iota(jnp.int32, (rows, cols), 0)
    c = lax.broadcasted_iota(jnp.int32, (rows, cols), 1)
    sel = (r % SAMPLE_PAD < n_valid) & (c == (r // SAMPLE_PAD) * n_valid + r % SAMPLE_PAD)
    return _dot_exactish(jnp.where(sel, 1.0, 0.0).astype(BF16), x_c)


def _unpad_rows(x_p, nb, n_valid):
    rows, cols = nb * n_valid, nb * SAMPLE_PAD
    r = lax.broadcasted_iota(jnp.int32, (rows, cols), 0)
    c = lax.broadcasted_iota(jnp.int32, (rows, cols), 1)
    sel = c == (r // n_valid) * SAMPLE_PAD + r % n_valid
    return _dot_exactish(jnp.where(sel, 1.0, 0.0).astype(BF16), x_p)


def _gate_act(z):
    cap = GATE_SOFTCAP * jnp.tanh(z * (1.0 / GATE_SOFTCAP))
    lsig = jnp.minimum(cap, 0.0) - jnp.log1p(jnp.exp(-jnp.abs(cap)))
    lane = lax.broadcasted_iota(jnp.int32, z.shape, 1)
    return jnp.where(lane < ML_HEADS, cap, lsig)


_DONE = object()


def _rounds(chains):
    live = [(c[0], c[1], c[2] if len(c) > 2 else 1) for c in chains]
    rnd = 0
    while live:
        for item in list(live):
            gen, start, stride = item
            if rnd >= start and (rnd - start) % stride == 0 and next(gen, _DONE) is _DONE:
                live.remove(item)
        rnd += 1
        yield


def _run(chains):
    for _ in _rounds(chains):
        pass


def _head_out(hh, po, hg):
    hn = hh * lax.rsqrt(jnp.mean(hh * hh, axis=-1, keepdims=True) + EPS) * hg
    return jax.nn.sigmoid(po) * hn


def _head_slices(p, h):
    q = p[:, h * ML_DK:(h + 1) * ML_DK]
    k = p[:, ML_QK + h * ML_DK:ML_QK + (h + 1) * ML_DK] * (ML_DK ** -0.5)
    v = p[:, 2 * ML_QK + h * ML_DV:2 * ML_QK + (h + 1) * ML_DV]
    po = p[:, 2 * ML_QK + ML_VO + h * ML_DV:2 * ML_QK + ML_VO + (h + 1) * ML_DV]
    return q, k, v, po


def _prompt_mixer_kernel(xc_ref, xnext_ref, g_ref, wqkvo_ref, wg_ref, bg_ref, hg_ref, wout_ref, *rest,
                         chunks_per_seq, n_cast):
    cast_in = rest[:n_cast]
    y_ref, C_ref, n_ref, m_ref = rest[n_cast:n_cast + 4]
    cast_out = rest[n_cast + 4:2 * n_cast + 4]
    p_s, gz_s = rest[2 * n_cast + 4:]
    R, T = xc_ref.shape[0], xc_ref.shape[1]
    f = pl.program_id(0)
    slot = f % 2

    def project(x_ref, s, r):
        xn = _rms(x_ref[r], g_ref[...]).astype(BF16)
        yield
        for j in range(ML_QKVO // PROJ_COLS):
            cols = slice(j * PROJ_COLS, (j + 1) * PROJ_COLS)
            p_s[s, r, :, cols] = _dot(xn, wqkvo_ref[:, cols])
            yield
        gz_s[s, r] = _dot(xn, wg_ref[...])
        yield

    @pl.when(f == 0)
    def _():
        _run([(project(xc_ref, 0, r), 0) for r in range(R)])

    @pl.when(f % chunks_per_seq == 0)
    def _():
        C_ref[...] = jnp.zeros_like(C_ref)
        n_ref[...] = jnp.zeros_like(n_ref)
        m_ref[...] = jnp.zeros_like(m_ref)

    def recurrence(r):
        G = _gate_act(gz_s[slot, r] + bg_ref[...])
        row = lax.broadcasted_iota(jnp.int32, (T, T), 0)
        col = lax.broadcasted_iota(jnp.int32, (T, T), 1)
        causal = col <= row
        Bc = _dot_exactish(jnp.where(causal, 1.0, 0.0).astype(BF16), G)
        yield
        Gt = G.T
        Bt = Bc.T
        yield
        parts = [None] * ML_HEADS

        def head(h):
            def cols(base, width):
                return p_s[slot, r, :, base + h * width:base + (h + 1) * width]

            q = cols(0, ML_DK)
            k = cols(ML_QK, ML_DK) * (ML_DK ** -0.5)
            v = cols(2 * ML_QK, ML_DV)
            qc, kc, vc = q.astype(BF16), k.astype(BF16), v.astype(BF16)
            qk = lax.dot_general(qc, kc, (((1,), (1,)), ((), ())), preferred_element_type=F32)
            yield
            li_col = G[:, h:h + 1]
            b_col = Bc[:, ML_HEADS + h:ML_HEADS + h + 1]
            dmat = jnp.where(causal, b_col - Bt[ML_HEADS + h:ML_HEADS + h + 1, :] + Gt[h:h + 1, :], -jnp.inf)
            m_prev = m_ref[r, h:h + 1, 0:1]
            inter = b_col + m_prev
            m_t = jnp.maximum(inter, jnp.max(dmat, axis=1, keepdims=True))
            s = qk * jnp.exp(dmat - m_t)
            a_inter = jnp.exp(inter - m_t)
            den = (jnp.sum(s, axis=1, keepdims=True)
                   + a_inter * jnp.sum(q * n_ref[r, h:h + 1, :], axis=1, keepdims=True))
            yield
            C = C_ref[r, h]
            num = _dot(s.astype(BF16), vc) + a_inter * _dot(qc, C.astype(BF16))
            yield
            hh = num * (1.0 / jnp.maximum(jnp.abs(den), jnp.exp(-m_t)))
            hs = _head_out(hh, cols(2 * ML_QK + ML_VO, ML_DV), hg_ref[:, h * ML_DV:(h + 1) * ML_DV])
            yield
            parts[h] = _dot(hs.astype(BF16), wout_ref[h * ML_DV:(h + 1) * ML_DV, :])
            yield
            m_new = m_t[T - 1:T, :]
            b_last = b_col[T - 1:T, :]
            decay = jnp.exp(b_last + m_prev - m_new)
            kw = k * jnp.exp(b_last - b_col + li_col - m_new)
            C_ref[r, h] = decay * C + lax.dot_general(kw.astype(BF16), vc, (((0,), (0,)), ((), ())),
                                                      preferred_element_type=F32)
            n_ref[r, h:h + 1, :] = decay * n_ref[r, h:h + 1, :] + jnp.sum(kw, axis=0, keepdims=True)
            m_ref[r, h:h + 1, :] = jnp.broadcast_to(m_new, (1, LANES))
            yield

        yield from _rounds([(head(h), 2 * h) for h in range(ML_HEADS)])
        y_ref[r] = xc_ref[r] + ((parts[0] + parts[1]) + (parts[2] + parts[3]))

    def casts():
        for src, dst in zip(cast_in, cast_out):
            dst[...] = src[...].astype(BF16)
            yield

    _run([(project(xnext_ref, 1 - slot, r), 0) for r in range(R)] + [(recurrence(r), r) for r in range(R)]
         + [(casts(), 0)])


def _prompt_mixer(x, g, wqkvo, wg, bg, hg, wout, to_cast):
    B, L, D = x.shape
    T = PROMPT_CHUNK
    assert L % T == 0
    R = PROMPT_ROWS
    assert B % R == 0
    nc = L // T
    steps = (B // R) * nc
    const = lambda f: (0, 0)
    cur = lambda f: (f // nc, f % nc, 0)
    nxt = lambda f: (jnp.minimum(f + 1, steps - 1) // nc, jnp.minimum(f + 1, steps - 1) % nc, 0)
    once = dict(pipeline_mode=pl.Buffered(1))
    vmem = _vmem_limit(3 * _nbytes((R, T, D), F32), _nbytes(wqkvo.shape, BF16) // 2,
                       _nbytes(wout.shape, BF16) // 2, _nbytes((R, ML_HEADS, ML_DK, ML_DV), F32),
                       2 * _nbytes((R, T, ML_QKVO), F32))
    for w in to_cast:
        assert w.shape[0] % (steps * 2 * SUBLANES) == 0
    cast_specs = [pl.BlockSpec((w.shape[0] // steps, w.shape[1]), lambda f: (f, 0)) for w in to_cast]
    outs = pl.pallas_call(
        functools.partial(_prompt_mixer_kernel, chunks_per_seq=nc, n_cast=len(to_cast)),
        grid=(steps,),
        in_specs=[
            pl.BlockSpec((R, T, D), cur),
            pl.BlockSpec((R, T, D), nxt),
            pl.BlockSpec((1, D), const),
            pl.BlockSpec((D, ML_QKVO), const, **once),
            pl.BlockSpec((D, LANES), const, **once),
            pl.BlockSpec((1, LANES), const),
            pl.BlockSpec((1, ML_VO), const),
            pl.BlockSpec((ML_VO, D), const, **once),
        ] + cast_specs,
        out_specs=[
            pl.BlockSpec((R, T, D), cur),
            pl.BlockSpec((R, ML_HEADS, ML_DK, ML_DV), lambda f: (f // nc, 0, 0, 0)),
            pl.BlockSpec((R, ML_HEADS, ML_DK), lambda f: (f // nc, 0, 0)),
            pl.BlockSpec((R, ML_HEADS, LANES), lambda f: (f // nc, 0, 0)),
        ] + cast_specs,
        out_shape=[
            jax.ShapeDtypeStruct((B, L, D), F32),
            jax.ShapeDtypeStruct((B, ML_HEADS, ML_DK, ML_DV), F32),
            jax.ShapeDtypeStruct((B, ML_HEADS, ML_DK), F32),
            jax.ShapeDtypeStruct((B, ML_HEADS, LANES), F32),
        ] + [jax.ShapeDtypeStruct(w.shape, BF16) for w in to_cast],
        scratch_shapes=[pltpu.VMEM((2, R, T, ML_QKVO), F32), pltpu.VMEM((2, R, T, LANES), F32)],
        compiler_params=pltpu.CompilerParams(
            dimension_semantics=("arbitrary",), vmem_limit_bytes=vmem),
        name="prompt_mlstm_mixer",
    )(x, x, g, wqkvo, wg, bg, hg, wout, *to_cast)
    return outs


def _sample_mixer_kernel(x_ref, g_ref, wqkvo_ref, wg_ref, bg_ref, hg_ref, wout_ref,
                         C0_ref, n0_ref, m0_ref,
                         y_ref, C_ref, n_ref, m_ref, hs_s, *, n_valid):
    T = SAMPLE_PAD
    nb = C0_ref.shape[0]
    R = nb * T
    x = _pad_rows(x_ref[...], nb, n_valid)
    xn = _rms(x, g_ref[...]).astype(BF16)
    p = _dot(xn, wqkvo_ref[...])

    rowt = lax.broadcasted_iota(jnp.int32, (R, LANES), 0) % T
    lane = lax.broadcasted_iota(jnp.int32, (R, LANES), 1)
    G = jnp.where(rowt < n_valid, _gate_act(_dot(xn, wg_ref[...]) + bg_ref[...]),
                  jnp.where(lane < ML_HEADS, -jnp.inf, 0.0))
    row = lax.broadcasted_iota(jnp.int32, (R, R), 0)
    col = lax.broadcasted_iota(jnp.int32, (R, R), 1)
    causal = (row // T == col // T) & (col <= row)
    Bc = _dot_exactish(jnp.where(causal, 1.0, 0.0).astype(BF16), jnp.where(lane < ML_HEADS, 0.0, G))
    Gt = G.T
    Bt = Bc.T

    def per_seq(fn):
        return jnp.concatenate([fn(a) for a in range(nb)], axis=0)

    def seq_last(colvec):
        return per_seq(lambda a: jnp.broadcast_to(colvec[T * a + T - 1:T * a + T, :], (T, 1)))

    def head(h):
        q, k, v, po = _head_slices(p, h)
        li_col = G[:, h:h + 1]
        b_col = Bc[:, ML_HEADS + h:ML_HEADS + h + 1]
        li_row = Gt[h:h + 1, :]
        b_row = Bt[ML_HEADS + h:ML_HEADS + h + 1, :]
        dmat = jnp.where(causal, b_col - (b_row - li_row), -jnp.inf)
        m_prev = per_seq(lambda a: jnp.broadcast_to(m0_ref[a, h:h + 1, 0:1], (T, 1)))
        n_rows = per_seq(lambda a: jnp.broadcast_to(n0_ref[a, h:h + 1, :], (T, ML_DK)))
        qc, kc, vc = q.astype(BF16), k.astype(BF16), v.astype(BF16)
        qk = lax.dot_general(qc, kc, (((1,), (1,)), ((), ())), preferred_element_type=F32)
        yield
        inter = b_col + m_prev
        m_t = jnp.maximum(inter, jnp.max(dmat, axis=1, keepdims=True))
        s = qk * jnp.exp(dmat - m_t)
        a_inter = jnp.exp(inter - m_t)
        yield
        qC = per_seq(lambda a: _dot(q[T * a:T * a + T, :], C0_ref[a, h]))
        yield
        num = _dot(s.astype(BF16), vc) + a_inter * qC
        den = jnp.sum(s, axis=1, keepdims=True) + a_inter * jnp.sum(q * n_rows, axis=1, keepdims=True)
        hh = num * (1.0 / jnp.maximum(jnp.abs(den), jnp.exp(-m_t)))
        hs_s[:, h * ML_DV:(h + 1) * ML_DV] = _head_out(
            hh, po, hg_ref[:, h * ML_DV:(h + 1) * ML_DV]).astype(BF16)
        yield
        m_new = seq_last(m_t)
        b_last = seq_last(b_col)
        decay = jnp.exp(b_last + m_prev - m_new)
        kw = k * jnp.exp(b_last - b_col + li_col - m_new)
        for a in range(nb):
            dec = decay[T * a:T * a + 1, :]
            kw_a = kw[T * a:T * a + T, :]
            upd = lax.dot_general(kw_a, v[T * a:T * a + T, :], (((0,), (0,)), ((), ())),
                                  preferred_element_type=F32)
            C_ref[a, h] = dec * C0_ref[a, h] + upd
            n_ref[a, h:h + 1, :] = dec * n0_ref[a, h:h + 1, :] + jnp.sum(kw_a, axis=0, keepdims=True)
            m_ref[a, h:h + 1, :] = jnp.broadcast_to(m_new[T * a:T * a + 1, :], (1, LANES))
            if a % 4 == 3:
                yield

    _run([(head(h), h) for h in range(ML_HEADS)])
    y_ref[...] = _unpad_rows(x + _dot(hs_s[...], wout_ref[...]), nb, n_valid)


def _sample_mixer(x, g, wqkvo, wg, bg, hg, wout, C0, n0, m0, n_valid):
    NT, D = x.shape
    nseq = C0.shape[0]
    nb = SAMPLE_NB
    R = nb * SAMPLE_PAD
    RC = nb * n_valid
    assert nseq % nb == 0 and NT == nseq * n_valid and RC % SUBLANES == 0
    const = lambda i: (0, 0)
    state_specs = [
        pl.BlockSpec((nb, ML_HEADS, ML_DK, ML_DV), lambda i: (i, 0, 0, 0)),
        pl.BlockSpec((nb, ML_HEADS, ML_DK), lambda i: (i, 0, 0)),
        pl.BlockSpec((nb, ML_HEADS, LANES), lambda i: (i, 0, 0)),
    ]
    vmem = _vmem_limit(2 * _nbytes((R, D), F32), _nbytes(wqkvo.shape, BF16), _nbytes(wout.shape, BF16),
                       2 * _nbytes((nb, ML_HEADS, ML_DK, ML_DV), F32), _nbytes((R, ML_QKVO), F32))
    return pl.pallas_call(
        functools.partial(_sample_mixer_kernel, n_valid=n_valid),
        grid=(nseq // nb,),
        in_specs=[
            pl.BlockSpec((RC, D), lambda i: (i, 0)),
            pl.BlockSpec((1, D), const),
            pl.BlockSpec((D, ML_QKVO), const),
            pl.BlockSpec((D, LANES), const),
            pl.BlockSpec((1, LANES), const),
            pl.BlockSpec((1, ML_VO), const),
            pl.BlockSpec((ML_VO, D), const),
        ] + state_specs,
        out_specs=[pl.BlockSpec((RC, D), lambda i: (i, 0))] + state_specs,
        out_shape=[
            jax.ShapeDtypeStruct((NT, D), F32),
            jax.ShapeDtypeStruct(C0.shape, F32),
            jax.ShapeDtypeStruct(n0.shape, F32),
            jax.ShapeDtypeStruct(m0.shape, F32),
        ],
        scratch_shapes=[pltpu.VMEM((R, ML_VO), BF16)],
        compiler_params=pltpu.CompilerParams(
            dimension_semantics=("arbitrary",), vmem_limit_bytes=vmem),
        name="sample_mlstm_mixer",
    )(x, g, wqkvo, wg, bg, hg, wout, C0, n0, m0)


def _dot_exactish_right(x, m01):
    hi, mid, lo = _split3(x)
    return _dot(hi, m01) + _dot(mid, m01) + _dot(lo, m01)


def _mlp_kernel(x_ref, g_ref, w1_ref, w2_ref, *rest):
    x = x_ref[...]
    xn = _rms(x, g_ref[...]).astype(BF16)
    acc = x
    for c in range(D_FF // FF_TILE):
        hcol = _dot(xn, w1_ref[:, c * FF_TILE:(c + 1) * FF_TILE])
        hcol = jnp.square(jnp.maximum(hcol, 0.0)).astype(BF16)
        acc = acc + _dot(hcol, w2_ref[c * FF_TILE:(c + 1) * FF_TILE, :])
    if len(rest) == 1:
        (y_ref,) = rest
        y_ref[...] = acc
        return
    gkv_ref, wkv_ref, kg_ref, y_ref, k_ref, v_ref = rest
    y_ref[...] = acc
    kv = _dot(_rms(acc, gkv_ref[...]).astype(BF16), wkv_ref[...])
    kraw = kv[:, :ATT_KV]
    v_ref[...] = kv[:, ATT_KV:]
    r = lax.broadcasted_iota(jnp.int32, (ATT_KV, ATT_KV), 0) // ATT_HD
    c = lax.broadcasted_iota(jnp.int32, (ATT_KV, ATT_KV), 1) // ATT_HD
    seg = jnp.where(r == c, 1.0, 0.0).astype(BF16)
    ss = _dot_exactish_right(kraw * kraw, seg)
    k_ref[...] = kraw * lax.rsqrt(ss * (1.0 / ATT_HD) + EPS) * kg_ref[...]


def _mlp(x, g, w1, w2, layer, kv=None):
    N, D = x.shape
    tm = min(ROW_TILE, N)
    assert N % tm == 0
    const = lambda i: (0, 0)
    rows = lambda i: (i, 0)
    once = dict(pipeline_mode=pl.Buffered(1))
    in_specs = [
        pl.BlockSpec((tm, D), rows),
        pl.BlockSpec((1, D), const),
        pl.BlockSpec((None, D, D_FF), lambda i: (layer, 0, 0), **once),
        pl.BlockSpec((None, D_FF, D), lambda i: (layer, 0, 0), **once),
    ]
    out_specs = [pl.BlockSpec((tm, D), rows)]
    out_shape = [jax.ShapeDtypeStruct((N, D), F32)]
    args = [x, g, w1, w2]
    if kv is not None:
        in_specs += [pl.BlockSpec((1, D), const), pl.BlockSpec((D, 2 * ATT_KV), const, **once),
                     pl.BlockSpec((1, ATT_KV), const)]
        out_specs += [pl.BlockSpec((tm, ATT_KV), rows)] * 2
        out_shape += [jax.ShapeDtypeStruct((N, ATT_KV), F32)] * 2
        args += list(kv)
    vmem = _vmem_limit(2 * _nbytes((tm, D), F32), _nbytes(w1.shape[1:], BF16) // 2,
                       _nbytes(w2.shape[1:], BF16) // 2, 2 * _nbytes((tm, FF_TILE), F32))
    out = pl.pallas_call(
        _mlp_kernel,
        grid=(N // tm,),
        in_specs=in_specs,
        out_specs=out_specs,
        out_shape=out_shape,
        compiler_params=pltpu.CompilerParams(
            dimension_semantics=("arbitrary",), vmem_limit_bytes=vmem),
        name="sqrelu_mlp",
    )(*args)
    return out[0] if kv is None else out


def _pair_queries(qraw, pr, qscale, col0=0):
    TQ = qraw.shape[0]
    lo = lax.broadcasted_iota(jnp.int32, (TQ, LANES), 1) < ATT_HD
    out = []
    for e in range(2):
        kvh = 2 * pr + e
        for g in range(ATT_GROUP):
            cc, half = divmod(g, 2)
            c0 = (2 * kvh + cc) * LANES - col0
            q2 = qraw[:, c0:c0 + LANES]
            qm = jnp.where(lo, q2, 0.0) if half == 0 else jnp.where(lo, 0.0, q2)
            ss = jnp.sum(qm * qm, axis=1, keepdims=True)
            qn = qm * lax.rsqrt(ss * (1.0 / ATT_HD) + EPS)
            if qscale is not None:
                qn = qn * qscale
            out.append(qn if half == e else pltpu.roll(qn, ATT_HD, 1))
    return out


def _pair_outputs(o_heads, rden, pr, store):
    TQ = o_heads[0].shape[0]
    lo = lax.broadcasted_iota(jnp.int32, (TQ, LANES), 1) < ATT_HD
    for e in range(2):
        kvh = 2 * pr + e
        for cc in range(2):
            tiles = []
            for half in range(2):
                o = o_heads[e * ATT_GROUP + 2 * cc + half] * rden[e * ATT_GROUP + 2 * cc + half]
                tiles.append(o if half == e else pltpu.roll(o, ATT_HD, 1))
            store((2 * kvh + cc) * LANES, jnp.where(lo, tiles[0], tiles[1]))


def _softmax_with_sink(parts, sink):
    assert all(s.shape == parts[0].shape for s in parts)
    M = jnp.maximum(sink, jnp.max(functools.reduce(jnp.maximum, parts), axis=1, keepdims=True))
    ps = [jnp.exp2(s - M) for s in parts]
    den = jnp.exp2(sink - M) + jnp.sum(functools.reduce(jnp.add, ps), axis=1, keepdims=True)
    return ps, 1.0 / den


def _prompt_attn_kernel(sinks_ref, x_ref, g_ref, wq_ref, qg_ref, kp_ref, kc_ref, vp_ref, vc_ref, wo_ref,
                        y_ref):
    TQ = WINDOW
    TR = x_ref.shape[0]
    kj = lax.broadcasted_iota(jnp.int32, (WINDOW + TQ, TQ), 0)
    qi = lax.broadcasted_iota(jnp.int32, (WINDOW + TQ, TQ), 1)
    band = (qi + WINDOW - kj >= 0) & (qi - kj <= 0)
    band_first = band & ((kj >= WINDOW) | (pl.program_id(1) > 0))
    kscale = qg_ref[...] * (ATT_HD ** -0.5 * LOG2E)
    slabs = []
    for pr in range(ATT_KVH // 2):
        sl = slice(pr * LANES, (pr + 1) * LANES)
        slabs.append(((jnp.concatenate([kp_ref[:, sl], kc_ref[:, sl]], axis=0) * kscale).astype(BF16),
                      jnp.concatenate([vp_ref[:, sl], vc_ref[:, sl]], axis=0).T))

    PW = 2 * ATT_GROUP * ATT_HD
    zeros_hd = jnp.zeros((ATT_HD, TQ), F32)
    ones_rows = jnp.ones((2 * SUBLANES, WINDOW + TQ), F32)

    def chain(cb):
        nsb = ATTN_CHAIN_BLOCKS
        rows = slice(cb * nsb * TQ, (cb + 1) * nsb * TQ)
        x = x_ref[rows, :]
        xn = _rms(x, g_ref[...]).astype(BF16)
        acc = x
        for pr in range(ATT_KVH // 2):
            kslab, vslab_t = slabs[pr]
            qt_all = _dot(xn, wq_ref[:, pr * PW:(pr + 1) * PW]).T
            yield
            outs = []
            for i in range(nsb):
                sb = cb * nsb + i
                keys = slice(sb * TQ, sb * TQ + WINDOW + TQ)
                mask = band_first if sb == 0 else band
                qt = qt_all[:, i * TQ:(i + 1) * TQ]
                tiles, sink_rows = [], []
                for e in range(2):
                    for g in range(ATT_GROUP):
                        blk = qt[(e * ATT_GROUP + g) * ATT_HD:(e * ATT_GROUP + g + 1) * ATT_HD, :]
                        qn = blk * lax.rsqrt(jnp.sum(blk * blk, axis=0, keepdims=True) * (1.0 / ATT_HD) + EPS)
                        tiles.append(jnp.concatenate([qn, zeros_hd] if e == 0 else [zeros_hd, qn], axis=0))
                        sink_rows.append(jnp.full((1, TQ), sinks_ref[(2 * pr + e) * ATT_GROUP + g] * LOG2E, F32))
                qmat = jnp.concatenate(tiles, axis=1).astype(BF16)
                st = _dot(kslab[keys, :], qmat)
                yield
                parts = []
                for e in range(2):
                    ps, ms = [], []
                    for g in range(ATT_GROUP):
                        c0 = (e * ATT_GROUP + g) * TQ
                        s_h = jnp.where(mask, st[:, c0:c0 + TQ], -jnp.inf)
                        m_h = jnp.maximum(jnp.max(s_h, axis=0, keepdims=True), sink_rows[e * ATT_GROUP + g])
                        ps.append(jnp.exp2(s_h - m_h).astype(BF16))
                        ms.append(m_h)
                    p_e = jnp.concatenate(ps, axis=1)
                    v_aug = jnp.concatenate([vslab_t[e * ATT_HD:(e + 1) * ATT_HD, keys], ones_rows],
                                            axis=0).astype(BF16)
                    ot = _dot(v_aug, p_e)
                    yield
                    sink_term = jnp.exp2(jnp.concatenate(sink_rows[e * ATT_GROUP:(e + 1) * ATT_GROUP], axis=1)
                                         - jnp.concatenate(ms, axis=1))
                    on = ot[0:ATT_HD, :] * (1.0 / (ot[ATT_HD:ATT_HD + 1, :] + sink_term))
                    parts += [on[:, g * TQ:(g + 1) * TQ] for g in range(ATT_GROUP)]
                outs.append(jnp.concatenate(parts, axis=0))
            o_pair = jnp.concatenate(outs, axis=1).T.astype(BF16)
            acc = acc + _dot(o_pair, wo_ref[pr * PW:(pr + 1) * PW, :])
            yield
        y_ref[rows, :] = acc

    _run([(chain(cb), 2 * cb) for cb in range(TR // (ATTN_CHAIN_BLOCKS * TQ))])


def _prompt_attn(x, k, v, sinks, g, wq, qg, wo):
    B, L, D = x.shape
    TR = ATTN_ROWS
    per = TR // WINDOW
    assert L % TR == 0 and TR % WINDOW == 0
    const = lambda b, i: (0, 0)
    cur = lambda b, i: (b, i, 0)
    prev = lambda b, i: (b, jnp.maximum(i * per - 1, 0), 0)
    vmem = _vmem_limit(2 * _nbytes((TR, D), F32), _nbytes(wq.shape, BF16), _nbytes(wo.shape, BF16),
                       4 * _nbytes((TR, ATT_KV), F32), 8 * _nbytes((8 * WINDOW, 2 * WINDOW), F32))
    return pl.pallas_call(
        _prompt_attn_kernel,
        grid=(B, L // TR),
        in_specs=[
            pl.BlockSpec(memory_space=pltpu.SMEM),
            pl.BlockSpec((None, TR, D), cur),
            pl.BlockSpec((1, D), const),
            pl.BlockSpec((D, D), const),
            pl.BlockSpec((1, LANES), const),
            pl.BlockSpec((None, WINDOW, ATT_KV), prev),
            pl.BlockSpec((None, TR, ATT_KV), cur),
            pl.BlockSpec((None, WINDOW, ATT_KV), prev),
            pl.BlockSpec((None, TR, ATT_KV), cur),
            pl.BlockSpec((D, D), const),
        ],
        out_specs=pl.BlockSpec((None, TR, D), cur),
        out_shape=jax.ShapeDtypeStruct((B, L, D), F32),
        compiler_params=pltpu.CompilerParams(
            dimension_semantics=("arbitrary", "arbitrary"), vmem_limit_bytes=vmem),
        name="prompt_window_attention",
    )(sinks, x, g, wq, qg, k, k, v, v, wo)


def _sample_attn_kernel(sinks_ref, x_ref, g_ref, wq_ref, qg_ref, kc_ref, vc_ref, wk_ref, wv_ref, wo_ref,
                        y_ref, nk_ref, nv_ref, o_s, *, n_valid):
    T = SAMPLE_PAD
    nb = wk_ref.shape[0]
    R = nb * T
    assert R == WINDOW
    x = _pad_rows(x_ref[...], nb, n_valid)
    q = _dot(_rms(x, g_ref[...]).astype(BF16), wq_ref[...])
    qscale = qg_ref[...] * (ATT_HD ** -0.5 * LOG2E)
    knew = _pad_rows(kc_ref[...], nb, n_valid)
    vnew = _pad_rows(vc_ref[...], nb, n_valid)

    cmask = (lax.broadcasted_iota(jnp.int32, (R, WINDOW), 1)
             >= lax.broadcasted_iota(jnp.int32, (R, WINDOW), 0) % T)
    row = lax.broadcasted_iota(jnp.int32, (R, R), 0)
    col = lax.broadcasted_iota(jnp.int32, (R, R), 1)
    nmask = (row // T == col // T) & (col % T <= row % T) & (col % T < n_valid)

    def store(off, val):
        o_s[:, off:off + LANES] = val.astype(BF16)

    def regroup(per_seq, i):
        return jnp.concatenate([per_seq[a][i * T:(i + 1) * T, :] for a in range(nb)], axis=0)

    def pair(pr):
        sl = slice(pr * LANES, (pr + 1) * LANES)
        qs = _pair_queries(q, pr, qscale)
        S_new = lax.dot_general(jnp.concatenate(qs, axis=0).astype(BF16), knew[:, sl].astype(BF16),
                                (((1,), (1,)), ((), ())), preferred_element_type=F32)
        yield
        sc = []
        for a in range(nb):
            q_a = jnp.concatenate([qi[T * a:T * a + T, :] for qi in qs], axis=0).astype(BF16)
            sc.append(_dot(q_a, wk_ref[a, sl, :].astype(BF16)))
            if a % 4 == 3:
                yield
        pcs, pns, rden = [], [], []
        for i in range(2 * ATT_GROUP):
            s_c = jnp.where(cmask, regroup(sc, i), -jnp.inf)
            s_n = jnp.where(nmask, S_new[i * R:(i + 1) * R, :], -jnp.inf)
            (p_c, p_n), r = _softmax_with_sink([s_c, s_n], sinks_ref[pr * 2 * ATT_GROUP + i] * LOG2E)
            pcs.append(p_c)
            pns.append(p_n)
            rden.append(r)
            if i % 2 == 1:
                yield
        O_new = _dot(jnp.concatenate(pns, axis=0).astype(BF16), vnew[:, sl].astype(BF16))
        yield
        oc = []
        for a in range(nb):
            p_a = jnp.concatenate([pc[T * a:T * a + T, :] for pc in pcs], axis=0).astype(BF16)
            oc.append(lax.dot_general(p_a, wv_ref[a, sl, :].astype(BF16), (((1,), (1,)), ((), ())),
                                      preferred_element_type=F32))
            if a % 4 == 3:
                yield
        _pair_outputs([O_new[i * R:(i + 1) * R, :] + regroup(oc, i) for i in range(2 * ATT_GROUP)],
                      rden, pr, store)
        yield

    def roll_cache():
        keep = lax.broadcasted_iota(jnp.int32, (ATT_KV, WINDOW), 1) < WINDOW - n_valid
        k_parts = _split3(knew.T)
        v_parts = _split3(vnew.T)
        group = 4
        rr = lax.broadcasted_iota(jnp.int32, (R, group * WINDOW), 0)
        cc = lax.broadcasted_iota(jnp.int32, (R, group * WINDOW), 1)
        hit = (cc % WINDOW == WINDOW - n_valid + rr % T) & (rr % T < n_valid)
        seq_off = rr // T - cc // WINDOW
        yield
        for g0 in range(0, nb, group):
            place = jnp.where(hit & (seq_off == g0), 1.0, 0.0).astype(BF16)
            new_k = _dot(k_parts[0], place) + _dot(k_parts[1], place) + _dot(k_parts[2], place)
            new_v = _dot(v_parts[0], place) + _dot(v_parts[1], place) + _dot(v_parts[2], place)
            for j in range(group):
                lanes = slice(j * WINDOW, (j + 1) * WINDOW)
                nk_ref[g0 + j] = jnp.where(keep, pltpu.roll(wk_ref[g0 + j], WINDOW - n_valid, 1), new_k[:, lanes])
                nv_ref[g0 + j] = jnp.where(keep, pltpu.roll(wv_ref[g0 + j], WINDOW - n_valid, 1), new_v[:, lanes])
            yield

    _run([(pair(0), 0), (pair(1), 2), (roll_cache(), 0)])
    y_ref[...] = _unpad_rows(x + _dot(o_s[...], wo_ref[...]), nb, n_valid)


def _sample_attn(x, k, v, win_k, win_v, sinks, g, wq, qg, wo, n_valid):
    NT, D = x.shape
    nseq = win_k.shape[0]
    nb = SAMPLE_NB
    R = nb * SAMPLE_PAD
    RC = nb * n_valid
    assert nseq % nb == 0 and NT == nseq * n_valid and RC % SUBLANES == 0
    const = lambda i: (0, 0)
    rows = lambda i: (i, 0)
    cache = pl.BlockSpec((nb, ATT_KV, WINDOW), lambda i: (i, 0, 0))
    vmem = _vmem_limit(2 * _nbytes((R, D), F32), _nbytes(wq.shape, BF16), _nbytes(wo.shape, BF16),
                       4 * _nbytes((nb, ATT_KV, WINDOW), F32), 2 * _nbytes((R, D), F32))
    return pl.pallas_call(
        functools.partial(_sample_attn_kernel, n_valid=n_valid),
        grid=(nseq // nb,),
        in_specs=[
            pl.BlockSpec(memory_space=pltpu.SMEM),
            pl.BlockSpec((RC, D), rows),
            pl.BlockSpec((1, D), const),
            pl.BlockSpec((D, D), const),
            pl.BlockSpec((1, LANES), const),
            pl.BlockSpec((RC, ATT_KV), rows),
            pl.BlockSpec((RC, ATT_KV), rows),
            cache, cache,
            pl.BlockSpec((D, D), const),
        ],
        out_specs=[pl.BlockSpec((RC, D), rows), cache, cache],
        out_shape=[jax.ShapeDtypeStruct((NT, D), F32),
                   jax.ShapeDtypeStruct(win_k.shape, F32), jax.ShapeDtypeStruct(win_v.shape, F32)],
        scratch_shapes=[pltpu.VMEM((R, D), BF16)],
        compiler_params=pltpu.CompilerParams(
            dimension_semantics=("arbitrary",), vmem_limit_bytes=vmem),
        name="sample_window_attention",
    )(sinks, x, g, wq, qg, k, v, win_k, win_v, wo)


def kernel(x_prompt, x_sample, state_mlstm_C, state_mlstm_n, state_mlstm_m, cache_win_k, cache_win_v,
           ml_norm_g, ml_w_in, ml_b_i, ml_b_f, ml_head_g, ml_w_out, kv_norm_g, w_kv, k_norm_g,
           att_norm_g, att_w_q, q_norm_g, att_sinks, att_w_o, mlp_norm_g, mlp_w1, mlp_w2):
    B, L, D = x_prompt.shape
    NS, LS, _ = x_sample.shape
    assert ml_w_in.shape[0] == 1 and att_w_q.shape[0] == 1 and mlp_w1.shape[0] == 2

    w_in = ml_w_in[0]
    wqkvo = w_in.astype(BF16)
    wg = jnp.pad(w_in[:, ML_QKVO:], ((0, 0), (0, LANES - 2 * ML_HEADS))).astype(BF16)
    bg = jnp.pad(jnp.concatenate([ml_b_i[0], ml_b_f[0]]), (0, LANES - 2 * ML_HEADS)).reshape(1, LANES)
    ml_g = ml_norm_g[0].reshape(1, D)
    hg = ml_head_g[0].reshape(1, ML_VO)
    wout = ml_w_out[0].astype(BF16)
    mlp_g = mlp_norm_g.reshape(2, 1, D)
    kv_g = kv_norm_g.reshape(1, D)
    kg = jnp.tile(k_norm_g, ATT_KVH).reshape(1, ATT_KV)
    att_g = att_norm_g[0].reshape(1, D)
    qg = jnp.tile(q_norm_g[0], 2).reshape(1, LANES)
    sinks = att_sinks[0]

    xp, p_C, p_n, p_m, w1, w2, wkv, wq, wo = _prompt_mixer(
        x_prompt, ml_g, wqkvo, wg, bg, hg, wout,
        [mlp_w1.reshape(2 * D, D_FF), mlp_w2.reshape(2 * D_FF, D), w_kv, att_w_q[0], att_w_o[0]])
    w1 = w1.reshape(2, D, D_FF)
    w2 = w2.reshape(2, D_FF, D)
    xp, kp, vp = _mlp(xp.reshape(B * L, D), mlp_g[0], w1, w2, 0, kv=(kv_g, wkv, kg))
    xp = _prompt_attn(xp.reshape(B, L, D), kp.reshape(B, L, ATT_KV), vp.reshape(B, L, ATT_KV),
                      sinks, att_g, wq, qg, wo)
    y_prompt = _mlp(xp.reshape(B * L, D), mlp_g[1], w1, w2, 1).reshape(B, L, D)
    p_wk = kp.reshape(B, L, ATT_KV)[:, L - WINDOW:].reshape(B, WINDOW, ATT_KVH, ATT_HD)
    p_wv = vp.reshape(B, L, ATT_KV)[:, L - WINDOW:].reshape(B, WINDOW, ATT_KVH, ATT_HD)

    m0 = jnp.broadcast_to(state_mlstm_m[0][:, :, None], (NS, ML_HEADS, LANES))
    xs, s_C, s_n, s_m = _sample_mixer(x_sample.reshape(NS * LS, D), ml_g, wqkvo, wg, bg, hg, wout,
                                      state_mlstm_C[0], state_mlstm_n[0], m0, LS)
    xs, ks, vs = _mlp(xs, mlp_g[0], w1, w2, 0, kv=(kv_g, wkv, kg))
    to_t = lambda c: c.transpose(0, 2, 3, 1).reshape(NS, ATT_KV, WINDOW)
    from_t = lambda c: c.reshape(NS, ATT_KVH, ATT_HD, WINDOW).transpose(0, 3, 1, 2)
    xs, s_wk, s_wv = _sample_attn(xs, ks, vs, to_t(cache_win_k), to_t(cache_win_v),
                                  sinks, att_g, wq, qg, wo, LS)
    y_sample = _mlp(xs, mlp_g[1], w1, w2, 1).reshape(NS, LS, D)

    return (y_prompt, y_sample,
            p_C[None], p_n[None], p_m[None, :, :, 0], p_wk, p_wv,
            s_C[None], s_n[None], s_m[None, :, :, 0],
            from_t(s_wk), from_t(s_wv))
```

```python
import functools

import jax
import jax.numpy as jnp
from jax import lax
from jax.experimental import pallas as pl
from jax.experimental.pallas import tpu as pltpu

F32 = jnp.float32
BF16 = jnp.bfloat16

D_MODEL = 1024
ML_HEADS = 4
ML_DK = 128
ML_DV = 256
ML_QK = ML_HEADS * ML_DK
ML_VO = ML_HEADS * ML_DV
ML_QKVO = 2 * ML_QK + 2 * ML_VO
GATE_SOFTCAP = 15.0
ATT_HD = 64
ATT_QH = 16
ATT_KVH = 4
ATT_GROUP = 4
ATT_KV = ATT_KVH * ATT_HD
WINDOW = 128
D_FF = 4 * D_MODEL
EPS = 1e-6
LOG2E = 1.4426950408889634

LANES = 128
SUBLANES = 8
VMEM_LIMIT_CAP = 56 * 1024 * 1024

PROMPT_CHUNK = 256
PROMPT_ROWS = 2
PROJ_COLS = 512
SAMPLE_PAD = SUBLANES
SAMPLE_NB = 16
ATTN_ROWS = 1024
ATTN_CHAIN_BLOCKS = 2
ROW_TILE = 1024
FF_TILE = 1024


def _vmem_limit(*block_bytes):
    need = 4 * sum(block_bytes) + (8 << 20)
    return int(min(max(need, 32 << 20), VMEM_LIMIT_CAP))


def _nbytes(shape, dtype):
    n = 1
    for s in shape:
        n *= s
    return n * jnp.dtype(dtype).itemsize


def _rms(x, g):
    return x * lax.rsqrt(jnp.mean(x * x, axis=-1, keepdims=True) + EPS) * g


def _dot(a, b):
    return jnp.dot(a, b, preferred_element_type=F32)


def _split3(x):
    hi = x.astype(BF16)
    r1 = x - hi.astype(F32)
    mid = r1.astype(BF16)
    lo = (r1 - mid.astype(F32)).astype(BF16)
    return hi, mid, lo


def _dot_exactish(m01, x):
    hi, mid, lo = _split3(x)
    return _dot(m01, hi) + _dot(m01, mid) + _dot(m01, lo)


def _pad_rows(x_c, nb, n_valid):
    rows, cols = nb * SAMPLE_PAD, nb * n_valid
    r = lax.broadcasted_iota(jnp.int32, (rows, cols), 0)
    c = lax.broadcasted_iota(jnp.int32, (rows, cols), 1)
    sel = (r % SAMPLE_PAD < n_valid) & (c == (r // SAMPLE_PAD) * n_valid + r % SAMPLE_PAD)
    return _dot_exactish(jnp.where(sel, 1.0, 0.0).astype(BF16), x_c)


def _unpad_rows(x_p, nb, n_valid):
    rows, cols = nb * n_valid, nb * SAMPLE_PAD
    r = lax.broadcasted_iota(jnp.int32, (rows, cols), 0)
    c = lax.broadcasted_iota(jnp.int32, (rows, cols), 1)
    sel = c == (r // n_valid) * SAMPLE_PAD + r % n_valid
    return _dot_exactish(jnp.where(sel, 1.0, 0.0).astype(BF16), x_p)


def _gate_act(z):
    cap = GATE_SOFTCAP * jnp.tanh(z * (1.0 / GATE_SOFTCAP))
    lsig = jnp.minimum(cap, 0.0) - jnp.log1p(jnp.exp(-jnp.abs(cap)))
    lane = lax.broadcasted_iota(jnp.int32, z.shape, 1)
    return jnp.where(lane < ML_HEADS, cap, lsig)


_DONE = object()


def _rounds(chains):
    live = [(c[0], c[1], c[2] if len(c) > 2 else 1) for c in chains]
    rnd = 0
    while live:
        for item in list(live):
            gen, start, stride = item
            if rnd >= start and (rnd - start) % stride == 0 and next(gen, _DONE) is _DONE:
                live.remove(item)
        rnd += 1
        yield


def _run(chains):
    for _ in _rounds(chains):
        pass


def _head_out(hh, po, hg):
    hn = hh * lax.rsqrt(jnp.mean(hh * hh, axis=-1, keepdims=True) + EPS) * hg
    return jax.nn.sigmoid(po) * hn


def _head_slices(p, h):
    q = p[:, h * ML_DK:(h + 1) * ML_DK]
    k = p[:, ML_QK + h * ML_DK:ML_QK + (h + 1) * ML_DK] * (ML_DK ** -0.5)
    v = p[:, 2 * ML_QK + h * ML_DV:2 * ML_QK + (h + 1) * ML_DV]
    po = p[:, 2 * ML_QK + ML_VO + h * ML_DV:2 * ML_QK + ML_VO + (h + 1) * ML_DV]
    return q, k, v, po


def _prompt_mixer_kernel(xc_ref, xnext_ref, g_ref, wqkvo_ref, wg_ref, bg_ref, hg_ref, wout_ref, *rest,
                         chunks_per_seq, n_cast):
    cast_in = rest[:n_cast]
    y_ref, C_ref, n_ref, m_ref = rest[n_cast:n_cast + 4]
    cast_out = rest[n_cast + 4:2 * n_cast + 4]
    p_s, gz_s = rest[2 * n_cast + 4:]
    R, T = xc_ref.shape[0], xc_ref.shape[1]
    f = pl.program_id(0)
    slot = f % 2

    def project(x_ref, s, r):
        xn = _rms(x_ref[r], g_ref[...]).astype(BF16)
        yield
        for j in range(ML_QKVO // PROJ_COLS):
            cols = slice(j * PROJ_COLS, (j + 1) * PROJ_COLS)
            p_s[s, r, :, cols] = _dot(xn, wqkvo_ref[:, cols])
            yield
        gz_s[s, r] = _dot(xn, wg_ref[...])
        yield

    @pl.when(f == 0)
    def _():
        _run([(project(xc_ref, 0, r), 0) for r in range(R)])

    @pl.when(f % chunks_per_seq == 0)
    def _():
        C_ref[...] = jnp.zeros_like(C_ref)
        n_ref[...] = jnp.zeros_like(n_ref)
        m_ref[...] = jnp.zeros_like(m_ref)

    def recurrence(r):
        G = _gate_act(gz_s[slot, r] + bg_ref[...])
        row = lax.broadcasted_iota(jnp.int32, (T, T), 0)
        col = lax.broadcasted_iota(jnp.int32, (T, T), 1)
        causal = col <= row
        Bc = _dot_exactish(jnp.where(causal, 1.0, 0.0).astype(BF16), G)
        yield
        Gt = G.T
        Bt = Bc.T
        yield
        parts = [None] * ML_HEADS

        def head(h):
            def cols(base, width):
                return p_s[slot, r, :, base + h * width:base + (h + 1) * width]

            q = cols(0, ML_DK)
            k = cols(ML_QK, ML_DK) * (ML_DK ** -0.5)
            v = cols(2 * ML_QK, ML_DV)
            qc, kc, vc = q.astype(BF16), k.astype(BF16), v.astype(BF16)
            qk = lax.dot_general(qc, kc, (((1,), (1,)), ((), ())), preferred_element_type=F32)
            yield
            li_col = G[:, h:h + 1]
            b_col = Bc[:, ML_HEADS + h:ML_HEADS + h + 1]
            dmat = jnp.where(causal, b_col - Bt[ML_HEADS + h:ML_HEADS + h + 1, :] + Gt[h:h + 1, :], -jnp.inf)
            m_prev = m_ref[r, h:h + 1, 0:1]
            inter = b_col + m_prev
            m_t = jnp.maximum(inter, jnp.max(dmat, axis=1, keepdims=True))
            s = qk * jnp.exp(dmat - m_t)
            a_inter = jnp.exp(inter - m_t)
            den = (jnp.sum(s, axis=1, keepdims=True)
                   + a_inter * jnp.sum(q * n_ref[r, h:h + 1, :], axis=1, keepdims=True))
            yield
            C = C_ref[r, h]
            num = _dot(s.astype(BF16), vc) + a_inter * _dot(qc, C.astype(BF16))
            yield
            hh = num * (1.0 / jnp.maximum(jnp.abs(den), jnp.exp(-m_t)))
            hs = _head_out(hh, cols(2 * ML_QK + ML_VO, ML_DV), hg_ref[:, h * ML_DV:(h + 1) * ML_DV])
            yield
            parts[h] = _dot(hs.astype(BF16), wout_ref[h * ML_DV:(h + 1) * ML_DV, :])
            yield
            m_new = m_t[T - 1:T, :]
            b_last = b_col[T - 1:T, :]
            decay = jnp.exp(b_last + m_prev - m_new)
            kw = k * jnp.exp(b_last - b_col + li_col - m_new)
            C_ref[r, h] = decay * C + lax.dot_general(kw.astype(BF16), vc, (((0,), (0,)), ((), ())),
                                                      preferred_element_type=F32)
            n_ref[r, h:h + 1, :] = decay * n_ref[r, h:h + 1, :] + jnp.sum(kw, axis=0, keepdims=True)
            m_ref[r, h:h + 1, :] = jnp.broadcast_to(m_new, (1, LANES))
            yield

        yield from _rounds([(head(h), 2 * h) for h in range(ML_HEADS)])
        y_ref[r] = xc_ref[r] + ((parts[0] + parts[1]) + (parts[2] + parts[3]))

    def casts():
        for src, dst in zip(cast_in, cast_out):
            dst[...] = src[...].astype(BF16)
            yield

    _run([(project(xnext_ref, 1 - slot, r), 0) for r in range(R)] + [(recurrence(r), r) for r in range(R)]
         + [(casts(), 0)])


def _prompt_mixer(x, g, wqkvo, wg, bg, hg, wout, to_cast):
    B, L, D = x.shape
    T = PROMPT_CHUNK
    assert L % T == 0
    R = PROMPT_ROWS
    assert B % R == 0
    nc = L // T
    steps = (B // R) * nc
    const = lambda f: (0, 0)
    cur = lambda f: (f // nc, f % nc, 0)
    nxt = lambda f: (jnp.minimum(f + 1, steps - 1) // nc, jnp.minimum(f + 1, steps - 1) % nc, 0)
    once = dict(pipeline_mode=pl.Buffered(1))
    vmem = _vmem_limit(3 * _nbytes((R, T, D), F32), _nbytes(wqkvo.shape, BF16) // 2,
                       _nbytes(wout.shape, BF16) // 2, _nbytes((R, ML_HEADS, ML_DK, ML_DV), F32),
                       2 * _nbytes((R, T, ML_QKVO), F32))
    for w in to_cast:
        assert w.shape[0] % (steps * 2 * SUBLANES) == 0
    cast_specs = [pl.BlockSpec((w.shape[0] // steps, w.shape[1]), lambda f: (f, 0)) for w in to_cast]
    outs = pl.pallas_call(
        functools.partial(_prompt_mixer_kernel, chunks_per_seq=nc, n_cast=len(to_cast)),
        grid=(steps,),
        in_specs=[
            pl.BlockSpec((R, T, D), cur),
            pl.BlockSpec((R, T, D), nxt),
            pl.BlockSpec((1, D), const),
            pl.BlockSpec((D, ML_QKVO), const, **once),
            pl.BlockSpec((D, LANES), const, **once),
            pl.BlockSpec((1, LANES), const),
            pl.BlockSpec((1, ML_VO), const),
            pl.BlockSpec((ML_VO, D), const, **once),
        ] + cast_specs,
        out_specs=[
            pl.BlockSpec((R, T, D), cur),
            pl.BlockSpec((R, ML_HEADS, ML_DK, ML_DV), lambda f: (f // nc, 0, 0, 0)),
            pl.BlockSpec((R, ML_HEADS, ML_DK), lambda f: (f // nc, 0, 0)),
            pl.BlockSpec((R, ML_HEADS, LANES), lambda f: (f // nc, 0, 0)),
        ] + cast_specs,
        out_shape=[
            jax.ShapeDtypeStruct((B, L, D), F32),
            jax.ShapeDtypeStruct((B, ML_HEADS, ML_DK, ML_DV), F32),
            jax.ShapeDtypeStruct((B, ML_HEADS, ML_DK), F32),
            jax.ShapeDtypeStruct((B, ML_HEADS, LANES), F32),
        ] + [jax.ShapeDtypeStruct(w.shape, BF16) for w in to_cast],
        scratch_shapes=[pltpu.VMEM((2, R, T, ML_QKVO), F32), pltpu.VMEM((2, R, T, LANES), F32)],
        compiler_params=pltpu.CompilerParams(
            dimension_semantics=("arbitrary",), vmem_limit_bytes=vmem),
        name="prompt_mlstm_mixer",
    )(x, x, g, wqkvo, wg, bg, hg, wout, *to_cast)
    return outs


def _sample_mixer_kernel(x_ref, g_ref, wqkvo_ref, wg_ref, bg_ref, hg_ref, wout_ref,
                         C0_ref, n0_ref, m0_ref,
                         y_ref, C_ref, n_ref, m_ref, hs_s, *, n_valid):
    T = SAMPLE_PAD
    nb = C0_ref.shape[0]
    R = nb * T
    x = _pad_rows(x_ref[...], nb, n_valid)
    xn = _rms(x, g_ref[...]).astype(BF16)
    p = _dot(xn, wqkvo_ref[...])

    rowt = lax.broadcasted_iota(jnp.int32, (R, LANES), 0) % T
    lane = lax.broadcasted_iota(jnp.int32, (R, LANES), 1)
    G = jnp.where(rowt < n_valid, _gate_act(_dot(xn, wg_ref[...]) + bg_ref[...]),
                  jnp.where(lane < ML_HEADS, -jnp.inf, 0.0))
    row = lax.broadcasted_iota(jnp.int32, (R, R), 0)
    col = lax.broadcasted_iota(jnp.int32, (R, R), 1)
    causal = (row // T == col // T) & (col <= row)
    Bc = _dot_exactish(jnp.where(causal, 1.0, 0.0).astype(BF16), jnp.where(lane < ML_HEADS, 0.0, G))
    Gt = G.T
    Bt = Bc.T

    def per_seq(fn):
        return jnp.concatenate([fn(a) for a in range(nb)], axis=0)

    def seq_last(colvec):
        return per_seq(lambda a: jnp.broadcast_to(colvec[T * a + T - 1:T * a + T, :], (T, 1)))

    def head(h):
        q, k, v, po = _head_slices(p, h)
        li_col = G[:, h:h + 1]
        b_col = Bc[:, ML_HEADS + h:ML_HEADS + h + 1]
        li_row = Gt[h:h + 1, :]
        b_row = Bt[ML_HEADS + h:ML_HEADS + h + 1, :]
        dmat = jnp.where(causal, b_col - (b_row - li_row), -jnp.inf)
        m_prev = per_seq(lambda a: jnp.broadcast_to(m0_ref[a, h:h + 1, 0:1], (T, 1)))
        n_rows = per_seq(lambda a: jnp.broadcast_to(n0_ref[a, h:h + 1, :], (T, ML_DK)))
        qc, kc, vc = q.astype(BF16), k.astype(BF16), v.astype(BF16)
        qk = lax.dot_general(qc, kc, (((1,), (1,)), ((), ())), preferred_element_type=F32)
        yield
        inter = b_col + m_prev
        m_t = jnp.maximum(inter, jnp.max(dmat, axis=1, keepdims=True))
        s = qk * jnp.exp(dmat - m_t)
        a_inter = jnp.exp(inter - m_t)
        yield
        qC = per_seq(lambda a: _dot(q[T * a:T * a + T, :], C0_ref[a, h]))
        yield
        num = _dot(s.astype(BF16), vc) + a_inter * qC
        den = jnp.sum(s, axis=1, keepdims=True) + a_inter * jnp.sum(q * n_rows, axis=1, keepdims=True)
        hh = num * (1.0 / jnp.maximum(jnp.abs(den), jnp.exp(-m_t)))
        hs_s[:, h * ML_DV:(h + 1) * ML_DV] = _head_out(
            hh, po, hg_ref[:, h * ML_DV:(h + 1) * ML_DV]).astype(BF16)
        yield
        m_new = seq_last(m_t)
        b_last = seq_last(b_col)
        decay = jnp.exp(b_last + m_prev - m_new)
        kw = k * jnp.exp(b_last - b_col + li_col - m_new)
        for a in range(nb):
            dec = decay[T * a:T * a + 1, :]
            kw_a = kw[T * a:T * a + T, :]
            upd = lax.dot_general(kw_a, v[T * a:T * a + T, :], (((0,), (0,)), ((), ())),
                                  preferred_element_type=F32)
            C_ref[a, h] = dec * C0_ref[a, h] + upd
            n_ref[a, h:h + 1, :] = dec * n0_ref[a, h:h + 1, :] + jnp.sum(kw_a, axis=0, keepdims=True)
            m_ref[a, h:h + 1, :] = jnp.broadcast_to(m_new[T * a:T * a + 1, :], (1, LANES))
            if a % 4 == 3:
                yield

    _run([(head(h), h) for h in range(ML_HEADS)])
    y_ref[...] = _unpad_rows(x + _dot(hs_s[...], wout_ref[...]), nb, n_valid)


def _sample_mixer(x, g, wqkvo, wg, bg, hg, wout, C0, n0, m0, n_valid):
    NT, D = x.shape
    nseq = C0.shape[0]
    nb = SAMPLE_NB
    R = nb * SAMPLE_PAD
    RC = nb * n_valid
    assert nseq % nb == 0 and NT == nseq * n_valid and RC % SUBLANES == 0
    const = lambda i: (0, 0)
    state_specs = [
        pl.BlockSpec((nb, ML_HEADS, ML_DK, ML_DV), lambda i: (i, 0, 0, 0)),
        pl.BlockSpec((nb, ML_HEADS, ML_DK), lambda i: (i, 0, 0)),
        pl.BlockSpec((nb, ML_HEADS, LANES), lambda i: (i, 0, 0)),
    ]
    vmem = _vmem_limit(2 * _nbytes((R, D), F32), _nbytes(wqkvo.shape, BF16), _nbytes(wout.shape, BF16),
                       2 * _nbytes((nb, ML_HEADS, ML_DK, ML_DV), F32), _nbytes((R, ML_QKVO), F32))
    return pl.pallas_call(
        functools.partial(_sample_mixer_kernel, n_valid=n_valid),
        grid=(nseq // nb,),
        in_specs=[
            pl.BlockSpec((RC, D), lambda i: (i, 0)),
            pl.BlockSpec((1, D), const),
            pl.BlockSpec((D, ML_QKVO), const),
            pl.BlockSpec((D, LANES), const),
            pl.BlockSpec((1, LANES), const),
            pl.BlockSpec((1, ML_VO), const),
            pl.BlockSpec((ML_VO, D), const),
        ] + state_specs,
        out_specs=[pl.BlockSpec((RC, D), lambda i: (i, 0))] + state_specs,
        out_shape=[
            jax.ShapeDtypeStruct((NT, D), F32),
            jax.ShapeDtypeStruct(C0.shape, F32),
            jax.ShapeDtypeStruct(n0.shape, F32),
            jax.ShapeDtypeStruct(m0.shape, F32),
        ],
        scratch_shapes=[pltpu.VMEM((R, ML_VO), BF16)],
        compiler_params=pltpu.CompilerParams(
            dimension_semantics=("arbitrary",), vmem_limit_bytes=vmem),
        name="sample_mlstm_mixer",
    )(x, g, wqkvo, wg, bg, hg, wout, C0, n0, m0)


def _dot_exactish_right(x, m01):
    hi, mid, lo = _split3(x)
    return _dot(hi, m01) + _dot(mid, m01) + _dot(lo, m01)


def _mlp_kernel(x_ref, g_ref, w1_ref, w2_ref, *rest):
    x = x_ref[...]
    xn = _rms(x, g_ref[...]).astype(BF16)
    acc = x
    for c in range(D_FF // FF_TILE):
        hcol = _dot(xn, w1_ref[:, c * FF_TILE:(c + 1) * FF_TILE])
        hcol = jnp.square(jnp.maximum(hcol, 0.0)).astype(BF16)
        acc = acc + _dot(hcol, w2_ref[c * FF_TILE:(c + 1) * FF_TILE, :])
    if len(rest) == 1:
        (y_ref,) = rest
        y_ref[...] = acc
        return
    gkv_ref, wkv_ref, kg_ref, y_ref, k_ref, v_ref = rest
    y_ref[...] = acc
    kv = _dot(_rms(acc, gkv_ref[...]).astype(BF16), wkv_ref[...])
    kraw = kv[:, :ATT_KV]
    v_ref[...] = kv[:, ATT_KV:]
    r = lax.broadcasted_iota(jnp.int32, (ATT_KV, ATT_KV), 0) // ATT_HD
    c = lax.broadcasted_iota(jnp.int32, (ATT_KV, ATT_KV), 1) // ATT_HD
    seg = jnp.where(r == c, 1.0, 0.0).astype(BF16)
    ss = _dot_exactish_right(kraw * kraw, seg)
    k_ref[...] = kraw * lax.rsqrt(ss * (1.0 / ATT_HD) + EPS) * kg_ref[...]


def _mlp(x, g, w1, w2, layer, kv=None):
    N, D = x.shape
    tm = min(ROW_TILE, N)
    assert N % tm == 0
    const = lambda i: (0, 0)
    rows = lambda i: (i, 0)
    once = dict(pipeline_mode=pl.Buffered(1))
    in_specs = [
        pl.BlockSpec((tm, D), rows),
        pl.BlockSpec((1, D), const),
        pl.BlockSpec((None, D, D_FF), lambda i: (layer, 0, 0), **once),
        pl.BlockSpec((None, D_FF, D), lambda i: (layer, 0, 0), **once),
    ]
    out_specs = [pl.BlockSpec((tm, D), rows)]
    out_shape = [jax.ShapeDtypeStruct((N, D), F32)]
    args = [x, g, w1, w2]
    if kv is not None:
        in_specs += [pl.BlockSpec((1, D), const), pl.BlockSpec((D, 2 * ATT_KV), const, **once),
                     pl.BlockSpec((1, ATT_KV), const)]
        out_specs += [pl.BlockSpec((tm, ATT_KV), rows)] * 2
        out_shape += [jax.ShapeDtypeStruct((N, ATT_KV), F32)] * 2
        args += list(kv)
    vmem = _vmem_limit(2 * _nbytes((tm, D), F32), _nbytes(w1.shape[1:], BF16) // 2,
                       _nbytes(w2.shape[1:], BF16) // 2, 2 * _nbytes((tm, FF_TILE), F32))
    out = pl.pallas_call(
        _mlp_kernel,
        grid=(N // tm,),
        in_specs=in_specs,
        out_specs=out_specs,
        out_shape=out_shape,
        compiler_params=pltpu.CompilerParams(
            dimension_semantics=("arbitrary",), vmem_limit_bytes=vmem),
        name="sqrelu_mlp",
    )(*args)
    return out[0] if kv is None else out


def _pair_queries(qraw, pr, qscale, col0=0):
    TQ = qraw.shape[0]
    lo = lax.broadcasted_iota(jnp.int32, (TQ, LANES), 1) < ATT_HD
    out = []
    for e in range(2):
        kvh = 2 * pr + e
        for g in range(ATT_GROUP):
            cc, half = divmod(g, 2)
            c0 = (2 * kvh + cc) * LANES - col0
            q2 = qraw[:, c0:c0 + LANES]
            qm = jnp.where(lo, q2, 0.0) if half == 0 else jnp.where(lo, 0.0, q2)
            ss = jnp.sum(qm * qm, axis=1, keepdims=True)
            qn = qm * lax.rsqrt(ss * (1.0 / ATT_HD) + EPS)
            if qscale is not None:
                qn = qn * qscale
            out.append(qn if half == e else pltpu.roll(qn, ATT_HD, 1))
    return out


def _pair_outputs(o_heads, rden, pr, store):
    TQ = o_heads[0].shape[0]
    lo = lax.broadcasted_iota(jnp.int32, (TQ, LANES), 1) < ATT_HD
    for e in range(2):
        kvh = 2 * pr + e
        for cc in range(2):
            tiles = []
            for half in range(2):
                o = o_heads[e * ATT_GROUP + 2 * cc + half] * rden[e * ATT_GROUP + 2 * cc + half]
                tiles.append(o if half == e else pltpu.roll(o, ATT_HD, 1))
            store((2 * kvh + cc) * LANES, jnp.where(lo, tiles[0], tiles[1]))


def _softmax_with_sink(parts, sink):
    assert all(s.shape == parts[0].shape for s in parts)
    M = jnp.maximum(sink, jnp.max(functools.reduce(jnp.maximum, parts), axis=1, keepdims=True))
    ps = [jnp.exp2(s - M) for s in parts]
    den = jnp.exp2(sink - M) + jnp.sum(functools.reduce(jnp.add, ps), axis=1, keepdims=True)
    return ps, 1.0 / den


def _prompt_attn_kernel(sinks_ref, x_ref, g_ref, wq_ref, qg_ref, gkv_ref, wkv_ref, kg_ref, wo_ref,
                        y_ref, pk_ref, pv_ref, kprev_s, vprev_s):
    TQ = WINDOW
    TR = x_ref.shape[0]
    kj = lax.broadcasted_iota(jnp.int32, (WINDOW + TQ, TQ), 0)
    qi = lax.broadcasted_iota(jnp.int32, (WINDOW + TQ, TQ), 1)
    band = (qi + WINDOW - kj >= 0) & (qi - kj <= 0)
    band_first = band & ((kj >= WINDOW) | (pl.program_id(1) > 0))
    kscale = qg_ref[...] * (ATT_HD ** -0.5 * LOG2E)

    @pl.when(pl.program_id(1) == 0)
    def _():
        kprev_s[...] = jnp.zeros_like(kprev_s)
        vprev_s[...] = jnp.zeros_like(vprev_s)

    n_chains = TR // (ATTN_CHAIN_BLOCKS * TQ)
    kv_rows = [None] * n_chains

    PW = 2 * ATT_GROUP * ATT_HD
    zeros_hd = jnp.zeros((ATT_HD, TQ), F32)
    ones_rows = jnp.ones((2 * SUBLANES, WINDOW + TQ), F32)

    def chain(cb):
        nsb = ATTN_CHAIN_BLOCKS
        rows = slice(cb * nsb * TQ, (cb + 1) * nsb * TQ)
        x = x_ref[rows, :]
        kv = _dot(_rms(x, gkv_ref[...]).astype(BF16), wkv_ref[...])
        kraw = kv[:, :ATT_KV]
        seg_r = lax.broadcasted_iota(jnp.int32, (ATT_KV, ATT_KV), 0) // ATT_HD
        seg_c = lax.broadcasted_iota(jnp.int32, (ATT_KV, ATT_KV), 1) // ATT_HD
        ss = _dot_exactish_right(kraw * kraw, jnp.where(seg_r == seg_c, 1.0, 0.0).astype(BF16))
        k_blk = kraw * lax.rsqrt(ss * (1.0 / ATT_HD) + EPS) * kg_ref[...]
        v_blk = kv[:, ATT_KV:]
        kv_rows[cb] = (k_blk, v_blk)
        if cb == 0:
            prev = (kprev_s[...], vprev_s[...])
        if cb == n_chains - 1:
            for dst, blk in ((kprev_s, k_blk), (pk_ref, k_blk), (vprev_s, v_blk), (pv_ref, v_blk)):
                dst[...] = blk[nsb * TQ - WINDOW:, :]
        yield
        if cb > 0:
            prev = tuple(blk[nsb * TQ - WINDOW:, :] for blk in kv_rows[cb - 1])
        k_all = jnp.concatenate([prev[0], k_blk], axis=0)
        v_all = jnp.concatenate([prev[1], v_blk], axis=0)
        slabs = []
        for pr in range(ATT_KVH // 2):
            sl = slice(pr * LANES, (pr + 1) * LANES)
            slabs.append(((k_all[:, sl] * kscale).astype(BF16), v_all[:, sl].T))
        xn = _rms(x, g_ref[...]).astype(BF16)
        acc = x
        for pr in range(ATT_KVH // 2):
            kslab, vslab_t = slabs[pr]
            qt_all = _dot(xn, wq_ref[:, pr * PW:(pr + 1) * PW]).T
            yield
            outs = []
            for i in range(nsb):
                sb = cb * nsb + i
                keys = slice(i * TQ, i * TQ + WINDOW + TQ)
                mask = band_first if sb == 0 else band
                qt = qt_all[:, i * TQ:(i + 1) * TQ]
                tiles, sink_rows = [], []
                for e in range(2):
                    for g in range(ATT_GROUP):
                        blk = qt[(e * ATT_GROUP + g) * ATT_HD:(e * ATT_GROUP + g + 1) * ATT_HD, :]
                        qn = blk * lax.rsqrt(jnp.sum(blk * blk, axis=0, keepdims=True) * (1.0 / ATT_HD) + EPS)
                        tiles.append(jnp.concatenate([qn, zeros_hd] if e == 0 else [zeros_hd, qn], axis=0))
                        sink_rows.append(jnp.full((1, TQ), sinks_ref[(2 * pr + e) * ATT_GROUP + g] * LOG2E, F32))
                qmat = jnp.concatenate(tiles, axis=1).astype(BF16)
                st = _dot(kslab[keys, :], qmat)
                yield
                parts = []
                for e in range(2):
                    ps, ms = [], []
                    for g in range(ATT_GROUP):
                        c0 = (e * ATT_GROUP + g) * TQ
                        s_h = jnp.where(mask, st[:, c0:c0 + TQ], -jnp.inf)
                        m_h = jnp.maximum(jnp.max(s_h, axis=0, keepdims=True), sink_rows[e * ATT_GROUP + g])
                        ps.append(jnp.exp2(s_h - m_h).astype(BF16))
                        ms.append(m_h)
                    p_e = jnp.concatenate(ps, axis=1)
                    v_aug = jnp.concatenate([vslab_t[e * ATT_HD:(e + 1) * ATT_HD, keys], ones_rows],
                                            axis=0).astype(BF16)
                    ot = _dot(v_aug, p_e)
                    yield
                    sink_term = jnp.exp2(jnp.concatenate(sink_rows[e * ATT_GROUP:(e + 1) * ATT_GROUP], axis=1)
                                         - jnp.concatenate(ms, axis=1))
                    on = ot[0:ATT_HD, :] * (1.0 / (ot[ATT_HD:ATT_HD + 1, :] + sink_term))
                    parts += [on[:, g * TQ:(g + 1) * TQ] for g in range(ATT_GROUP)]
                outs.append(jnp.concatenate(parts, axis=0))
            o_pair = jnp.concatenate(outs, axis=1).T.astype(BF16)
            acc = acc + _dot(o_pair, wo_ref[pr * PW:(pr + 1) * PW, :])
            yield
        y_ref[rows, :] = acc

    _run([(chain(cb), 0) for cb in range(n_chains)])


def _prompt_attn(x, sinks, g, wq, qg, kv, wo):
    B, L, D = x.shape
    TR = ATTN_ROWS
    assert L % TR == 0 and TR % (ATTN_CHAIN_BLOCKS * WINDOW) == 0
    gkv, wkv, kg = kv
    const = lambda b, i: (0, 0)
    cur = lambda b, i: (b, i, 0)
    win = pl.BlockSpec((None, WINDOW, ATT_KV), lambda b, i: (b, 0, 0))
    vmem = _vmem_limit(2 * _nbytes((TR, D), F32), _nbytes(wq.shape, BF16), _nbytes(wo.shape, BF16),
                       _nbytes(wkv.shape, BF16), 8 * _nbytes((8 * WINDOW, 2 * WINDOW), F32))
    return pl.pallas_call(
        _prompt_attn_kernel,
        grid=(B, L // TR),
        in_specs=[
            pl.BlockSpec(memory_space=pltpu.SMEM),
            pl.BlockSpec((None, TR, D), cur),
            pl.BlockSpec((1, D), const),
            pl.BlockSpec((D, D), const),
            pl.BlockSpec((1, LANES), const),
            pl.BlockSpec((1, D), const),
            pl.BlockSpec((D, 2 * ATT_KV), const),
            pl.BlockSpec((1, ATT_KV), const),
            pl.BlockSpec((D, D), const),
        ],
        out_specs=[pl.BlockSpec((None, TR, D), cur), win, win],
        out_shape=[jax.ShapeDtypeStruct((B, L, D), F32),
                   jax.ShapeDtypeStruct((B, WINDOW, ATT_KV), F32), jax.ShapeDtypeStruct((B, WINDOW, ATT_KV), F32)],
        scratch_shapes=[pltpu.VMEM((WINDOW, ATT_KV), F32), pltpu.VMEM((WINDOW, ATT_KV), F32)],
        compiler_params=pltpu.CompilerParams(
            dimension_semantics=("arbitrary", "arbitrary"), vmem_limit_bytes=vmem),
        name="prompt_window_attention",
    )(sinks, x, g, wq, qg, gkv, wkv, kg, wo)


def _sample_attn_kernel(sinks_ref, x_ref, g_ref, wq_ref, qg_ref, kc_ref, vc_ref, wk_ref, wv_ref, wo_ref,
                        y_ref, nk_ref, nv_ref, o_s, *, n_valid):
    T = SAMPLE_PAD
    nb = wk_ref.shape[0]
    R = nb * T
    assert R == WINDOW
    x = _pad_rows(x_ref[...], nb, n_valid)
    q = _dot(_rms(x, g_ref[...]).astype(BF16), wq_ref[...])
    qscale = qg_ref[...] * (ATT_HD ** -0.5 * LOG2E)
    knew = _pad_rows(kc_ref[...], nb, n_valid)
    vnew = _pad_rows(vc_ref[...], nb, n_valid)

    cmask = (lax.broadcasted_iota(jnp.int32, (R, WINDOW), 1)
             >= lax.broadcasted_iota(jnp.int32, (R, WINDOW), 0) % T)
    row = lax.broadcasted_iota(jnp.int32, (R, R), 0)
    col = lax.broadcasted_iota(jnp.int32, (R, R), 1)
    nmask = (row // T == col // T) & (col % T <= row % T) & (col % T < n_valid)

    def store(off, val):
        o_s[:, off:off + LANES] = val.astype(BF16)

    def regroup(per_seq, i):
        return jnp.concatenate([per_seq[a][i * T:(i + 1) * T, :] for a in range(nb)], axis=0)

    def pair(pr):
        sl = slice(pr * LANES, (pr + 1) * LANES)
        qs = _pair_queries(q, pr, qscale)
        S_new = lax.dot_general(jnp.concatenate(qs, axis=0).astype(BF16), knew[:, sl].astype(BF16),
                                (((1,), (1,)), ((), ())), preferred_element_type=F32)
        yield
        sc = []
        for a in range(nb):
            q_a = jnp.concatenate([qi[T * a:T * a + T, :] for qi in qs], axis=0).astype(BF16)
            sc.append(_dot(q_a, wk_ref[a, sl, :].astype(BF16)))
            if a % 4 == 3:
                yield
        pcs, pns, rden = [], [], []
        for i in range(2 * ATT_GROUP):
            s_c = jnp.where(cmask, regroup(sc, i), -jnp.inf)
            s_n = jnp.where(nmask, S_new[i * R:(i + 1) * R, :], -jnp.inf)
            (p_c, p_n), r = _softmax_with_sink([s_c, s_n], sinks_ref[pr * 2 * ATT_GROUP + i] * LOG2E)
            pcs.append(p_c)
            pns.append(p_n)
            rden.append(r)
            if i % 2 == 1:
                yield
        O_new = _dot(jnp.concatenate(pns, axis=0).astype(BF16), vnew[:, sl].astype(BF16))
        yield
        oc = []
        for a in range(nb):
            p_a = jnp.concatenate([pc[T * a:T * a + T, :] for pc in pcs], axis=0).astype(BF16)
            oc.append(lax.dot_general(p_a, wv_ref[a, sl, :].astype(BF16), (((1,), (1,)), ((), ())),
                                      preferred_element_type=F32))
            if a % 4 == 3:
                yield
        _pair_outputs([O_new[i * R:(i + 1) * R, :] + regroup(oc, i) for i in range(2 * ATT_GROUP)],
                      rden, pr, store)
        yield

    def roll_cache():
        keep = lax.broadcasted_iota(jnp.int32, (ATT_KV, WINDOW), 1) < WINDOW - n_valid
        k_parts = _split3(knew.T)
        v_parts = _split3(vnew.T)
        group = 4
        rr = lax.broadcasted_iota(jnp.int32, (R, group * WINDOW), 0)
        cc = lax.broadcasted_iota(jnp.int32, (R, group * WINDOW), 1)
        hit = (cc % WINDOW == WINDOW - n_valid + rr % T) & (rr % T < n_valid)
        seq_off = rr // T - cc // WINDOW
        yield
        for g0 in range(0, nb, group):
            place = jnp.where(hit & (seq_off == g0), 1.0, 0.0).astype(BF16)
            new_k = _dot(k_parts[0], place) + _dot(k_parts[1], place) + _dot(k_parts[2], place)
            new_v = _dot(v_parts[0], place) + _dot(v_parts[1], place) + _dot(v_parts[2], place)
            for j in range(group):
                lanes = slice(j * WINDOW, (j + 1) * WINDOW)
                nk_ref[g0 + j] = jnp.where(keep, pltpu.roll(wk_ref[g0 + j], WINDOW - n_valid, 1), new_k[:, lanes])
                nv_ref[g0 + j] = jnp.where(keep, pltpu.roll(wv_ref[g0 + j], WINDOW - n_valid, 1), new_v[:, lanes])
            yield

    _run([(pair(0), 0), (pair(1), 2), (roll_cache(), 0)])
    y_ref[...] = _unpad_rows(x + _dot(o_s[...], wo_ref[...]), nb, n_valid)


def _sample_attn(x, k, v, win_k, win_v, sinks, g, wq, qg, wo, n_valid):
    NT, D = x.shape
    nseq = win_k.shape[0]
    nb = SAMPLE_NB
    R = nb * SAMPLE_PAD
    RC = nb * n_valid
    assert nseq % nb == 0 and NT == nseq * n_valid and RC % SUBLANES == 0
    const = lambda i: (0, 0)
    rows = lambda i: (i, 0)
    cache = pl.BlockSpec((nb, ATT_KV, WINDOW), lambda i: (i, 0, 0))
    vmem = _vmem_limit(2 * _nbytes((R, D), F32), _nbytes(wq.shape, BF16), _nbytes(wo.shape, BF16),
                       4 * _nbytes((nb, ATT_KV, WINDOW), F32), 2 * _nbytes((R, D), F32))
    return pl.pallas_call(
        functools.partial(_sample_attn_kernel, n_valid=n_valid),
        grid=(nseq // nb,),
        in_specs=[
            pl.BlockSpec(memory_space=pltpu.SMEM),
            pl.BlockSpec((RC, D), rows),
            pl.BlockSpec((1, D), const),
            pl.BlockSpec((D, D), const),
            pl.BlockSpec((1, LANES), const),
            pl.BlockSpec((RC, ATT_KV), rows),
            pl.BlockSpec((RC, ATT_KV), rows),
            cache, cache,
            pl.BlockSpec((D, D), const),
        ],
        out_specs=[pl.BlockSpec((RC, D), rows), cache, cache],
        out_shape=[jax.ShapeDtypeStruct((NT, D), F32),
                   jax.ShapeDtypeStruct(win_k.shape, F32), jax.ShapeDtypeStruct(win_v.shape, F32)],
        scratch_shapes=[pltpu.VMEM((R, D), BF16)],
        compiler_params=pltpu.CompilerParams(
            dimension_semantics=("arbitrary",), vmem_limit_bytes=vmem),
        name="sample_window_attention",
    )(sinks, x, g, wq, qg, k, v, win_k, win_v, wo)


def kernel(x_prompt, x_sample, state_mlstm_C, state_mlstm_n, state_mlstm_m, cache_win_k, cache_win_v,
           ml_norm_g, ml_w_in, ml_b_i, ml_b_f, ml_head_g, ml_w_out, kv_norm_g, w_kv, k_norm_g,
           att_norm_g, att_w_q, q_norm_g, att_sinks, att_w_o, mlp_norm_g, mlp_w1, mlp_w2):
    B, L, D = x_prompt.shape
    NS, LS, _ = x_sample.shape
    assert ml_w_in.shape[0] == 1 and att_w_q.shape[0] == 1 and mlp_w1.shape[0] == 2

    w_in = ml_w_in[0]
    wqkvo = w_in.astype(BF16)
    wg = jnp.pad(w_in[:, ML_QKVO:], ((0, 0), (0, LANES - 2 * ML_HEADS))).astype(BF16)
    bg = jnp.pad(jnp.concatenate([ml_b_i[0], ml_b_f[0]]), (0, LANES - 2 * ML_HEADS)).reshape(1, LANES)
    ml_g = ml_norm_g[0].reshape(1, D)
    hg = ml_head_g[0].reshape(1, ML_VO)
    wout = ml_w_out[0].astype(BF16)
    mlp_g = mlp_norm_g.reshape(2, 1, D)
    kv_g = kv_norm_g.reshape(1, D)
    kg = jnp.tile(k_norm_g, ATT_KVH).reshape(1, ATT_KV)
    att_g = att_norm_g[0].reshape(1, D)
    qg = jnp.tile(q_norm_g[0], 2).reshape(1, LANES)
    sinks = att_sinks[0]

    xp, p_C, p_n, p_m, w1, w2, wkv, wq, wo = _prompt_mixer(
        x_prompt, ml_g, wqkvo, wg, bg, hg, wout,
        [mlp_w1.reshape(2 * D, D_FF), mlp_w2.reshape(2 * D_FF, D), w_kv, att_w_q[0], att_w_o[0]])
    w1 = w1.reshape(2, D, D_FF)
    w2 = w2.reshape(2, D_FF, D)
    xp = _mlp(xp.reshape(B * L, D), mlp_g[0], w1, w2, 0)
    xp, p_wk, p_wv = _prompt_attn(xp.reshape(B, L, D), sinks, att_g, wq, qg, (kv_g, wkv, kg), wo)
    y_prompt = _mlp(xp.reshape(B * L, D), mlp_g[1], w1, w2, 1).reshape(B, L, D)
    p_wk = p_wk.reshape(B, WINDOW, ATT_KVH, ATT_HD)
    p_wv = p_wv.reshape(B, WINDOW, ATT_KVH, ATT_HD)

    m0 = jnp.broadcast_to(state_mlstm_m[0][:, :, None], (NS, ML_HEADS, LANES))
    xs, s_C, s_n, s_m = _sample_mixer(x_sample.reshape(NS * LS, D), ml_g, wqkvo, wg, bg, hg, wout,
                                      state_mlstm_C[0], state_mlstm_n[0], m0, LS)
    xs, ks, vs = _mlp(xs, mlp_g[0], w1, w2, 0, kv=(kv_g, wkv, kg))
    to_t = lambda c: c.transpose(0, 2, 3, 1).reshape(NS, ATT_KV, WINDOW)
    from_t = lambda c: c.reshape(NS, ATT_KVH, ATT_HD, WINDOW).transpose(0, 3, 1, 2)
    xs, s_wk, s_wv = _sample_attn(xs, ks, vs, to_t(cache_win_k), to_t(cache_win_v),
                                  sinks, att_g, wq, qg, wo, LS)
    y_sample = _mlp(xs, mlp_g[1], w1, w2, 1).reshape(NS, LS, D)

    return (y_prompt, y_sample,
            p_C[None], p_n[None], p_m[None, :, :, 0], p_wk, p_wv,
            s_C[None], s_n[None], s_m[None, :, :, 0],
            from_t(s_wk), from_t(s_wv))
```

```python
import functools

import jax
import jax.numpy as jnp
from jax import lax
from jax.experimental import pallas as pl
from jax.experimental.pallas import tpu as pltpu

F32 = jnp.float32
BF16 = jnp.bfloat16

D_MODEL = 1024
ML_HEADS = 4
ML_DK = 128
ML_DV = 256
ML_QK = ML_HEADS * ML_DK
ML_VO = ML_HEADS * ML_DV
ML_QKVO = 2 * ML_QK + 2 * ML_VO
GATE_SOFTCAP = 15.0
ATT_HD = 64
ATT_QH = 16
ATT_KVH = 4
ATT_GROUP = 4
ATT_KV = ATT_KVH * ATT_HD
WINDOW = 128
D_FF = 4 * D_MODEL
EPS = 1e-6
LOG2E = 1.4426950408889634

LANES = 128
SUBLANES = 8
VMEM_LIMIT_CAP = 56 * 1024 * 1024

PROMPT_CHUNK = 256
PROMPT_ROWS = 2
PROJ_COLS = 512
SAMPLE_PAD = SUBLANES
SAMPLE_NB = 16
ATTN_ROWS = 1024
ATTN_CHAIN_BLOCKS = 2
ROW_TILE = 1024
FF_TILE = 1024


def _vmem_limit(*block_bytes):
    need = 4 * sum(block_bytes) + (8 << 20)
    return int(min(max(need, 32 << 20), VMEM_LIMIT_CAP))


def _nbytes(shape, dtype):
    n = 1
    for s in shape:
        n *= s
    return n * jnp.dtype(dtype).itemsize


def _rms(x, g):
    return x * lax.rsqrt(jnp.mean(x * x, axis=-1, keepdims=True) + EPS) * g


def _dot(a, b):
    return jnp.dot(a, b, preferred_element_type=F32)


def _split3(x):
    hi = x.astype(BF16)
    r1 = x - hi.astype(F32)
    mid = r1.astype(BF16)
    lo = (r1 - mid.astype(F32)).astype(BF16)
    return hi, mid, lo


def _dot_exactish(m01, x):
    hi, mid, lo = _split3(x)
    return _dot(m01, hi) + _dot(m01, mid) + _dot(m01, lo)


def _pad_rows(x_c, nb, n_valid):
    rows, cols = nb * SAMPLE_PAD, nb * n_valid
    r = lax.broadcasted_iota(jnp.int32, (rows, cols), 0)
    c = lax.broadcasted_iota(jnp.int32, (rows, cols), 1)
    sel = (r % SAMPLE_PAD < n_valid) & (c == (r // SAMPLE_PAD) * n_valid + r % SAMPLE_PAD)
    return _dot_exactish(jnp.where(sel, 1.0, 0.0).astype(BF16), x_c)


def _unpad_rows(x_p, nb, n_valid):
    rows, cols = nb * n_valid, nb * SAMPLE_PAD
    r = lax.broadcasted_iota(jnp.int32, (rows, cols), 0)
    c = lax.broadcasted_iota(jnp.int32, (rows, cols), 1)
    sel = c == (r // n_valid) * SAMPLE_PAD + r % n_valid
    return _dot_exactish(jnp.where(sel, 1.0, 0.0).astype(BF16), x_p)


def _gate_act(z):
    cap = GATE_SOFTCAP * jnp.tanh(z * (1.0 / GATE_SOFTCAP))
    lsig = jnp.minimum(cap, 0.0) - jnp.log1p(jnp.exp(-jnp.abs(cap)))
    lane = lax.broadcasted_iota(jnp.int32, z.shape, 1)
    return jnp.where(lane < ML_HEADS, cap, lsig)


_DONE = object()


def _rounds(chains):
    live = [(c[0], c[1], c[2] if len(c) > 2 else 1) for c in chains]
    rnd = 0
    while live:
        for item in list(live):
            gen, start, stride = item
            if rnd >= start and (rnd - start) % stride == 0 and next(gen, _DONE) is _DONE:
                live.remove(item)
        rnd += 1
        yield


def _run(chains):
    for _ in _rounds(chains):
        pass


def _head_out(hh, po, hg):
    hn = hh * lax.rsqrt(jnp.mean(hh * hh, axis=-1, keepdims=True) + EPS) * hg
    return jax.nn.sigmoid(po) * hn


def _head_slices(p, h):
    q = p[:, h * ML_DK:(h + 1) * ML_DK]
    k = p[:, ML_QK + h * ML_DK:ML_QK + (h + 1) * ML_DK] * (ML_DK ** -0.5)
    v = p[:, 2 * ML_QK + h * ML_DV:2 * ML_QK + (h + 1) * ML_DV]
    po = p[:, 2 * ML_QK + ML_VO + h * ML_DV:2 * ML_QK + ML_VO + (h + 1) * ML_DV]
    return q, k, v, po


def _prompt_mixer_kernel(xc_ref, xnext_ref, g_ref, wqkvo_ref, wg_ref, bg_ref, hg_ref, wout_ref, *rest,
                         chunks_per_seq, n_cast):
    cast_in = rest[:n_cast]
    y_ref, C_ref, n_ref, m_ref = rest[n_cast:n_cast + 4]
    cast_out = rest[n_cast + 4:2 * n_cast + 4]
    p_s, gz_s = rest[2 * n_cast + 4:]
    R, T = xc_ref.shape[0], xc_ref.shape[1]
    f = pl.program_id(0)
    slot = f % 2

    def project(x_ref, s, r):
        xn = _rms(x_ref[r], g_ref[...]).astype(BF16)
        yield
        for j in range(ML_QKVO // PROJ_COLS):
            cols = slice(j * PROJ_COLS, (j + 1) * PROJ_COLS)
            p_s[s, r, :, cols] = _dot(xn, wqkvo_ref[:, cols])
            yield
        gz_s[s, r] = _dot(xn, wg_ref[...])
        yield

    @pl.when(f == 0)
    def _():
        _run([(project(xc_ref, 0, r), 0) for r in range(R)])

    @pl.when(f % chunks_per_seq == 0)
    def _():
        C_ref[...] = jnp.zeros_like(C_ref)
        n_ref[...] = jnp.zeros_like(n_ref)
        m_ref[...] = jnp.zeros_like(m_ref)

    def recurrence(r):
        G = _gate_act(gz_s[slot, r] + bg_ref[...])
        row = lax.broadcasted_iota(jnp.int32, (T, T), 0)
        col = lax.broadcasted_iota(jnp.int32, (T, T), 1)
        causal = col <= row
        Bc = _dot_exactish(jnp.where(causal, 1.0, 0.0).astype(BF16), G)
        yield
        Gt = G.T
        Bt = Bc.T
        yield
        parts = [None] * ML_HEADS

        def head(h):
            def cols(base, width):
                return p_s[slot, r, :, base + h * width:base + (h + 1) * width]

            q = cols(0, ML_DK)
            k = cols(ML_QK, ML_DK) * (ML_DK ** -0.5)
            v = cols(2 * ML_QK, ML_DV)
            qc, kc, vc = q.astype(BF16), k.astype(BF16), v.astype(BF16)
            qk = lax.dot_general(qc, kc, (((1,), (1,)), ((), ())), preferred_element_type=F32)
            yield
            li_col = G[:, h:h + 1]
            b_col = Bc[:, ML_HEADS + h:ML_HEADS + h + 1]
            dmat = jnp.where(causal, b_col - Bt[ML_HEADS + h:ML_HEADS + h + 1, :] + Gt[h:h + 1, :], -jnp.inf)
            m_prev = m_ref[r, h:h + 1, 0:1]
            inter = b_col + m_prev
            m_t = jnp.maximum(inter, jnp.max(dmat, axis=1, keepdims=True))
            s = qk * jnp.exp(dmat - m_t)
            a_inter = jnp.exp(inter - m_t)
            den = (jnp.sum(s, axis=1, keepdims=True)
                   + a_inter * jnp.sum(q * n_ref[r, h:h + 1, :], axis=1, keepdims=True))
            yield
            C = C_ref[r, h]
            num = _dot(s.astype(BF16), vc) + a_inter * _dot(qc, C.astype(BF16))
            yield
            hh = num * (1.0 / jnp.maximum(jnp.abs(den), jnp.exp(-m_t)))
            hs = _head_out(hh, cols(2 * ML_QK + ML_VO, ML_DV), hg_ref[:, h * ML_DV:(h + 1) * ML_DV])
            yield
            parts[h] = _dot(hs.astype(BF16), wout_ref[h * ML_DV:(h + 1) * ML_DV, :])
            yield
            m_new = m_t[T - 1:T, :]
            b_last = b_col[T - 1:T, :]
            decay = jnp.exp(b_last + m_prev - m_new)
            kw = k * jnp.exp(b_last - b_col + li_col - m_new)
            C_ref[r, h] = decay * C + lax.dot_general(kw.astype(BF16), vc, (((0,), (0,)), ((), ())),
                                                      preferred_element_type=F32)
            n_ref[r, h:h + 1, :] = decay * n_ref[r, h:h + 1, :] + jnp.sum(kw, axis=0, keepdims=True)
            m_ref[r, h:h + 1, :] = jnp.broadcast_to(m_new, (1, LANES))
            yield

        yield from _rounds([(head(h), 2 * h) for h in range(ML_HEADS)])
        y_ref[r] = xc_ref[r] + ((parts[0] + parts[1]) + (parts[2] + parts[3]))

    def casts():
        for src, dst in zip(cast_in, cast_out):
            dst[...] = src[...].astype(BF16)
            yield

    _run([(project(xnext_ref, 1 - slot, r), 0) for r in range(R)] + [(recurrence(r), r) for r in range(R)]
         + [(casts(), 0)])


def _prompt_mixer(x, g, wqkvo, wg, bg, hg, wout, to_cast):
    B, L, D = x.shape
    T = PROMPT_CHUNK
    assert L % T == 0
    R = PROMPT_ROWS
    assert B % R == 0
    nc = L // T
    steps = (B // R) * nc
    const = lambda f: (0, 0)
    cur = lambda f: (f // nc, f % nc, 0)
    nxt = lambda f: (jnp.minimum(f + 1, steps - 1) // nc, jnp.minimum(f + 1, steps - 1) % nc, 0)
    once = dict(pipeline_mode=pl.Buffered(1))
    vmem = _vmem_limit(3 * _nbytes((R, T, D), F32), _nbytes(wqkvo.shape, BF16) // 2,
                       _nbytes(wout.shape, BF16) // 2, _nbytes((R, ML_HEADS, ML_DK, ML_DV), F32),
                       2 * _nbytes((R, T, ML_QKVO), F32))
    for w in to_cast:
        assert w.shape[0] % (steps * 2 * SUBLANES) == 0
    cast_specs = [pl.BlockSpec((w.shape[0] // steps, w.shape[1]), lambda f: (f, 0)) for w in to_cast]
    outs = pl.pallas_call(
        functools.partial(_prompt_mixer_kernel, chunks_per_seq=nc, n_cast=len(to_cast)),
        grid=(steps,),
        in_specs=[
            pl.BlockSpec((R, T, D), cur),
            pl.BlockSpec((R, T, D), nxt),
            pl.BlockSpec((1, D), const),
            pl.BlockSpec((D, ML_QKVO), const, **once),
            pl.BlockSpec((D, LANES), const, **once),
            pl.BlockSpec((1, LANES), const),
            pl.BlockSpec((1, ML_VO), const),
            pl.BlockSpec((ML_VO, D), const, **once),
        ] + cast_specs,
        out_specs=[
            pl.BlockSpec((R, T, D), cur),
            pl.BlockSpec((R, ML_HEADS, ML_DK, ML_DV), lambda f: (f // nc, 0, 0, 0)),
            pl.BlockSpec((R, ML_HEADS, ML_DK), lambda f: (f // nc, 0, 0)),
            pl.BlockSpec((R, ML_HEADS, LANES), lambda f: (f // nc, 0, 0)),
        ] + cast_specs,
        out_shape=[
            jax.ShapeDtypeStruct((B, L, D), F32),
            jax.ShapeDtypeStruct((B, ML_HEADS, ML_DK, ML_DV), F32),
            jax.ShapeDtypeStruct((B, ML_HEADS, ML_DK), F32),
            jax.ShapeDtypeStruct((B, ML_HEADS, LANES), F32),
        ] + [jax.ShapeDtypeStruct(w.shape, BF16) for w in to_cast],
        scratch_shapes=[pltpu.VMEM((2, R, T, ML_QKVO), F32), pltpu.VMEM((2, R, T, LANES), F32)],
        compiler_params=pltpu.CompilerParams(
            dimension_semantics=("arbitrary",), vmem_limit_bytes=vmem),
        name="prompt_mlstm_mixer",
    )(x, x, g, wqkvo, wg, bg, hg, wout, *to_cast)
    return outs


def _sample_mixer_kernel(x_ref, g_ref, wqkvo_ref, wg_ref, bg_ref, hg_ref, wout_ref,
                         C0_ref, n0_ref, m0_ref,
                         y_ref, C_ref, n_ref, m_ref, hs_s, *, n_valid):
    T = SAMPLE_PAD
    nb = C0_ref.shape[0]
    R = nb * T
    x = _pad_rows(x_ref[...], nb, n_valid)
    xn = _rms(x, g_ref[...]).astype(BF16)
    p = _dot(xn, wqkvo_ref[...])

    rowt = lax.broadcasted_iota(jnp.int32, (R, LANES), 0) % T
    lane = lax.broadcasted_iota(jnp.int32, (R, LANES), 1)
    G = jnp.where(rowt < n_valid, _gate_act(_dot(xn, wg_ref[...]) + bg_ref[...]),
                  jnp.where(lane < ML_HEADS, -jnp.inf, 0.0))
    row = lax.broadcasted_iota(jnp.int32, (R, R), 0)
    col = lax.broadcasted_iota(jnp.int32, (R, R), 1)
    causal = (row // T == col // T) & (col <= row)
    Bc = _dot_exactish(jnp.where(causal, 1.0, 0.0).astype(BF16), jnp.where(lane < ML_HEADS, 0.0, G))
    Gt = G.T
    Bt = Bc.T

    def per_seq(fn):
        return jnp.concatenate([fn(a) for a in range(nb)], axis=0)

    def seq_last(colvec):
        return per_seq(lambda a: jnp.broadcast_to(colvec[T * a + T - 1:T * a + T, :], (T, 1)))

    def head(h):
        q, k, v, po = _head_slices(p, h)
        li_col = G[:, h:h + 1]
        b_col = Bc[:, ML_HEADS + h:ML_HEADS + h + 1]
        li_row = Gt[h:h + 1, :]
        b_row = Bt[ML_HEADS + h:ML_HEADS + h + 1, :]
        dmat = jnp.where(causal, b_col - (b_row - li_row), -jnp.inf)
        m_prev = per_seq(lambda a: jnp.broadcast_to(m0_ref[a, h:h + 1, 0:1], (T, 1)))
        n_rows = per_seq(lambda a: jnp.broadcast_to(n0_ref[a, h:h + 1, :], (T, ML_DK)))
        qc, kc, vc = q.astype(BF16), k.astype(BF16), v.astype(BF16)
        qk = lax.dot_general(qc, kc, (((1,), (1,)), ((), ())), preferred_element_type=F32)
        yield
        inter = b_col + m_prev
        m_t = jnp.maximum(inter, jnp.max(dmat, axis=1, keepdims=True))
        s = qk * jnp.exp(dmat - m_t)
        a_inter = jnp.exp(inter - m_t)
        yield
        qC = per_seq(lambda a: _dot(q[T * a:T * a + T, :], C0_ref[a, h]))
        yield
        num = _dot(s.astype(BF16), vc) + a_inter * qC
        den = jnp.sum(s, axis=1, keepdims=True) + a_inter * jnp.sum(q * n_rows, axis=1, keepdims=True)
        hh = num * (1.0 / jnp.maximum(jnp.abs(den), jnp.exp(-m_t)))
        hs_s[:, h * ML_DV:(h + 1) * ML_DV] = _head_out(
            hh, po, hg_ref[:, h * ML_DV:(h + 1) * ML_DV]).astype(BF16)
        yield
        m_new = seq_last(m_t)
        b_last = seq_last(b_col)
        decay = jnp.exp(b_last + m_prev - m_new)
        kw = k * jnp.exp(b_last - b_col + li_col - m_new)
        for a in range(nb):
            dec = decay[T * a:T * a + 1, :]
            kw_a = kw[T * a:T * a + T, :]
            upd = lax.dot_general(kw_a, v[T * a:T * a + T, :], (((0,), (0,)), ((), ())),
                                  preferred_element_type=F32)
            C_ref[a, h] = dec * C0_ref[a, h] + upd
            n_ref[a, h:h + 1, :] = dec * n0_ref[a, h:h + 1, :] + jnp.sum(kw_a, axis=0, keepdims=True)
            m_ref[a, h:h + 1, :] = jnp.broadcast_to(m_new[T * a:T * a + 1, :], (1, LANES))
            if a % 4 == 3:
                yield

    _run([(head(h), h) for h in range(ML_HEADS)])
    y_ref[...] = _unpad_rows(x + _dot(hs_s[...], wout_ref[...]), nb, n_valid)


def _sample_mixer(x, g, wqkvo, wg, bg, hg, wout, C0, n0, m0, n_valid):
    NT, D = x.shape
    nseq = C0.shape[0]
    nb = SAMPLE_NB
    R = nb * SAMPLE_PAD
    RC = nb * n_valid
    assert nseq % nb == 0 and NT == nseq * n_valid and RC % SUBLANES == 0
    const = lambda i: (0, 0)
    state_specs = [
        pl.BlockSpec((nb, ML_HEADS, ML_DK, ML_DV), lambda i: (i, 0, 0, 0)),
        pl.BlockSpec((nb, ML_HEADS, ML_DK), lambda i: (i, 0, 0)),
        pl.BlockSpec((nb, ML_HEADS, LANES), lambda i: (i, 0, 0)),
    ]
    vmem = _vmem_limit(2 * _nbytes((R, D), F32), _nbytes(wqkvo.shape, BF16), _nbytes(wout.shape, BF16),
                       2 * _nbytes((nb, ML_HEADS, ML_DK, ML_DV), F32), _nbytes((R, ML_QKVO), F32))
    return pl.pallas_call(
        functools.partial(_sample_mixer_kernel, n_valid=n_valid),
        grid=(nseq // nb,),
        in_specs=[
            pl.BlockSpec((RC, D), lambda i: (i, 0)),
            pl.BlockSpec((1, D), const),
            pl.BlockSpec((D, ML_QKVO), const),
            pl.BlockSpec((D, LANES), const),
            pl.BlockSpec((1, LANES), const),
            pl.BlockSpec((1, ML_VO), const),
            pl.BlockSpec((ML_VO, D), const),
        ] + state_specs,
        out_specs=[pl.BlockSpec((RC, D), lambda i: (i, 0))] + state_specs,
        out_shape=[
            jax.ShapeDtypeStruct((NT, D), F32),
            jax.ShapeDtypeStruct(C0.shape, F32),
            jax.ShapeDtypeStruct(n0.shape, F32),
            jax.ShapeDtypeStruct(m0.shape, F32),
        ],
        scratch_shapes=[pltpu.VMEM((R, ML_VO), BF16)],
        compiler_params=pltpu.CompilerParams(
            dimension_semantics=("arbitrary",), vmem_limit_bytes=vmem),
        name="sample_mlstm_mixer",
    )(x, g, wqkvo, wg, bg, hg, wout, C0, n0, m0)


def _dot_exactish_right(x, m01):
    hi, mid, lo = _split3(x)
    return _dot(hi, m01) + _dot(mid, m01) + _dot(lo, m01)


def _mlp_kernel(x_ref, g_ref, w1_ref, w2_ref, *rest):
    x = x_ref[...]
    xn = _rms(x, g_ref[...]).astype(BF16)
    acc = x
    for c in range(D_FF // FF_TILE):
        hcol = _dot(xn, w1_ref[:, c * FF_TILE:(c + 1) * FF_TILE])
        hcol = jnp.square(jnp.maximum(hcol, 0.0)).astype(BF16)
        acc = acc + _dot(hcol, w2_ref[c * FF_TILE:(c + 1) * FF_TILE, :])
    if len(rest) == 1:
        (y_ref,) = rest
        y_ref[...] = acc
        return
    gkv_ref, wkv_ref, kg_ref, y_ref, k_ref, v_ref = rest
    y_ref[...] = acc
    kv = _dot(_rms(acc, gkv_ref[...]).astype(BF16), wkv_ref[...])
    kraw = kv[:, :ATT_KV]
    v_ref[...] = kv[:, ATT_KV:]
    r = lax.broadcasted_iota(jnp.int32, (ATT_KV, ATT_KV), 0) // ATT_HD
    c = lax.broadcasted_iota(jnp.int32, (ATT_KV, ATT_KV), 1) // ATT_HD
    seg = jnp.where(r == c, 1.0, 0.0).astype(BF16)
    ss = _dot_exactish_right(kraw * kraw, seg)
    k_ref[...] = kraw * lax.rsqrt(ss * (1.0 / ATT_HD) + EPS) * kg_ref[...]


def _mlp(x, g, w1, w2, layer, kv=None):
    N, D = x.shape
    tm = min(ROW_TILE, N)
    assert N % tm == 0
    const = lambda i: (0, 0)
    rows = lambda i: (i, 0)
    once = dict(pipeline_mode=pl.Buffered(1))
    in_specs = [
        pl.BlockSpec((tm, D), rows),
        pl.BlockSpec((1, D), const),
        pl.BlockSpec((None, D, D_FF), lambda i: (layer, 0, 0), **once),
        pl.BlockSpec((None, D_FF, D), lambda i: (layer, 0, 0), **once),
    ]
    out_specs = [pl.BlockSpec((tm, D), rows)]
    out_shape = [jax.ShapeDtypeStruct((N, D), F32)]
    args = [x, g, w1, w2]
    if kv is not None:
        in_specs += [pl.BlockSpec((1, D), const), pl.BlockSpec((D, 2 * ATT_KV), const, **once),
                     pl.BlockSpec((1, ATT_KV), const)]
        out_specs += [pl.BlockSpec((tm, ATT_KV), rows)] * 2
        out_shape += [jax.ShapeDtypeStruct((N, ATT_KV), F32)] * 2
        args += list(kv)
    vmem = _vmem_limit(2 * _nbytes((tm, D), F32), _nbytes(w1.shape[1:], BF16) // 2,
                       _nbytes(w2.shape[1:], BF16) // 2, 2 * _nbytes((tm, FF_TILE), F32))
    out = pl.pallas_call(
        _mlp_kernel,
        grid=(N // tm,),
        in_specs=in_specs,
        out_specs=out_specs,
        out_shape=out_shape,
        compiler_params=pltpu.CompilerParams(
            dimension_semantics=("arbitrary",), vmem_limit_bytes=vmem),
        name="sqrelu_mlp",
    )(*args)
    return out[0] if kv is None else out


def _pair_queries(qraw, pr, qscale, col0=0):
    TQ = qraw.shape[0]
    lo = lax.broadcasted_iota(jnp.int32, (TQ, LANES), 1) < ATT_HD
    out = []
    for e in range(2):
        kvh = 2 * pr + e
        for g in range(ATT_GROUP):
            cc, half = divmod(g, 2)
            c0 = (2 * kvh + cc) * LANES - col0
            q2 = qraw[:, c0:c0 + LANES]
            qm = jnp.where(lo, q2, 0.0) if half == 0 else jnp.where(lo, 0.0, q2)
            ss = jnp.sum(qm * qm, axis=1, keepdims=True)
            qn = qm * lax.rsqrt(ss * (1.0 / ATT_HD) + EPS)
            if qscale is not None:
                qn = qn * qscale
            out.append(qn if half == e else pltpu.roll(qn, ATT_HD, 1))
    return out


def _pair_outputs(o_heads, rden, pr, store):
    TQ = o_heads[0].shape[0]
    lo = lax.broadcasted_iota(jnp.int32, (TQ, LANES), 1) < ATT_HD
    for e in range(2):
        kvh = 2 * pr + e
        for cc in range(2):
            tiles = []
            for half in range(2):
                o = o_heads[e * ATT_GROUP + 2 * cc + half] * rden[e * ATT_GROUP + 2 * cc + half]
                tiles.append(o if half == e else pltpu.roll(o, ATT_HD, 1))
            store((2 * kvh + cc) * LANES, jnp.where(lo, tiles[0], tiles[1]))


def _softmax_with_sink(parts, sink):
    assert all(s.shape == parts[0].shape for s in parts)
    M = jnp.maximum(sink, jnp.max(functools.reduce(jnp.maximum, parts), axis=1, keepdims=True))
    ps = [jnp.exp2(s - M) for s in parts]
    den = jnp.exp2(sink - M) + jnp.sum(functools.reduce(jnp.add, ps), axis=1, keepdims=True)
    return ps, 1.0 / den


def _prompt_attn_kernel(sinks_ref, x_ref, g_ref, wq_ref, qg_ref, gkv_ref, wkv_ref, kg_ref, wo_ref,
                        y_ref, pk_ref, pv_ref, kprev_s, vprev_s):
    TQ = WINDOW
    TR = x_ref.shape[0]
    kj = lax.broadcasted_iota(jnp.int32, (WINDOW + TQ, TQ), 0)
    qi = lax.broadcasted_iota(jnp.int32, (WINDOW + TQ, TQ), 1)
    band = (qi + WINDOW - kj >= 0) & (qi - kj <= 0)
    band_first = band & ((kj >= WINDOW) | (pl.program_id(1) > 0))
    kscale = qg_ref[...] * (ATT_HD ** -0.5 * LOG2E)

    @pl.when(pl.program_id(1) == 0)
    def _():
        kprev_s[...] = jnp.zeros_like(kprev_s)
        vprev_s[...] = jnp.zeros_like(vprev_s)

    n_chains = TR // (ATTN_CHAIN_BLOCKS * TQ)
    kv_rows = [None] * n_chains

    PW = 2 * ATT_GROUP * ATT_HD
    zeros_hd = jnp.zeros((ATT_HD, TQ), F32)
    ones_rows = jnp.ones((2 * SUBLANES, WINDOW + TQ), F32)

    def chain(cb):
        nsb = ATTN_CHAIN_BLOCKS
        rows = slice(cb * nsb * TQ, (cb + 1) * nsb * TQ)
        x = x_ref[rows, :]
        kv = _dot(_rms(x, gkv_ref[...]).astype(BF16), wkv_ref[...])
        kraw = kv[:, :ATT_KV]
        seg_r = lax.broadcasted_iota(jnp.int32, (ATT_KV, ATT_KV), 0) // ATT_HD
        seg_c = lax.broadcasted_iota(jnp.int32, (ATT_KV, ATT_KV), 1) // ATT_HD
        ss = _dot_exactish_right(kraw * kraw, jnp.where(seg_r == seg_c, 1.0, 0.0).astype(BF16))
        k_blk = kraw * lax.rsqrt(ss * (1.0 / ATT_HD) + EPS) * kg_ref[...]
        v_blk = kv[:, ATT_KV:]
        kv_rows[cb] = (k_blk, v_blk)
        if cb == 0:
            prev = (kprev_s[...], vprev_s[...])
        if cb == n_chains - 1:
            for dst, blk in ((kprev_s, k_blk), (pk_ref, k_blk), (vprev_s, v_blk), (pv_ref, v_blk)):
                dst[...] = blk[nsb * TQ - WINDOW:, :]
        yield
        if cb > 0:
            prev = tuple(blk[nsb * TQ - WINDOW:, :] for blk in kv_rows[cb - 1])
        k_all = jnp.concatenate([prev[0], k_blk], axis=0)
        v_all = jnp.concatenate([prev[1], v_blk], axis=0)
        slabs = []
        for pr in range(ATT_KVH // 2):
            sl = slice(pr * LANES, (pr + 1) * LANES)
            slabs.append(((k_all[:, sl] * kscale).astype(BF16), v_all[:, sl].T))
        xn = _rms(x, g_ref[...]).astype(BF16)
        acc = x
        for pr in range(ATT_KVH // 2):
            kslab, vslab_t = slabs[pr]
            qt_all = _dot(xn, wq_ref[:, pr * PW:(pr + 1) * PW]).T
            yield
            outs = []
            for i in range(nsb):
                sb = cb * nsb + i
                keys = slice(i * TQ, i * TQ + WINDOW + TQ)
                mask = band_first if sb == 0 else band
                qt = qt_all[:, i * TQ:(i + 1) * TQ]
                tiles, sink_rows = [], []
                for e in range(2):
                    for g in range(ATT_GROUP):
                        blk = qt[(e * ATT_GROUP + g) * ATT_HD:(e * ATT_GROUP + g + 1) * ATT_HD, :]
                        qn = blk * lax.rsqrt(jnp.sum(blk * blk, axis=0, keepdims=True) * (1.0 / ATT_HD) + EPS)
                        tiles.append(jnp.concatenate([qn, zeros_hd] if e == 0 else [zeros_hd, qn], axis=0))
                        sink_rows.append(jnp.full((1, TQ), sinks_ref[(2 * pr + e) * ATT_GROUP + g] * LOG2E, F32))
                qmat = jnp.concatenate(tiles, axis=1).astype(BF16)
                st = _dot(kslab[keys, :], qmat)
                yield
                parts = []
                for e in range(2):
                    ps, ms = [], []
                    for g in range(ATT_GROUP):
                        c0 = (e * ATT_GROUP + g) * TQ
                        s_h = jnp.where(mask, st[:, c0:c0 + TQ], -jnp.inf)
                        m_h = jnp.maximum(jnp.max(s_h, axis=0, keepdims=True), sink_rows[e * ATT_GROUP + g])
                        ps.append(jnp.exp2(s_h - m_h).astype(BF16))
                        ms.append(m_h)
                    p_e = jnp.concatenate(ps, axis=1)
                    v_aug = jnp.concatenate([vslab_t[e * ATT_HD:(e + 1) * ATT_HD, keys], ones_rows],
                                            axis=0).astype(BF16)
                    ot = _dot(v_aug, p_e)
                    yield
                    sink_term = jnp.exp2(jnp.concatenate(sink_rows[e * ATT_GROUP:(e + 1) * ATT_GROUP], axis=1)
                                         - jnp.concatenate(ms, axis=1))
                    on = ot[0:ATT_HD, :] * (1.0 / (ot[ATT_HD:ATT_HD + 1, :] + sink_term))
                    parts += [on[:, g * TQ:(g + 1) * TQ] for g in range(ATT_GROUP)]
                outs.append(jnp.concatenate(parts, axis=0))
            o_pair = jnp.concatenate(outs, axis=1).T.astype(BF16)
            acc = acc + _dot(o_pair, wo_ref[pr * PW:(pr + 1) * PW, :])
            yield
        y_ref[rows, :] = acc

    _run([(chain(cb), 2 * cb) for cb in range(n_chains)])


def _prompt_attn(x, sinks, g, wq, qg, kv, wo):
    B, L, D = x.shape
    TR = ATTN_ROWS
    assert L % TR == 0 and TR % (ATTN_CHAIN_BLOCKS * WINDOW) == 0
    gkv, wkv, kg = kv
    const = lambda b, i: (0, 0)
    cur = lambda b, i: (b, i, 0)
    win = pl.BlockSpec((None, WINDOW, ATT_KV), lambda b, i: (b, 0, 0))
    vmem = _vmem_limit(2 * _nbytes((TR, D), F32), _nbytes(wq.shape, BF16), _nbytes(wo.shape, BF16),
                       _nbytes(wkv.shape, BF16), 8 * _nbytes((8 * WINDOW, 2 * WINDOW), F32))
    return pl.pallas_call(
        _prompt_attn_kernel,
        grid=(B, L // TR),
        in_specs=[
            pl.BlockSpec(memory_space=pltpu.SMEM),
            pl.BlockSpec((None, TR, D), cur),
            pl.BlockSpec((1, D), const),
            pl.BlockSpec((D, D), const),
            pl.BlockSpec((1, LANES), const),
            pl.BlockSpec((1, D), const),
            pl.BlockSpec((D, 2 * ATT_KV), const),
            pl.BlockSpec((1, ATT_KV), const),
            pl.BlockSpec((D, D), const),
        ],
        out_specs=[pl.BlockSpec((None, TR, D), cur), win, win],
        out_shape=[jax.ShapeDtypeStruct((B, L, D), F32),
                   jax.ShapeDtypeStruct((B, WINDOW, ATT_KV), F32), jax.ShapeDtypeStruct((B, WINDOW, ATT_KV), F32)],
        scratch_shapes=[pltpu.VMEM((WINDOW, ATT_KV), F32), pltpu.VMEM((WINDOW, ATT_KV), F32)],
        compiler_params=pltpu.CompilerParams(
            dimension_semantics=("arbitrary", "arbitrary"), vmem_limit_bytes=vmem),
        name="prompt_window_attention",
    )(sinks, x, g, wq, qg, gkv, wkv, kg, wo)


def _sample_attn_kernel(sinks_ref, x_ref, g_ref, wq_ref, qg_ref, kc_ref, vc_ref, wk_ref, wv_ref, wo_ref,
                        y_ref, nk_ref, nv_ref, o_s, *, n_valid):
    T = SAMPLE_PAD
    nb = wk_ref.shape[0]
    R = nb * T
    assert R == WINDOW
    x = _pad_rows(x_ref[...], nb, n_valid)
    q = _dot(_rms(x, g_ref[...]).astype(BF16), wq_ref[...])
    qscale = qg_ref[...] * (ATT_HD ** -0.5 * LOG2E)
    knew = _pad_rows(kc_ref[...], nb, n_valid)
    vnew = _pad_rows(vc_ref[...], nb, n_valid)

    cmask = (lax.broadcasted_iota(jnp.int32, (R, WINDOW), 1)
             >= lax.broadcasted_iota(jnp.int32, (R, WINDOW), 0) % T)
    row = lax.broadcasted_iota(jnp.int32, (R, R), 0)
    col = lax.broadcasted_iota(jnp.int32, (R, R), 1)
    nmask = (row // T == col // T) & (col % T <= row % T) & (col % T < n_valid)

    def store(off, val):
        o_s[:, off:off + LANES] = val.astype(BF16)

    def regroup(per_seq, i):
        return jnp.concatenate([per_seq[a][i * T:(i + 1) * T, :] for a in range(nb)], axis=0)

    def pair(pr):
        sl = slice(pr * LANES, (pr + 1) * LANES)
        qs = _pair_queries(q, pr, qscale)
        S_new = lax.dot_general(jnp.concatenate(qs, axis=0).astype(BF16), knew[:, sl].astype(BF16),
                                (((1,), (1,)), ((), ())), preferred_element_type=F32)
        yield
        sc = []
        for a in range(nb):
            q_a = jnp.concatenate([qi[T * a:T * a + T, :] for qi in qs], axis=0).astype(BF16)
            sc.append(_dot(q_a, wk_ref[a, sl, :].astype(BF16)))
            if a % 4 == 3:
                yield
        pcs, pns, rden = [], [], []
        for i in range(2 * ATT_GROUP):
            s_c = jnp.where(cmask, regroup(sc, i), -jnp.inf)
            s_n = jnp.where(nmask, S_new[i * R:(i + 1) * R, :], -jnp.inf)
            (p_c, p_n), r = _softmax_with_sink([s_c, s_n], sinks_ref[pr * 2 * ATT_GROUP + i] * LOG2E)
            pcs.append(p_c)
            pns.append(p_n)
            rden.append(r)
            if i % 2 == 1:
                yield
        O_new = _dot(jnp.concatenate(pns, axis=0).astype(BF16), vnew[:, sl].astype(BF16))
        yield
        oc = []
        for a in range(nb):
            p_a = jnp.concatenate([pc[T * a:T * a + T, :] for pc in pcs], axis=0).astype(BF16)
            oc.append(lax.dot_general(p_a, wv_ref[a, sl, :].astype(BF16), (((1,), (1,)), ((), ())),
                                      preferred_element_type=F32))
            if a % 4 == 3:
                yield
        _pair_outputs([O_new[i * R:(i + 1) * R, :] + regroup(oc, i) for i in range(2 * ATT_GROUP)],
                      rden, pr, store)
        yield

    def roll_cache():
        keep = lax.broadcasted_iota(jnp.int32, (ATT_KV, WINDOW), 1) < WINDOW - n_valid
        k_parts = _split3(knew.T)
        v_parts = _split3(vnew.T)
        group = 4
        rr = lax.broadcasted_iota(jnp.int32, (R, group * WINDOW), 0)
        cc = lax.broadcasted_iota(jnp.int32, (R, group * WINDOW), 1)
        hit = (cc % WINDOW == WINDOW - n_valid + rr % T) & (rr % T < n_valid)
        seq_off = rr // T - cc // WINDOW
        yield
        for g0 in range(0, nb, group):
            place = jnp.where(hit & (seq_off == g0), 1.0, 0.0).astype(BF16)
            new_k = _dot(k_parts[0], place) + _dot(k_parts[1], place) + _dot(k_parts[2], place)
            new_v = _dot(v_parts[0], place) + _dot(v_parts[1], place) + _dot(v_parts[2], place)
            for j in range(group):
                lanes = slice(j * WINDOW, (j + 1) * WINDOW)
                nk_ref[g0 + j] = jnp.where(keep, pltpu.roll(wk_ref[g0 + j], WINDOW - n_valid, 1), new_k[:, lanes])
                nv_ref[g0 + j] = jnp.where(keep, pltpu.roll(wv_ref[g0 + j], WINDOW - n_valid, 1), new_v[:, lanes])
            yield

    _run([(pair(0), 0), (pair(1), 2), (roll_cache(), 0)])
    y_ref[...] = _unpad_rows(x + _dot(o_s[...], wo_ref[...]), nb, n_valid)


def _sample_attn(x, k, v, win_k, win_v, sinks, g, wq, qg, wo, n_valid):
    NT, D = x.shape
    nseq = win_k.shape[0]
    nb = SAMPLE_NB
    R = nb * SAMPLE_PAD
    RC = nb * n_valid
    assert nseq % nb == 0 and NT == nseq * n_valid and RC % SUBLANES == 0
    const = lambda i: (0, 0)
    rows = lambda i: (i, 0)
    cache = pl.BlockSpec((nb, ATT_KV, WINDOW), lambda i: (i, 0, 0))
    vmem = _vmem_limit(2 * _nbytes((R, D), F32), _nbytes(wq.shape, BF16), _nbytes(wo.shape, BF16),
                       4 * _nbytes((nb, ATT_KV, WINDOW), F32), 2 * _nbytes((R, D), F32))
    return pl.pallas_call(
        functools.partial(_sample_attn_kernel, n_valid=n_valid),
        grid=(nseq // nb,),
        in_specs=[
            pl.BlockSpec(memory_space=pltpu.SMEM),
            pl.BlockSpec((RC, D), rows),
            pl.BlockSpec((1, D), const),
            pl.BlockSpec((D, D), const),
            pl.BlockSpec((1, LANES), const),
            pl.BlockSpec((RC, ATT_KV), rows),
            pl.BlockSpec((RC, ATT_KV), rows),
            cache, cache,
            pl.BlockSpec((D, D), const),
        ],
        out_specs=[pl.BlockSpec((RC, D), rows), cache, cache],
        out_shape=[jax.ShapeDtypeStruct((NT, D), F32),
                   jax.ShapeDtypeStruct(win_k.shape, F32), jax.ShapeDtypeStruct(win_v.shape, F32)],
        scratch_shapes=[pltpu.VMEM((R, D), BF16)],
        compiler_params=pltpu.CompilerParams(
            dimension_semantics=("arbitrary",), vmem_limit_bytes=vmem),
        name="sample_window_attention",
    )(sinks, x, g, wq, qg, k, v, win_k, win_v, wo)


def kernel(x_prompt, x_sample, state_mlstm_C, state_mlstm_n, state_mlstm_m, cache_win_k, cache_win_v,
           ml_norm_g, ml_w_in, ml_b_i, ml_b_f, ml_head_g, ml_w_out, kv_norm_g, w_kv, k_norm_g,
           att_norm_g, att_w_q, q_norm_g, att_sinks, att_w_o, mlp_norm_g, mlp_w1, mlp_w2):
    B, L, D = x_prompt.shape
    NS, LS, _ = x_sample.shape
    assert ml_w_in.shape[0] == 1 and att_w_q.shape[0] == 1 and mlp_w1.shape[0] == 2

    w_in = ml_w_in[0]
    wqkvo = w_in.astype(BF16)
    wg = jnp.pad(w_in[:, ML_QKVO:], ((0, 0), (0, LANES - 2 * ML_HEADS))).astype(BF16)
    bg = jnp.pad(jnp.concatenate([ml_b_i[0], ml_b_f[0]]), (0, LANES - 2 * ML_HEADS)).reshape(1, LANES)
    ml_g = ml_norm_g[0].reshape(1, D)
    hg = ml_head_g[0].reshape(1, ML_VO)
    wout = ml_w_out[0].astype(BF16)
    mlp_g = mlp_norm_g.reshape(2, 1, D)
    kv_g = kv_norm_g.reshape(1, D)
    kg = jnp.tile(k_norm_g, ATT_KVH).reshape(1, ATT_KV)
    att_g = att_norm_g[0].reshape(1, D)
    qg = jnp.tile(q_norm_g[0], 2).reshape(1, LANES)
    sinks = att_sinks[0]

    xp, p_C, p_n, p_m, w1, w2, wkv, wq, wo = _prompt_mixer(
        x_prompt, ml_g, wqkvo, wg, bg, hg, wout,
        [mlp_w1.reshape(2 * D, D_FF), mlp_w2.reshape(2 * D_FF, D), w_kv, att_w_q[0], att_w_o[0]])
    w1 = w1.reshape(2, D, D_FF)
    w2 = w2.reshape(2, D_FF, D)
    xp = _mlp(xp.reshape(B * L, D), mlp_g[0], w1, w2, 0)
    xp, p_wk, p_wv = _prompt_attn(xp.reshape(B, L, D), sinks, att_g, wq, qg, (kv_g, wkv, kg), wo)
    y_prompt = _mlp(xp.reshape(B * L, D), mlp_g[1], w1, w2, 1).reshape(B, L, D)
    p_wk = p_wk.reshape(B, WINDOW, ATT_KVH, ATT_HD)
    p_wv = p_wv.reshape(B, WINDOW, ATT_KVH, ATT_HD)

    m0 = jnp.broadcast_to(state_mlstm_m[0][:, :, None], (NS, ML_HEADS, LANES))
    xs, s_C, s_n, s_m = _sample_mixer(x_sample.reshape(NS * LS, D), ml_g, wqkvo, wg, bg, hg, wout,
                                      state_mlstm_C[0], state_mlstm_n[0], m0, LS)
    xs, ks, vs = _mlp(xs, mlp_g[0], w1, w2, 0, kv=(kv_g, wkv, kg))
    to_t = lambda c: c.transpose(0, 2, 3, 1).reshape(NS, ATT_KV, WINDOW)
    from_t = lambda c: c.reshape(NS, ATT_KVH, ATT_HD, WINDOW).transpose(0, 3, 1, 2)
    xs, s_wk, s_wv = _sample_attn(xs, ks, vs, to_t(cache_win_k), to_t(cache_win_v),
                                  sinks, att_g, wq, qg, wo, LS)
    y_sample = _mlp(xs, mlp_g[1], w1, w2, 1).reshape(NS, LS, D)

    return (y_prompt, y_sample,
            p_C[None], p_n[None], p_m[None, :, :, 0], p_wk, p_wv,
            s_C[None], s_n[None], s_m[None, :, :, 0],
            from_t(s_wk), from_t(s_wv))
```

```python
import functools

import jax
import jax.numpy as jnp
from jax import lax
from jax.experimental import pallas as pl
from jax.experimental.pallas import tpu as pltpu

F32 = jnp.float32
BF16 = jnp.bfloat16

D_MODEL = 1024
ML_HEADS = 4
ML_DK = 128
ML_DV = 256
ML_QK = ML_HEADS * ML_DK
ML_VO = ML_HEADS * ML_DV
ML_QKVO = 2 * ML_QK + 2 * ML_VO
GATE_SOFTCAP = 15.0
ATT_HD = 64
ATT_QH = 16
ATT_KVH = 4
ATT_GROUP = 4
ATT_KV = ATT_KVH * ATT_HD
WINDOW = 128
D_FF = 4 * D_MODEL
EPS = 1e-6
LOG2E = 1.4426950408889634

LANES = 128
SUBLANES = 8
VMEM_LIMIT_CAP = 56 * 1024 * 1024

PROMPT_CHUNK = 256
PROMPT_ROWS = 2
PROJ_COLS = 512
SAMPLE_PAD = SUBLANES
SAMPLE_NB = 16
ATTN_ROWS = 1024
ATTN_CHAIN_BLOCKS = 2
ROW_TILE = 1024
FF_TILE = 1024


def _vmem_limit(*block_bytes):
    need = 4 * sum(block_bytes) + (8 << 20)
    return int(min(max(need, 32 << 20), VMEM_LIMIT_CAP))


def _nbytes(shape, dtype):
    n = 1
    for s in shape:
        n *= s
    return n * jnp.dtype(dtype).itemsize


def _rms(x, g):
    return x * lax.rsqrt(jnp.mean(x * x, axis=-1, keepdims=True) + EPS) * g


def _dot(a, b):
    return jnp.dot(a, b, preferred_element_type=F32)


def _split3(x):
    hi = x.astype(BF16)
    r1 = x - hi.astype(F32)
    mid = r1.astype(BF16)
    lo = (r1 - mid.astype(F32)).astype(BF16)
    return hi, mid, lo


def _dot_exactish(m01, x):
    hi, mid, lo = _split3(x)
    return _dot(m01, hi) + _dot(m01, mid) + _dot(m01, lo)


def _pad_rows(x_c, nb, n_valid):
    rows, cols = nb * SAMPLE_PAD, nb * n_valid
    r = lax.broadcasted_iota(jnp.int32, (rows, cols), 0)
    c = lax.broadcasted_iota(jnp.int32, (rows, cols), 1)
    sel = (r % SAMPLE_PAD < n_valid) & (c == (r // SAMPLE_PAD) * n_valid + r % SAMPLE_PAD)
    return _dot_exactish(jnp.where(sel, 1.0, 0.0).astype(BF16), x_c)


def _unpad_rows(x_p, nb, n_valid):
    rows, cols = nb * n_valid, nb * SAMPLE_PAD
    r = lax.broadcasted_iota(jnp.int32, (rows, cols), 0)
    c = lax.broadcasted_iota(jnp.int32, (rows, cols), 1)
    sel = c == (r // n_valid) * SAMPLE_PAD + r % n_valid
    return _dot_exactish(jnp.where(sel, 1.0, 0.0).astype(BF16), x_p)


def _gate_act(z):
    cap = GATE_SOFTCAP * jnp.tanh(z * (1.0 / GATE_SOFTCAP))
    lsig = jnp.minimum(cap, 0.0) - jnp.log1p(jnp.exp(-jnp.abs(cap)))
    lane = lax.broadcasted_iota(jnp.int32, z.shape, 1)
    return jnp.where(lane < ML_HEADS, cap, lsig)


_DONE = object()


def _rounds(chains):
    live = [(c[0], c[1], c[2] if len(c) > 2 else 1) for c in chains]
    rnd = 0
    while live:
        for item in list(live):
            gen, start, stride = item
            if rnd >= start and (rnd - start) % stride == 0 and next(gen, _DONE) is _DONE:
                live.remove(item)
        rnd += 1
        yield


def _run(chains):
    for _ in _rounds(chains):
        pass


def _head_out(hh, po, hg):
    hn = hh * lax.rsqrt(jnp.mean(hh * hh, axis=-1, keepdims=True) + EPS) * hg
    return jax.nn.sigmoid(po) * hn


def _head_slices(p, h):
    q = p[:, h * ML_DK:(h + 1) * ML_DK]
    k = p[:, ML_QK + h * ML_DK:ML_QK + (h + 1) * ML_DK] * (ML_DK ** -0.5)
    v = p[:, 2 * ML_QK + h * ML_DV:2 * ML_QK + (h + 1) * ML_DV]
    po = p[:, 2 * ML_QK + ML_VO + h * ML_DV:2 * ML_QK + ML_VO + (h + 1) * ML_DV]
    return q, k, v, po


def _prompt_mixer_kernel(xc_ref, xnext_ref, g_ref, wqkvo_ref, wg_ref, bg_ref, hg_ref, wout_ref, *rest,
                         chunks_per_seq, n_cast):
    cast_in = rest[:n_cast]
    y_ref, C_ref, n_ref, m_ref = rest[n_cast:n_cast + 4]
    cast_out = rest[n_cast + 4:2 * n_cast + 4]
    p_s, gz_s = rest[2 * n_cast + 4:]
    R, T = xc_ref.shape[0], xc_ref.shape[1]
    f = pl.program_id(0)
    slot = f % 2

    def project(x_ref, s, r):
        xn = _rms(x_ref[r], g_ref[...]).astype(BF16)
        yield
        for j in range(ML_QKVO // PROJ_COLS):
            cols = slice(j * PROJ_COLS, (j + 1) * PROJ_COLS)
            p_s[s, r, :, cols] = _dot(xn, wqkvo_ref[:, cols])
            yield
        gz_s[s, r] = _dot(xn, wg_ref[...])
        yield

    @pl.when(f == 0)
    def _():
        _run([(project(xc_ref, 0, r), 0) for r in range(R)])

    @pl.when(f % chunks_per_seq == 0)
    def _():
        C_ref[...] = jnp.zeros_like(C_ref)
        n_ref[...] = jnp.zeros_like(n_ref)
        m_ref[...] = jnp.zeros_like(m_ref)

    def recurrence(r):
        G = _gate_act(gz_s[slot, r] + bg_ref[...])
        row = lax.broadcasted_iota(jnp.int32, (T, T), 0)
        col = lax.broadcasted_iota(jnp.int32, (T, T), 1)
        causal = col <= row
        Bc = _dot_exactish(jnp.where(causal, 1.0, 0.0).astype(BF16), G)
        yield
        Gt = G.T
        Bt = Bc.T
        yield
        parts = [None] * ML_HEADS

        def head(h):
            def cols(base, width):
                return p_s[slot, r, :, base + h * width:base + (h + 1) * width]

            q = cols(0, ML_DK)
            k = cols(ML_QK, ML_DK) * (ML_DK ** -0.5)
            v = cols(2 * ML_QK, ML_DV)
            qc, kc, vc = q.astype(BF16), k.astype(BF16), v.astype(BF16)
            qk = lax.dot_general(qc, kc, (((1,), (1,)), ((), ())), preferred_element_type=F32)
            yield
            li_col = G[:, h:h + 1]
            b_col = Bc[:, ML_HEADS + h:ML_HEADS + h + 1]
            dmat = jnp.where(causal, b_col - Bt[ML_HEADS + h:ML_HEADS + h + 1, :] + Gt[h:h + 1, :], -jnp.inf)
            m_prev = m_ref[r, h:h + 1, 0:1]
            inter = b_col + m_prev
            m_t = jnp.maximum(inter, jnp.max(dmat, axis=1, keepdims=True))
            s = qk * jnp.exp(dmat - m_t)
            a_inter = jnp.exp(inter - m_t)
            den = (jnp.sum(s, axis=1, keepdims=True)
                   + a_inter * jnp.sum(q * n_ref[r, h:h + 1, :], axis=1, keepdims=True))
            yield
            C = C_ref[r, h]
            num = _dot(s.astype(BF16), vc) + a_inter * _dot(qc, C.astype(BF16))
            yield
            hh = num * (1.0 / jnp.maximum(jnp.abs(den), jnp.exp(-m_t)))
            hs = _head_out(hh, cols(2 * ML_QK + ML_VO, ML_DV), hg_ref[:, h * ML_DV:(h + 1) * ML_DV])
            yield
            parts[h] = _dot(hs.astype(BF16), wout_ref[h * ML_DV:(h + 1) * ML_DV, :])
            yield
            m_new = m_t[T - 1:T, :]
            b_last = b_col[T - 1:T, :]
            decay = jnp.exp(b_last + m_prev - m_new)
            kw = k * jnp.exp(b_last - b_col + li_col - m_new)
            C_ref[r, h] = decay * C + lax.dot_general(kw.astype(BF16), vc, (((0,), (0,)), ((), ())),
                                                      preferred_element_type=F32)
            n_ref[r, h:h + 1, :] = decay * n_ref[r, h:h + 1, :] + jnp.sum(kw, axis=0, keepdims=True)
            m_ref[r, h:h + 1, :] = jnp.broadcast_to(m_new, (1, LANES))
            yield

        yield from _rounds([(head(h), 2 * h) for h in range(ML_HEADS)])
        y_ref[r] = xc_ref[r] + ((parts[0] + parts[1]) + (parts[2] + parts[3]))

    def casts():
        for src, dst in zip(cast_in, cast_out):
            dst[...] = src[...].astype(BF16)
            yield

    _run([(project(xnext_ref, 1 - slot, r), 0) for r in range(R)] + [(recurrence(r), r) for r in range(R)]
         + [(casts(), 0)])


def _prompt_mixer(x, g, wqkvo, wg, bg, hg, wout, to_cast):
    B, L, D = x.shape
    T = PROMPT_CHUNK
    assert L % T == 0
    R = PROMPT_ROWS
    assert B % R == 0
    nc = L // T
    steps = (B // R) * nc
    const = lambda f: (0, 0)
    cur = lambda f: (f // nc, f % nc, 0)
    nxt = lambda f: (jnp.minimum(f + 1, steps - 1) // nc, jnp.minimum(f + 1, steps - 1) % nc, 0)
    once = dict(pipeline_mode=pl.Buffered(1))
    vmem = _vmem_limit(3 * _nbytes((R, T, D), F32), _nbytes(wqkvo.shape, BF16) // 2,
                       _nbytes(wout.shape, BF16) // 2, _nbytes((R, ML_HEADS, ML_DK, ML_DV), F32),
                       2 * _nbytes((R, T, ML_QKVO), F32))
    for w in to_cast:
        assert w.shape[0] % (steps * 2 * SUBLANES) == 0
    cast_specs = [pl.BlockSpec((w.shape[0] // steps, w.shape[1]), lambda f: (f, 0)) for w in to_cast]
    outs = pl.pallas_call(
        functools.partial(_prompt_mixer_kernel, chunks_per_seq=nc, n_cast=len(to_cast)),
        grid=(steps,),
        in_specs=[
            pl.BlockSpec((R, T, D), cur),
            pl.BlockSpec((R, T, D), nxt),
            pl.BlockSpec((1, D), const),
            pl.BlockSpec((D, ML_QKVO), const, **once),
            pl.BlockSpec((D, LANES), const, **once),
            pl.BlockSpec((1, LANES), const),
            pl.BlockSpec((1, ML_VO), const),
            pl.BlockSpec((ML_VO, D), const, **once),
        ] + cast_specs,
        out_specs=[
            pl.BlockSpec((R, T, D), cur),
            pl.BlockSpec((R, ML_HEADS, ML_DK, ML_DV), lambda f: (f // nc, 0, 0, 0)),
            pl.BlockSpec((R, ML_HEADS, ML_DK), lambda f: (f // nc, 0, 0)),
            pl.BlockSpec((R, ML_HEADS, LANES), lambda f: (f // nc, 0, 0)),
        ] + cast_specs,
        out_shape=[
            jax.ShapeDtypeStruct((B, L, D), F32),
            jax.ShapeDtypeStruct((B, ML_HEADS, ML_DK, ML_DV), F32),
            jax.ShapeDtypeStruct((B, ML_HEADS, ML_DK), F32),
            jax.ShapeDtypeStruct((B, ML_HEADS, LANES), F32),
        ] + [jax.ShapeDtypeStruct(w.shape, BF16) for w in to_cast],
        scratch_shapes=[pltpu.VMEM((2, R, T, ML_QKVO), F32), pltpu.VMEM((2, R, T, LANES), F32)],
        compiler_params=pltpu.CompilerParams(
            dimension_semantics=("arbitrary",), vmem_limit_bytes=vmem),
        name="prompt_mlstm_mixer",
    )(x, x, g, wqkvo, wg, bg, hg, wout, *to_cast)
    return outs


def _sample_mixer_kernel(x_ref, g_ref, wqkvo_ref, wg_ref, bg_ref, hg_ref, wout_ref,
                         C0_ref, n0_ref, m0_ref,
                         y_ref, C_ref, n_ref, m_ref, hs_s, *, n_valid):
    T = SAMPLE_PAD
    nb = C0_ref.shape[0]
    R = nb * T
    x = _pad_rows(x_ref[...], nb, n_valid)
    xn = _rms(x, g_ref[...]).astype(BF16)
    p = _dot(xn, wqkvo_ref[...])

    rowt = lax.broadcasted_iota(jnp.int32, (R, LANES), 0) % T
    lane = lax.broadcasted_iota(jnp.int32, (R, LANES), 1)
    G = jnp.where(rowt < n_valid, _gate_act(_dot(xn, wg_ref[...]) + bg_ref[...]),
                  jnp.where(lane < ML_HEADS, -jnp.inf, 0.0))
    row = lax.broadcasted_iota(jnp.int32, (R, R), 0)
    col = lax.broadcasted_iota(jnp.int32, (R, R), 1)
    causal = (row // T == col // T) & (col <= row)
    Bc = _dot_exactish(jnp.where(causal, 1.0, 0.0).astype(BF16), jnp.where(lane < ML_HEADS, 0.0, G))
    Gt = G.T
    Bt = Bc.T

    def per_seq(fn):
        return jnp.concatenate([fn(a) for a in range(nb)], axis=0)

    def seq_last(colvec):
        return per_seq(lambda a: jnp.broadcast_to(colvec[T * a + T - 1:T * a + T, :], (T, 1)))

    def head(h):
        q, k, v, po = _head_slices(p, h)
        li_col = G[:, h:h + 1]
        b_col = Bc[:, ML_HEADS + h:ML_HEADS + h + 1]
        li_row = Gt[h:h + 1, :]
        b_row = Bt[ML_HEADS + h:ML_HEADS + h + 1, :]
        dmat = jnp.where(causal, b_col - (b_row - li_row), -jnp.inf)
        m_prev = per_seq(lambda a: jnp.broadcast_to(m0_ref[a, h:h + 1, 0:1], (T, 1)))
        n_rows = per_seq(lambda a: jnp.broadcast_to(n0_ref[a, h:h + 1, :], (T, ML_DK)))
        qc, kc, vc = q.astype(BF16), k.astype(BF16), v.astype(BF16)
        qk = lax.dot_general(qc, kc, (((1,), (1,)), ((), ())), preferred_element_type=F32)
        yield
        inter = b_col + m_prev
        m_t = jnp.maximum(inter, jnp.max(dmat, axis=1, keepdims=True))
        s = qk * jnp.exp(dmat - m_t)
        a_inter = jnp.exp(inter - m_t)
        yield
        qC = per_seq(lambda a: _dot(q[T * a:T * a + T, :], C0_ref[a, h]))
        yield
        num = _dot(s.astype(BF16), vc) + a_inter * qC
        den = jnp.sum(s, axis=1, keepdims=True) + a_inter * jnp.sum(q * n_rows, axis=1, keepdims=True)
        hh = num * (1.0 / jnp.maximum(jnp.abs(den), jnp.exp(-m_t)))
        hs_s[:, h * ML_DV:(h + 1) * ML_DV] = _head_out(
            hh, po, hg_ref[:, h * ML_DV:(h + 1) * ML_DV]).astype(BF16)
        yield
        m_new = seq_last(m_t)
        b_last = seq_last(b_col)
        decay = jnp.exp(b_last + m_prev - m_new)
        kw = k * jnp.exp(b_last - b_col + li_col - m_new)
        for a in range(nb):
            dec = decay[T * a:T * a + 1, :]
            kw_a = kw[T * a:T * a + T, :]
            upd = lax.dot_general(kw_a, v[T * a:T * a + T, :], (((0,), (0,)), ((), ())),
                                  preferred_element_type=F32)
            C_ref[a, h] = dec * C0_ref[a, h] + upd
            n_ref[a, h:h + 1, :] = dec * n0_ref[a, h:h + 1, :] + jnp.sum(kw_a, axis=0, keepdims=True)
            m_ref[a, h:h + 1, :] = jnp.broadcast_to(m_new[T * a:T * a + 1, :], (1, LANES))
            if a % 4 == 3:
                yield

    _run([(head(h), h) for h in range(ML_HEADS)])
    y_ref[...] = _unpad_rows(x + _dot(hs_s[...], wout_ref[...]), nb, n_valid)


def _sample_mixer(x, g, wqkvo, wg, bg, hg, wout, C0, n0, m0, n_valid):
    NT, D = x.shape
    nseq = C0.shape[0]
    nb = SAMPLE_NB
    R = nb * SAMPLE_PAD
    RC = nb * n_valid
    assert nseq % nb == 0 and NT == nseq * n_valid and RC % SUBLANES == 0
    const = lambda i: (0, 0)
    state_specs = [
        pl.BlockSpec((nb, ML_HEADS, ML_DK, ML_DV), lambda i: (i, 0, 0, 0)),
        pl.BlockSpec((nb, ML_HEADS, ML_DK), lambda i: (i, 0, 0)),
        pl.BlockSpec((nb, ML_HEADS, LANES), lambda i: (i, 0, 0)),
    ]
    vmem = _vmem_limit(2 * _nbytes((R, D), F32), _nbytes(wqkvo.shape, BF16), _nbytes(wout.shape, BF16),
                       2 * _nbytes((nb, ML_HEADS, ML_DK, ML_DV), F32), _nbytes((R, ML_QKVO), F32))
    return pl.pallas_call(
        functools.partial(_sample_mixer_kernel, n_valid=n_valid),
        grid=(nseq // nb,),
        in_specs=[
            pl.BlockSpec((RC, D), lambda i: (i, 0)),
            pl.BlockSpec((1, D), const),
            pl.BlockSpec((D, ML_QKVO), const),
            pl.BlockSpec((D, LANES), const),
            pl.BlockSpec((1, LANES), const),
            pl.BlockSpec((1, ML_VO), const),
            pl.BlockSpec((ML_VO, D), const),
        ] + state_specs,
        out_specs=[pl.BlockSpec((RC, D), lambda i: (i, 0))] + state_specs,
        out_shape=[
            jax.ShapeDtypeStruct((NT, D), F32),
            jax.ShapeDtypeStruct(C0.shape, F32),
            jax.ShapeDtypeStruct(n0.shape, F32),
            jax.ShapeDtypeStruct(m0.shape, F32),
        ],
        scratch_shapes=[pltpu.VMEM((R, ML_VO), BF16)],
        compiler_params=pltpu.CompilerParams(
            dimension_semantics=("arbitrary",), vmem_limit_bytes=vmem),
        name="sample_mlstm_mixer",
    )(x, g, wqkvo, wg, bg, hg, wout, C0, n0, m0)


def _dot_exactish_right(x, m01):
    hi, mid, lo = _split3(x)
    return _dot(hi, m01) + _dot(mid, m01) + _dot(lo, m01)


def _mlp_kernel(x_ref, g_ref, w1_ref, w2_ref, *rest):
    x = x_ref[...]
    xn = _rms(x, g_ref[...]).astype(BF16)
    acc = x
    for c in range(D_FF // FF_TILE):
        hcol = _dot(xn, w1_ref[:, c * FF_TILE:(c + 1) * FF_TILE])
        hcol = jnp.square(jnp.maximum(hcol, 0.0)).astype(BF16)
        acc = acc + _dot(hcol, w2_ref[c * FF_TILE:(c + 1) * FF_TILE, :])
    if len(rest) == 1:
        (y_ref,) = rest
        y_ref[...] = acc
        return
    gkv_ref, wkv_ref, kg_ref, y_ref, k_ref, v_ref = rest
    y_ref[...] = acc
    kv = _dot(_rms(acc, gkv_ref[...]).astype(BF16), wkv_ref[...])
    kraw = kv[:, :ATT_KV]
    v_ref[...] = kv[:, ATT_KV:]
    r = lax.broadcasted_iota(jnp.int32, (ATT_KV, ATT_KV), 0) // ATT_HD
    c = lax.broadcasted_iota(jnp.int32, (ATT_KV, ATT_KV), 1) // ATT_HD
    seg = jnp.where(r == c, 1.0, 0.0).astype(BF16)
    ss = _dot_exactish_right(kraw * kraw, seg)
    k_ref[...] = kraw * lax.rsqrt(ss * (1.0 / ATT_HD) + EPS) * kg_ref[...]


def _mlp(x, g, w1, w2, layer, kv=None):
    N, D = x.shape
    tm = min(ROW_TILE, N)
    assert N % tm == 0
    const = lambda i: (0, 0)
    rows = lambda i: (i, 0)
    once = dict(pipeline_mode=pl.Buffered(1))
    in_specs = [
        pl.BlockSpec((tm, D), rows),
        pl.BlockSpec((1, D), const),
        pl.BlockSpec((None, D, D_FF), lambda i: (layer, 0, 0), **once),
        pl.BlockSpec((None, D_FF, D), lambda i: (layer, 0, 0), **once),
    ]
    out_specs = [pl.BlockSpec((tm, D), rows)]
    out_shape = [jax.ShapeDtypeStruct((N, D), F32)]
    args = [x, g, w1, w2]
    if kv is not None:
        in_specs += [pl.BlockSpec((1, D), const), pl.BlockSpec((D, 2 * ATT_KV), const, **once),
                     pl.BlockSpec((1, ATT_KV), const)]
        out_specs += [pl.BlockSpec((tm, ATT_KV), rows)] * 2
        out_shape += [jax.ShapeDtypeStruct((N, ATT_KV), F32)] * 2
        args += list(kv)
    vmem = _vmem_limit(2 * _nbytes((tm, D), F32), _nbytes(w1.shape[1:], BF16) // 2,
                       _nbytes(w2.shape[1:], BF16) // 2, 2 * _nbytes((tm, FF_TILE), F32))
    out = pl.pallas_call(
        _mlp_kernel,
        grid=(N // tm,),
        in_specs=in_specs,
        out_specs=out_specs,
        out_shape=out_shape,
        compiler_params=pltpu.CompilerParams(
            dimension_semantics=("arbitrary",), vmem_limit_bytes=vmem),
        name="sqrelu_mlp",
    )(*args)
    return out[0] if kv is None else out


def _pair_queries(qraw, pr, qscale, col0=0):
    TQ = qraw.shape[0]
    lo = lax.broadcasted_iota(jnp.int32, (TQ, LANES), 1) < ATT_HD
    out = []
    for e in range(2):
        kvh = 2 * pr + e
        for g in range(ATT_GROUP):
            cc, half = divmod(g, 2)
            c0 = (2 * kvh + cc) * LANES - col0
            q2 = qraw[:, c0:c0 + LANES]
            qm = jnp.where(lo, q2, 0.0) if half == 0 else jnp.where(lo, 0.0, q2)
            ss = jnp.sum(qm * qm, axis=1, keepdims=True)
            qn = qm * lax.rsqrt(ss * (1.0 / ATT_HD) + EPS)
            if qscale is not None:
                qn = qn * qscale
            out.append(qn if half == e else pltpu.roll(qn, ATT_HD, 1))
    return out


def _pair_outputs(o_heads, rden, pr, store):
    TQ = o_heads[0].shape[0]
    lo = lax.broadcasted_iota(jnp.int32, (TQ, LANES), 1) < ATT_HD
    for e in range(2):
        kvh = 2 * pr + e
        for cc in range(2):
            tiles = []
            for half in range(2):
                o = o_heads[e * ATT_GROUP + 2 * cc + half] * rden[e * ATT_GROUP + 2 * cc + half]
                tiles.append(o if half == e else pltpu.roll(o, ATT_HD, 1))
            store((2 * kvh + cc) * LANES, jnp.where(lo, tiles[0], tiles[1]))


def _softmax_with_sink(parts, sink):
    assert all(s.shape == parts[0].shape for s in parts)
    M = jnp.maximum(sink, jnp.max(functools.reduce(jnp.maximum, parts), axis=1, keepdims=True))
    ps = [jnp.exp2(s - M) for s in parts]
    den = jnp.exp2(sink - M) + jnp.sum(functools.reduce(jnp.add, ps), axis=1, keepdims=True)
    return ps, 1.0 / den


def _prompt_attn_kernel(sinks_ref, x_ref, g_ref, wq_ref, qg_ref, gkv_ref, wkv_ref, kg_ref, wo_ref,
                        y_ref, pk_ref, pv_ref, kprev_s, vprev_s):
    TQ = WINDOW
    TR = x_ref.shape[0]
    kj = lax.broadcasted_iota(jnp.int32, (WINDOW + TQ, TQ), 0)
    qi = lax.broadcasted_iota(jnp.int32, (WINDOW + TQ, TQ), 1)
    band = (qi + WINDOW - kj >= 0) & (qi - kj <= 0)
    band_first = band & ((kj >= WINDOW) | (pl.program_id(1) > 0))
    kscale = qg_ref[...] * (ATT_HD ** -0.5 * LOG2E)

    @pl.when(pl.program_id(1) == 0)
    def _():
        kprev_s[...] = jnp.zeros_like(kprev_s)
        vprev_s[...] = jnp.zeros_like(vprev_s)

    n_chains = TR // (ATTN_CHAIN_BLOCKS * TQ)
    kv_rows = [None] * n_chains

    PW = 2 * ATT_GROUP * ATT_HD
    zeros_hd = jnp.zeros((ATT_HD, TQ), F32)
    ones_rows = jnp.ones((2 * SUBLANES, WINDOW + TQ), F32)

    def chain(cb):
        nsb = ATTN_CHAIN_BLOCKS
        rows = slice(cb * nsb * TQ, (cb + 1) * nsb * TQ)
        x = x_ref[rows, :]
        kv = _dot(_rms(x, gkv_ref[...]).astype(BF16), wkv_ref[...])
        kraw = kv[:, :ATT_KV]
        seg_r = lax.broadcasted_iota(jnp.int32, (ATT_KV, ATT_KV), 0) // ATT_HD
        seg_c = lax.broadcasted_iota(jnp.int32, (ATT_KV, ATT_KV), 1) // ATT_HD
        ss = _dot_exactish_right(kraw * kraw, jnp.where(seg_r == seg_c, 1.0, 0.0).astype(BF16))
        k_blk = kraw * lax.rsqrt(ss * (1.0 / ATT_HD) + EPS) * kg_ref[...]
        v_blk = kv[:, ATT_KV:]
        kv_rows[cb] = (k_blk, v_blk)
        if cb == 0:
            prev = (kprev_s[...], vprev_s[...])
        if cb == n_chains - 1:
            for dst, blk in ((kprev_s, k_blk), (pk_ref, k_blk), (vprev_s, v_blk), (pv_ref, v_blk)):
                dst[...] = blk[nsb * TQ - WINDOW:, :]
        yield
        if cb > 0:
            prev = tuple(blk[nsb * TQ - WINDOW:, :] for blk in kv_rows[cb - 1])
        k_all = jnp.concatenate([prev[0], k_blk], axis=0)
        v_all = jnp.concatenate([prev[1], v_blk], axis=0)
        slabs = []
        for pr in range(ATT_KVH // 2):
            sl = slice(pr * LANES, (pr + 1) * LANES)
            slabs.append(((k_all[:, sl] * kscale).astype(BF16), v_all[:, sl].T))
        xn = _rms(x, g_ref[...]).astype(BF16)
        acc = x
        for pr in range(ATT_KVH // 2):
            kslab, vslab_t = slabs[pr]
            qt_all = _dot(xn, wq_ref[:, pr * PW:(pr + 1) * PW]).T
            yield
            outs = []
            for i in range(nsb):
                sb = cb * nsb + i
                keys = slice(i * TQ, i * TQ + WINDOW + TQ)
                mask = band_first if sb == 0 else band
                qt = qt_all[:, i * TQ:(i + 1) * TQ]
                tiles, sink_rows = [], []
                for e in range(2):
                    for g in range(ATT_GROUP):
                        blk = qt[(e * ATT_GROUP + g) * ATT_HD:(e * ATT_GROUP + g + 1) * ATT_HD, :]
                        qn = blk * lax.rsqrt(jnp.sum(blk * blk, axis=0, keepdims=True) * (1.0 / ATT_HD) + EPS)
                        tiles.append(jnp.concatenate([qn, zeros_hd] if e == 0 else [zeros_hd, qn], axis=0))
                        sink_rows.append(jnp.full((1, TQ), sinks_ref[(2 * pr + e) * ATT_GROUP + g] * LOG2E, F32))
                qmat = jnp.concatenate(tiles, axis=1).astype(BF16)
                st = _dot(kslab[keys, :], qmat)
                yield
                parts = []
                for e in range(2):
                    ps, ms = [], []
                    for g in range(ATT_GROUP):
                        c0 = (e * ATT_GROUP + g) * TQ
                        s_h = jnp.where(mask, st[:, c0:c0 + TQ], -jnp.inf)
                        m_h = jnp.maximum(jnp.max(s_h, axis=0, keepdims=True), sink_rows[e * ATT_GROUP + g])
                        ps.append(jnp.exp2(s_h - m_h).astype(BF16))
                        ms.append(m_h)
                    p_e = jnp.concatenate(ps, axis=1)
                    v_aug = jnp.concatenate([vslab_t[e * ATT_HD:(e + 1) * ATT_HD, keys], ones_rows],
                                            axis=0).astype(BF16)
                    ot = _dot(v_aug, p_e)
                    yield
                    sink_term = jnp.exp2(jnp.concatenate(sink_rows[e * ATT_GROUP:(e + 1) * ATT_GROUP], axis=1)
                                         - jnp.concatenate(ms, axis=1))
                    on = ot[0:ATT_HD, :] * (1.0 / (ot[ATT_HD:ATT_HD + 1, :] + sink_term))
                    parts += [on[:, g * TQ:(g + 1) * TQ] for g in range(ATT_GROUP)]
                outs.append(jnp.concatenate(parts, axis=0))
            o_pair = jnp.concatenate(outs, axis=1).T.astype(BF16)
            acc = acc + _dot(o_pair, wo_ref[pr * PW:(pr + 1) * PW, :])
            yield
        y_ref[rows, :] = acc

    _run([(chain(cb), cb) for cb in range(n_chains)])


def _prompt_attn(x, sinks, g, wq, qg, kv, wo):
    B, L, D = x.shape
    TR = ATTN_ROWS
    assert L % TR == 0 and TR % (ATTN_CHAIN_BLOCKS * WINDOW) == 0
    gkv, wkv, kg = kv
    const = lambda b, i: (0, 0)
    cur = lambda b, i: (b, i, 0)
    win = pl.BlockSpec((None, WINDOW, ATT_KV), lambda b, i: (b, 0, 0))
    vmem = _vmem_limit(2 * _nbytes((TR, D), F32), _nbytes(wq.shape, BF16), _nbytes(wo.shape, BF16),
                       _nbytes(wkv.shape, BF16), 8 * _nbytes((8 * WINDOW, 2 * WINDOW), F32))
    return pl.pallas_call(
        _prompt_attn_kernel,
        grid=(B, L // TR),
        in_specs=[
            pl.BlockSpec(memory_space=pltpu.SMEM),
            pl.BlockSpec((None, TR, D), cur),
            pl.BlockSpec((1, D), const),
            pl.BlockSpec((D, D), const),
            pl.BlockSpec((1, LANES), const),
            pl.BlockSpec((1, D), const),
            pl.BlockSpec((D, 2 * ATT_KV), const),
            pl.BlockSpec((1, ATT_KV), const),
            pl.BlockSpec((D, D), const),
        ],
        out_specs=[pl.BlockSpec((None, TR, D), cur), win, win],
        out_shape=[jax.ShapeDtypeStruct((B, L, D), F32),
                   jax.ShapeDtypeStruct((B, WINDOW, ATT_KV), F32), jax.ShapeDtypeStruct((B, WINDOW, ATT_KV), F32)],
        scratch_shapes=[pltpu.VMEM((WINDOW, ATT_KV), F32), pltpu.VMEM((WINDOW, ATT_KV), F32)],
        compiler_params=pltpu.CompilerParams(
            dimension_semantics=("arbitrary", "arbitrary"), vmem_limit_bytes=vmem),
        name="prompt_window_attention",
    )(sinks, x, g, wq, qg, gkv, wkv, kg, wo)


def _sample_attn_kernel(sinks_ref, x_ref, g_ref, wq_ref, qg_ref, kc_ref, vc_ref, wk_ref, wv_ref, wo_ref,
                        y_ref, nk_ref, nv_ref, o_s, *, n_valid):
    T = SAMPLE_PAD
    nb = wk_ref.shape[0]
    R = nb * T
    assert R == WINDOW
    x = _pad_rows(x_ref[...], nb, n_valid)
    q = _dot(_rms(x, g_ref[...]).astype(BF16), wq_ref[...])
    qscale = qg_ref[...] * (ATT_HD ** -0.5 * LOG2E)
    knew = _pad_rows(kc_ref[...], nb, n_valid)
    vnew = _pad_rows(vc_ref[...], nb, n_valid)

    cmask = (lax.broadcasted_iota(jnp.int32, (R, WINDOW), 1)
             >= lax.broadcasted_iota(jnp.int32, (R, WINDOW), 0) % T)
    row = lax.broadcasted_iota(jnp.int32, (R, R), 0)
    col = lax.broadcasted_iota(jnp.int32, (R, R), 1)
    nmask = (row // T == col // T) & (col % T <= row % T) & (col % T < n_valid)

    def store(off, val):
        o_s[:, off:off + LANES] = val.astype(BF16)

    def regroup(per_seq, i):
        return jnp.concatenate([per_seq[a][i * T:(i + 1) * T, :] for a in range(nb)], axis=0)

    def pair(pr):
        sl = slice(pr * LANES, (pr + 1) * LANES)
        qs = _pair_queries(q, pr, qscale)
        S_new = lax.dot_general(jnp.concatenate(qs, axis=0).astype(BF16), knew[:, sl].astype(BF16),
                                (((1,), (1,)), ((), ())), preferred_element_type=F32)
        yield
        sc = []
        for a in range(nb):
            q_a = jnp.concatenate([qi[T * a:T * a + T, :] for qi in qs], axis=0).astype(BF16)
            sc.append(_dot(q_a, wk_ref[a, sl, :].astype(BF16)))
            if a % 4 == 3:
                yield
        pcs, pns, rden = [], [], []
        for i in range(2 * ATT_GROUP):
            s_c = jnp.where(cmask, regroup(sc, i), -jnp.inf)
            s_n = jnp.where(nmask, S_new[i * R:(i + 1) * R, :], -jnp.inf)
            (p_c, p_n), r = _softmax_with_sink([s_c, s_n], sinks_ref[pr * 2 * ATT_GROUP + i] * LOG2E)
            pcs.append(p_c)
            pns.append(p_n)
            rden.append(r)
            if i % 2 == 1:
                yield
        O_new = _dot(jnp.concatenate(pns, axis=0).astype(BF16), vnew[:, sl].astype(BF16))
        yield
        oc = []
        for a in range(nb):
            p_a = jnp.concatenate([pc[T * a:T * a + T, :] for pc in pcs], axis=0).astype(BF16)
            oc.append(lax.dot_general(p_a, wv_ref[a, sl, :].astype(BF16), (((1,), (1,)), ((), ())),
                                      preferred_element_type=F32))
            if a % 4 == 3:
                yield
        _pair_outputs([O_new[i * R:(i + 1) * R, :] + regroup(oc, i) for i in range(2 * ATT_GROUP)],
                      rden, pr, store)
        yield

    def roll_cache():
        keep = lax.broadcasted_iota(jnp.int32, (ATT_KV, WINDOW), 1) < WINDOW - n_valid
        k_parts = _split3(knew.T)
        v_parts = _split3(vnew.T)
        group = 4
        rr = lax.broadcasted_iota(jnp.int32, (R, group * WINDOW), 0)
        cc = lax.broadcasted_iota(jnp.int32, (R, group * WINDOW), 1)
        hit = (cc % WINDOW == WINDOW - n_valid + rr % T) & (rr % T < n_valid)
        seq_off = rr // T - cc // WINDOW
        yield
        for g0 in range(0, nb, group):
            place = jnp.where(hit & (seq_off == g0), 1.0, 0.0).astype(BF16)
            new_k = _dot(k_parts[0], place) + _dot(k_parts[1], place) + _dot(k_parts[2], place)
            new_v = _dot(v_parts[0], place) + _dot(v_parts[1], place) + _dot(v_parts[2], place)
            for j in range(group):
                lanes = slice(j * WINDOW, (j + 1) * WINDOW)
                nk_ref[g0 + j] = jnp.where(keep, pltpu.roll(wk_ref[g0 + j], WINDOW - n_valid, 1), new_k[:, lanes])
                nv_ref[g0 + j] = jnp.where(keep, pltpu.roll(wv_ref[g0 + j], WINDOW - n_valid, 1), new_v[:, lanes])
            yield

    _run([(pair(0), 0), (pair(1), 2), (roll_cache(), 0)])
    y_ref[...] = _unpad_rows(x + _dot(o_s[...], wo_ref[...]), nb, n_valid)


def _sample_attn(x, k, v, win_k, win_v, sinks, g, wq, qg, wo, n_valid):
    NT, D = x.shape
    nseq = win_k.shape[0]
    nb = SAMPLE_NB
    R = nb * SAMPLE_PAD
    RC = nb * n_valid
    assert nseq % nb == 0 and NT == nseq * n_valid and RC % SUBLANES == 0
    const = lambda i: (0, 0)
    rows = lambda i: (i, 0)
    cache = pl.BlockSpec((nb, ATT_KV, WINDOW), lambda i: (i, 0, 0))
    vmem = _vmem_limit(2 * _nbytes((R, D), F32), _nbytes(wq.shape, BF16), _nbytes(wo.shape, BF16),
                       4 * _nbytes((nb, ATT_KV, WINDOW), F32), 2 * _nbytes((R, D), F32))
    return pl.pallas_call(
        functools.partial(_sample_attn_kernel, n_valid=n_valid),
        grid=(nseq // nb,),
        in_specs=[
            pl.BlockSpec(memory_space=pltpu.SMEM),
            pl.BlockSpec((RC, D), rows),
            pl.BlockSpec((1, D), const),
            pl.BlockSpec((D, D), const),
            pl.BlockSpec((1, LANES), const),
            pl.BlockSpec((RC, ATT_KV), rows),
            pl.BlockSpec((RC, ATT_KV), rows),
            cache, cache,
            pl.BlockSpec((D, D), const),
        ],
        out_specs=[pl.BlockSpec((RC, D), rows), cache, cache],
        out_shape=[jax.ShapeDtypeStruct((NT, D), F32),
                   jax.ShapeDtypeStruct(win_k.shape, F32), jax.ShapeDtypeStruct(win_v.shape, F32)],
        scratch_shapes=[pltpu.VMEM((R, D), BF16)],
        compiler_params=pltpu.CompilerParams(
            dimension_semantics=("arbitrary",), vmem_limit_bytes=vmem),
        name="sample_window_attention",
    )(sinks, x, g, wq, qg, k, v, win_k, win_v, wo)


def kernel(x_prompt, x_sample, state_mlstm_C, state_mlstm_n, state_mlstm_m, cache_win_k, cache_win_v,
           ml_norm_g, ml_w_in, ml_b_i, ml_b_f, ml_head_g, ml_w_out, kv_norm_g, w_kv, k_norm_g,
           att_norm_g, att_w_q, q_norm_g, att_sinks, att_w_o, mlp_norm_g, mlp_w1, mlp_w2):
    B, L, D = x_prompt.shape
    NS, LS, _ = x_sample.shape
    assert ml_w_in.shape[0] == 1 and att_w_q.shape[0] == 1 and mlp_w1.shape[0] == 2

    w_in = ml_w_in[0]
    wqkvo = w_in.astype(BF16)
    wg = jnp.pad(w_in[:, ML_QKVO:], ((0, 0), (0, LANES - 2 * ML_HEADS))).astype(BF16)
    bg = jnp.pad(jnp.concatenate([ml_b_i[0], ml_b_f[0]]), (0, LANES - 2 * ML_HEADS)).reshape(1, LANES)
    ml_g = ml_norm_g[0].reshape(1, D)
    hg = ml_head_g[0].reshape(1, ML_VO)
    wout = ml_w_out[0].astype(BF16)
    mlp_g = mlp_norm_g.reshape(2, 1, D)
    kv_g = kv_norm_g.reshape(1, D)
    kg = jnp.tile(k_norm_g, ATT_KVH).reshape(1, ATT_KV)
    att_g = att_norm_g[0].reshape(1, D)
    qg = jnp.tile(q_norm_g[0], 2).reshape(1, LANES)
    sinks = att_sinks[0]

    xp, p_C, p_n, p_m, w1, w2, wkv, wq, wo = _prompt_mixer(
        x_prompt, ml_g, wqkvo, wg, bg, hg, wout,
        [mlp_w1.reshape(2 * D, D_FF), mlp_w2.reshape(2 * D_FF, D), w_kv, att_w_q[0], att_w_o[0]])
    w1 = w1.reshape(2, D, D_FF)
    w2 = w2.reshape(2, D_FF, D)
    xp = _mlp(xp.reshape(B * L, D), mlp_g[0], w1, w2, 0)
    xp, p_wk, p_wv = _prompt_attn(xp.reshape(B, L, D), sinks, att_g, wq, qg, (kv_g, wkv, kg), wo)
    y_prompt = _mlp(xp.reshape(B * L, D), mlp_g[1], w1, w2, 1).reshape(B, L, D)
    p_wk = p_wk.reshape(B, WINDOW, ATT_KVH, ATT_HD)
    p_wv = p_wv.reshape(B, WINDOW, ATT_KVH, ATT_HD)

    m0 = jnp.broadcast_to(state_mlstm_m[0][:, :, None], (NS, ML_HEADS, LANES))
    xs, s_C, s_n, s_m = _sample_mixer(x_sample.reshape(NS * LS, D), ml_g, wqkvo, wg, bg, hg, wout,
                                      state_mlstm_C[0], state_mlstm_n[0], m0, LS)
    xs, ks, vs = _mlp(xs, mlp_g[0], w1, w2, 0, kv=(kv_g, wkv, kg))
    to_t = lambda c: c.transpose(0, 2, 3, 1).reshape(NS, ATT_KV, WINDOW)
    from_t = lambda c: c.reshape(NS, ATT_KVH, ATT_HD, WINDOW).transpose(0, 3, 1, 2)
    xs, s_wk, s_wv = _sample_attn(xs, ks, vs, to_t(cache_win_k), to_t(cache_win_v),
                                  sinks, att_g, wq, qg, wo, LS)
    y_sample = _mlp(xs, mlp_g[1], w1, w2, 1).reshape(NS, LS, D)

    return (y_prompt, y_sample,
            p_C[None], p_n[None], p_m[None, :, :, 0], p_wk, p_wv,
            s_C[None], s_n[None], s_m[None, :, :, 0],
            from_t(s_wk), from_t(s_wv))
```
